```python
import math
import jax, jax.numpy as jnp
from jax import lax
import numpy as np

D_MODEL = 2048
BATCH = 4
SEQ = 8192
DEPTH = 1

HEAD_DIM = 128
A_HEADS = 8
DILATION_GROUPS = ((128, 1), (512, 4), (2048, 16))
N_GROUPS = 3
A_BLOCK = 64
NEG_INF = -1e30
B_QK_HEADS = 8
B_V_HEADS = 16
B_DK = 128
B_DV = 128
CONV_W = 5
CHUNK = 64
RMS_EPS = 1e-6
N_EXPERTS = 32
TOP_K = 4
D_FF = D_MODEL
SWIGLU_ALPHA = 1.702
SWIGLU_LIMIT = 7.0
MOE_BLOCK = 512
PLE_DIM = 256
DEEPNORM_ALPHA = (2 * DEPTH) ** 0.25
DEEPNORM_BETA = (8 * DEPTH) ** -0.25
LN_EPS = 1e-5
A_QKV = N_GROUPS * A_HEADS * HEAD_DIM
B_QK = B_QK_HEADS * B_DK
B_VZ = B_V_HEADS * B_DV
B_GATES = 4 * B_V_HEADS
SPLITS = (A_QKV, A_QKV, A_QKV, B_QK, B_QK, B_VZ, B_VZ, B_GATES, 2 * D_MODEL)
N_IN_COLS = sum(SPLITS)

kernel_name = 'hybrid_dilated_attn_deltanet_moe_encoder'


def layer_norm(x, g, b):
    x = x.astype(jnp.float32)
    mu = jnp.mean(x, axis=-1, keepdims=True)
    var = jnp.mean(jnp.square(x - mu), axis=-1, keepdims=True)
    return (x - mu) * lax.rsqrt(var + LN_EPS) * g.astype(jnp.float32) + b.astype(jnp.float32)


def alibi_slopes():
    n = N_GROUPS * A_HEADS
    s = 2.0 ** (-8.0 * np.arange(1, n + 1) / n)
    return s.astype(np.float32).reshape(N_GROUPS, A_HEADS)


def dilated_window_attention(q, k, v, dilation, n_side, slopes):
    bsz, seq, heads, dh = q.shape
    r = dilation
    L = seq // r
    nb = -(-L // A_BLOCK)
    Lp = nb * A_BLOCK
    span = A_BLOCK + 2 * n_side

    def split(t):
        return t.reshape(bsz, L, r, heads, dh).transpose(0, 2, 1, 3, 4).reshape(bsz * r, L, heads, dh)

    qs = jnp.pad(split(q), ((0, 0), (0, Lp - L), (0, 0), (0, 0)))
    kpad = ((0, 0), (n_side, Lp - L + n_side), (0, 0), (0, 0))
    ks = jnp.pad(split(k), kpad)
    vs = jnp.pad(split(v), kpad)
    qb = qs.reshape(bsz * r, nb, A_BLOCK, heads, dh)
    key_idx = np.arange(nb)[:, None] * A_BLOCK + np.arange(span)[None, :]
    kb = ks[:, key_idx]
    vb = vs[:, key_idx]
    key_pos = key_idx - n_side
    key_valid = (key_pos >= 0) & (key_pos < L)
    delta = np.arange(span)[None, :] - n_side - np.arange(A_BLOCK)[:, None]
    mask = (np.abs(delta) <= n_side)[None] & key_valid[:, None, :]
    bias = -(slopes[:, None, None] * (r * np.abs(delta)).astype(np.float32)[None])
    s = jnp.einsum('znqhd,znkhd->znhqk', qb.astype(jnp.float32), kb.astype(jnp.float32)) * (dh ** -0.5)
    s = jnp.where(mask[None, :, None], s + jnp.asarray(bias)[None, None], NEG_INF)
    lse = jax.nn.logsumexp(s, axis=-1)
    pr = jnp.exp(s - lse[..., None])
    o = jnp.einsum('znhqk,znkhd->znqhd', pr, vb.astype(jnp.float32))
    o = o.reshape(bsz * r, Lp, heads, dh)[:, :L]
    lse = lse.transpose(0, 1, 3, 2).reshape(bsz * r, Lp, heads)[:, :L]

    def merge(t):
        t = t.reshape((bsz, r, L) + t.shape[2:])
        return jnp.swapaxes(t, 1, 2).reshape((bsz, seq) + t.shape[3:])

    return merge(o), merge(lse)


def chunk_gated_delta_rule(q, k, v, beta, g):
    bsz, seq, h, dk = q.shape
    dv = v.shape[-1]
    n = seq // CHUNK

    def chunks(t):
        return t.reshape(bsz, n, CHUNK, h, -1).transpose(0, 3, 1, 2, 4)

    q, k, v = chunks(q), chunks(k), chunks(v)
    beta = beta.reshape(bsz, n, CHUNK, h).transpose(0, 3, 1, 2)
    g = jnp.cumsum(g.reshape(bsz, n, CHUNK, h).transpose(0, 3, 1, 2), axis=-1)
    incl = np.tril(np.ones((CHUNK, CHUNK), bool))
    strict = np.tril(np.ones((CHUNK, CHUNK), bool), -1)
    diff = g[..., :, None] - g[..., None, :]
    decay = jnp.where(incl, jnp.exp(jnp.where(incl, diff, 0.0)), 0.0)
    k_beta = k * beta[..., None]
    lower = jnp.where(strict, jnp.einsum('bhnck,bhnsk->bhncs', k_beta, k) * decay, 0.0)
    eye = jnp.eye(CHUNK, dtype=q.dtype)
    t_mat = lax.linalg.triangular_solve(lower + eye, jnp.broadcast_to(eye, lower.shape),
                                        left_side=True, lower=True)
    u = jnp.einsum('bhncs,bhnsv->bhncv', t_mat, v * beta[..., None])
    w = jnp.einsum('bhncs,bhnsk->bhnck', t_mat, k_beta * jnp.exp(g)[..., None])
    qk = jnp.where(incl, jnp.einsum('bhnck,bhnsk->bhncs', q, k) * decay, 0.0)
    q_decay = q * jnp.exp(g)[..., None]
    k_decay = k * jnp.exp(g[..., -1:] - g)[..., None]
    g_last = jnp.exp(g[..., -1])

    def step(state, xs):
        qd, kd, u_c, w_c, qk_c, gl = xs
        v_new = u_c - jnp.einsum('bhck,bhkv->bhcv', w_c, state)
        o = jnp.einsum('bhck,bhkv->bhcv', qd, state) + jnp.einsum('bhcs,bhsv->bhcv', qk_c, v_new)
        state = state * gl[..., None, None] + jnp.einsum('bhck,bhcv->bhkv', kd, v_new)
        return state, o

    xs = tuple(jnp.moveaxis(t, 2, 0) for t in (q_decay, k_decay, u, w, qk, g_last))
    state0 = jnp.zeros((bsz, h, dk, dv), q.dtype)
    _, o = lax.scan(step, state0, xs)
    return o.transpose(1, 0, 3, 2, 4).reshape(bsz, seq, h, dv)


def l2norm(t):
    return t * lax.rsqrt(jnp.sum(t * t, axis=-1, keepdims=True) + 1e-6)


def gated_deltanet(qkv, z, ba, conv_w, a_log, dt_bias, norm_w):
    bsz, seq, c = qkv.shape
    f32 = jnp.float32
    qkv = lax.conv_general_dilated(qkv, conv_w[:, None, :].astype(qkv.dtype), window_strides=(1,),
                                   padding=[(CONV_W // 2, CONV_W // 2)],
                                   dimension_numbers=('NWC', 'WIO', 'NWC'), feature_group_count=c)
    qkv = jax.nn.silu(qkv.astype(f32))
    q, k, v = jnp.split(qkv, [B_QK, 2 * B_QK], axis=-1)
    rep = B_V_HEADS // B_QK_HEADS
    q = jnp.repeat(l2norm(q.reshape(bsz, seq, B_QK_HEADS, B_DK)), rep, axis=2) * (B_DK ** -0.5)
    k = jnp.repeat(l2norm(k.reshape(bsz, seq, B_QK_HEADS, B_DK)), rep, axis=2)
    v = v.reshape(bsz, seq, B_V_HEADS, B_DV)
    b_f, a_f, b_b, a_b = jnp.split(ba.astype(f32), 4, axis=-1)
    a_log = a_log.astype(f32)
    dt_bias = dt_bias.astype(f32)
    g_f = -jnp.exp(a_log[0]) * jax.nn.softplus(a_f + dt_bias[0])
    g_b = -jnp.exp(a_log[1]) * jax.nn.softplus(a_b + dt_bias[1])
    o_f = chunk_gated_delta_rule(q, k, v, jax.nn.sigmoid(b_f), g_f)
    fl = lambda t: jnp.flip(t, axis=1)
    o_b = fl(chunk_gated_delta_rule(fl(q), fl(k), fl(v), fl(jax.nn.sigmoid(b_b)), fl(g_b)))
    o = o_f + o_b
    zg = jax.nn.silu(z.astype(f32).reshape(bsz, seq, B_V_HEADS, B_DV))
    o = o * lax.rsqrt(jnp.mean(o * o, axis=-1, keepdims=True) + RMS_EPS) * norm_w.astype(f32) * zg
    return o.reshape(bsz, seq, B_VZ)


def moe_swiglu(h, router_w, router_b, w_gate_up, b_gate_up, w_down, b_down):
    n_tok, d = h.shape
    f32 = jnp.float32
    logits = (h @ router_w).astype(f32) + router_b.astype(f32)
    top_val, top_idx = lax.top_k(logits, TOP_K)
    gates = jax.nn.softmax(top_val, axis=-1)
    n_assign = n_tok * TOP_K
    flat_e = top_idx.reshape(-1)
    order = jnp.argsort(flat_e)
    sorted_e = flat_e[order]
    sorted_tok = (order // TOP_K).astype(jnp.int32)
    sorted_gate = gates.reshape(-1)[order]
    counts = jnp.bincount(flat_e, length=N_EXPERTS)
    padded = (counts + MOE_BLOCK - 1) // MOE_BLOCK * MOE_BLOCK
    start = jnp.cumsum(counts) - counts
    pad_end = jnp.cumsum(padded)
    pad_start = pad_end - padded
    dest = pad_start[sorted_e] + jnp.arange(n_assign, dtype=jnp.int32) - start[sorted_e]
    n_pad = n_assign + N_EXPERTS * MOE_BLOCK
    n_blocks = n_pad // MOE_BLOCK
    buf_tok = jnp.zeros((n_pad,), jnp.int32).at[dest].set(sorted_tok)
    buf_gate = jnp.zeros((n_pad,), f32).at[dest].set(sorted_gate)
    block_start = jnp.arange(n_blocks, dtype=jnp.int32) * MOE_BLOCK
    block_expert = jnp.minimum(jnp.searchsorted(pad_end, block_start, side='right'), N_EXPERTS - 1)
    xs = h[buf_tok].reshape(n_blocks, MOE_BLOCK, d)

    def expert_block(args):
        xb, e = args
        gu = xb @ w_gate_up[e] + b_gate_up[e]
        gate = jnp.minimum(gu[:, 0::2], SWIGLU_LIMIT)
        up = jnp.clip(gu[:, 1::2], -SWIGLU_LIMIT, SWIGLU_LIMIT)
        act = (up + 1.0) * gate * jax.nn.sigmoid(gate * SWIGLU_ALPHA)
        return act @ w_down[e] + b_down[e]

    ys = lax.map(expert_block, (xs, block_expert)).reshape(n_pad, d)
    return jax.ops.segment_sum(ys.astype(f32) * buf_gate[:, None], buf_tok, num_segments=n_tok)


def setup_inputs(seed: int = 0) -> dict:
    key = jax.random.key(seed)
    ks = jax.random.split(key, 26)
    f32 = jnp.float32
    nrm = lambda k, shape, scale: jax.random.normal(k, shape, f32) * scale
    beta = DEEPNORM_BETA
    x = nrm(ks[0], (BATCH, SEQ, D_MODEL), 1.0)
    p = nrm(ks[1], (DEPTH, BATCH, SEQ, PLE_DIM), 1.0)
    col_scale = np.ones((N_IN_COLS,), np.float32)
    col_scale[2 * A_QKV:3 * A_QKV] = beta
    vb0 = 3 * A_QKV + 2 * B_QK
    col_scale[vb0:vb0 + B_VZ] = beta
    w_in = nrm(ks[2], (DEPTH, D_MODEL, N_IN_COLS), D_MODEL ** -0.5) * jnp.asarray(col_scale)
    b_gate = nrm(ks[3], (DEPTH, 2 * D_MODEL), 0.02)
    conv_w = nrm(ks[4], (DEPTH, CONV_W, 2 * B_QK + B_VZ), CONV_W ** -0.5)
    a_log = jnp.log(jax.random.uniform(ks[5], (DEPTH, 2, B_V_HEADS), f32, 1.0, 16.0))
    dt = jnp.exp(jax.random.uniform(ks[6], (DEPTH, 2, B_V_HEADS), f32, math.log(1e-3), math.log(1e-1)))
    dt_bias = dt + jnp.log(-jnp.expm1(-dt))
    dn_norm_w = 1.0 + nrm(ks[7], (DEPTH, B_DV), 0.02)
    w_branch_a = nrm(ks[8], (DEPTH, A_HEADS * HEAD_DIM, D_MODEL), (A_HEADS * HEAD_DIM) ** -0.5 * beta)
    w_branch_b = nrm(ks[9], (DEPTH, B_VZ, D_MODEL), B_VZ ** -0.5 * beta)
    w_out = nrm(ks[10], (DEPTH, D_MODEL, D_MODEL), D_MODEL ** -0.5 * beta)
    ln1_g = 1.0 + nrm(ks[11], (DEPTH, D_MODEL), 0.02)
    ln1_b = nrm(ks[12], (DEPTH, D_MODEL), 0.02)
    router_w = nrm(ks[13], (DEPTH, D_MODEL, N_EXPERTS), D_MODEL ** -0.5)
    router_b = nrm(ks[14], (DEPTH, N_EXPERTS), 0.01)
    w_gate_up = nrm(ks[15], (DEPTH, N_EXPERTS, D_MODEL, 2 * D_FF), D_MODEL ** -0.5)
    b_gate_up = nrm(ks[16], (DEPTH, N_EXPERTS, 2 * D_FF), 0.01)
    w_down = nrm(ks[17], (DEPTH, N_EXPERTS, D_FF, D_MODEL), D_FF ** -0.5 * beta)
    b_down = nrm(ks[18], (DEPTH, N_EXPERTS, D_MODEL), 0.01)
    w_ple = nrm(ks[19], (DEPTH, PLE_DIM, D_MODEL), PLE_DIM ** -0.5 * beta)
    w_ple_gate = nrm(ks[20], (DEPTH, D_MODEL, D_MODEL), D_MODEL ** -0.5)
    b_ple_gate = nrm(ks[21], (DEPTH, D_MODEL), 0.02)
    ln2_g = 1.0 + nrm(ks[22], (DEPTH, D_MODEL), 0.02)
    ln2_b = nrm(ks[23], (DEPTH, D_MODEL), 0.02)
    return {'x': x, 'p': p, 'w_in': w_in, 'b_gate': b_gate, 'conv_w': conv_w, 'a_log': a_log,
            'dt_bias': dt_bias, 'dn_norm_w': dn_norm_w, 'w_branch_a': w_branch_a,
            'w_branch_b': w_branch_b, 'w_out': w_out, 'ln1_g': ln1_g, 'ln1_b': ln1_b,
            'router_w': router_w, 'router_b': router_b, 'w_gate_up': w_gate_up,
            'b_gate_up': b_gate_up, 'w_down': w_down, 'b_down': b_down, 'w_ple': w_ple,
            'w_ple_gate': w_ple_gate, 'b_ple_gate': b_ple_gate, 'ln2_g': ln2_g, 'ln2_b': ln2_b}


def reference(x, p, w_in, b_gate, conv_w, a_log, dt_bias, dn_norm_w, w_branch_a, w_branch_b,
              w_out, ln1_g, ln1_b, router_w, router_b, w_gate_up, b_gate_up, w_down, b_down,
              w_ple, w_ple_gate, b_ple_gate, ln2_g, ln2_b):
    bsz, seq, _ = x.shape
    dt = x.dtype
    slopes = alibi_slopes()
    split_idx = np.cumsum(SPLITS)[:-1].tolist()
    for i in range(DEPTH):
        u = x @ w_in[i]
        qa, ka, va, qb, kb, vb, zb, bab, gpre = jnp.split(u, split_idx, axis=-1)
        hs = (bsz, seq, N_GROUPS, A_HEADS, HEAD_DIM)
        qa, ka, va = qa.reshape(hs), ka.reshape(hs), va.reshape(hs)
        outs, lses = [], []
        for gi, (win, dil) in enumerate(DILATION_GROUPS):
            o_g, l_g = dilated_window_attention(qa[:, :, gi], ka[:, :, gi], va[:, :, gi],
                                                dil, win // (2 * dil), slopes[gi])
            outs.append(o_g)
            lses.append(l_g)
        wts = jax.nn.softmax(jnp.stack(lses, axis=0), axis=0)
        o_a = jnp.einsum('gbsh,gbshd->bshd', wts, jnp.stack(outs, axis=0))
        o_a = o_a.reshape(bsz, seq, A_HEADS * HEAD_DIM).astype(dt)
        o_b = gated_deltanet(jnp.concatenate([qb, kb, vb], axis=-1), zb, bab, conv_w[i],
                             a_log[i], dt_bias[i], dn_norm_w[i]).astype(dt)
        gates = jax.nn.sigmoid(gpre.astype(jnp.float32) + b_gate[i].astype(jnp.float32))
        g_a, g_b = jnp.split(gates, 2, axis=-1)
        merged = g_a * (o_a @ w_branch_a[i]) + g_b * (o_b @ w_branch_b[i])
        mix = merged.astype(dt) @ w_out[i]
        x = layer_norm(DEEPNORM_ALPHA * x + mix, ln1_g[i], ln1_b[i]).astype(dt)
        y = moe_swiglu(x.reshape(bsz * seq, D_MODEL), router_w[i], router_b[i], w_gate_up[i],
                       b_gate_up[i], w_down[i], b_down[i]).reshape(bsz, seq, D_MODEL)
        ple = jax.nn.sigmoid((x @ w_ple_gate[i]).astype(jnp.float32) + b_ple_gate[i].astype(jnp.float32)) \
            * (p[i] @ w_ple[i]).astype(jnp.float32)
        x = layer_norm(DEEPNORM_ALPHA * x + y + ple, ln2_g[i], ln2_b[i]).astype(dt)
    return x
```

```python
import functools

import numpy as np
import jax
import jax.numpy as jnp
from jax import lax
from jax.experimental import pallas as pl
from jax.experimental.pallas import tpu as pltpu

F32 = jnp.float32
BF16 = jnp.bfloat16

D_MODEL = 2048
HEAD_DIM = 128
A_HEADS = 8
DILATION_GROUPS = ((128, 1), (512, 4), (2048, 16))
N_GROUPS = 3
NEG_INF = -1e30
B_QK_HEADS = 8
B_V_HEADS = 16
B_DK = 128
B_DV = 128
CONV_W = 5
RMS_EPS = 1e-6
N_EXPERTS = 32
TOP_K = 4
SWIGLU_ALPHA = 1.702
SWIGLU_LIMIT = 7.0
PLE_DIM = 256
DEPTH = 1
DEEPNORM_ALPHA = (2 * DEPTH) ** 0.25
LN_EPS = 1e-5
A_QKV = N_GROUPS * A_HEADS * HEAD_DIM
B_QK = B_QK_HEADS * B_DK
B_VZ = B_V_HEADS * B_DV
B_GATES = 4 * B_V_HEADS

LANES = 128
N_SIDE = 64
Q_SUB = 128
DN_CHUNK = 64
TRI_BASE = 16
MOE_BLOCK = 512
MOE_TF = 512
VMEM_LIMIT = 56 * 1024 * 1024


def _cparams(sem):
    return pltpu.CompilerParams(dimension_semantics=sem, vmem_limit_bytes=VMEM_LIMIT)


def _sigmoid(x):
    return 1.0 / (1.0 + jnp.exp(-x))


def _dot(a, b):
    return jnp.dot(a, b, preferred_element_type=F32)


def _dot_nt(a, b):
    return lax.dot_general(a, b, (((1,), (1,)), ((), ())), preferred_element_type=F32)


def _dot_tn(a, b):
    return lax.dot_general(a, b, (((0,), (0,)), ((), ())), preferred_element_type=F32)


def _mm_kernel(x_ref, w_ref, o_ref):
    o_ref[...] = _dot(x_ref[...], w_ref[...]).astype(o_ref.dtype)


def _matmul(x, w, out_dtype, tm, tn):
    m, k = x.shape
    n = w.shape[1]
    return pl.pallas_call(
        _mm_kernel,
        grid=(m // tm, n // tn),
        in_specs=[pl.BlockSpec((tm, k), lambda i, j: (i, 0)),
                  pl.BlockSpec((k, tn), lambda i, j: (0, j))],
        out_specs=pl.BlockSpec((tm, tn), lambda i, j: (i, j)),
        out_shape=jax.ShapeDtypeStruct((m, n), out_dtype),
        compiler_params=_cparams(("parallel", "parallel")),
        name="in_proj",
    )(x, w)


def _attn_kernel(q_ref, kp_ref, kc_ref, kn_ref, vp_ref, vc_ref, vn_ref, o_ref, lse_ref, kbuf, vbuf,
                 *, dil, sub_len, tl, slopes):
    i0 = pl.program_id(2) * tl
    kbuf[0:N_SIDE, :] = kp_ref[0]
    kbuf[N_SIDE:N_SIDE + tl, :] = kc_ref[0]
    kbuf[N_SIDE + tl:, :] = kn_ref[0]
    vbuf[0:N_SIDE, :] = vp_ref[0]
    vbuf[N_SIDE:N_SIDE + tl, :] = vc_ref[0]
    vbuf[N_SIDE + tl:, :] = vn_ref[0]
    span = Q_SUB + 2 * N_SIDE
    qq = lax.broadcasted_iota(jnp.int32, (Q_SUB, span), 0)
    kk = lax.broadcasted_iota(jnp.int32, (Q_SUB, span), 1)
    delta = kk - N_SIDE - qq
    absd = jnp.abs(delta)
    band = absd <= N_SIDE
    dist = (dil * absd).astype(F32)
    lane = lax.broadcasted_iota(jnp.int32, (Q_SUB, LANES), 1)
    scale = HEAD_DIM ** -0.5
    for j in range(tl // Q_SUB):
        pos = i0 + (j * Q_SUB - N_SIDE) + kk
        valid = band & (pos >= 0) & (pos < sub_len)
        lse_tile = jnp.zeros((Q_SUB, LANES), F32)
        for h in range(A_HEADS):
            hs = slice(h * HEAD_DIM, (h + 1) * HEAD_DIM)
            q = q_ref[0, j * Q_SUB:(j + 1) * Q_SUB, hs]
            k = kbuf[j * Q_SUB:j * Q_SUB + span, hs]
            v = vbuf[j * Q_SUB:j * Q_SUB + span, hs]
            s = _dot_nt(q, k) * scale
            s = jnp.where(valid, s - float(slopes[h]) * dist, NEG_INF)
            m = jnp.max(s, axis=1, keepdims=True)
            p = jnp.exp(s - m)
            l = jnp.sum(p, axis=1, keepdims=True)
            o = _dot(p.astype(BF16), v) / l
            o_ref[0, j * Q_SUB:(j + 1) * Q_SUB, hs] = o.astype(o_ref.dtype)
            lse_tile = jnp.where(lane == h, m + jnp.log(l), lse_tile)
        lse_ref[0, j * Q_SUB:(j + 1) * Q_SUB, :] = lse_tile


def _attention_group(u_a, gi, dil, slopes):
    bsz, seq, _ = u_a.shape
    sub_len = seq // dil
    tl = min(512, sub_len)
    assert sub_len % tl == 0 and tl % Q_SUB == 0 and sub_len % N_SIDE == 0
    width = A_HEADS * HEAD_DIM
    cols = 3 * A_QKV // width
    view = u_a.reshape(bsz, sub_len, dil * 3 * A_QKV)
    halo_per_tile = tl // N_SIDE
    n_halo = sub_len // N_SIDE
    qcol = lambda c: c * cols + gi
    kcol = lambda c: c * cols + N_GROUPS + gi
    vcol = lambda c: c * cols + 2 * N_GROUPS + gi
    prev = lambda li: jnp.maximum(li * halo_per_tile - 1, 0)
    nxt = lambda li: jnp.minimum((li + 1) * halo_per_tile, n_halo - 1)
    halo = lambda colf, rowf: pl.BlockSpec((1, N_SIDE, width), lambda b, c, li: (b, rowf(li), colf(c)))
    cur = lambda colf: pl.BlockSpec((1, tl, width), lambda b, c, li: (b, li, colf(c)))
    kern = functools.partial(_attn_kernel, dil=dil, sub_len=sub_len, tl=tl, slopes=tuple(float(s) for s in slopes))
    o, lse = pl.pallas_call(
        kern,
        grid=(bsz, dil, sub_len // tl),
        in_specs=[cur(qcol), halo(kcol, prev), cur(kcol), halo(kcol, nxt),
                  halo(vcol, prev), cur(vcol), halo(vcol, nxt)],
        out_specs=[pl.BlockSpec((1, tl, width), lambda b, c, li: (b, li, c)),
                   pl.BlockSpec((1, tl, LANES), lambda b, c, li: (b, li, c))],
        out_shape=[jax.ShapeDtypeStruct((bsz, sub_len, dil * width), BF16),
                   jax.ShapeDtypeStruct((bsz, sub_len, dil * LANES), F32)],
        scratch_shapes=[pltpu.VMEM((tl + 2 * N_SIDE, width), BF16),
                        pltpu.VMEM((tl + 2 * N_SIDE, width), BF16)],
        compiler_params=_cparams(("parallel", "parallel", "parallel")),
        name=f"dilated_attn_g{gi}",
    )(view, view, view, view, view, view, view)
    return o.reshape(bsz * seq, width), lse.reshape(bsz * seq, LANES)


def _alibi_slopes():
    n = N_GROUPS * A_HEADS
    s = 2.0 ** (-8.0 * np.arange(1, n + 1) / n)
    return s.astype(np.float32).reshape(N_GROUPS, A_HEADS)


def _dn_prep_kernel(prev_ref, cur_ref, next_ref, cw_ref, bab_ref, gp_ref, q_ref, k_ref, v_ref, g_ref, *, ts):
    ti = pl.program_id(1)
    nt = pl.num_programs(1)
    halo = CONV_W // 2
    keep_prev = (ti > 0).astype(F32)
    keep_next = (ti < nt - 1).astype(F32)
    for c in range((2 * B_QK + B_VZ) // LANES):
        cs = slice(c * LANES, (c + 1) * LANES)
        xp = prev_ref[0, :, cs].astype(F32)[8:16] * keep_prev
        xc = cur_ref[0, :, cs].astype(F32)
        xn = next_ref[0, :, cs].astype(F32)[0:8] * keep_next
        ext = jnp.concatenate([xp, xc, xn], axis=0)
        acc = jnp.zeros((ts, LANES), F32)
        for j in range(CONV_W):
            off = 8 - halo + j
            acc = acc + ext[off:off + ts, :] * cw_ref[j:j + 1, cs]
        y = acc * _sigmoid(acc)
        if c < 2 * B_QK // LANES:
            y = y * lax.rsqrt(jnp.sum(y * y, axis=1, keepdims=True) + 1e-6)
        if c < B_QK // LANES:
            q_ref[0, :, cs] = y * (B_DK ** -0.5)
        elif c < 2 * B_QK // LANES:
            k_ref[0, :, c * LANES - B_QK:(c + 1) * LANES - B_QK] = y
        else:
            v_ref[0, :, c * LANES - 2 * B_QK:(c + 1) * LANES - 2 * B_QK] = y
    x = bab_ref[0]
    neg_a = gp_ref[0:1, :]
    dtb = gp_ref[1:2, :]
    is_g = gp_ref[2:3, :] > 0.5
    z = x + dtb
    softplus = jnp.maximum(z, 0.0) + jnp.log(1.0 + jnp.exp(-jnp.abs(z)))
    g_ref[0] = jnp.where(is_g, neg_a * softplus, _sigmoid(x))


def _dn_prep(u_b, conv_w, bab, gate_params):
    bsz, seq, _ = u_b.shape
    ts = min(256, seq)
    cq = 2 * B_QK + B_VZ
    nhalo = seq // 16
    per = ts // 16
    kern = functools.partial(_dn_prep_kernel, ts=ts)
    return pl.pallas_call(
        kern,
        grid=(bsz, seq // ts),
        in_specs=[pl.BlockSpec((1, 16, cq), lambda b, t: (b, jnp.maximum(t * per - 1, 0), 0)),
                  pl.BlockSpec((1, ts, cq), lambda b, t: (b, t, 0)),
                  pl.BlockSpec((1, 16, cq), lambda b, t: (b, jnp.minimum((t + 1) * per, nhalo - 1), 0)),
                  pl.BlockSpec((8, cq), lambda b, t: (0, 0)),
                  pl.BlockSpec((1, ts, LANES), lambda b, t: (b, t, 0)),
                  pl.BlockSpec((8, LANES), lambda b, t: (0, 0))],
        out_specs=[pl.BlockSpec((1, ts, B_QK), lambda b, t: (b, t, 0)),
                   pl.BlockSpec((1, ts, B_QK), lambda b, t: (b, t, 0)),
                   pl.BlockSpec((1, ts, B_VZ), lambda b, t: (b, t, 0)),
                   pl.BlockSpec((1, ts, LANES), lambda b, t: (b, t, 0))],
        out_shape=[jax.ShapeDtypeStruct((bsz, seq, B_QK), F32),
                   jax.ShapeDtypeStruct((bsz, seq, B_QK), F32),
                   jax.ShapeDtypeStruct((bsz, seq, B_VZ), F32),
                   jax.ShapeDtypeStruct((bsz, seq, LANES), F32)],
        compiler_params=_cparams(("parallel", "parallel")),
        name="dn_prep",
    )(u_b, u_b, u_b, conv_w, bab, gate_params)


def _tri_inverse(lm, ii, jj, eye, c):
    bf = lambda t: t.astype(BF16)
    d = jnp.where(ii // TRI_BASE == jj // TRI_BASE, lm, 0.0)
    d2 = _dot(bf(d), bf(d))
    d4 = _dot(bf(d2), bf(d2))
    d8 = _dot(bf(d4), bf(d4))
    p = eye - d
    p = p + _dot(bf(p), bf(d2))
    p = p + _dot(bf(p), bf(d4))
    p = p + _dot(bf(p), bf(d8))
    size = TRI_BASE
    while size < c:
        off = (ii // (2 * size) == jj // (2 * size)) & (ii // size != jj // size)
        cm = jnp.where(off, lm, 0.0)
        pc = _dot(bf(p), bf(cm))
        p = p - _dot(bf(pc), bf(p))
        size *= 2
    return p


def _dn_scan_kernel(q_ref, k_ref, v_ref, g_ref, gt_ref, o_ref, s_ref, *, c):
    d = pl.program_id(1)
    n = pl.program_id(2)

    @pl.when(n == 0)
    def _():
        s_ref[...] = jnp.zeros_like(s_ref)

    ii = lax.broadcasted_iota(jnp.int32, (c, c), 0)
    jj = lax.broadcasted_iota(jnp.int32, (c, c), 1)
    lo = jnp.where(d == 0, ii - jj, jj - ii)
    incl = lo >= 0
    strict = lo > 0
    eye = (ii == jj).astype(F32)
    tri = incl.astype(F32)
    tri_t = (lo <= 0).astype(F32)
    g_all = g_ref[0, 0]
    gt_all = gt_ref[0, 0, 0]
    hp = lax.Precision.HIGHEST
    gc_all = jnp.dot(tri, g_all, precision=hp, preferred_element_type=F32)
    gct_all = jnp.dot(gt_all, tri_t, precision=hp, preferred_element_type=F32)
    tot_all = jnp.sum(g_all, axis=0, keepdims=True)
    rep = B_V_HEADS // B_QK_HEADS
    for hq in range(B_QK_HEADS):
        qs = slice(hq * B_DK, (hq + 1) * B_DK)
        q = q_ref[0, :, qs]
        k = k_ref[0, :, qs]
        kb16 = k.astype(BF16)
        gram = _dot_nt(kb16, kb16)
        qk_raw = _dot_nt(q.astype(BF16), kb16)
        for e in range(rep):
            h = hq * rep + e
            vs = slice(h * B_DV, (h + 1) * B_DV)
            v = v_ref[0, :, vs]
            beta = g_all[:, h:h + 1]
            gc = gc_all[:, B_V_HEADS + h:B_V_HEADS + h + 1]
            gcr = gct_all[B_V_HEADS + h:B_V_HEADS + h + 1, :]
            tot = tot_all[:, B_V_HEADS + h:B_V_HEADS + h + 1]
            dec = jnp.where(incl, jnp.exp(jnp.where(incl, gc - gcr, 0.0)), 0.0)
            lm = jnp.where(strict, beta * gram * dec, 0.0)
            t_mat = _tri_inverse(lm, ii, jj, eye, c)
            egc = jnp.exp(gc)
            rhs = jnp.concatenate([v * beta, k * (beta * egc)], axis=1)
            uw = _dot(t_mat.astype(BF16), rhs.astype(BF16))
            u = uw[:, :B_DV]
            w = uw[:, B_DV:]
            state = s_ref[h]
            wq = jnp.concatenate([w, q * egc], axis=0)
            ws_qs = _dot(wq.astype(BF16), state.astype(BF16))
            v_new = u - ws_qs[:c]
            qk = jnp.where(incl, qk_raw * dec, 0.0)
            o = ws_qs[c:] + _dot(qk.astype(BF16), v_new.astype(BF16))
            kd = k * jnp.exp(tot - gc)
            s_ref[h] = state * jnp.exp(tot) + _dot_tn(kd.astype(BF16), v_new.astype(BF16))
            o_ref[0, 0, :, vs] = o


def _dn_scan(qn, kn, vn, gates_dir, gates_dir_t):
    bsz, seq, _ = qn.shape
    c = DN_CHUNK
    n = seq // c
    order = lambda d, i: jnp.where(d == 0, i, n - 1 - i)
    kern = functools.partial(_dn_scan_kernel, c=c)
    return pl.pallas_call(
        kern,
        grid=(bsz, 2, n),
        in_specs=[pl.BlockSpec((1, c, B_QK), lambda b, d, i: (b, order(d, i), 0)),
                  pl.BlockSpec((1, c, B_QK), lambda b, d, i: (b, order(d, i), 0)),
                  pl.BlockSpec((1, c, B_VZ), lambda b, d, i: (b, order(d, i), 0)),
                  pl.BlockSpec((1, 1, c, 2 * B_V_HEADS), lambda b, d, i: (d, b, order(d, i), 0)),
                  pl.BlockSpec((1, 1, 1, 2 * B_V_HEADS, c), lambda b, d, i: (d, b, order(d, i), 0, 0))],
        out_specs=pl.BlockSpec((1, 1, c, B_VZ), lambda b, d, i: (d, b, order(d, i), 0)),
        out_shape=jax.ShapeDtypeStruct((2, bsz, seq, B_VZ), F32),
        scratch_shapes=[pltpu.VMEM((B_V_HEADS, B_DK, B_DV), F32)],
        compiler_params=_cparams(("parallel", "parallel", "arbitrary")),
        name="dn_scan",
    )(qn, kn, vn, gates_dir, gates_dir_t)


def _branch_a_kernel(o0_ref, o1_ref, o2_ref, l0_ref, l1_ref, l2_ref, w_ref, gp_ref, bg_ref, out_ref):
    ls = [l0_ref[...], l1_ref[...], l2_ref[...]]
    m = jnp.maximum(jnp.maximum(ls[0], ls[1]), ls[2])
    es = [jnp.exp(l - m) for l in ls]
    den = es[0] + es[1] + es[2]
    ws = [e / den for e in es]
    os_ = [o0_ref, o1_ref, o2_ref]
    parts = []
    for h in range(A_HEADS):
        hs = slice(h * HEAD_DIM, (h + 1) * HEAD_DIM)
        acc = ws[0][:, h:h + 1] * os_[0][:, hs].astype(F32)
        acc = acc + ws[1][:, h:h + 1] * os_[1][:, hs].astype(F32)
        acc = acc + ws[2][:, h:h + 1] * os_[2][:, hs].astype(F32)
        parts.append(acc.astype(BF16))
    oa = jnp.concatenate(parts, axis=1)
    y = _dot(oa, w_ref[...])
    out_ref[...] = _sigmoid(gp_ref[...] + bg_ref[...]) * y


def _branch_a(outs, lses, w_a, gpre, b_gate, tm):
    t = outs[0].shape[0]
    wd = A_HEADS * HEAD_DIM
    row = lambda width: pl.BlockSpec((tm, width), lambda i: (i, 0))
    return pl.pallas_call(
        _branch_a_kernel,
        grid=(t // tm,),
        in_specs=[row(wd), row(wd), row(wd), row(LANES), row(LANES), row(LANES),
                  pl.BlockSpec((wd, D_MODEL), lambda i: (0, 0)),
                  pl.BlockSpec((tm, D_MODEL), lambda i: (i, 0)),
                  pl.BlockSpec((1, D_MODEL), lambda i: (0, 0))],
        out_specs=row(D_MODEL),
        out_shape=jax.ShapeDtypeStruct((t, D_MODEL), F32),
        compiler_params=_cparams(("parallel",)),
        name="branch_a",
    )(*outs, *lses, w_a, gpre, b_gate)


def _branch_b_kernel(of_ref, ob_ref, z_ref, nw_ref, w_ref, gp_ref, bg_ref, a_ref, out_ref):
    nw = nw_ref[...]
    parts = []
    for h in range(B_V_HEADS):
        hs = slice(h * B_DV, (h + 1) * B_DV)
        o = of_ref[0, :, hs] + ob_ref[0, :, hs]
        z = z_ref[:, hs].astype(F32)
        o = o * lax.rsqrt(jnp.mean(o * o, axis=1, keepdims=True) + RMS_EPS) * nw * (z * _sigmoid(z))
        parts.append(o.astype(BF16))
    ob = jnp.concatenate(parts, axis=1)
    y = _dot(ob, w_ref[...])
    out_ref[...] = (a_ref[...] + _sigmoid(gp_ref[...] + bg_ref[...]) * y).astype(out_ref.dtype)


def _branch_b(o_dirs, u_b2d, norm_w, w_b, gpre, b_gate, a_part, tm):
    t = a_part.shape[0]
    return pl.pallas_call(
        _branch_b_kernel,
        grid=(t // tm,),
        in_specs=[pl.BlockSpec((1, tm, B_VZ), lambda i: (0, i, 0)),
                  pl.BlockSpec((1, tm, B_VZ), lambda i: (1, i, 0)),
                  pl.BlockSpec((tm, B_VZ), lambda i: (i, (2 * B_QK + B_VZ) // B_VZ)),
                  pl.BlockSpec((1, B_DV), lambda i: (0, 0)),
                  pl.BlockSpec((B_VZ, D_MODEL), lambda i: (0, 0)),
                  pl.BlockSpec((tm, D_MODEL), lambda i: (i, 1)),
                  pl.BlockSpec((1, D_MODEL), lambda i: (0, 1)),
                  pl.BlockSpec((tm, D_MODEL), lambda i: (i, 0))],
        out_specs=pl.BlockSpec((tm, D_MODEL), lambda i: (i, 0)),
        out_shape=jax.ShapeDtypeStruct((t, D_MODEL), BF16),
        compiler_params=_cparams(("parallel",)),
        name="branch_b",
    )(o_dirs, o_dirs, u_b2d, norm_w, w_b, gpre, b_gate, a_part)


def _layer_norm(y, g, b):
    mu = jnp.mean(y, axis=1, keepdims=True)
    yc = y - mu
    var = jnp.mean(yc * yc, axis=1, keepdims=True)
    return yc * lax.rsqrt(var + LN_EPS) * g + b


def _out_ln_kernel(m_ref, w_ref, x_ref, g_ref, b_ref, out_ref):
    mix = _dot(m_ref[...], w_ref[...])
    out_ref[...] = _layer_norm(DEEPNORM_ALPHA * x_ref[...] + mix, g_ref[...], b_ref[...])


def _out_ln(merged, w_out, x2d, ln_g, ln_b, tm):
    t = merged.shape[0]
    vec = pl.BlockSpec((1, D_MODEL), lambda i: (0, 0))
    return pl.pallas_call(
        _out_ln_kernel,
        grid=(t // tm,),
        in_specs=[pl.BlockSpec((tm, D_MODEL), lambda i: (i, 0)),
                  pl.BlockSpec((D_MODEL, D_MODEL), lambda i: (0, 0)),
                  pl.BlockSpec((tm, D_MODEL), lambda i: (i, 0)), vec, vec],
        out_specs=pl.BlockSpec((tm, D_MODEL), lambda i: (i, 0)),
        out_shape=jax.ShapeDtypeStruct((t, D_MODEL), F32),
        compiler_params=_cparams(("parallel",)),
        name="out_proj_ln1",
    )(merged, w_out, x2d, ln_g, ln_b)


def _router_kernel(x_ref, rw_ref, rb_ref, idx_ref, gate_ref, rank_ref, cnt_ref, carry, *, tm):
    @pl.when(pl.program_id(0) == 0)
    def _():
        carry[...] = jnp.zeros_like(carry)

    lane = lax.broadcasted_iota(jnp.int32, (tm, LANES), 1)
    lane_f = lane.astype(F32)
    logits = jnp.dot(x_ref[...], rw_ref[...], precision=lax.Precision.HIGHEST, preferred_element_type=F32) + rb_ref[...]
    cur = jnp.where(lane < N_EXPERTS, logits, -jnp.inf)
    vals, idxs = [], []
    for _k in range(TOP_K):
        m = jnp.max(cur, axis=1, keepdims=True)
        idx = jnp.min(jnp.where(cur == m, lane_f, float(LANES)), axis=1, keepdims=True).astype(jnp.int32)
        vals.append(m)
        idxs.append(idx)
        cur = jnp.where(lane == idx, -jnp.inf, cur)
    es = [jnp.exp(v - vals[0]) for v in vals]
    den = es[0] + es[1] + es[2] + es[3]
    onehot = jnp.zeros((tm, LANES), F32)
    for idx in idxs:
        onehot = onehot + (lane == idx).astype(F32)
    ri = lax.broadcasted_iota(jnp.int32, (tm, tm), 0)
    ci = lax.broadcasted_iota(jnp.int32, (tm, tm), 1)
    before = (ci < ri).astype(BF16)
    prefix = _dot(before, onehot.astype(BF16)) + carry[0:1, :]
    idx_out = jnp.zeros((tm, LANES), jnp.int32)
    gate_out = jnp.zeros((tm, LANES), F32)
    rank_out = jnp.zeros((tm, LANES), jnp.int32)
    for k in range(TOP_K):
        rk = jnp.sum(jnp.where(lane == idxs[k], prefix, 0.0), axis=1, keepdims=True)
        idx_out = jnp.where(lane == k, idxs[k], idx_out)
        gate_out = jnp.where(lane == k, es[k] / den, gate_out)
        rank_out = jnp.where(lane == k, rk.astype(jnp.int32), rank_out)
    idx_ref[...] = idx_out
    gate_ref[...] = gate_out
    rank_ref[...] = rank_out
    total = carry[0:1, :] + jnp.sum(onehot, axis=0, keepdims=True)
    carry[...] = jnp.broadcast_to(total, carry.shape)
    cnt_ref[...] = jnp.broadcast_to(total, cnt_ref.shape)


def _router(x1, rw, rb, tm):
    t = x1.shape[0]
    row = pl.BlockSpec((tm, LANES), lambda i: (i, 0))
    kern = functools.partial(_router_kernel, tm=tm)
    return pl.pallas_call(
        kern,
        grid=(t // tm,),
        in_specs=[pl.BlockSpec((tm, D_MODEL), lambda i: (i, 0)),
                  pl.BlockSpec((D_MODEL, LANES), lambda i: (0, 0)),
                  pl.BlockSpec((1, LANES), lambda i: (0, 0))],
        out_specs=[row, row, row, pl.BlockSpec((8, LANES), lambda i: (0, 0))],
        out_shape=[jax.ShapeDtypeStruct((t, LANES), jnp.int32),
                   jax.ShapeDtypeStruct((t, LANES), F32),
                   jax.ShapeDtypeStruct((t, LANES), jnp.int32),
                   jax.ShapeDtypeStruct((8, LANES), F32)],
        scratch_shapes=[pltpu.VMEM((8, LANES), F32)],
        compiler_params=_cparams(("arbitrary",)),
        name="router",
    )(x1, rw, rb)


def _row_copy(src_hbm, row, dst, slot, sem):
    return pltpu.make_async_copy(src_hbm.at[pl.ds(row, 1)], dst.at[pl.ds(slot, 1)], sem)


def _gather_kernel(nused_ref, tok_ref, x_hbm, o_ref, xbuf, sem):
    i = pl.program_id(0)

    @pl.when(i < nused_ref[0])
    def _():
        def issue(r, carry):
            _row_copy(x_hbm, tok_ref[0, 0, r], xbuf, r, sem).start()
            return carry

        lax.fori_loop(0, MOE_BLOCK, issue, 0)

        def wait(r, carry):
            _row_copy(x_hbm, 0, xbuf, r, sem).wait()
            return carry

        lax.fori_loop(0, MOE_BLOCK, wait, 0)
        o_ref[...] = xbuf[...].astype(BF16)

    @pl.when(i >= nused_ref[0])
    def _():
        o_ref[...] = jnp.zeros_like(o_ref)


def _gather_rows(x1, buf_tok, n_used):
    n_pad = buf_tok.shape[0]
    nb = n_pad // MOE_BLOCK
    grid_spec = pltpu.PrefetchScalarGridSpec(
        num_scalar_prefetch=1,
        grid=(nb,),
        in_specs=[pl.BlockSpec((1, 1, MOE_BLOCK), lambda i, nu: (i, 0, 0), memory_space=pltpu.SMEM),
                  pl.BlockSpec(memory_space=pl.ANY)],
        out_specs=pl.BlockSpec((MOE_BLOCK, D_MODEL), lambda i, nu: (i, 0)),
        scratch_shapes=[pltpu.VMEM((MOE_BLOCK, D_MODEL), F32), pltpu.SemaphoreType.DMA(())],
    )
    return pl.pallas_call(
        _gather_kernel,
        grid_spec=grid_spec,
        out_shape=jax.ShapeDtypeStruct((n_pad, D_MODEL), BF16),
        compiler_params=_cparams(("arbitrary",)),
        name="moe_gather",
    )(n_used, buf_tok.reshape(nb, 1, MOE_BLOCK), x1)


def _ffn_kernel(be_ref, nused_ref, x_ref, wg_ref, wu_ref, bg_ref, bu_ref, wd_ref, bd_ref, o_ref, acc):
    i = pl.program_id(0)
    f = pl.program_id(1)
    nf = pl.num_programs(1)

    @pl.when(f == 0)
    def _():
        acc[...] = jnp.broadcast_to(bd_ref[0], acc.shape)

    @pl.when(i < nused_ref[0])
    def _():
        x = x_ref[...]
        g = _dot(x, wg_ref[0]) + bg_ref[0]
        u = _dot(x, wu_ref[0]) + bu_ref[0]
        gate = jnp.minimum(g, SWIGLU_LIMIT)
        up = jnp.clip(u, -SWIGLU_LIMIT, SWIGLU_LIMIT)
        act = (up + 1.0) * gate * _sigmoid(gate * SWIGLU_ALPHA)
        acc[...] += _dot(act.astype(BF16), wd_ref[0])

    @pl.when(f == nf - 1)
    def _():
        o_ref[...] = acc[...]


def _expert_ffn(xs, block_expert, n_used, wg, wu, bg, bu, wd, bd):
    n_pad = xs.shape[0]
    nb = n_pad // MOE_BLOCK
    d_ff = wg.shape[2]
    nf = d_ff // MOE_TF
    grid_spec = pltpu.PrefetchScalarGridSpec(
        num_scalar_prefetch=2,
        grid=(nb, nf),
        in_specs=[pl.BlockSpec((MOE_BLOCK, D_MODEL), lambda i, f, be, nu: (i, 0)),
                  pl.BlockSpec((1, D_MODEL, MOE_TF), lambda i, f, be, nu: (be[i], 0, f)),
                  pl.BlockSpec((1, D_MODEL, MOE_TF), lambda i, f, be, nu: (be[i], 0, f)),
                  pl.BlockSpec((1, 1, MOE_TF), lambda i, f, be, nu: (be[i], 0, f)),
                  pl.BlockSpec((1, 1, MOE_TF), lambda i, f, be, nu: (be[i], 0, f)),
                  pl.BlockSpec((1, MOE_TF, D_MODEL), lambda i, f, be, nu: (be[i], f, 0)),
                  pl.BlockSpec((1, 1, D_MODEL), lambda i, f, be, nu: (be[i], 0, 0))],
        out_specs=pl.BlockSpec((MOE_BLOCK, D_MODEL), lambda i, f, be, nu: (i, 0)),
        scratch_shapes=[pltpu.VMEM((MOE_BLOCK, D_MODEL), F32)],
    )
    return pl.pallas_call(
        _ffn_kernel,
        grid_spec=grid_spec,
        out_shape=jax.ShapeDtypeStruct((n_pad, D_MODEL), F32),
        compiler_params=_cparams(("arbitrary", "arbitrary")),
        name="expert_ffn",
    )(block_expert, n_used, xs, wg, wu, bg, bu, wd, bd)


def _final_kernel(dest_ref, gate_ref, x_ref, p_ref, wpg_ref, bpg_ref, wple_ref, g_ref, b_ref, ys_hbm, out_ref,
                  rows, sem, *, tm):
    def issue(r, carry):
        _row_copy(ys_hbm, dest_ref[0, 0, r], rows.at[r % TOP_K], r // TOP_K, sem).start()
        return carry

    lax.fori_loop(0, tm * TOP_K, issue, 0)
    x = x_ref[...]
    pg = _dot(x.astype(BF16), wpg_ref[...]) + bpg_ref[...]
    ple = _sigmoid(pg) * _dot(p_ref[...].astype(BF16), wple_ref[...])

    def wait(r, carry):
        _row_copy(ys_hbm, 0, rows.at[0], 0, sem).wait()
        return carry

    lax.fori_loop(0, tm * TOP_K, wait, 0)
    gates = gate_ref[...]
    y = gates[:, 0:1] * rows[0]
    for k in range(1, TOP_K):
        y = y + gates[:, k:k + 1] * rows[k]
    out_ref[...] = _layer_norm(DEEPNORM_ALPHA * x + y + ple, g_ref[...], b_ref[...])


def _final(dest, gates, x1, p2d, w_pg, b_pg, w_ple, ln_g, ln_b, ys, tm):
    t = x1.shape[0]
    vec = pl.BlockSpec((1, D_MODEL), lambda i: (0, 0))
    kern = functools.partial(_final_kernel, tm=tm)
    return pl.pallas_call(
        kern,
        grid=(t // tm,),
        in_specs=[pl.BlockSpec((1, 1, tm * TOP_K), lambda i: (i, 0, 0), memory_space=pltpu.SMEM),
                  pl.BlockSpec((tm, LANES), lambda i: (i, 0)),
                  pl.BlockSpec((tm, D_MODEL), lambda i: (i, 0)),
                  pl.BlockSpec((tm, PLE_DIM), lambda i: (i, 0)),
                  pl.BlockSpec((D_MODEL, D_MODEL), lambda i: (0, 0)), vec,
                  pl.BlockSpec((PLE_DIM, D_MODEL), lambda i: (0, 0)), vec, vec,
                  pl.BlockSpec(memory_space=pl.ANY)],
        out_specs=pl.BlockSpec((tm, D_MODEL), lambda i: (i, 0)),
        out_shape=jax.ShapeDtypeStruct((t, D_MODEL), F32),
        scratch_shapes=[pltpu.VMEM((TOP_K, tm, D_MODEL), F32), pltpu.SemaphoreType.DMA(())],
        compiler_params=_cparams(("arbitrary",)),
        name="combine_ple_ln2",
    )(dest.reshape(t // tm, 1, tm * TOP_K), gates, x1, p2d, w_pg, b_pg, w_ple, ln_g, ln_b, ys)


def _row_tile(t):
    return min(512, t)


def _layer(x, p, w_in, b_gate, conv_w, a_log, dt_bias, dn_norm_w, w_branch_a, w_branch_b, w_out, ln1_g, ln1_b,
           router_w, router_b, w_gate_up, b_gate_up, w_down, b_down, w_ple, w_ple_gate, b_ple_gate, ln2_g, ln2_b):
    bsz, seq, _ = x.shape
    t = bsz * seq
    tm = _row_tile(t)
    x2d = x.reshape(t, D_MODEL)
    xb = x2d.astype(BF16)

    c_a = 3 * A_QKV
    c_b = 2 * B_QK + 2 * B_VZ
    w_bf = w_in.astype(BF16)
    u_a = _matmul(xb, w_bf[:, :c_a], BF16, tm, 512)
    u_b = _matmul(xb, w_bf[:, c_a:c_a + c_b], BF16, tm, 512)
    w_bab = jnp.pad(w_bf[:, c_a + c_b:c_a + c_b + B_GATES], ((0, 0), (0, LANES - B_GATES)))
    bab = _matmul(xb, w_bab, F32, tm, LANES)
    gpre = _matmul(xb, w_bf[:, c_a + c_b + B_GATES:], F32, tm, 512)

    slopes = _alibi_slopes()
    u_a3 = u_a.reshape(bsz, seq, c_a)
    outs, lses = [], []
    for gi, (_win, dil) in enumerate(DILATION_GROUPS):
        o_g, l_g = _attention_group(u_a3, gi, dil, slopes[gi])
        outs.append(o_g)
        lses.append(l_g)

    cw = jnp.pad(conv_w.astype(F32), ((0, 8 - CONV_W), (0, 0)))
    lane_is_g = (np.arange(LANES) % (2 * B_V_HEADS) >= B_V_HEADS) & (np.arange(LANES) < B_GATES)
    neg_a = jnp.zeros((LANES,), F32).at[B_V_HEADS:2 * B_V_HEADS].set(-jnp.exp(a_log[0].astype(F32)))
    neg_a = neg_a.at[3 * B_V_HEADS:4 * B_V_HEADS].set(-jnp.exp(a_log[1].astype(F32)))
    dtb = jnp.zeros((LANES,), F32).at[B_V_HEADS:2 * B_V_HEADS].set(dt_bias[0].astype(F32))
    dtb = dtb.at[3 * B_V_HEADS:4 * B_V_HEADS].set(dt_bias[1].astype(F32))
    gate_params = jnp.zeros((8, LANES), F32).at[0].set(neg_a).at[1].set(dtb).at[2].set(jnp.asarray(lane_is_g, F32))
    qn, kn, vn, gts = _dn_prep(u_b.reshape(bsz, seq, c_b), cw, bab.reshape(bsz, seq, LANES), gate_params)
    nh2 = 2 * B_V_HEADS
    gates_dir = jnp.stack([gts[..., :nh2], gts[..., nh2:2 * nh2]], axis=0)
    n_chunks = seq // DN_CHUNK
    gates_dir_t = gates_dir.reshape(2, bsz, n_chunks, DN_CHUNK, nh2).transpose(0, 1, 2, 4, 3)
    o_dirs = _dn_scan(qn, kn, vn, gates_dir, gates_dir_t).reshape(2, t, B_VZ)

    bg = b_gate.astype(F32).reshape(1, 2 * D_MODEL)
    a_part = _branch_a(outs, lses, w_branch_a.astype(BF16), gpre, bg, tm)
    merged = _branch_b(o_dirs, u_b, dn_norm_w.astype(F32).reshape(1, B_DV), w_branch_b.astype(BF16), gpre, bg,
                       a_part, tm)
    x1 = _out_ln(merged, w_out.astype(BF16), x2d, ln1_g.reshape(1, -1), ln1_b.reshape(1, -1), tm)

    rw = jnp.pad(router_w.astype(F32), ((0, 0), (0, LANES - N_EXPERTS)))
    rb = jnp.pad(router_b.astype(F32), (0, LANES - N_EXPERTS)).reshape(1, LANES)
    idx, gates, rank, cnt = _router(x1, rw, rb, tm)
    counts = cnt[0, :N_EXPERTS].astype(jnp.int32)
    padded = (counts + MOE_BLOCK - 1) // MOE_BLOCK * MOE_BLOCK
    pad_end = jnp.cumsum(padded)
    pad_start = pad_end - padded
    dest = pad_start[idx[:, :TOP_K]] + rank[:, :TOP_K]
    n_pad = t * TOP_K + N_EXPERTS * MOE_BLOCK
    nb = n_pad // MOE_BLOCK
    tok = jnp.broadcast_to(jnp.arange(t, dtype=jnp.int32)[:, None], (t, TOP_K))
    buf_tok = jnp.zeros((n_pad,), jnp.int32).at[dest.reshape(-1)].set(tok.reshape(-1))
    block_start = jnp.arange(nb, dtype=jnp.int32) * MOE_BLOCK
    block_expert = jnp.minimum(jnp.searchsorted(pad_end, block_start, side='right'), N_EXPERTS - 1).astype(jnp.int32)
    n_used = (pad_end[-1:] // MOE_BLOCK).astype(jnp.int32)

    xs = _gather_rows(x1, buf_tok, n_used)
    d_ff = w_down.shape[1]
    wgu = w_gate_up.reshape(N_EXPERTS, D_MODEL, d_ff, 2)
    bgu = b_gate_up.astype(F32).reshape(N_EXPERTS, 1, d_ff, 2)
    ys = _expert_ffn(xs, block_expert, n_used, wgu[..., 0].astype(BF16), wgu[..., 1].astype(BF16),
                     bgu[..., 0], bgu[..., 1], w_down.astype(BF16),
                     b_down.astype(F32).reshape(N_EXPERTS, 1, D_MODEL))

    out = _final(dest, gates, x1, p.reshape(t, PLE_DIM), w_ple_gate.astype(BF16),
                 b_ple_gate.astype(F32).reshape(1, -1), w_ple.astype(BF16), ln2_g.reshape(1, -1),
                 ln2_b.reshape(1, -1), ys, min(256, t))
    return out.reshape(bsz, seq, D_MODEL)


def kernel(x, p, w_in, b_gate, conv_w, a_log, dt_bias, dn_norm_w, w_branch_a, w_branch_b, w_out, ln1_g, ln1_b,
           router_w, router_b, w_gate_up, b_gate_up, w_down, b_down, w_ple, w_ple_gate, b_ple_gate, ln2_g, ln2_b):
    assert w_in.shape[0] == DEPTH
    return _layer(x, p[0], w_in[0], b_gate[0], conv_w[0], a_log[0], dt_bias[0], dn_norm_w[0], w_branch_a[0],
                  w_branch_b[0], w_out[0], ln1_g[0], ln1_b[0], router_w[0], router_b[0], w_gate_up[0],
                  b_gate_up[0], w_down[0], b_down[0], w_ple[0], w_ple_gate[0], b_ple_gate[0], ln2_g[0], ln2_b[0])
```

```python
import functools

import numpy as np
import jax
import jax.numpy as jnp
from jax import lax
from jax.experimental import pallas as pl
from jax.experimental.pallas import tpu as pltpu

F32 = jnp.float32
BF16 = jnp.bfloat16

D_MODEL = 2048
HEAD_DIM = 128
A_HEADS = 8
DILATION_GROUPS = ((128, 1), (512, 4), (2048, 16))
N_GROUPS = 3
NEG_INF = -1e30
B_QK_HEADS = 8
B_V_HEADS = 16
B_DK = 128
B_DV = 128
CONV_W = 5
RMS_EPS = 1e-6
N_EXPERTS = 32
TOP_K = 4
SWIGLU_ALPHA = 1.702
SWIGLU_LIMIT = 7.0
PLE_DIM = 256
DEPTH = 1
DEEPNORM_ALPHA = (2 * DEPTH) ** 0.25
LN_EPS = 1e-5
A_QKV = N_GROUPS * A_HEADS * HEAD_DIM
B_QK = B_QK_HEADS * B_DK
B_VZ = B_V_HEADS * B_DV
B_GATES = 4 * B_V_HEADS

LANES = 128
N_SIDE = 64
Q_SUB = 128
DN_CHUNK = 64
DN_PAIR = B_V_HEADS // B_QK_HEADS
TRI_BASE = 16
MOE_BLOCK = 512
MOE_TF = 512
VMEM_LIMIT = 56 * 1024 * 1024


def _cparams(sem):
    return pltpu.CompilerParams(dimension_semantics=sem, vmem_limit_bytes=VMEM_LIMIT)


def _sigmoid(x):
    return 1.0 / (1.0 + jnp.exp(-x))


def _dot(a, b):
    return jnp.dot(a, b, preferred_element_type=F32)


def _dot_nt(a, b):
    return lax.dot_general(a, b, (((1,), (1,)), ((), ())), preferred_element_type=F32)


def _dot_tn(a, b):
    return lax.dot_general(a, b, (((0,), (0,)), ((), ())), preferred_element_type=F32)


def _mm_kernel(x_ref, w_ref, o_ref):
    o_ref[...] = _dot(x_ref[...], w_ref[...]).astype(o_ref.dtype)


def _matmul(x, w, out_dtype, tm, tn):
    m, k = x.shape
    n = w.shape[1]
    return pl.pallas_call(
        _mm_kernel,
        grid=(m // tm, n // tn),
        in_specs=[pl.BlockSpec((tm, k), lambda i, j: (i, 0)),
                  pl.BlockSpec((k, tn), lambda i, j: (0, j))],
        out_specs=pl.BlockSpec((tm, tn), lambda i, j: (i, j)),
        out_shape=jax.ShapeDtypeStruct((m, n), out_dtype),
        compiler_params=_cparams(("parallel", "parallel")),
        name="in_proj",
    )(x, w)


def _attn_kernel(q_ref, kp_ref, kc_ref, kn_ref, vp_ref, vc_ref, vn_ref, o_ref, lse_ref, kbuf, vbuf,
                 *, dil, sub_len, tl, slopes):
    i0 = pl.program_id(2) * tl
    kbuf[0:N_SIDE, :] = kp_ref[0]
    kbuf[N_SIDE:N_SIDE + tl, :] = kc_ref[0]
    kbuf[N_SIDE + tl:, :] = kn_ref[0]
    vbuf[0:N_SIDE, :] = vp_ref[0]
    vbuf[N_SIDE:N_SIDE + tl, :] = vc_ref[0]
    vbuf[N_SIDE + tl:, :] = vn_ref[0]
    span = Q_SUB + 2 * N_SIDE
    qq = lax.broadcasted_iota(jnp.int32, (Q_SUB, span), 0)
    kk = lax.broadcasted_iota(jnp.int32, (Q_SUB, span), 1)
    delta = kk - N_SIDE - qq
    absd = jnp.abs(delta)
    band = absd <= N_SIDE
    dist = (dil * absd).astype(F32)
    lane = lax.broadcasted_iota(jnp.int32, (Q_SUB, LANES), 1)
    scale = HEAD_DIM ** -0.5
    for j in range(tl // Q_SUB):
        pos = i0 + (j * Q_SUB - N_SIDE) + kk
        valid = band & (pos >= 0) & (pos < sub_len)
        lse_tile = jnp.zeros((Q_SUB, LANES), F32)
        for h in range(A_HEADS):
            hs = slice(h * HEAD_DIM, (h + 1) * HEAD_DIM)
            q = q_ref[0, j * Q_SUB:(j + 1) * Q_SUB, hs]
            k = kbuf[j * Q_SUB:j * Q_SUB + span, hs]
            v = vbuf[j * Q_SUB:j * Q_SUB + span, hs]
            s = _dot_nt(q, k) * scale
            s = jnp.where(valid, s - float(slopes[h]) * dist, NEG_INF)
            m = jnp.max(s, axis=1, keepdims=True)
            p = jnp.exp(s - m)
            l = jnp.sum(p, axis=1, keepdims=True)
            o = _dot(p.astype(BF16), v) / l
            o_ref[0, j * Q_SUB:(j + 1) * Q_SUB, hs] = o.astype(o_ref.dtype)
            lse_tile = jnp.where(lane == h, m + jnp.log(l), lse_tile)
        lse_ref[0, j * Q_SUB:(j + 1) * Q_SUB, :] = lse_tile


def _attention_group(u_a, gi, dil, slopes):
    bsz, seq, _ = u_a.shape
    sub_len = seq // dil
    tl = min(512, sub_len)
    assert sub_len % tl == 0 and tl % Q_SUB == 0 and sub_len % N_SIDE == 0
    width = A_HEADS * HEAD_DIM
    cols = 3 * A_QKV // width
    view = u_a.reshape(bsz, sub_len, dil * 3 * A_QKV)
    halo_per_tile = tl // N_SIDE
    n_halo = sub_len // N_SIDE
    qcol = lambda c: c * cols + gi
    kcol = lambda c: c * cols + N_GROUPS + gi
    vcol = lambda c: c * cols + 2 * N_GROUPS + gi
    prev = lambda li: jnp.maximum(li * halo_per_tile - 1, 0)
    nxt = lambda li: jnp.minimum((li + 1) * halo_per_tile, n_halo - 1)
    halo = lambda colf, rowf: pl.BlockSpec((1, N_SIDE, width), lambda b, c, li: (b, rowf(li), colf(c)))
    cur = lambda colf: pl.BlockSpec((1, tl, width), lambda b, c, li: (b, li, colf(c)))
    kern = functools.partial(_attn_kernel, dil=dil, sub_len=sub_len, tl=tl, slopes=tuple(float(s) for s in slopes))
    o, lse = pl.pallas_call(
        kern,
        grid=(bsz, dil, sub_len // tl),
        in_specs=[cur(qcol), halo(kcol, prev), cur(kcol), halo(kcol, nxt),
                  halo(vcol, prev), cur(vcol), halo(vcol, nxt)],
        out_specs=[pl.BlockSpec((1, tl, width), lambda b, c, li: (b, li, c)),
                   pl.BlockSpec((1, tl, LANES), lambda b, c, li: (b, li, c))],
        out_shape=[jax.ShapeDtypeStruct((bsz, sub_len, dil * width), BF16),
                   jax.ShapeDtypeStruct((bsz, sub_len, dil * LANES), F32)],
        scratch_shapes=[pltpu.VMEM((tl + 2 * N_SIDE, width), BF16),
                        pltpu.VMEM((tl + 2 * N_SIDE, width), BF16)],
        compiler_params=_cparams(("parallel", "parallel", "parallel")),
        name=f"dilated_attn_g{gi}",
    )(view, view, view, view, view, view, view)
    return o.reshape(bsz * seq, width), lse.reshape(bsz * seq, LANES)


def _alibi_slopes():
    n = N_GROUPS * A_HEADS
    s = 2.0 ** (-8.0 * np.arange(1, n + 1) / n)
    return s.astype(np.float32).reshape(N_GROUPS, A_HEADS)


def _dn_prep_kernel(prev_ref, cur_ref, next_ref, cw_ref, bab_ref, gp_ref, q_ref, k_ref, v_ref, g_ref, *, ts):
    ti = pl.program_id(1)
    nt = pl.num_programs(1)
    halo = CONV_W // 2
    keep_prev = (ti > 0).astype(F32)
    keep_next = (ti < nt - 1).astype(F32)
    for c in range((2 * B_QK + B_VZ) // LANES):
        cs = slice(c * LANES, (c + 1) * LANES)
        xp = prev_ref[0, :, cs].astype(F32)[8:16] * keep_prev
        xc = cur_ref[0, :, cs].astype(F32)
        xn = next_ref[0, :, cs].astype(F32)[0:8] * keep_next
        ext = jnp.concatenate([xp, xc, xn], axis=0)
        acc = jnp.zeros((ts, LANES), F32)
        for j in range(CONV_W):
            off = 8 - halo + j
            acc = acc + ext[off:off + ts, :] * cw_ref[j:j + 1, cs]
        y = acc * _sigmoid(acc)
        if c < 2 * B_QK // LANES:
            y = y * lax.rsqrt(jnp.sum(y * y, axis=1, keepdims=True) + 1e-6)
        if c < B_QK // LANES:
            q_ref[0, :, cs] = y * (B_DK ** -0.5)
        elif c < 2 * B_QK // LANES:
            k_ref[0, :, c * LANES - B_QK:(c + 1) * LANES - B_QK] = y
        else:
            v_ref[0, :, c * LANES - 2 * B_QK:(c + 1) * LANES - 2 * B_QK] = y
    x = bab_ref[0]
    neg_a = gp_ref[0:1, :]
    dtb = gp_ref[1:2, :]
    is_g = gp_ref[2:3, :] > 0.5
    z = x + dtb
    softplus = jnp.maximum(z, 0.0) + jnp.log(1.0 + jnp.exp(-jnp.abs(z)))
    g_ref[0] = jnp.where(is_g, neg_a * softplus, _sigmoid(x))


def _dn_prep(u_b, conv_w, bab, gate_params):
    bsz, seq, _ = u_b.shape
    ts = min(256, seq)
    cq = 2 * B_QK + B_VZ
    nhalo = seq // 16
    per = ts // 16
    kern = functools.partial(_dn_prep_kernel, ts=ts)
    return pl.pallas_call(
        kern,
        grid=(bsz, seq // ts),
        in_specs=[pl.BlockSpec((1, 16, cq), lambda b, t: (b, jnp.maximum(t * per - 1, 0), 0)),
                  pl.BlockSpec((1, ts, cq), lambda b, t: (b, t, 0)),
                  pl.BlockSpec((1, 16, cq), lambda b, t: (b, jnp.minimum((t + 1) * per, nhalo - 1), 0)),
                  pl.BlockSpec((8, cq), lambda b, t: (0, 0)),
                  pl.BlockSpec((1, ts, LANES), lambda b, t: (b, t, 0)),
                  pl.BlockSpec((8, LANES), lambda b, t: (0, 0))],
        out_specs=[pl.BlockSpec((1, ts, B_QK), lambda b, t: (b, t, 0)),
                   pl.BlockSpec((1, ts, B_QK), lambda b, t: (b, t, 0)),
                   pl.BlockSpec((1, ts, B_VZ), lambda b, t: (b, t, 0)),
                   pl.BlockSpec((1, ts, LANES), lambda b, t: (b, t, 0))],
        out_shape=[jax.ShapeDtypeStruct((bsz, seq, B_QK), F32),
                   jax.ShapeDtypeStruct((bsz, seq, B_QK), F32),
                   jax.ShapeDtypeStruct((bsz, seq, B_VZ), F32),
                   jax.ShapeDtypeStruct((bsz, seq, LANES), F32)],
        compiler_params=_cparams(("parallel", "parallel")),
        name="dn_prep",
    )(u_b, u_b, u_b, conv_w, bab, gate_params)


def _dn_intra_kernel(q_ref, k_ref, v_ref, g_ref, gt_ref, u_ref, wq_ref, kq_ref, et_ref, *, c):
    n_units = 2 * DN_PAIR
    w4 = n_units * c
    hp = lax.Precision.HIGHEST
    bf = lambda t_: t_.astype(BF16)
    ii = lax.broadcasted_iota(jnp.int32, (c, w4), 0)
    ll = lax.broadcasted_iota(jnp.int32, (c, w4), 1)
    jj = ll % c
    ub = ll // c
    ub_row = ub[0:1, :]
    lo = jnp.where(ub >= DN_PAIR, jj - ii, ii - jj)
    incl = lo >= 0
    strict = lo > 0
    eye = (ii == jj).astype(F32)
    blk = (ii // TRI_BASE) == (jj // TRI_BASE)

    def pack(parts, sel):
        out = parts[n_units - 1]
        for u_ in range(n_units - 2, -1, -1):
            out = jnp.where(sel == u_, parts[u_], out)
        return out

    unit_mask = [jnp.where(ub == u_, 1.0, 0.0).astype(BF16) for u_ in range(n_units)]

    def block_diag(y16):
        return jnp.concatenate([y16 * m_ for m_ in unit_mask], axis=0)

    def mm(xs, ys):
        return [_dot(bf(x_), block_diag(bf(y_))) for x_, y_ in zip(xs, ys)]

    ri = lax.broadcasted_iota(jnp.int32, (c, c), 0)
    ci = lax.broadcasted_iota(jnp.int32, (c, c), 1)
    g_all = g_ref[0]
    gc_dir = [jnp.dot((ci <= ri).astype(F32), g_all, precision=hp, preferred_element_type=F32),
              jnp.dot((ci >= ri).astype(F32), g_all, precision=hp, preferred_element_type=F32)]
    tri4 = (lo <= 0).astype(F32)
    gcr_all = jnp.dot(gt_ref[0, 0], tri4, precision=hp, preferred_element_type=F32)
    tot_all = jnp.sum(g_all, axis=0, keepdims=True)

    pairs = list(range(B_V_HEADS // DN_PAIR))
    qs_, ks_, lms, qkms, betas, egcs, kscales, etots = [], [], [], [], [], [], [], []
    for p in pairs:
        cs = slice(p * B_DK, (p + 1) * B_DK)
        q = q_ref[0, :, cs]
        k = k_ref[0, :, cs]
        k16 = bf(k)
        k4 = jnp.concatenate([k16] * n_units, axis=0)
        gram = _dot_nt(k16, k4)
        qk = _dot_nt(bf(q), k4)
        beta_u, gc_u, gcr_u, tot_u = [], [], [], []
        for u_ in range(n_units):
            d_, e_ = divmod(u_, DN_PAIR)
            h = p * DN_PAIR + e_
            bl = d_ * 2 * B_V_HEADS + h
            gl = bl + B_V_HEADS
            beta_u.append(g_all[:, bl:bl + 1])
            gc_u.append(gc_dir[d_][:, gl:gl + 1])
            gcr_u.append(gcr_all[gl:gl + 1, :])
            tot_u.append(tot_all[:, gl:gl + 1])
        gc_p = pack(gc_u, ub)
        gcr_p = pack(gcr_u, ub_row)
        tot_p = pack(tot_u, ub_row)
        dec = jnp.where(incl, jnp.exp(jnp.where(incl, gc_p - gcr_p, 0.0)), 0.0)
        lms.append(jnp.where(strict, pack(beta_u, ub) * gram * dec, 0.0))
        qkms.append(jnp.where(incl, qk * dec, 0.0))
        qs_.append(q)
        ks_.append(k)
        betas.append(beta_u)
        egcs.append([jnp.exp(g_) for g_ in gc_u])
        kscales.append(jnp.exp(tot_p - gcr_p))
        etots.append([jnp.exp(t_) for t_ in tot_u])

    d1 = [jnp.where(blk, lm, 0.0) for lm in lms]
    d2 = mm(d1, d1)
    d4 = mm(d2, d2)
    d8 = mm(d4, d4)
    tm_ = [eye - d_ for d_ in d1]
    for dk in (d2, d4, d8):
        tm_ = [a + b for a, b in zip(tm_, mm(tm_, dk))]
    size = TRI_BASE
    while size < c:
        off = ((ii // (2 * size)) == (jj // (2 * size))) & ((ii // size) != (jj // size))
        cm = [jnp.where(off, lm, 0.0) for lm in lms]
        pc = mm(tm_, cm)
        tm_ = [a - b for a, b in zip(tm_, mm(pc, tm_))]
        size *= 2

    lane2 = lax.broadcasted_iota(jnp.int32, (1, DN_PAIR * B_DV), 1)
    for p in pairs:
        q, k = qs_[p], ks_[p]
        rows = []
        for u_ in range(n_units):
            d_, e_ = divmod(u_, DN_PAIR)
            h = p * DN_PAIR + e_
            v = v_ref[0, :, h * B_DV:(h + 1) * B_DV]
            beta = betas[p][u_]
            rows.append(jnp.concatenate([v * beta, k * (beta * egcs[p][u_])], axis=1))
        uw = _dot(block_diag(bf(tm_[p])), bf(jnp.concatenate(rows, axis=0)))
        k_t = jnp.transpose(jnp.concatenate([k] * DN_PAIR, axis=0))
        for d_ in range(2):
            u0, u1 = d_ * DN_PAIR, d_ * DN_PAIR + 1
            u_ref[d_, 0, 0, p] = jnp.concatenate([uw[u0 * c:(u0 + 1) * c, :B_DV], uw[u1 * c:(u1 + 1) * c, :B_DV]], axis=1)
            wq = jnp.concatenate([uw[u0 * c:(u0 + 1) * c, B_DV:], q * egcs[p][u0],
                                  uw[u1 * c:(u1 + 1) * c, B_DV:], q * egcs[p][u1]], axis=0)
            wq_ref[d_, 0, 0, p] = bf(wq)
            ls = slice(d_ * DN_PAIR * c, (d_ + 1) * DN_PAIR * c)
            kq = jnp.concatenate([k_t * kscales[p][:, ls], qkms[p][:, ls]], axis=0)
            kq_ref[d_, 0, 0, p] = bf(kq)
            et_ref[d_, 0, 0, p:p + 1, :] = jnp.where(lane2 < B_DV, etots[p][u0], etots[p][u1])


def _dn_intra(qn, kn, vn, gts, gts_t):
    bsz, seq, _ = qn.shape
    c = DN_CHUNK
    assert DN_PAIR * c == B_DK and seq % c == 0
    n = seq // c
    npair = B_V_HEADS // DN_PAIR
    wide = DN_PAIR * B_DV
    kern = functools.partial(_dn_intra_kernel, c=c)
    out5 = lambda r, cdim: pl.BlockSpec((2, 1, 1, npair, r, cdim), lambda b, i: (0, b, i, 0, 0, 0))
    return pl.pallas_call(
        kern,
        grid=(bsz, n),
        in_specs=[pl.BlockSpec((1, c, B_QK), lambda b, i: (b, i, 0)),
                  pl.BlockSpec((1, c, B_QK), lambda b, i: (b, i, 0)),
                  pl.BlockSpec((1, c, B_VZ), lambda b, i: (b, i, 0)),
                  pl.BlockSpec((1, c, LANES), lambda b, i: (b, i, 0)),
                  pl.BlockSpec((1, 1, LANES, c), lambda b, i: (b, i, 0, 0))],
        out_specs=[out5(c, wide), out5(2 * DN_PAIR * c, B_DK), out5(B_DK + c, DN_PAIR * c),
                   pl.BlockSpec((2, 1, 1, npair, wide), lambda b, i: (0, b, i, 0, 0))],
        out_shape=[jax.ShapeDtypeStruct((2, bsz, n, npair, c, wide), F32),
                   jax.ShapeDtypeStruct((2, bsz, n, npair, 2 * DN_PAIR * c, B_DK), BF16),
                   jax.ShapeDtypeStruct((2, bsz, n, npair, B_DK + c, DN_PAIR * c), BF16),
                   jax.ShapeDtypeStruct((2, bsz, n, npair, wide), F32)],
        compiler_params=_cparams(("parallel", "parallel")),
        name="dn_intra",
    )(qn, kn, vn, gts, gts_t)


def _dn_state_kernel(uf_ref, wqf_ref, kqf_ref, etf_ref, ub_ref, wqb_ref, kqb_ref, etb_ref, of_ref, ob_ref, s_ref, *, c):
    @pl.when(pl.program_id(1) == 0)
    def _():
        s_ref[...] = jnp.zeros_like(s_ref)

    npair = B_V_HEADS // DN_PAIR
    bf = lambda t_: t_.astype(BF16)
    chains = [(d_, p) for d_ in range(2) for p in range(npair)]
    refs = ((uf_ref, wqf_ref, kqf_ref, etf_ref, of_ref), (ub_ref, wqb_ref, kqb_ref, etb_ref, ob_ref))
    states = [s_ref[d_, p] for d_, p in chains]
    a_res = [_dot(refs[d_][1][0, 0, 0, p], bf(s_)) for (d_, p), s_ in zip(chains, states)]
    zero = jnp.zeros((c, B_DV), F32)
    b_res = []
    for (d_, p), a_ in zip(chains, a_res):
        u = refs[d_][0][0, 0, 0, p]
        v0 = u[:, :B_DV] - a_[0:c, :B_DV]
        v1 = u[:, B_DV:] - a_[2 * c:3 * c, B_DV:]
        bd_v = jnp.concatenate([jnp.concatenate([v0, zero], axis=1), jnp.concatenate([zero, v1], axis=1)], axis=0)
        b_res.append(_dot(refs[d_][2][0, 0, 0, p], bf(bd_v)))
    for (d_, p), a_, b_, s_ in zip(chains, a_res, b_res, states):
        s_ref[d_, p] = s_ * refs[d_][3][0, 0, 0, p:p + 1, :] + b_[:B_DK]
        o_ref = refs[d_][4]
        o_ref[0, :, (2 * p) * B_DV:(2 * p + 1) * B_DV] = a_[c:2 * c, :B_DV] + b_[B_DK:, :B_DV]
        o_ref[0, :, (2 * p + 1) * B_DV:(2 * p + 2) * B_DV] = a_[3 * c:4 * c, B_DV:] + b_[B_DK:, B_DV:]


def _dn_state(u_all, wq_all, kq_all, et_all, seq):
    _, bsz, n, npair, c, wide = u_all.shape
    kern = functools.partial(_dn_state_kernel, c=c)
    fwd = lambda r, cdim: pl.BlockSpec((1, 1, 1, npair, r, cdim), lambda b, i: (0, b, i, 0, 0, 0))
    bwd = lambda r, cdim: pl.BlockSpec((1, 1, 1, npair, r, cdim), lambda b, i: (1, b, n - 1 - i, 0, 0, 0))
    et_f = pl.BlockSpec((1, 1, 1, npair, wide), lambda b, i: (0, b, i, 0, 0))
    et_b = pl.BlockSpec((1, 1, 1, npair, wide), lambda b, i: (1, b, n - 1 - i, 0, 0))
    shapes = ((c, wide), (2 * DN_PAIR * c, B_DK), (B_DK + c, DN_PAIR * c))
    return pl.pallas_call(
        kern,
        grid=(bsz, n),
        in_specs=[fwd(*shapes[0]), fwd(*shapes[1]), fwd(*shapes[2]), et_f,
                  bwd(*shapes[0]), bwd(*shapes[1]), bwd(*shapes[2]), et_b],
        out_specs=[pl.BlockSpec((1, c, B_VZ), lambda b, i: (b, i, 0)),
                   pl.BlockSpec((1, c, B_VZ), lambda b, i: (b, n - 1 - i, 0))],
        out_shape=[jax.ShapeDtypeStruct((bsz, seq, B_VZ), F32), jax.ShapeDtypeStruct((bsz, seq, B_VZ), F32)],
        scratch_shapes=[pltpu.VMEM((2, npair, B_DK, wide), F32)],
        compiler_params=_cparams(("parallel", "arbitrary")),
        name="dn_state",
    )(u_all, wq_all, kq_all, et_all, u_all, wq_all, kq_all, et_all)


def _branch_a_kernel(o0_ref, o1_ref, o2_ref, l0_ref, l1_ref, l2_ref, w_ref, gp_ref, bg_ref, out_ref):
    ls = [l0_ref[...], l1_ref[...], l2_ref[...]]
    m = jnp.maximum(jnp.maximum(ls[0], ls[1]), ls[2])
    es = [jnp.exp(l - m) for l in ls]
    den = es[0] + es[1] + es[2]
    ws = [e / den for e in es]
    os_ = [o0_ref, o1_ref, o2_ref]
    parts = []
    for h in range(A_HEADS):
        hs = slice(h * HEAD_DIM, (h + 1) * HEAD_DIM)
        acc = ws[0][:, h:h + 1] * os_[0][:, hs].astype(F32)
        acc = acc + ws[1][:, h:h + 1] * os_[1][:, hs].astype(F32)
        acc = acc + ws[2][:, h:h + 1] * os_[2][:, hs].astype(F32)
        parts.append(acc.astype(BF16))
    oa = jnp.concatenate(parts, axis=1)
    y = _dot(oa, w_ref[...])
    out_ref[...] = _sigmoid(gp_ref[...] + bg_ref[...]) * y


def _branch_a(outs, lses, w_a, gpre, b_gate, tm):
    t = outs[0].shape[0]
    wd = A_HEADS * HEAD_DIM
    row = lambda width: pl.BlockSpec((tm, width), lambda i: (i, 0))
    return pl.pallas_call(
        _branch_a_kernel,
        grid=(t // tm,),
        in_specs=[row(wd), row(wd), row(wd), row(LANES), row(LANES), row(LANES),
                  pl.BlockSpec((wd, D_MODEL), lambda i: (0, 0)),
                  pl.BlockSpec((tm, D_MODEL), lambda i: (i, 0)),
                  pl.BlockSpec((1, D_MODEL), lambda i: (0, 0))],
        out_specs=row(D_MODEL),
        out_shape=jax.ShapeDtypeStruct((t, D_MODEL), F32),
        compiler_params=_cparams(("parallel",)),
        name="branch_a",
    )(*outs, *lses, w_a, gpre, b_gate)


def _branch_b_kernel(of_ref, ob_ref, z_ref, nw_ref, w_ref, gp_ref, bg_ref, a_ref, out_ref):
    nw = nw_ref[...]
    parts = []
    for h in range(B_V_HEADS):
        hs = slice(h * B_DV, (h + 1) * B_DV)
        o = of_ref[:, hs] + ob_ref[:, hs]
        z = z_ref[:, hs].astype(F32)
        o = o * lax.rsqrt(jnp.mean(o * o, axis=1, keepdims=True) + RMS_EPS) * nw * (z * _sigmoid(z))
        parts.append(o.astype(BF16))
    ob = jnp.concatenate(parts, axis=1)
    y = _dot(ob, w_ref[...])
    out_ref[...] = (a_ref[...] + _sigmoid(gp_ref[...] + bg_ref[...]) * y).astype(out_ref.dtype)


def _branch_b(o_f, o_b, u_b2d, norm_w, w_b, gpre, b_gate, a_part, tm):
    t = a_part.shape[0]
    return pl.pallas_call(
        _branch_b_kernel,
        grid=(t // tm,),
        in_specs=[pl.BlockSpec((tm, B_VZ), lambda i: (i, 0)),
                  pl.BlockSpec((tm, B_VZ), lambda i: (i, 0)),
                  pl.BlockSpec((tm, B_VZ), lambda i: (i, (2 * B_QK + B_VZ) // B_VZ)),
                  pl.BlockSpec((1, B_DV), lambda i: (0, 0)),
                  pl.BlockSpec((B_VZ, D_MODEL), lambda i: (0, 0)),
                  pl.BlockSpec((tm, D_MODEL), lambda i: (i, 1)),
                  pl.BlockSpec((1, D_MODEL), lambda i: (0, 1)),
                  pl.BlockSpec((tm, D_MODEL), lambda i: (i, 0))],
        out_specs=pl.BlockSpec((tm, D_MODEL), lambda i: (i, 0)),
        out_shape=jax.ShapeDtypeStruct((t, D_MODEL), BF16),
        compiler_params=_cparams(("parallel",)),
        name="branch_b",
    )(o_f, o_b, u_b2d, norm_w, w_b, gpre, b_gate, a_part)


def _layer_norm(y, g, b):
    mu = jnp.mean(y, axis=1, keepdims=True)
    yc = y - mu
    var = jnp.mean(yc * yc, axis=1, keepdims=True)
    return yc * lax.rsqrt(var + LN_EPS) * g + b


def _out_ln_kernel(m_ref, w_ref, x_ref, g_ref, b_ref, out_ref):
    mix = _dot(m_ref[...], w_ref[...])
    out_ref[...] = _layer_norm(DEEPNORM_ALPHA * x_ref[...] + mix, g_ref[...], b_ref[...])


def _out_ln(merged, w_out, x2d, ln_g, ln_b, tm):
    t = merged.shape[0]
    vec = pl.BlockSpec((1, D_MODEL), lambda i: (0, 0))
    return pl.pallas_call(
        _out_ln_kernel,
        grid=(t // tm,),
        in_specs=[pl.BlockSpec((tm, D_MODEL), lambda i: (i, 0)),
                  pl.BlockSpec((D_MODEL, D_MODEL), lambda i: (0, 0)),
                  pl.BlockSpec((tm, D_MODEL), lambda i: (i, 0)), vec, vec],
        out_specs=pl.BlockSpec((tm, D_MODEL), lambda i: (i, 0)),
        out_shape=jax.ShapeDtypeStruct((t, D_MODEL), F32),
        compiler_params=_cparams(("parallel",)),
        name="out_proj_ln1",
    )(merged, w_out, x2d, ln_g, ln_b)


def _router_kernel(x_ref, rw_ref, rb_ref, idx_ref, gate_ref, rank_ref, cnt_ref, carry, *, tm):
    @pl.when(pl.program_id(0) == 0)
    def _():
        carry[...] = jnp.zeros_like(carry)

    lane = lax.broadcasted_iota(jnp.int32, (tm, LANES), 1)
    lane_f = lane.astype(F32)
    logits = jnp.dot(x_ref[...], rw_ref[...], precision=lax.Precision.HIGHEST, preferred_element_type=F32) + rb_ref[...]
    cur = jnp.where(lane < N_EXPERTS, logits, -jnp.inf)
    vals, idxs = [], []
    for _k in range(TOP_K):
        m = jnp.max(cur, axis=1, keepdims=True)
        idx = jnp.min(jnp.where(cur == m, lane_f, float(LANES)), axis=1, keepdims=True).astype(jnp.int32)
        vals.append(m)
        idxs.append(idx)
        cur = jnp.where(lane == idx, -jnp.inf, cur)
    es = [jnp.exp(v - vals[0]) for v in vals]
    den = es[0] + es[1] + es[2] + es[3]
    onehot = jnp.zeros((tm, LANES), F32)
    for idx in idxs:
        onehot = onehot + (lane == idx).astype(F32)
    ri = lax.broadcasted_iota(jnp.int32, (tm, tm), 0)
    ci = lax.broadcasted_iota(jnp.int32, (tm, tm), 1)
    before = (ci < ri).astype(BF16)
    prefix = _dot(before, onehot.astype(BF16)) + carry[0:1, :]
    idx_out = jnp.zeros((tm, LANES), jnp.int32)
    gate_out = jnp.zeros((tm, LANES), F32)
    rank_out = jnp.zeros((tm, LANES), jnp.int32)
    for k in range(TOP_K):
        rk = jnp.sum(jnp.where(lane == idxs[k], prefix, 0.0), axis=1, keepdims=True)
        idx_out = jnp.where(lane == k, idxs[k], idx_out)
        gate_out = jnp.where(lane == k, es[k] / den, gate_out)
        rank_out = jnp.where(lane == k, rk.astype(jnp.int32), rank_out)
    idx_ref[...] = idx_out
    gate_ref[...] = gate_out
    rank_ref[...] = rank_out
    total = carry[0:1, :] + jnp.sum(onehot, axis=0, keepdims=True)
    carry[...] = jnp.broadcast_to(total, carry.shape)
    cnt_ref[...] = jnp.broadcast_to(total, cnt_ref.shape)


def _router(x1, rw, rb, tm):
    t = x1.shape[0]
    row = pl.BlockSpec((tm, LANES), lambda i: (i, 0))
    kern = functools.partial(_router_kernel, tm=tm)
    return pl.pallas_call(
        kern,
        grid=(t // tm,),
        in_specs=[pl.BlockSpec((tm, D_MODEL), lambda i: (i, 0)),
                  pl.BlockSpec((D_MODEL, LANES), lambda i: (0, 0)),
                  pl.BlockSpec((1, LANES), lambda i: (0, 0))],
        out_specs=[row, row, row, pl.BlockSpec((8, LANES), lambda i: (0, 0))],
        out_shape=[jax.ShapeDtypeStruct((t, LANES), jnp.int32),
                   jax.ShapeDtypeStruct((t, LANES), F32),
                   jax.ShapeDtypeStruct((t, LANES), jnp.int32),
                   jax.ShapeDtypeStruct((8, LANES), F32)],
        scratch_shapes=[pltpu.VMEM((8, LANES), F32)],
        compiler_params=_cparams(("arbitrary",)),
        name="router",
    )(x1, rw, rb)


def _row_copy(src_hbm, row, dst, slot, sem):
    return pltpu.make_async_copy(src_hbm.at[pl.ds(row, 1)], dst.at[pl.ds(slot, 1)], sem)


def _gather_kernel(nused_ref, tok_ref, x_hbm, o_ref, xbuf, sem):
    i = pl.program_id(0)

    @pl.when(i < nused_ref[0])
    def _():
        def issue(r, carry):
            _row_copy(x_hbm, tok_ref[0, 0, r], xbuf, r, sem).start()
            return carry

        lax.fori_loop(0, MOE_BLOCK, issue, 0)

        def wait(r, carry):
            _row_copy(x_hbm, 0, xbuf, r, sem).wait()
            return carry

        lax.fori_loop(0, MOE_BLOCK, wait, 0)
        o_ref[...] = xbuf[...].astype(BF16)

    @pl.when(i >= nused_ref[0])
    def _():
        o_ref[...] = jnp.zeros_like(o_ref)


def _gather_rows(x1, buf_tok, n_used):
    n_pad = buf_tok.shape[0]
    nb = n_pad // MOE_BLOCK
    grid_spec = pltpu.PrefetchScalarGridSpec(
        num_scalar_prefetch=1,
        grid=(nb,),
        in_specs=[pl.BlockSpec((1, 1, MOE_BLOCK), lambda i, nu: (i, 0, 0), memory_space=pltpu.SMEM),
                  pl.BlockSpec(memory_space=pl.ANY)],
        out_specs=pl.BlockSpec((MOE_BLOCK, D_MODEL), lambda i, nu: (i, 0)),
        scratch_shapes=[pltpu.VMEM((MOE_BLOCK, D_MODEL), F32), pltpu.SemaphoreType.DMA(())],
    )
    return pl.pallas_call(
        _gather_kernel,
        grid_spec=grid_spec,
        out_shape=jax.ShapeDtypeStruct((n_pad, D_MODEL), BF16),
        compiler_params=_cparams(("arbitrary",)),
        name="moe_gather",
    )(n_used, buf_tok.reshape(nb, 1, MOE_BLOCK), x1)


def _ffn_kernel(be_ref, nused_ref, x_ref, wg_ref, wu_ref, bg_ref, bu_ref, wd_ref, bd_ref, o_ref, acc):
    i = pl.program_id(0)
    f = pl.program_id(1)
    nf = pl.num_programs(1)

    @pl.when(f == 0)
    def _():
        acc[...] = jnp.broadcast_to(bd_ref[0], acc.shape)

    @pl.when(i < nused_ref[0])
    def _():
        x = x_ref[...]
        g = _dot(x, wg_ref[0]) + bg_ref[0]
        u = _dot(x, wu_ref[0]) + bu_ref[0]
        gate = jnp.minimum(g, SWIGLU_LIMIT)
        up = jnp.clip(u, -SWIGLU_LIMIT, SWIGLU_LIMIT)
        act = (up + 1.0) * gate * _sigmoid(gate * SWIGLU_ALPHA)
        acc[...] += _dot(act.astype(BF16), wd_ref[0])

    @pl.when(f == nf - 1)
    def _():
        o_ref[...] = acc[...]


def _expert_ffn(xs, block_expert, n_used, wg, wu, bg, bu, wd, bd):
    n_pad = xs.shape[0]
    nb = n_pad // MOE_BLOCK
    d_ff = wg.shape[2]
    nf = d_ff // MOE_TF
    grid_spec = pltpu.PrefetchScalarGridSpec(
        num_scalar_prefetch=2,
        grid=(nb, nf),
        in_specs=[pl.BlockSpec((MOE_BLOCK, D_MODEL), lambda i, f, be, nu: (i, 0)),
                  pl.BlockSpec((1, D_MODEL, MOE_TF), lambda i, f, be, nu: (be[i], 0, f)),
                  pl.BlockSpec((1, D_MODEL, MOE_TF), lambda i, f, be, nu: (be[i], 0, f)),
                  pl.BlockSpec((1, 1, MOE_TF), lambda i, f, be, nu: (be[i], 0, f)),
                  pl.BlockSpec((1, 1, MOE_TF), lambda i, f, be, nu: (be[i], 0, f)),
                  pl.BlockSpec((1, MOE_TF, D_MODEL), lambda i, f, be, nu: (be[i], f, 0)),
                  pl.BlockSpec((1, 1, D_MODEL), lambda i, f, be, nu: (be[i], 0, 0))],
        out_specs=pl.BlockSpec((MOE_BLOCK, D_MODEL), lambda i, f, be, nu: (i, 0)),
        scratch_shapes=[pltpu.VMEM((MOE_BLOCK, D_MODEL), F32)],
    )
    return pl.pallas_call(
        _ffn_kernel,
        grid_spec=grid_spec,
        out_shape=jax.ShapeDtypeStruct((n_pad, D_MODEL), F32),
        compiler_params=_cparams(("arbitrary", "arbitrary")),
        name="expert_ffn",
    )(block_expert, n_used, xs, wg, wu, bg, bu, wd, bd)


def _final_kernel(dest_ref, gate_ref, x_ref, p_ref, wpg_ref, bpg_ref, wple_ref, g_ref, b_ref, ys_hbm, out_ref,
                  rows, sem, *, tm):
    def issue(r, carry):
        _row_copy(ys_hbm, dest_ref[0, 0, r], rows.at[r % TOP_K], r // TOP_K, sem).start()
        return carry

    lax.fori_loop(0, tm * TOP_K, issue, 0)
    x = x_ref[...]
    pg = _dot(x.astype(BF16), wpg_ref[...]) + bpg_ref[...]
    ple = _sigmoid(pg) * _dot(p_ref[...].astype(BF16), wple_ref[...])

    def wait(r, carry):
        _row_copy(ys_hbm, 0, rows.at[0], 0, sem).wait()
        return carry

    lax.fori_loop(0, tm * TOP_K, wait, 0)
    gates = gate_ref[...]
    y = gates[:, 0:1] * rows[0]
    for k in range(1, TOP_K):
        y = y + gates[:, k:k + 1] * rows[k]
    out_ref[...] = _layer_norm(DEEPNORM_ALPHA * x + y + ple, g_ref[...], b_ref[...])


def _final(dest, gates, x1, p2d, w_pg, b_pg, w_ple, ln_g, ln_b, ys, tm):
    t = x1.shape[0]
    vec = pl.BlockSpec((1, D_MODEL), lambda i: (0, 0))
    kern = functools.partial(_final_kernel, tm=tm)
    return pl.pallas_call(
        kern,
        grid=(t // tm,),
        in_specs=[pl.BlockSpec((1, 1, tm * TOP_K), lambda i: (i, 0, 0), memory_space=pltpu.SMEM),
                  pl.BlockSpec((tm, LANES), lambda i: (i, 0)),
                  pl.BlockSpec((tm, D_MODEL), lambda i: (i, 0)),
                  pl.BlockSpec((tm, PLE_DIM), lambda i: (i, 0)),
                  pl.BlockSpec((D_MODEL, D_MODEL), lambda i: (0, 0)), vec,
                  pl.BlockSpec((PLE_DIM, D_MODEL), lambda i: (0, 0)), vec, vec,
                  pl.BlockSpec(memory_space=pl.ANY)],
        out_specs=pl.BlockSpec((tm, D_MODEL), lambda i: (i, 0)),
        out_shape=jax.ShapeDtypeStruct((t, D_MODEL), F32),
        scratch_shapes=[pltpu.VMEM((TOP_K, tm, D_MODEL), F32), pltpu.SemaphoreType.DMA(())],
        compiler_params=_cparams(("arbitrary",)),
        name="combine_ple_ln2",
    )(dest.reshape(t // tm, 1, tm * TOP_K), gates, x1, p2d, w_pg, b_pg, w_ple, ln_g, ln_b, ys)


def _row_tile(t):
    return min(512, t)


def _layer(x, p, w_in, b_gate, conv_w, a_log, dt_bias, dn_norm_w, w_branch_a, w_branch_b, w_out, ln1_g, ln1_b,
           router_w, router_b, w_gate_up, b_gate_up, w_down, b_down, w_ple, w_ple_gate, b_ple_gate, ln2_g, ln2_b):
    bsz, seq, _ = x.shape
    t = bsz * seq
    tm = _row_tile(t)
    x2d = x.reshape(t, D_MODEL)
    xb = x2d.astype(BF16)

    c_a = 3 * A_QKV
    c_b = 2 * B_QK + 2 * B_VZ
    w_bf = w_in.astype(BF16)
    u_a = _matmul(xb, w_bf[:, :c_a], BF16, tm, 512)
    u_b = _matmul(xb, w_bf[:, c_a:c_a + c_b], BF16, tm, 512)
    w_bab = jnp.pad(w_bf[:, c_a + c_b:c_a + c_b + B_GATES], ((0, 0), (0, LANES - B_GATES)))
    bab = _matmul(xb, w_bab, F32, tm, LANES)
    gpre = _matmul(xb, w_bf[:, c_a + c_b + B_GATES:], F32, tm, 512)

    slopes = _alibi_slopes()
    u_a3 = u_a.reshape(bsz, seq, c_a)
    outs, lses = [], []
    for gi, (_win, dil) in enumerate(DILATION_GROUPS):
        o_g, l_g = _attention_group(u_a3, gi, dil, slopes[gi])
        outs.append(o_g)
        lses.append(l_g)

    cw = jnp.pad(conv_w.astype(F32), ((0, 8 - CONV_W), (0, 0)))
    lane_is_g = (np.arange(LANES) % (2 * B_V_HEADS) >= B_V_HEADS) & (np.arange(LANES) < B_GATES)
    neg_a = jnp.zeros((LANES,), F32).at[B_V_HEADS:2 * B_V_HEADS].set(-jnp.exp(a_log[0].astype(F32)))
    neg_a = neg_a.at[3 * B_V_HEADS:4 * B_V_HEADS].set(-jnp.exp(a_log[1].astype(F32)))
    dtb = jnp.zeros((LANES,), F32).at[B_V_HEADS:2 * B_V_HEADS].set(dt_bias[0].astype(F32))
    dtb = dtb.at[3 * B_V_HEADS:4 * B_V_HEADS].set(dt_bias[1].astype(F32))
    gate_params = jnp.zeros((8, LANES), F32).at[0].set(neg_a).at[1].set(dtb).at[2].set(jnp.asarray(lane_is_g, F32))
    qn, kn, vn, gts = _dn_prep(u_b.reshape(bsz, seq, c_b), cw, bab.reshape(bsz, seq, LANES), gate_params)
    gts_t = gts.reshape(bsz, seq // DN_CHUNK, DN_CHUNK, LANES).transpose(0, 1, 3, 2)
    o_f, o_b = _dn_state(*_dn_intra(qn, kn, vn, gts, gts_t), seq)

    bg = b_gate.astype(F32).reshape(1, 2 * D_MODEL)
    a_part = _branch_a(outs, lses, w_branch_a.astype(BF16), gpre, bg, tm)
    merged = _branch_b(o_f.reshape(t, B_VZ), o_b.reshape(t, B_VZ), u_b, dn_norm_w.astype(F32).reshape(1, B_DV), w_branch_b.astype(BF16), gpre, bg,
                       a_part, tm)
    x1 = _out_ln(merged, w_out.astype(BF16), x2d, ln1_g.reshape(1, -1), ln1_b.reshape(1, -1), tm)

    rw = jnp.pad(router_w.astype(F32), ((0, 0), (0, LANES - N_EXPERTS)))
    rb = jnp.pad(router_b.astype(F32), (0, LANES - N_EXPERTS)).reshape(1, LANES)
    idx, gates, rank, cnt = _router(x1, rw, rb, tm)
    counts = cnt[0, :N_EXPERTS].astype(jnp.int32)
    padded = (counts + MOE_BLOCK - 1) // MOE_BLOCK * MOE_BLOCK
    pad_end = jnp.cumsum(padded)
    pad_start = pad_end - padded
    dest = pad_start[idx[:, :TOP_K]] + rank[:, :TOP_K]
    n_pad = t * TOP_K + N_EXPERTS * MOE_BLOCK
    nb = n_pad // MOE_BLOCK
    tok = jnp.broadcast_to(jnp.arange(t, dtype=jnp.int32)[:, None], (t, TOP_K))
    buf_tok = jnp.zeros((n_pad,), jnp.int32).at[dest.reshape(-1)].set(tok.reshape(-1))
    block_start = jnp.arange(nb, dtype=jnp.int32) * MOE_BLOCK
    block_expert = jnp.minimum(jnp.searchsorted(pad_end, block_start, side='right'), N_EXPERTS - 1).astype(jnp.int32)
    n_used = (pad_end[-1:] // MOE_BLOCK).astype(jnp.int32)

    xs = _gather_rows(x1, buf_tok, n_used)
    d_ff = w_down.shape[1]
    wgu = w_gate_up.reshape(N_EXPERTS, D_MODEL, d_ff, 2)
    bgu = b_gate_up.astype(F32).reshape(N_EXPERTS, 1, d_ff, 2)
    ys = _expert_ffn(xs, block_expert, n_used, wgu[..., 0].astype(BF16), wgu[..., 1].astype(BF16),
                     bgu[..., 0], bgu[..., 1], w_down.astype(BF16),
                     b_down.astype(F32).reshape(N_EXPERTS, 1, D_MODEL))

    out = _final(dest, gates, x1, p.reshape(t, PLE_DIM), w_ple_gate.astype(BF16),
                 b_ple_gate.astype(F32).reshape(1, -1), w_ple.astype(BF16), ln2_g.reshape(1, -1),
                 ln2_b.reshape(1, -1), ys, min(256, t))
    return out.reshape(bsz, seq, D_MODEL)


def kernel(x, p, w_in, b_gate, conv_w, a_log, dt_bias, dn_norm_w, w_branch_a, w_branch_b, w_out, ln1_g, ln1_b,
           router_w, router_b, w_gate_up, b_gate_up, w_down, b_down, w_ple, w_ple_gate, b_ple_gate, ln2_g, ln2_b):
    assert w_in.shape[0] == DEPTH
    return _layer(x, p[0], w_in[0], b_gate[0], conv_w[0], a_log[0], dt_bias[0], dn_norm_w[0], w_branch_a[0],
                  w_branch_b[0], w_out[0], ln1_g[0], ln1_b[0], router_w[0], router_b[0], w_gate_up[0],
                  b_gate_up[0], w_down[0], b_down[0], w_ple[0], w_ple_gate[0], b_ple_gate[0], ln2_g[0], ln2_b[0])
```

```python
import functools

import numpy as np
import jax
import jax.numpy as jnp
from jax import lax
from jax.experimental import pallas as pl
from jax.experimental.pallas import tpu as pltpu

F32 = jnp.float32
BF16 = jnp.bfloat16

D_MODEL = 2048
HEAD_DIM = 128
A_HEADS = 8
DILATION_GROUPS = ((128, 1), (512, 4), (2048, 16))
N_GROUPS = 3
NEG_INF = -1e30
B_QK_HEADS = 8
B_V_HEADS = 16
B_DK = 128
B_DV = 128
CONV_W = 5
RMS_EPS = 1e-6
N_EXPERTS = 32
TOP_K = 4
SWIGLU_ALPHA = 1.702
SWIGLU_LIMIT = 7.0
PLE_DIM = 256
DEPTH = 1
DEEPNORM_ALPHA = (2 * DEPTH) ** 0.25
LN_EPS = 1e-5
A_QKV = N_GROUPS * A_HEADS * HEAD_DIM
B_QK = B_QK_HEADS * B_DK
B_VZ = B_V_HEADS * B_DV
B_GATES = 4 * B_V_HEADS

LANES = 128
N_SIDE = 64
Q_SUB = 128
DN_CHUNK = 64
DN_PAIR = B_V_HEADS // B_QK_HEADS
TRI_BASE = 16
IN_PROJ_TM = 1024
MOE_BLOCK = 1024
MOE_TF = 512
VMEM_LIMIT = 56 * 1024 * 1024


def _cparams(sem):
    return pltpu.CompilerParams(dimension_semantics=sem, vmem_limit_bytes=VMEM_LIMIT)


def _sigmoid(x):
    return 1.0 / (1.0 + jnp.exp(-x))


def _dot(a, b):
    return jnp.dot(a, b, preferred_element_type=F32)


def _dot_nt(a, b):
    return lax.dot_general(a, b, (((1,), (1,)), ((), ())), preferred_element_type=F32)


def _dot_tn(a, b):
    return lax.dot_general(a, b, (((0,), (0,)), ((), ())), preferred_element_type=F32)


def _mm_kernel(x_ref, w_ref, o_ref):
    o_ref[...] = _dot(x_ref[...], w_ref[...]).astype(o_ref.dtype)


def _matmul(x, w, out_dtype, tm, tn):
    m, k = x.shape
    n = w.shape[1]
    return pl.pallas_call(
        _mm_kernel,
        grid=(m // tm, n // tn),
        in_specs=[pl.BlockSpec((tm, k), lambda i, j: (i, 0)),
                  pl.BlockSpec((k, tn), lambda i, j: (0, j))],
        out_specs=pl.BlockSpec((tm, tn), lambda i, j: (i, j)),
        out_shape=jax.ShapeDtypeStruct((m, n), out_dtype),
        compiler_params=_cparams(("parallel", "parallel")),
        name="in_proj",
    )(x, w)


def _mm_classes_kernel(x_ref, w_ref, o_ref, acc_ref, *, dil, rows):
    y = _dot(x_ref[...], w_ref[...])
    for j in range(acc_ref.shape[0]):
        ls = slice(j * LANES, (j + 1) * LANES)
        acc_ref[j] = y[:, ls]
        for c in range(dil):
            o_ref[0, c, :, ls] = acc_ref[j, pl.ds(c, rows, stride=dil), :].astype(o_ref.dtype)


def _matmul_classes(x, w, bsz, dil, tm, tn):
    m, k = x.shape
    n = w.shape[1]
    seq = m // bsz
    tiles_per_seq = seq // tm
    rows = tm // dil
    kern = functools.partial(_mm_classes_kernel, dil=dil, rows=rows)
    return pl.pallas_call(
        kern,
        grid=(m // tm, n // tn),
        in_specs=[pl.BlockSpec((tm, k), lambda i, j: (i, 0)),
                  pl.BlockSpec((k, tn), lambda i, j: (0, j))],
        out_specs=pl.BlockSpec((1, dil, rows, tn), lambda i, j: (i // tiles_per_seq, 0, i % tiles_per_seq, j)),
        out_shape=jax.ShapeDtypeStruct((bsz, dil, seq // dil, n), BF16),
        scratch_shapes=[pltpu.VMEM((tn // LANES, tm, LANES), F32)],
        compiler_params=_cparams(("parallel", "parallel")),
        name=f"in_proj_dil{dil}",
    )(x, w)


def _attn_kernel(q_ref, kp_ref, kc_ref, kn_ref, vp_ref, vc_ref, vn_ref, o_ref, lse_ref, kbuf, vbuf,
                 *, dil, sub_len, tl, slopes):
    i0 = pl.program_id(2) * tl
    kbuf[0:N_SIDE, :] = kp_ref[...]
    kbuf[N_SIDE:N_SIDE + tl, :] = kc_ref[...]
    kbuf[N_SIDE + tl:, :] = kn_ref[...]
    vbuf[0:N_SIDE, :] = vp_ref[...]
    vbuf[N_SIDE:N_SIDE + tl, :] = vc_ref[...]
    vbuf[N_SIDE + tl:, :] = vn_ref[...]
    span = Q_SUB + 2 * N_SIDE
    qq = lax.broadcasted_iota(jnp.int32, (Q_SUB, span), 0)
    kk = lax.broadcasted_iota(jnp.int32, (Q_SUB, span), 1)
    delta = kk - N_SIDE - qq
    absd = jnp.abs(delta)
    band = absd <= N_SIDE
    dist = (dil * absd).astype(F32)
    lane = lax.broadcasted_iota(jnp.int32, (Q_SUB, LANES), 1)
    scale = HEAD_DIM ** -0.5
    for j in range(tl // Q_SUB):
        pos = i0 + (j * Q_SUB - N_SIDE) + kk
        valid = band & (pos >= 0) & (pos < sub_len)
        lse_tile = jnp.zeros((Q_SUB, LANES), F32)
        for h in range(A_HEADS):
            hs = slice(h * HEAD_DIM, (h + 1) * HEAD_DIM)
            q = q_ref[j * Q_SUB:(j + 1) * Q_SUB, hs]
            k = kbuf[j * Q_SUB:j * Q_SUB + span, hs]
            v = vbuf[j * Q_SUB:j * Q_SUB + span, hs]
            s = _dot_nt(q, k) * scale
            s = jnp.where(valid, s - float(slopes[h]) * dist, NEG_INF)
            m = jnp.max(s, axis=1, keepdims=True)
            p = jnp.exp(s - m)
            l = jnp.sum(p, axis=1, keepdims=True)
            o = _dot(p.astype(BF16), v) / l
            o_ref[j * Q_SUB:(j + 1) * Q_SUB, hs] = o.astype(o_ref.dtype)
            lse_tile = jnp.where(lane == h, m + jnp.log(l), lse_tile)
        lse_ref[j * Q_SUB:(j + 1) * Q_SUB, :] = lse_tile


def _attention_group(qkv, gi, dil, slopes):
    bsz, _, sub_len, _ = qkv.shape
    tl = min(512, sub_len)
    assert sub_len % tl == 0 and tl % Q_SUB == 0 and sub_len % N_SIDE == 0
    width = A_HEADS * HEAD_DIM
    halo_per_tile = tl // N_SIDE
    n_halo = sub_len // N_SIDE
    prev = lambda li: jnp.maximum(li * halo_per_tile - 1, 0)
    nxt = lambda li: jnp.minimum((li + 1) * halo_per_tile, n_halo - 1)
    halo = lambda col, rowf: pl.BlockSpec((None, None, N_SIDE, width), lambda b, c, li: (b, c, rowf(li), col))
    cur = lambda col: pl.BlockSpec((None, None, tl, width), lambda b, c, li: (b, c, li, col))
    kern = functools.partial(_attn_kernel, dil=dil, sub_len=sub_len, tl=tl, slopes=tuple(float(s) for s in slopes))
    return pl.pallas_call(
        kern,
        grid=(bsz, dil, sub_len // tl),
        in_specs=[cur(0), halo(1, prev), cur(1), halo(1, nxt), halo(2, prev), cur(2), halo(2, nxt)],
        out_specs=[pl.BlockSpec((None, None, tl, width), lambda b, c, li: (b, c, li, 0)),
                   pl.BlockSpec((None, None, tl, LANES), lambda b, c, li: (b, c, li, 0))],
        out_shape=[jax.ShapeDtypeStruct((bsz, dil, sub_len, width), BF16),
                   jax.ShapeDtypeStruct((bsz, dil, sub_len, LANES), F32)],
        scratch_shapes=[pltpu.VMEM((tl + 2 * N_SIDE, width), BF16),
                        pltpu.VMEM((tl + 2 * N_SIDE, width), BF16)],
        compiler_params=_cparams(("parallel", "parallel", "parallel")),
        name=f"dilated_attn_g{gi}",
    )(qkv, qkv, qkv, qkv, qkv, qkv, qkv)


def _alibi_slopes():
    n = N_GROUPS * A_HEADS
    s = 2.0 ** (-8.0 * np.arange(1, n + 1) / n)
    return s.astype(np.float32).reshape(N_GROUPS, A_HEADS)


def _dn_prep_kernel(prev_ref, cur_ref, next_ref, cw_ref, bab_ref, gp_ref, q_ref, k_ref, v_ref, g_ref, *, ts):
    ti = pl.program_id(1)
    nt = pl.num_programs(1)
    halo = CONV_W // 2
    keep_prev = (ti > 0).astype(F32)
    keep_next = (ti < nt - 1).astype(F32)
    for c in range((2 * B_QK + B_VZ) // LANES):
        cs = slice(c * LANES, (c + 1) * LANES)
        xp = prev_ref[0, :, cs].astype(F32)[8:16] * keep_prev
        xc = cur_ref[0, :, cs].astype(F32)
        xn = next_ref[0, :, cs].astype(F32)[0:8] * keep_next
        ext = jnp.concatenate([xp, xc, xn], axis=0)
        acc = jnp.zeros((ts, LANES), F32)
        for j in range(CONV_W):
            off = 8 - halo + j
            acc = acc + ext[off:off + ts, :] * cw_ref[j:j + 1, cs]
        y = acc * _sigmoid(acc)
        if c < 2 * B_QK // LANES:
            y = y * lax.rsqrt(jnp.sum(y * y, axis=1, keepdims=True) + 1e-6)
        if c < B_QK // LANES:
            q_ref[0, :, cs] = y * (B_DK ** -0.5)
        elif c < 2 * B_QK // LANES:
            k_ref[0, :, c * LANES - B_QK:(c + 1) * LANES - B_QK] = y
        else:
            v_ref[0, :, c * LANES - 2 * B_QK:(c + 1) * LANES - 2 * B_QK] = y
    x = bab_ref[0]
    neg_a = gp_ref[0:1, :]
    dtb = gp_ref[1:2, :]
    is_g = gp_ref[2:3, :] > 0.5
    z = x + dtb
    softplus = jnp.maximum(z, 0.0) + jnp.log(1.0 + jnp.exp(-jnp.abs(z)))
    g_ref[0] = jnp.where(is_g, neg_a * softplus, _sigmoid(x))


def _dn_prep(u_b, conv_w, bab, gate_params):
    bsz, seq, _ = u_b.shape
    ts = min(256, seq)
    cq = 2 * B_QK + B_VZ
    nhalo = seq // 16
    per = ts // 16
    kern = functools.partial(_dn_prep_kernel, ts=ts)
    return pl.pallas_call(
        kern,
        grid=(bsz, seq // ts),
        in_specs=[pl.BlockSpec((1, 16, cq), lambda b, t: (b, jnp.maximum(t * per - 1, 0), 0)),
                  pl.BlockSpec((1, ts, cq), lambda b, t: (b, t, 0)),
                  pl.BlockSpec((1, 16, cq), lambda b, t: (b, jnp.minimum((t + 1) * per, nhalo - 1), 0)),
                  pl.BlockSpec((8, cq), lambda b, t: (0, 0)),
                  pl.BlockSpec((1, ts, LANES), lambda b, t: (b, t, 0)),
                  pl.BlockSpec((8, LANES), lambda b, t: (0, 0))],
        out_specs=[pl.BlockSpec((1, ts, B_QK), lambda b, t: (b, t, 0)),
                   pl.BlockSpec((1, ts, B_QK), lambda b, t: (b, t, 0)),
                   pl.BlockSpec((1, ts, B_VZ), lambda b, t: (b, t, 0)),
                   pl.BlockSpec((1, ts, LANES), lambda b, t: (b, t, 0))],
        out_shape=[jax.ShapeDtypeStruct((bsz, seq, B_QK), F32),
                   jax.ShapeDtypeStruct((bsz, seq, B_QK), F32),
                   jax.ShapeDtypeStruct((bsz, seq, B_VZ), F32),
                   jax.ShapeDtypeStruct((bsz, seq, LANES), F32)],
        compiler_params=_cparams(("parallel", "parallel")),
        name="dn_prep",
    )(u_b, u_b, u_b, conv_w, bab, gate_params)


def _dn_intra_kernel(q_ref, k_ref, v_ref, g_ref, gt_ref, u_ref, wq_ref, kq_ref, et_ref, *, c):
    n_units = 2 * DN_PAIR
    w4 = n_units * c
    hp = lax.Precision.HIGHEST
    bf = lambda t_: t_.astype(BF16)
    ii = lax.broadcasted_iota(jnp.int32, (c, w4), 0)
    ll = lax.broadcasted_iota(jnp.int32, (c, w4), 1)
    jj = ll % c
    ub = ll // c
    ub_row = ub[0:1, :]
    lo = jnp.where(ub >= DN_PAIR, jj - ii, ii - jj)
    incl = lo >= 0
    strict = lo > 0
    eye = (ii == jj).astype(F32)
    blk = (ii // TRI_BASE) == (jj // TRI_BASE)

    def pack(parts, sel):
        out = parts[n_units - 1]
        for u_ in range(n_units - 2, -1, -1):
            out = jnp.where(sel == u_, parts[u_], out)
        return out

    unit_mask = [jnp.where(ub == u_, 1.0, 0.0).astype(BF16) for u_ in range(n_units)]

    def block_diag(y16):
        return jnp.concatenate([y16 * m_ for m_ in unit_mask], axis=0)

    def mm(xs, ys):
        return [_dot(bf(x_), block_diag(bf(y_))) for x_, y_ in zip(xs, ys)]

    ri = lax.broadcasted_iota(jnp.int32, (c, c), 0)
    ci = lax.broadcasted_iota(jnp.int32, (c, c), 1)
    g_all = g_ref[0]
    gc_dir = [jnp.dot((ci <= ri).astype(F32), g_all, precision=hp, preferred_element_type=F32),
              jnp.dot((ci >= ri).astype(F32), g_all, precision=hp, preferred_element_type=F32)]
    tri4 = (lo <= 0).astype(F32)
    gcr_all = jnp.dot(gt_ref[0, 0], tri4, precision=hp, preferred_element_type=F32)
    tot_all = jnp.sum(g_all, axis=0, keepdims=True)

    pairs = list(range(B_V_HEADS // DN_PAIR))
    qs_, ks_, lms, qkms, betas, egcs, kscales, etots = [], [], [], [], [], [], [], []
    for p in pairs:
        cs = slice(p * B_DK, (p + 1) * B_DK)
        q = q_ref[0, :, cs]
        k = k_ref[0, :, cs]
        k16 = bf(k)
        k4 = jnp.concatenate([k16] * n_units, axis=0)
        gram = _dot_nt(k16, k4)
        qk = _dot_nt(bf(q), k4)
        beta_u, gc_u, gcr_u, tot_u = [], [], [], []
        for u_ in range(n_units):
            d_, e_ = divmod(u_, DN_PAIR)
            h = p * DN_PAIR + e_
            bl = d_ * 2 * B_V_HEADS + h
            gl = bl + B_V_HEADS
            beta_u.append(g_all[:, bl:bl + 1])
            gc_u.append(gc_dir[d_][:, gl:gl + 1])
            gcr_u.append(gcr_all[gl:gl + 1, :])
            tot_u.append(tot_all[:, gl:gl + 1])
        gc_p = pack(gc_u, ub)
        gcr_p = pack(gcr_u, ub_row)
        tot_p = pack(tot_u, ub_row)
        dec = jnp.where(incl, jnp.exp(jnp.where(incl, gc_p - gcr_p, 0.0)), 0.0)
        lms.append(jnp.where(strict, pack(beta_u, ub) * gram * dec, 0.0))
        qkms.append(jnp.where(incl, qk * dec, 0.0))
        qs_.append(q)
        ks_.append(k)
        betas.append(beta_u)
        egcs.append([jnp.exp(g_) for g_ in gc_u])
        kscales.append(jnp.exp(tot_p - gcr_p))
        etots.append([jnp.exp(t_) for t_ in tot_u])

    d1 = [jnp.where(blk, lm, 0.0) for lm in lms]
    d2 = mm(d1, d1)
    d4 = mm(d2, d2)
    d8 = mm(d4, d4)
    tm_ = [eye - d_ for d_ in d1]
    for dk in (d2, d4, d8):
        tm_ = [a + b for a, b in zip(tm_, mm(tm_, dk))]
    size = TRI_BASE
    while size < c:
        off = ((ii // (2 * size)) == (jj // (2 * size))) & ((ii // size) != (jj // size))
        cm = [jnp.where(off, lm, 0.0) for lm in lms]
        pc = mm(tm_, cm)
        tm_ = [a - b for a, b in zip(tm_, mm(pc, tm_))]
        size *= 2

    lane2 = lax.broadcasted_iota(jnp.int32, (1, DN_PAIR * B_DV), 1)
    for p in pairs:
        q, k = qs_[p], ks_[p]
        rows = []
        for u_ in range(n_units):
            d_, e_ = divmod(u_, DN_PAIR)
            h = p * DN_PAIR + e_
            v = v_ref[0, :, h * B_DV:(h + 1) * B_DV]
            beta = betas[p][u_]
            rows.append(jnp.concatenate([v * beta, k * (beta * egcs[p][u_])], axis=1))
        uw = _dot(block_diag(bf(tm_[p])), bf(jnp.concatenate(rows, axis=0)))
        k_t = jnp.transpose(jnp.concatenate([k] * DN_PAIR, axis=0))
        for d_ in range(2):
            u0, u1 = d_ * DN_PAIR, d_ * DN_PAIR + 1
            u_ref[d_, 0, 0, p] = jnp.concatenate([uw[u0 * c:(u0 + 1) * c, :B_DV], uw[u1 * c:(u1 + 1) * c, :B_DV]], axis=1)
            wq = jnp.concatenate([uw[u0 * c:(u0 + 1) * c, B_DV:], q * egcs[p][u0],
                                  uw[u1 * c:(u1 + 1) * c, B_DV:], q * egcs[p][u1]], axis=0)
            wq_ref[d_, 0, 0, p] = bf(wq)
            ls = slice(d_ * DN_PAIR * c, (d_ + 1) * DN_PAIR * c)
            kq = jnp.concatenate([k_t * kscales[p][:, ls], qkms[p][:, ls]], axis=0)
            kq_ref[d_, 0, 0, p] = bf(kq)
            et_ref[d_, 0, 0, p:p + 1, :] = jnp.where(lane2 < B_DV, etots[p][u0], etots[p][u1])


def _dn_intra(qn, kn, vn, gts, gts_t):
    bsz, seq, _ = qn.shape
    c = DN_CHUNK
    assert DN_PAIR * c == B_DK and seq % c == 0
    n = seq // c
    npair = B_V_HEADS // DN_PAIR
    wide = DN_PAIR * B_DV
    kern = functools.partial(_dn_intra_kernel, c=c)
    out5 = lambda r, cdim: pl.BlockSpec((2, 1, 1, npair, r, cdim), lambda b, i: (0, b, i, 0, 0, 0))
    return pl.pallas_call(
        kern,
        grid=(bsz, n),
        in_specs=[pl.BlockSpec((1, c, B_QK), lambda b, i: (b, i, 0)),
                  pl.BlockSpec((1, c, B_QK), lambda b, i: (b, i, 0)),
                  pl.BlockSpec((1, c, B_VZ), lambda b, i: (b, i, 0)),
                  pl.BlockSpec((1, c, LANES), lambda b, i: (b, i, 0)),
                  pl.BlockSpec((1, 1, LANES, c), lambda b, i: (b, i, 0, 0))],
        out_specs=[out5(c, wide), out5(2 * DN_PAIR * c, B_DK), out5(B_DK + c, DN_PAIR * c),
                   pl.BlockSpec((2, 1, 1, npair, wide), lambda b, i: (0, b, i, 0, 0))],
        out_shape=[jax.ShapeDtypeStruct((2, bsz, n, npair, c, wide), F32),
                   jax.ShapeDtypeStruct((2, bsz, n, npair, 2 * DN_PAIR * c, B_DK), BF16),
                   jax.ShapeDtypeStruct((2, bsz, n, npair, B_DK + c, DN_PAIR * c), BF16),
                   jax.ShapeDtypeStruct((2, bsz, n, npair, wide), F32)],
        compiler_params=_cparams(("parallel", "parallel")),
        name="dn_intra",
    )(qn, kn, vn, gts, gts_t)


def _dn_state_kernel(uf_ref, wqf_ref, kqf_ref, etf_ref, ub_ref, wqb_ref, kqb_ref, etb_ref, of_ref, ob_ref, s_ref, *, c):
    @pl.when(pl.program_id(1) == 0)
    def _():
        s_ref[...] = jnp.zeros_like(s_ref)

    npair = B_V_HEADS // DN_PAIR
    bf = lambda t_: t_.astype(BF16)
    chains = [(d_, p) for d_ in range(2) for p in range(npair)]
    refs = ((uf_ref, wqf_ref, kqf_ref, etf_ref, of_ref), (ub_ref, wqb_ref, kqb_ref, etb_ref, ob_ref))
    states = [s_ref[d_, p] for d_, p in chains]
    a_res = [_dot(refs[d_][1][0, 0, 0, p], bf(s_)) for (d_, p), s_ in zip(chains, states)]
    zero = jnp.zeros((c, B_DV), F32)
    b_res = []
    for (d_, p), a_ in zip(chains, a_res):
        u = refs[d_][0][0, 0, 0, p]
        v0 = u[:, :B_DV] - a_[0:c, :B_DV]
        v1 = u[:, B_DV:] - a_[2 * c:3 * c, B_DV:]
        bd_v = jnp.concatenate([jnp.concatenate([v0, zero], axis=1), jnp.concatenate([zero, v1], axis=1)], axis=0)
        b_res.append(_dot(refs[d_][2][0, 0, 0, p], bf(bd_v)))
    for (d_, p), a_, b_, s_ in zip(chains, a_res, b_res, states):
        s_ref[d_, p] = s_ * refs[d_][3][0, 0, 0, p:p + 1, :] + b_[:B_DK]
        o_ref = refs[d_][4]
        o_ref[0, :, (2 * p) * B_DV:(2 * p + 1) * B_DV] = a_[c:2 * c, :B_DV] + b_[B_DK:, :B_DV]
        o_ref[0, :, (2 * p + 1) * B_DV:(2 * p + 2) * B_DV] = a_[3 * c:4 * c, B_DV:] + b_[B_DK:, B_DV:]


def _dn_state(u_all, wq_all, kq_all, et_all, seq):
    _, bsz, n, npair, c, wide = u_all.shape
    kern = functools.partial(_dn_state_kernel, c=c)
    fwd = lambda r, cdim: pl.BlockSpec((1, 1, 1, npair, r, cdim), lambda b, i: (0, b, i, 0, 0, 0))
    bwd = lambda r, cdim: pl.BlockSpec((1, 1, 1, npair, r, cdim), lambda b, i: (1, b, n - 1 - i, 0, 0, 0))
    et_f = pl.BlockSpec((1, 1, 1, npair, wide), lambda b, i: (0, b, i, 0, 0))
    et_b = pl.BlockSpec((1, 1, 1, npair, wide), lambda b, i: (1, b, n - 1 - i, 0, 0))
    shapes = ((c, wide), (2 * DN_PAIR * c, B_DK), (B_DK + c, DN_PAIR * c))
    return pl.pallas_call(
        kern,
        grid=(bsz, n),
        in_specs=[fwd(*shapes[0]), fwd(*shapes[1]), fwd(*shapes[2]), et_f,
                  bwd(*shapes[0]), bwd(*shapes[1]), bwd(*shapes[2]), et_b],
        out_specs=[pl.BlockSpec((1, c, B_VZ), lambda b, i: (b, i, 0)),
                   pl.BlockSpec((1, c, B_VZ), lambda b, i: (b, n - 1 - i, 0))],
        out_shape=[jax.ShapeDtypeStruct((bsz, seq, B_VZ), F32), jax.ShapeDtypeStruct((bsz, seq, B_VZ), F32)],
        scratch_shapes=[pltpu.VMEM((2, npair, B_DK, wide), F32)],
        compiler_params=_cparams(("parallel", "arbitrary")),
        name="dn_state",
    )(u_all, wq_all, kq_all, et_all, u_all, wq_all, kq_all, et_all)


def _branch_a_kernel(o0_ref, o1_ref, o2_ref, l0_ref, l1_ref, l2_ref, w_ref, gp_ref, bg_ref, out_ref, o_sc, l_sc,
                     *, dils, tm):
    for g, (o_ref, l_ref) in enumerate(((o0_ref, l0_ref), (o1_ref, l1_ref), (o2_ref, l2_ref))):
        r = dils[g]
        for c in range(r):
            l_sc[g, pl.ds(c, tm // r, stride=r), :] = l_ref[c]
            for h in range(A_HEADS):
                o_sc[g, h, pl.ds(c, tm // r, stride=r), :] = o_ref[c, :, h * HEAD_DIM:(h + 1) * HEAD_DIM].astype(F32)
    ls = [l_sc[g] for g in range(N_GROUPS)]
    m = jnp.maximum(jnp.maximum(ls[0], ls[1]), ls[2])
    es = [jnp.exp(l - m) for l in ls]
    den = es[0] + es[1] + es[2]
    ws = [e / den for e in es]
    parts = []
    for h in range(A_HEADS):
        hs = slice(h * HEAD_DIM, (h + 1) * HEAD_DIM)
        acc = ws[0][:, h:h + 1] * o_sc[0, h]
        acc = acc + ws[1][:, h:h + 1] * o_sc[1, h]
        acc = acc + ws[2][:, h:h + 1] * o_sc[2, h]
        parts.append(acc.astype(BF16))
    oa = jnp.concatenate(parts, axis=1)
    y = _dot(oa, w_ref[...])
    out_ref[...] = _sigmoid(gp_ref[...] + bg_ref[...]) * y


def _branch_a(outs, lses, w_a, gpre, b_gate, tm):
    bsz = outs[0].shape[0]
    dils = tuple(o.shape[1] for o in outs)
    seq = dils[0] * outs[0].shape[2]
    t = bsz * seq
    per_seq = seq // tm
    wd = A_HEADS * HEAD_DIM
    cls = lambda r, width: pl.BlockSpec((None, r, tm // r, width), lambda i: (i // per_seq, 0, i % per_seq, 0))
    kern = functools.partial(_branch_a_kernel, dils=dils, tm=tm)
    return pl.pallas_call(
        kern,
        grid=(t // tm,),
        in_specs=[cls(dils[0], wd), cls(dils[1], wd), cls(dils[2], wd),
                  cls(dils[0], LANES), cls(dils[1], LANES), cls(dils[2], LANES),
                  pl.BlockSpec((wd, D_MODEL), lambda i: (0, 0)),
                  pl.BlockSpec((tm, D_MODEL), lambda i: (i, 0)),
                  pl.BlockSpec((1, D_MODEL), lambda i: (0, 0))],
        out_specs=pl.BlockSpec((tm, D_MODEL), lambda i: (i, 0)),
        out_shape=jax.ShapeDtypeStruct((t, D_MODEL), F32),
        scratch_shapes=[pltpu.VMEM((N_GROUPS, A_HEADS, tm, HEAD_DIM), F32), pltpu.VMEM((N_GROUPS, tm, LANES), F32)],
        compiler_params=_cparams(("parallel",)),
        name="branch_a",
    )(*outs, *lses, w_a, gpre, b_gate)


def _branch_b_kernel(of_ref, ob_ref, z_ref, nw_ref, w_ref, gp_ref, bg_ref, a_ref, out_ref):
    nw = nw_ref[...]
    parts = []
    for h in range(B_V_HEADS):
        hs = slice(h * B_DV, (h + 1) * B_DV)
        o = of_ref[:, hs] + ob_ref[:, hs]
        z = z_ref[:, hs].astype(F32)
        o = o * lax.rsqrt(jnp.mean(o * o, axis=1, keepdims=True) + RMS_EPS) * nw * (z * _sigmoid(z))
        parts.append(o.astype(BF16))
    ob = jnp.concatenate(parts, axis=1)
    y = _dot(ob, w_ref[...])
    out_ref[...] = (a_ref[...] + _sigmoid(gp_ref[...] + bg_ref[...]) * y).astype(out_ref.dtype)


def _branch_b(o_f, o_b, u_b2d, norm_w, w_b, gpre, b_gate, a_part, tm):
    t = a_part.shape[0]
    return pl.pallas_call(
        _branch_b_kernel,
        grid=(t // tm,),
        in_specs=[pl.BlockSpec((tm, B_VZ), lambda i: (i, 0)),
                  pl.BlockSpec((tm, B_VZ), lambda i: (i, 0)),
                  pl.BlockSpec((tm, B_VZ), lambda i: (i, (2 * B_QK + B_VZ) // B_VZ)),
                  pl.BlockSpec((1, B_DV), lambda i: (0, 0)),
                  pl.BlockSpec((B_VZ, D_MODEL), lambda i: (0, 0)),
                  pl.BlockSpec((tm, D_MODEL), lambda i: (i, 1)),
                  pl.BlockSpec((1, D_MODEL), lambda i: (0, 1)),
                  pl.BlockSpec((tm, D_MODEL), lambda i: (i, 0))],
        out_specs=pl.BlockSpec((tm, D_MODEL), lambda i: (i, 0)),
        out_shape=jax.ShapeDtypeStruct((t, D_MODEL), BF16),
        compiler_params=_cparams(("parallel",)),
        name="branch_b",
    )(o_f, o_b, u_b2d, norm_w, w_b, gpre, b_gate, a_part)


def _layer_norm(y, g, b):
    mu = jnp.mean(y, axis=1, keepdims=True)
    yc = y - mu
    var = jnp.mean(yc * yc, axis=1, keepdims=True)
    return yc * lax.rsqrt(var + LN_EPS) * g + b


def _out_ln_kernel(m_ref, w_ref, x_ref, g_ref, b_ref, out_ref):
    mix = _dot(m_ref[...], w_ref[...])
    out_ref[...] = _layer_norm(DEEPNORM_ALPHA * x_ref[...] + mix, g_ref[...], b_ref[...])


def _out_ln(merged, w_out, x2d, ln_g, ln_b, tm):
    t = merged.shape[0]
    vec = pl.BlockSpec((1, D_MODEL), lambda i: (0, 0))
    return pl.pallas_call(
        _out_ln_kernel,
        grid=(t // tm,),
        in_specs=[pl.BlockSpec((tm, D_MODEL), lambda i: (i, 0)),
                  pl.BlockSpec((D_MODEL, D_MODEL), lambda i: (0, 0)),
                  pl.BlockSpec((tm, D_MODEL), lambda i: (i, 0)), vec, vec],
        out_specs=pl.BlockSpec((tm, D_MODEL), lambda i: (i, 0)),
        out_shape=jax.ShapeDtypeStruct((t, D_MODEL), F32),
        compiler_params=_cparams(("parallel",)),
        name="out_proj_ln1",
    )(merged, w_out, x2d, ln_g, ln_b)


def _router_kernel(x_ref, rw_ref, rb_ref, idx_ref, gate_ref, rank_ref, cnt_ref, carry, *, tm):
    @pl.when(pl.program_id(0) == 0)
    def _():
        carry[...] = jnp.zeros_like(carry)

    lane = lax.broadcasted_iota(jnp.int32, (tm, LANES), 1)
    lane_f = lane.astype(F32)
    logits = jnp.dot(x_ref[...], rw_ref[...], precision=lax.Precision.HIGHEST, preferred_element_type=F32) + rb_ref[...]
    cur = jnp.where(lane < N_EXPERTS, logits, -jnp.inf)
    vals, idxs = [], []
    for _k in range(TOP_K):
        m = jnp.max(cur, axis=1, keepdims=True)
        idx = jnp.min(jnp.where(cur == m, lane_f, float(LANES)), axis=1, keepdims=True).astype(jnp.int32)
        vals.append(m)
        idxs.append(idx)
        cur = jnp.where(lane == idx, -jnp.inf, cur)
    es = [jnp.exp(v - vals[0]) for v in vals]
    den = es[0] + es[1] + es[2] + es[3]
    onehot = jnp.zeros((tm, LANES), F32)
    for idx in idxs:
        onehot = onehot + (lane == idx).astype(F32)
    ri = lax.broadcasted_iota(jnp.int32, (tm, tm), 0)
    ci = lax.broadcasted_iota(jnp.int32, (tm, tm), 1)
    before = (ci < ri).astype(BF16)
    prefix = _dot(before, onehot.astype(BF16)) + carry[0:1, :]
    idx_out = jnp.zeros((tm, LANES), jnp.int32)
    gate_out = jnp.zeros((tm, LANES), F32)
    rank_out = jnp.zeros((tm, LANES), jnp.int32)
    for k in range(TOP_K):
        rk = jnp.sum(jnp.where(lane == idxs[k], prefix, 0.0), axis=1, keepdims=True)
        idx_out = jnp.where(lane == k, idxs[k], idx_out)
        gate_out = jnp.where(lane == k, es[k] / den, gate_out)
        rank_out = jnp.where(lane == k, rk.astype(jnp.int32), rank_out)
    idx_ref[...] = idx_out
    gate_ref[...] = gate_out
    rank_ref[...] = rank_out
    total = carry[0:1, :] + jnp.sum(onehot, axis=0, keepdims=True)
    carry[...] = jnp.broadcast_to(total, carry.shape)
    cnt_ref[...] = jnp.broadcast_to(total, cnt_ref.shape)


def _router(x1, rw, rb, tm):
    t = x1.shape[0]
    row = pl.BlockSpec((tm, LANES), lambda i: (i, 0))
    kern = functools.partial(_router_kernel, tm=tm)
    return pl.pallas_call(
        kern,
        grid=(t // tm,),
        in_specs=[pl.BlockSpec((tm, D_MODEL), lambda i: (i, 0)),
                  pl.BlockSpec((D_MODEL, LANES), lambda i: (0, 0)),
                  pl.BlockSpec((1, LANES), lambda i: (0, 0))],
        out_specs=[row, row, row, pl.BlockSpec((8, LANES), lambda i: (0, 0))],
        out_shape=[jax.ShapeDtypeStruct((t, LANES), jnp.int32),
                   jax.ShapeDtypeStruct((t, LANES), F32),
                   jax.ShapeDtypeStruct((t, LANES), jnp.int32),
                   jax.ShapeDtypeStruct((8, LANES), F32)],
        scratch_shapes=[pltpu.VMEM((8, LANES), F32)],
        compiler_params=_cparams(("arbitrary",)),
        name="router",
    )(x1, rw, rb)


def _row_copy(src_hbm, row, dst, slot, sem):
    return pltpu.make_async_copy(src_hbm.at[pl.ds(row, 1)], dst.at[pl.ds(slot, 1)], sem)


def _gather_kernel(nused_ref, tok_ref, x_hbm, o_ref, xbuf, sem):
    i = pl.program_id(0)

    @pl.when(i < nused_ref[0])
    def _():
        def issue(r, carry):
            _row_copy(x_hbm, tok_ref[0, 0, r], xbuf, r, sem).start()
            return carry

        lax.fori_loop(0, MOE_BLOCK, issue, 0)

        def wait(r, carry):
            _row_copy(x_hbm, 0, xbuf, r, sem).wait()
            return carry

        lax.fori_loop(0, MOE_BLOCK, wait, 0)
        o_ref[...] = xbuf[...].astype(BF16)

    @pl.when(i >= nused_ref[0])
    def _():
        o_ref[...] = jnp.zeros_like(o_ref)


def _gather_rows(x1, buf_tok, n_used):
    n_pad = buf_tok.shape[0]
    nb = n_pad // MOE_BLOCK
    grid_spec = pltpu.PrefetchScalarGridSpec(
        num_scalar_prefetch=1,
        grid=(nb,),
        in_specs=[pl.BlockSpec((1, 1, MOE_BLOCK), lambda i, nu: (i, 0, 0), memory_space=pltpu.SMEM),
                  pl.BlockSpec(memory_space=pl.ANY)],
        out_specs=pl.BlockSpec((MOE_BLOCK, D_MODEL), lambda i, nu: (i, 0)),
        scratch_shapes=[pltpu.VMEM((MOE_BLOCK, D_MODEL), F32), pltpu.SemaphoreType.DMA(())],
    )
    return pl.pallas_call(
        _gather_kernel,
        grid_spec=grid_spec,
        out_shape=jax.ShapeDtypeStruct((n_pad, D_MODEL), BF16),
        compiler_params=_cparams(("arbitrary",)),
        name="moe_gather",
    )(n_used, buf_tok.reshape(nb, 1, MOE_BLOCK), x1)


def _ffn_kernel(be_ref, nused_ref, x_ref, wg_ref, wu_ref, bg_ref, bu_ref, wd_ref, bd_ref, o_ref, acc):
    i = pl.program_id(0)
    f = pl.program_id(1)
    nf = pl.num_programs(1)

    @pl.when(f == 0)
    def _():
        acc[...] = jnp.broadcast_to(bd_ref[0], acc.shape)

    @pl.when(i < nused_ref[0])
    def _():
        x = x_ref[...]
        g = _dot(x, wg_ref[0, 0]) + bg_ref[0]
        u = _dot(x, wu_ref[0, 0]) + bu_ref[0]
        gate = jnp.minimum(g, SWIGLU_LIMIT)
        up = jnp.clip(u, -SWIGLU_LIMIT, SWIGLU_LIMIT)
        act = (up + 1.0) * gate * _sigmoid(gate * SWIGLU_ALPHA)
        acc[...] += _dot(act.astype(BF16), wd_ref[0])

    @pl.when(f == nf - 1)
    def _():
        o_ref[...] = acc[...]


def _expert_ffn(xs, block_expert, n_used, wgu, bg, bu, wd, bd):
    n_pad = xs.shape[0]
    nb = n_pad // MOE_BLOCK
    d_ff = wgu.shape[3]
    nf = d_ff // MOE_TF
    grid_spec = pltpu.PrefetchScalarGridSpec(
        num_scalar_prefetch=2,
        grid=(nb, nf),
        in_specs=[pl.BlockSpec((MOE_BLOCK, D_MODEL), lambda i, f, be, nu: (i, 0)),
                  pl.BlockSpec((1, 1, D_MODEL, MOE_TF), lambda i, f, be, nu: (be[i], 0, 0, f)),
                  pl.BlockSpec((1, 1, D_MODEL, MOE_TF), lambda i, f, be, nu: (be[i], 1, 0, f)),
                  pl.BlockSpec((1, 1, MOE_TF), lambda i, f, be, nu: (be[i], 0, f)),
                  pl.BlockSpec((1, 1, MOE_TF), lambda i, f, be, nu: (be[i], 0, f)),
                  pl.BlockSpec((1, MOE_TF, D_MODEL), lambda i, f, be, nu: (be[i], f, 0)),
                  pl.BlockSpec((1, 1, D_MODEL), lambda i, f, be, nu: (be[i], 0, 0))],
        out_specs=pl.BlockSpec((MOE_BLOCK, D_MODEL), lambda i, f, be, nu: (i, 0)),
        scratch_shapes=[pltpu.VMEM((MOE_BLOCK, D_MODEL), F32)],
    )
    return pl.pallas_call(
        _ffn_kernel,
        grid_spec=grid_spec,
        out_shape=jax.ShapeDtypeStruct((n_pad, D_MODEL), F32),
        compiler_params=_cparams(("arbitrary", "arbitrary")),
        name="expert_ffn",
    )(block_expert, n_used, xs, wgu, wgu, bg, bu, wd, bd)


def _final_kernel(dest_ref, gate_ref, x_ref, p_ref, wpg_ref, bpg_ref, wple_ref, g_ref, b_ref, ys_hbm, out_ref,
                  rows, sem, *, tm):
    def issue(r, carry):
        _row_copy(ys_hbm, dest_ref[0, 0, r], rows.at[r % TOP_K], r // TOP_K, sem).start()
        return carry

    lax.fori_loop(0, tm * TOP_K, issue, 0)
    x = x_ref[...]
    pg = _dot(x.astype(BF16), wpg_ref[...]) + bpg_ref[...]
    ple = _sigmoid(pg) * _dot(p_ref[...].astype(BF16), wple_ref[...])

    def wait(r, carry):
        _row_copy(ys_hbm, 0, rows.at[0], 0, sem).wait()
        return carry

    lax.fori_loop(0, tm * TOP_K, wait, 0)
    gates = gate_ref[...]
    y = gates[:, 0:1] * rows[0]
    for k in range(1, TOP_K):
        y = y + gates[:, k:k + 1] * rows[k]
    out_ref[...] = _layer_norm(DEEPNORM_ALPHA * x + y + ple, g_ref[...], b_ref[...])


def _final(dest, gates, x1, p2d, w_pg, b_pg, w_ple, ln_g, ln_b, ys, tm):
    t = x1.shape[0]
    vec = pl.BlockSpec((1, D_MODEL), lambda i: (0, 0))
    kern = functools.partial(_final_kernel, tm=tm)
    return pl.pallas_call(
        kern,
        grid=(t // tm,),
        in_specs=[pl.BlockSpec((1, 1, tm * TOP_K), lambda i: (i, 0, 0), memory_space=pltpu.SMEM),
                  pl.BlockSpec((tm, LANES), lambda i: (i, 0)),
                  pl.BlockSpec((tm, D_MODEL), lambda i: (i, 0)),
                  pl.BlockSpec((tm, PLE_DIM), lambda i: (i, 0)),
                  pl.BlockSpec((D_MODEL, D_MODEL), lambda i: (0, 0)), vec,
                  pl.BlockSpec((PLE_DIM, D_MODEL), lambda i: (0, 0)), vec, vec,
                  pl.BlockSpec(memory_space=pl.ANY)],
        out_specs=pl.BlockSpec((tm, D_MODEL), lambda i: (i, 0)),
        out_shape=jax.ShapeDtypeStruct((t, D_MODEL), F32),
        scratch_shapes=[pltpu.VMEM((TOP_K, tm, D_MODEL), F32), pltpu.SemaphoreType.DMA(())],
        compiler_params=_cparams(("arbitrary",)),
        name="combine_ple_ln2",
    )(dest.reshape(t // tm, 1, tm * TOP_K), gates, x1, p2d, w_pg, b_pg, w_ple, ln_g, ln_b, ys)


def _row_tile(t):
    return min(512, t)


def _layer(x, p, w_in, b_gate, conv_w, a_log, dt_bias, dn_norm_w, w_branch_a, w_branch_b, w_out, ln1_g, ln1_b,
           router_w, router_b, w_gate_up, b_gate_up, w_down, b_down, w_ple, w_ple_gate, b_ple_gate, ln2_g, ln2_b):
    bsz, seq, _ = x.shape
    t = bsz * seq
    tm = _row_tile(t)
    x2d = x.reshape(t, D_MODEL)
    xb = x2d.astype(BF16)

    c_a = 3 * A_QKV
    c_b = 2 * B_QK + 2 * B_VZ
    w_bf = w_in.astype(BF16)
    tp = min(IN_PROJ_TM, seq)
    u_b = _matmul(xb, w_bf[:, c_a:c_a + c_b], BF16, tp, 512)
    w_bab = jnp.pad(w_bf[:, c_a + c_b:c_a + c_b + B_GATES], ((0, 0), (0, LANES - B_GATES)))
    bab = _matmul(xb, w_bab, F32, tp, LANES)
    gpre = _matmul(xb, w_bf[:, c_a + c_b + B_GATES:], F32, tp, 512)

    slopes = _alibi_slopes()
    gw = A_HEADS * HEAD_DIM
    outs, lses = [], []
    for gi, (_win, dil) in enumerate(DILATION_GROUPS):
        w_g = jnp.concatenate([w_bf[:, part * A_QKV + gi * gw:part * A_QKV + (gi + 1) * gw] for part in range(3)], axis=1)
        o_g, l_g = _attention_group(_matmul_classes(xb, w_g, bsz, dil, tp, 512), gi, dil, slopes[gi])
        outs.append(o_g)
        lses.append(l_g)

    cw = jnp.pad(conv_w.astype(F32), ((0, 8 - CONV_W), (0, 0)))
    lane_is_g = (np.arange(LANES) % (2 * B_V_HEADS) >= B_V_HEADS) & (np.arange(LANES) < B_GATES)
    neg_a = jnp.zeros((LANES,), F32).at[B_V_HEADS:2 * B_V_HEADS].set(-jnp.exp(a_log[0].astype(F32)))
    neg_a = neg_a.at[3 * B_V_HEADS:4 * B_V_HEADS].set(-jnp.exp(a_log[1].astype(F32)))
    dtb = jnp.zeros((LANES,), F32).at[B_V_HEADS:2 * B_V_HEADS].set(dt_bias[0].astype(F32))
    dtb = dtb.at[3 * B_V_HEADS:4 * B_V_HEADS].set(dt_bias[1].astype(F32))
    gate_params = jnp.zeros((8, LANES), F32).at[0].set(neg_a).at[1].set(dtb).at[2].set(jnp.asarray(lane_is_g, F32))
    qn, kn, vn, gts = _dn_prep(u_b.reshape(bsz, seq, c_b), cw, bab.reshape(bsz, seq, LANES), gate_params)
    gts_t = gts.reshape(bsz, seq // DN_CHUNK, DN_CHUNK, LANES).transpose(0, 1, 3, 2)
    o_f, o_b = _dn_state(*_dn_intra(qn, kn, vn, gts, gts_t), seq)

    bg = b_gate.astype(F32).reshape(1, 2 * D_MODEL)
    a_part = _branch_a(outs, lses, w_branch_a.astype(BF16), gpre, bg, tm)
    merged = _branch_b(o_f.reshape(t, B_VZ), o_b.reshape(t, B_VZ), u_b, dn_norm_w.astype(F32).reshape(1, B_DV), w_branch_b.astype(BF16), gpre, bg,
                       a_part, tm)
    x1 = _out_ln(merged, w_out.astype(BF16), x2d, ln1_g.reshape(1, -1), ln1_b.reshape(1, -1), tm)

    rw = jnp.pad(router_w.astype(F32), ((0, 0), (0, LANES - N_EXPERTS)))
    rb = jnp.pad(router_b.astype(F32), (0, LANES - N_EXPERTS)).reshape(1, LANES)
    idx, gates, rank, cnt = _router(x1, rw, rb, tm)
    counts = cnt[0, :N_EXPERTS].astype(jnp.int32)
    padded = (counts + MOE_BLOCK - 1) // MOE_BLOCK * MOE_BLOCK
    pad_end = jnp.cumsum(padded)
    pad_start = pad_end - padded
    dest = pad_start[idx[:, :TOP_K]] + rank[:, :TOP_K]
    n_pad = t * TOP_K + N_EXPERTS * MOE_BLOCK
    nb = n_pad // MOE_BLOCK
    tok = jnp.broadcast_to(jnp.arange(t, dtype=jnp.int32)[:, None], (t, TOP_K))
    buf_tok = jnp.zeros((n_pad,), jnp.int32).at[dest.reshape(-1)].set(tok.reshape(-1))
    block_start = jnp.arange(nb, dtype=jnp.int32) * MOE_BLOCK
    block_expert = jnp.minimum(jnp.searchsorted(pad_end, block_start, side='right'), N_EXPERTS - 1).astype(jnp.int32)
    n_used = (pad_end[-1:] // MOE_BLOCK).astype(jnp.int32)

    xs = _gather_rows(x1, buf_tok, n_used)
    d_ff = w_down.shape[1]
    wgu = w_gate_up.reshape(N_EXPERTS, D_MODEL, d_ff, 2).transpose(0, 3, 1, 2).astype(BF16)
    bgu = b_gate_up.astype(F32).reshape(N_EXPERTS, 1, d_ff, 2)
    ys = _expert_ffn(xs, block_expert, n_used, wgu, bgu[..., 0], bgu[..., 1], w_down.astype(BF16),
                     b_down.astype(F32).reshape(N_EXPERTS, 1, D_MODEL))

    out = _final(dest, gates, x1, p.reshape(t, PLE_DIM), w_ple_gate.astype(BF16),
                 b_ple_gate.astype(F32).reshape(1, -1), w_ple.astype(BF16), ln2_g.reshape(1, -1),
                 ln2_b.reshape(1, -1), ys, min(256, t))
    return out.reshape(bsz, seq, D_MODEL)


def kernel(x, p, w_in, b_gate, conv_w, a_log, dt_bias, dn_norm_w, w_branch_a, w_branch_b, w_out, ln1_g, ln1_b,
           router_w, router_b, w_gate_up, b_gate_up, w_down, b_down, w_ple, w_ple_gate, b_ple_gate, ln2_g, ln2_b):
    assert w_in.shape[0] == DEPTH
    return _layer(x, p[0], w_in[0], b_gate[0], conv_w[0], a_log[0], dt_bias[0], dn_norm_w[0], w_branch_a[0],
                  w_branch_b[0], w_out[0], ln1_g[0], ln1_b[0], router_w[0], router_b[0], w_gate_up[0],
                  b_gate_up[0], w_down[0], b_down[0], w_ple[0], w_ple_gate[0], b_ple_gate[0], ln2_g[0], ln2_b[0])
```

```python
import functools

import numpy as np
import jax
import jax.numpy as jnp
from jax import lax
from jax.experimental import pallas as pl
from jax.experimental.pallas import tpu as pltpu

F32 = jnp.float32
BF16 = jnp.bfloat16

D_MODEL = 2048
HEAD_DIM = 128
A_HEADS = 8
DILATION_GROUPS = ((128, 1), (512, 4), (2048, 16))
N_GROUPS = 3
NEG_INF = -1e30
B_QK_HEADS = 8
B_V_HEADS = 16
B_DK = 128
B_DV = 128
CONV_W = 5
RMS_EPS = 1e-6
N_EXPERTS = 32
TOP_K = 4
SWIGLU_ALPHA = 1.702
SWIGLU_LIMIT = 7.0
PLE_DIM = 256
DEPTH = 1
DEEPNORM_ALPHA = (2 * DEPTH) ** 0.25
LN_EPS = 1e-5
A_QKV = N_GROUPS * A_HEADS * HEAD_DIM
B_QK = B_QK_HEADS * B_DK
B_VZ = B_V_HEADS * B_DV
B_GATES = 4 * B_V_HEADS

LANES = 128
N_SIDE = 64
Q_SUB = 128
DN_CHUNK = 64
DN_PAIR = B_V_HEADS // B_QK_HEADS
TRI_BASE = 16
ROW_SUBLANES = 8
IN_PROJ_TM = 1024
MOE_BLOCK = 1024
MOE_TF = 512
VMEM_LIMIT = 56 * 1024 * 1024
assert D_MODEL == 2 * ROW_SUBLANES * LANES


def _cparams(sem):
    return pltpu.CompilerParams(dimension_semantics=sem, vmem_limit_bytes=VMEM_LIMIT)


def _sigmoid(x):
    return 1.0 / (1.0 + jnp.exp(-x))


def _dot(a, b):
    return jnp.dot(a, b, preferred_element_type=F32)


def _dot_nt(a, b):
    return lax.dot_general(a, b, (((1,), (1,)), ((), ())), preferred_element_type=F32)


def _dot_tn(a, b):
    return lax.dot_general(a, b, (((0,), (0,)), ((), ())), preferred_element_type=F32)


def _mm_kernel(x_ref, w_ref, o_ref):
    o_ref[...] = _dot(x_ref[...], w_ref[...]).astype(o_ref.dtype)


def _matmul(x, w, out_dtype, tm, tn):
    m, k = x.shape
    n = w.shape[1]
    return pl.pallas_call(
        _mm_kernel,
        grid=(m // tm, n // tn),
        in_specs=[pl.BlockSpec((tm, k), lambda i, j: (i, 0)),
                  pl.BlockSpec((k, tn), lambda i, j: (0, j))],
        out_specs=pl.BlockSpec((tm, tn), lambda i, j: (i, j)),
        out_shape=jax.ShapeDtypeStruct((m, n), out_dtype),
        compiler_params=_cparams(("parallel", "parallel")),
        name="in_proj",
    )(x, w)


def _mm_classes_kernel(x_ref, w_ref, o_ref, acc_ref, *, dil, rows):
    y = _dot(x_ref[...], w_ref[...])
    for j in range(acc_ref.shape[0]):
        ls = slice(j * LANES, (j + 1) * LANES)
        acc_ref[j] = y[:, ls]
        for c in range(dil):
            o_ref[0, c, :, ls] = acc_ref[j, pl.ds(c, rows, stride=dil), :].astype(o_ref.dtype)


def _matmul_classes(x, w, bsz, dil, tm, tn):
    m, k = x.shape
    n = w.shape[1]
    seq = m // bsz
    tiles_per_seq = seq // tm
    rows = tm // dil
    kern = functools.partial(_mm_classes_kernel, dil=dil, rows=rows)
    return pl.pallas_call(
        kern,
        grid=(m // tm, n // tn),
        in_specs=[pl.BlockSpec((tm, k), lambda i, j: (i, 0)),
                  pl.BlockSpec((k, tn), lambda i, j: (0, j))],
        out_specs=pl.BlockSpec((1, dil, rows, tn), lambda i, j: (i // tiles_per_seq, 0, i % tiles_per_seq, j)),
        out_shape=jax.ShapeDtypeStruct((bsz, dil, seq // dil, n), BF16),
        scratch_shapes=[pltpu.VMEM((tn // LANES, tm, LANES), F32)],
        compiler_params=_cparams(("parallel", "parallel")),
        name=f"in_proj_dil{dil}",
    )(x, w)


def _attn_kernel(q_ref, kp_ref, kc_ref, kn_ref, vp_ref, vc_ref, vn_ref, o_ref, lse_ref, kbuf, vbuf,
                 *, dil, sub_len, tl, slopes):
    i0 = pl.program_id(2) * tl
    kbuf[0:N_SIDE, :] = kp_ref[...]
    kbuf[N_SIDE:N_SIDE + tl, :] = kc_ref[...]
    kbuf[N_SIDE + tl:, :] = kn_ref[...]
    vbuf[0:N_SIDE, :] = vp_ref[...]
    vbuf[N_SIDE:N_SIDE + tl, :] = vc_ref[...]
    vbuf[N_SIDE + tl:, :] = vn_ref[...]
    span = Q_SUB + 2 * N_SIDE
    qq = lax.broadcasted_iota(jnp.int32, (Q_SUB, span), 0)
    kk = lax.broadcasted_iota(jnp.int32, (Q_SUB, span), 1)
    delta = kk - N_SIDE - qq
    absd = jnp.abs(delta)
    band = absd <= N_SIDE
    dist = (dil * absd).astype(F32)
    lane = lax.broadcasted_iota(jnp.int32, (Q_SUB, LANES), 1)
    scale = HEAD_DIM ** -0.5
    for j in range(tl // Q_SUB):
        pos = i0 + (j * Q_SUB - N_SIDE) + kk
        valid = band & (pos >= 0) & (pos < sub_len)
        lse_tile = jnp.zeros((Q_SUB, LANES), F32)
        for h in range(A_HEADS):
            hs = slice(h * HEAD_DIM, (h + 1) * HEAD_DIM)
            q = q_ref[j * Q_SUB:(j + 1) * Q_SUB, hs]
            k = kbuf[j * Q_SUB:j * Q_SUB + span, hs]
            v = vbuf[j * Q_SUB:j * Q_SUB + span, hs]
            s = _dot_nt(q, k) * scale
            s = jnp.where(valid, s - float(slopes[h]) * dist, NEG_INF)
            m = jnp.max(s, axis=1, keepdims=True)
            p = jnp.exp(s - m)
            l = jnp.sum(p, axis=1, keepdims=True)
            o = _dot(p.astype(BF16), v) / l
            o_ref[j * Q_SUB:(j + 1) * Q_SUB, hs] = o.astype(o_ref.dtype)
            lse_tile = jnp.where(lane == h, m + jnp.log(l), lse_tile)
        lse_ref[j * Q_SUB:(j + 1) * Q_SUB, :] = lse_tile


def _attention_group(qkv, gi, dil, slopes):
    bsz, _, sub_len, _ = qkv.shape
    tl = min(512, sub_len)
    assert sub_len % tl == 0 and tl % Q_SUB == 0 and sub_len % N_SIDE == 0
    width = A_HEADS * HEAD_DIM
    halo_per_tile = tl // N_SIDE
    n_halo = sub_len // N_SIDE
    prev = lambda li: jnp.maximum(li * halo_per_tile - 1, 0)
    nxt = lambda li: jnp.minimum((li + 1) * halo_per_tile, n_halo - 1)
    halo = lambda col, rowf: pl.BlockSpec((None, None, N_SIDE, width), lambda b, c, li: (b, c, rowf(li), col))
    cur = lambda col: pl.BlockSpec((None, None, tl, width), lambda b, c, li: (b, c, li, col))
    kern = functools.partial(_attn_kernel, dil=dil, sub_len=sub_len, tl=tl, slopes=tuple(float(s) for s in slopes))
    return pl.pallas_call(
        kern,
        grid=(bsz, dil, sub_len // tl),
        in_specs=[cur(0), halo(1, prev), cur(1), halo(1, nxt), halo(2, prev), cur(2), halo(2, nxt)],
        out_specs=[pl.BlockSpec((None, None, tl, width), lambda b, c, li: (b, c, li, 0)),
                   pl.BlockSpec((None, None, tl, LANES), lambda b, c, li: (b, c, li, 0))],
        out_shape=[jax.ShapeDtypeStruct((bsz, dil, sub_len, width), BF16),
                   jax.ShapeDtypeStruct((bsz, dil, sub_len, LANES), F32)],
        scratch_shapes=[pltpu.VMEM((tl + 2 * N_SIDE, width), BF16),
                        pltpu.VMEM((tl + 2 * N_SIDE, width), BF16)],
        compiler_params=_cparams(("parallel", "parallel", "parallel")),
        name=f"dilated_attn_g{gi}",
    )(qkv, qkv, qkv, qkv, qkv, qkv, qkv)


def _alibi_slopes():
    n = N_GROUPS * A_HEADS
    s = 2.0 ** (-8.0 * np.arange(1, n + 1) / n)
    return s.astype(np.float32).reshape(N_GROUPS, A_HEADS)


def _dn_prep_kernel(prev_ref, cur_ref, next_ref, cw_ref, bab_ref, gp_ref, q_ref, k_ref, v_ref, g_ref, *, ts):
    ti = pl.program_id(1)
    nt = pl.num_programs(1)
    halo = CONV_W // 2
    keep_prev = (ti > 0).astype(F32)
    keep_next = (ti < nt - 1).astype(F32)
    for c in range((2 * B_QK + B_VZ) // LANES):
        cs = slice(c * LANES, (c + 1) * LANES)
        xp = prev_ref[0, :, cs].astype(F32)[8:16] * keep_prev
        xc = cur_ref[0, :, cs].astype(F32)
        xn = next_ref[0, :, cs].astype(F32)[0:8] * keep_next
        ext = jnp.concatenate([xp, xc, xn], axis=0)
        acc = jnp.zeros((ts, LANES), F32)
        for j in range(CONV_W):
            off = 8 - halo + j
            acc = acc + ext[off:off + ts, :] * cw_ref[j:j + 1, cs]
        y = acc * _sigmoid(acc)
        if c < 2 * B_QK // LANES:
            y = y * lax.rsqrt(jnp.sum(y * y, axis=1, keepdims=True) + 1e-6)
        if c < B_QK // LANES:
            q_ref[0, :, cs] = y * (B_DK ** -0.5)
        elif c < 2 * B_QK // LANES:
            k_ref[0, :, c * LANES - B_QK:(c + 1) * LANES - B_QK] = y
        else:
            v_ref[0, :, c * LANES - 2 * B_QK:(c + 1) * LANES - 2 * B_QK] = y
    x = bab_ref[0]
    neg_a = gp_ref[0:1, :]
    dtb = gp_ref[1:2, :]
    is_g = gp_ref[2:3, :] > 0.5
    z = x + dtb
    softplus = jnp.maximum(z, 0.0) + jnp.log(1.0 + jnp.exp(-jnp.abs(z)))
    g_ref[0] = jnp.where(is_g, neg_a * softplus, _sigmoid(x))


def _dn_prep(u_b, conv_w, bab, gate_params):
    bsz, seq, _ = u_b.shape
    ts = min(256, seq)
    cq = 2 * B_QK + B_VZ
    nhalo = seq // 16
    per = ts // 16
    kern = functools.partial(_dn_prep_kernel, ts=ts)
    return pl.pallas_call(
        kern,
        grid=(bsz, seq // ts),
        in_specs=[pl.BlockSpec((1, 16, cq), lambda b, t: (b, jnp.maximum(t * per - 1, 0), 0)),
                  pl.BlockSpec((1, ts, cq), lambda b, t: (b, t, 0)),
                  pl.BlockSpec((1, 16, cq), lambda b, t: (b, jnp.minimum((t + 1) * per, nhalo - 1), 0)),
                  pl.BlockSpec((8, cq), lambda b, t: (0, 0)),
                  pl.BlockSpec((1, ts, LANES), lambda b, t: (b, t, 0)),
                  pl.BlockSpec((8, LANES), lambda b, t: (0, 0))],
        out_specs=[pl.BlockSpec((1, ts, B_QK), lambda b, t: (b, t, 0)),
                   pl.BlockSpec((1, ts, B_QK), lambda b, t: (b, t, 0)),
                   pl.BlockSpec((1, ts, B_VZ), lambda b, t: (b, t, 0)),
                   pl.BlockSpec((1, ts, LANES), lambda b, t: (b, t, 0))],
        out_shape=[jax.ShapeDtypeStruct((bsz, seq, B_QK), F32),
                   jax.ShapeDtypeStruct((bsz, seq, B_QK), F32),
                   jax.ShapeDtypeStruct((bsz, seq, B_VZ), F32),
                   jax.ShapeDtypeStruct((bsz, seq, LANES), F32)],
        compiler_params=_cparams(("parallel", "parallel")),
        name="dn_prep",
    )(u_b, u_b, u_b, conv_w, bab, gate_params)


def _dn_intra_kernel(q_ref, k_ref, v_ref, g_ref, gt_ref, u_ref, wq_ref, kq_ref, et_ref, *, c):
    n_units = 2 * DN_PAIR
    w4 = n_units * c
    hp = lax.Precision.HIGHEST
    bf = lambda t_: t_.astype(BF16)
    ii = lax.broadcasted_iota(jnp.int32, (c, w4), 0)
    ll = lax.broadcasted_iota(jnp.int32, (c, w4), 1)
    jj = ll % c
    ub = ll // c
    ub_row = ub[0:1, :]
    lo = jnp.where(ub >= DN_PAIR, jj - ii, ii - jj)
    incl = lo >= 0
    strict = lo > 0
    eye = (ii == jj).astype(F32)
    blk = (ii // TRI_BASE) == (jj // TRI_BASE)

    def pack(parts, sel):
        out = parts[n_units - 1]
        for u_ in range(n_units - 2, -1, -1):
            out = jnp.where(sel == u_, parts[u_], out)
        return out

    unit_mask = [jnp.where(ub == u_, 1.0, 0.0).astype(BF16) for u_ in range(n_units)]

    def block_diag(y16):
        return jnp.concatenate([y16 * m_ for m_ in unit_mask], axis=0)

    def mm(xs, ys):
        return [_dot(bf(x_), block_diag(bf(y_))) for x_, y_ in zip(xs, ys)]

    ri = lax.broadcasted_iota(jnp.int32, (c, c), 0)
    ci = lax.broadcasted_iota(jnp.int32, (c, c), 1)
    g_all = g_ref[0]
    gc_dir = [jnp.dot((ci <= ri).astype(F32), g_all, precision=hp, preferred_element_type=F32),
              jnp.dot((ci >= ri).astype(F32), g_all, precision=hp, preferred_element_type=F32)]
    tri4 = (lo <= 0).astype(F32)
    gcr_all = jnp.dot(gt_ref[0, 0], tri4, precision=hp, preferred_element_type=F32)
    tot_all = jnp.sum(g_all, axis=0, keepdims=True)

    pairs = list(range(B_V_HEADS // DN_PAIR))
    qs_, ks_, lms, qkms, betas, egcs, kscales, etots = [], [], [], [], [], [], [], []
    for p in pairs:
        cs = slice(p * B_DK, (p + 1) * B_DK)
        q = q_ref[0, :, cs]
        k = k_ref[0, :, cs]
        k16 = bf(k)
        k4 = jnp.concatenate([k16] * n_units, axis=0)
        gram = _dot_nt(k16, k4)
        qk = _dot_nt(bf(q), k4)
        beta_u, gc_u, gcr_u, tot_u = [], [], [], []
        for u_ in range(n_units):
            d_, e_ = divmod(u_, DN_PAIR)
            h = p * DN_PAIR + e_
            bl = d_ * 2 * B_V_HEADS + h
            gl = bl + B_V_HEADS
            beta_u.append(g_all[:, bl:bl + 1])
            gc_u.append(gc_dir[d_][:, gl:gl + 1])
            gcr_u.append(gcr_all[gl:gl + 1, :])
            tot_u.append(tot_all[:, gl:gl + 1])
        gc_p = pack(gc_u, ub)
        gcr_p = pack(gcr_u, ub_row)
        tot_p = pack(tot_u, ub_row)
        dec = jnp.where(incl, jnp.exp(jnp.where(incl, gc_p - gcr_p, 0.0)), 0.0)
        lms.append(jnp.where(strict, pack(beta_u, ub) * gram * dec, 0.0))
        qkms.append(jnp.where(incl, qk * dec, 0.0))
        qs_.append(q)
        ks_.append(k)
        betas.append(beta_u)
        egcs.append([jnp.exp(g_) for g_ in gc_u])
        kscales.append(jnp.exp(tot_p - gcr_p))
        etots.append([jnp.exp(t_) for t_ in tot_u])

    d1 = [jnp.where(blk, lm, 0.0) for lm in lms]
    d2 = mm(d1, d1)
    d4 = mm(d2, d2)
    d8 = mm(d4, d4)
    tm_ = [eye - d_ for d_ in d1]
    for dk in (d2, d4, d8):
        tm_ = [a + b for a, b in zip(tm_, mm(tm_, dk))]
    size = TRI_BASE
    while size < c:
        off = ((ii // (2 * size)) == (jj // (2 * size))) & ((ii // size) != (jj // size))
        cm = [jnp.where(off, lm, 0.0) for lm in lms]
        pc = mm(tm_, cm)
        tm_ = [a - b for a, b in zip(tm_, mm(pc, tm_))]
        size *= 2

    lane2 = lax.broadcasted_iota(jnp.int32, (1, DN_PAIR * B_DV), 1)
    for p in pairs:
        q, k = qs_[p], ks_[p]
        rows = []
        for u_ in range(n_units):
            d_, e_ = divmod(u_, DN_PAIR)
            h = p * DN_PAIR + e_
            v = v_ref[0, :, h * B_DV:(h + 1) * B_DV]
            beta = betas[p][u_]
            rows.append(jnp.concatenate([v * beta, k * (beta * egcs[p][u_])], axis=1))
        uw = _dot(block_diag(bf(tm_[p])), bf(jnp.concatenate(rows, axis=0)))
        k_t = jnp.transpose(jnp.concatenate([k] * DN_PAIR, axis=0))
        for d_ in range(2):
            u0, u1 = d_ * DN_PAIR, d_ * DN_PAIR + 1
            u_ref[d_, 0, 0, p] = jnp.concatenate([uw[u0 * c:(u0 + 1) * c, :B_DV], uw[u1 * c:(u1 + 1) * c, :B_DV]], axis=1)
            wq = jnp.concatenate([uw[u0 * c:(u0 + 1) * c, B_DV:], q * egcs[p][u0],
                                  uw[u1 * c:(u1 + 1) * c, B_DV:], q * egcs[p][u1]], axis=0)
            wq_ref[d_, 0, 0, p] = bf(wq)
            ls = slice(d_ * DN_PAIR * c, (d_ + 1) * DN_PAIR * c)
            kq = jnp.concatenate([k_t * kscales[p][:, ls], qkms[p][:, ls]], axis=0)
            kq_ref[d_, 0, 0, p] = bf(kq)
            et_ref[d_, 0, 0, p:p + 1, :] = jnp.where(lane2 < B_DV, etots[p][u0], etots[p][u1])


def _dn_intra(qn, kn, vn, gts, gts_t):
    bsz, seq, _ = qn.shape
    c = DN_CHUNK
    assert DN_PAIR * c == B_DK and seq % c == 0
    n = seq // c
    npair = B_V_HEADS // DN_PAIR
    wide = DN_PAIR * B_DV
    kern = functools.partial(_dn_intra_kernel, c=c)
    out5 = lambda r, cdim: pl.BlockSpec((2, 1, 1, npair, r, cdim), lambda b, i: (0, b, i, 0, 0, 0))
    return pl.pallas_call(
        kern,
        grid=(bsz, n),
        in_specs=[pl.BlockSpec((1, c, B_QK), lambda b, i: (b, i, 0)),
                  pl.BlockSpec((1, c, B_QK), lambda b, i: (b, i, 0)),
                  pl.BlockSpec((1, c, B_VZ), lambda b, i: (b, i, 0)),
                  pl.BlockSpec((1, c, LANES), lambda b, i: (b, i, 0)),
                  pl.BlockSpec((1, 1, LANES, c), lambda b, i: (b, i, 0, 0))],
        out_specs=[out5(c, wide), out5(2 * DN_PAIR * c, B_DK), out5(B_DK + c, DN_PAIR * c),
                   pl.BlockSpec((2, 1, 1, npair, wide), lambda b, i: (0, b, i, 0, 0))],
        out_shape=[jax.ShapeDtypeStruct((2, bsz, n, npair, c, wide), F32),
                   jax.ShapeDtypeStruct((2, bsz, n, npair, 2 * DN_PAIR * c, B_DK), BF16),
                   jax.ShapeDtypeStruct((2, bsz, n, npair, B_DK + c, DN_PAIR * c), BF16),
                   jax.ShapeDtypeStruct((2, bsz, n, npair, wide), F32)],
        compiler_params=_cparams(("parallel", "parallel")),
        name="dn_intra",
    )(qn, kn, vn, gts, gts_t)


def _dn_state_kernel(uf_ref, wqf_ref, kqf_ref, etf_ref, ub_ref, wqb_ref, kqb_ref, etb_ref, of_ref, ob_ref, s_ref, *, c):
    @pl.when(pl.program_id(1) == 0)
    def _():
        s_ref[...] = jnp.zeros_like(s_ref)

    npair = B_V_HEADS // DN_PAIR
    bf = lambda t_: t_.astype(BF16)
    chains = [(d_, p) for d_ in range(2) for p in range(npair)]
    refs = ((uf_ref, wqf_ref, kqf_ref, etf_ref, of_ref), (ub_ref, wqb_ref, kqb_ref, etb_ref, ob_ref))
    states = [s_ref[d_, p] for d_, p in chains]
    a_res = [_dot(refs[d_][1][0, 0, 0, p], bf(s_)) for (d_, p), s_ in zip(chains, states)]
    zero = jnp.zeros((c, B_DV), F32)
    b_res = []
    for (d_, p), a_ in zip(chains, a_res):
        u = refs[d_][0][0, 0, 0, p]
        v0 = u[:, :B_DV] - a_[0:c, :B_DV]
        v1 = u[:, B_DV:] - a_[2 * c:3 * c, B_DV:]
        bd_v = jnp.concatenate([jnp.concatenate([v0, zero], axis=1), jnp.concatenate([zero, v1], axis=1)], axis=0)
        b_res.append(_dot(refs[d_][2][0, 0, 0, p], bf(bd_v)))
    for (d_, p), a_, b_, s_ in zip(chains, a_res, b_res, states):
        s_ref[d_, p] = s_ * refs[d_][3][0, 0, 0, p:p + 1, :] + b_[:B_DK]
        o_ref = refs[d_][4]
        o_ref[0, :, (2 * p) * B_DV:(2 * p + 1) * B_DV] = a_[c:2 * c, :B_DV] + b_[B_DK:, :B_DV]
        o_ref[0, :, (2 * p + 1) * B_DV:(2 * p + 2) * B_DV] = a_[3 * c:4 * c, B_DV:] + b_[B_DK:, B_DV:]


def _dn_state(u_all, wq_all, kq_all, et_all, seq):
    _, bsz, n, npair, c, wide = u_all.shape
    kern = functools.partial(_dn_state_kernel, c=c)
    fwd = lambda r, cdim: pl.BlockSpec((1, 1, 1, npair, r, cdim), lambda b, i: (0, b, i, 0, 0, 0))
    bwd = lambda r, cdim: pl.BlockSpec((1, 1, 1, npair, r, cdim), lambda b, i: (1, b, n - 1 - i, 0, 0, 0))
    et_f = pl.BlockSpec((1, 1, 1, npair, wide), lambda b, i: (0, b, i, 0, 0))
    et_b = pl.BlockSpec((1, 1, 1, npair, wide), lambda b, i: (1, b, n - 1 - i, 0, 0))
    shapes = ((c, wide), (2 * DN_PAIR * c, B_DK), (B_DK + c, DN_PAIR * c))
    return pl.pallas_call(
        kern,
        grid=(bsz, n),
        in_specs=[fwd(*shapes[0]), fwd(*shapes[1]), fwd(*shapes[2]), et_f,
                  bwd(*shapes[0]), bwd(*shapes[1]), bwd(*shapes[2]), et_b],
        out_specs=[pl.BlockSpec((1, c, B_VZ), lambda b, i: (b, i, 0)),
                   pl.BlockSpec((1, c, B_VZ), lambda b, i: (b, n - 1 - i, 0))],
        out_shape=[jax.ShapeDtypeStruct((bsz, seq, B_VZ), F32), jax.ShapeDtypeStruct((bsz, seq, B_VZ), F32)],
        scratch_shapes=[pltpu.VMEM((2, npair, B_DK, wide), F32)],
        compiler_params=_cparams(("parallel", "arbitrary")),
        name="dn_state",
    )(u_all, wq_all, kq_all, et_all, u_all, wq_all, kq_all, et_all)


def _branch_a_kernel(o0_ref, o1_ref, o2_ref, l0_ref, l1_ref, l2_ref, w_ref, gp_ref, bg_ref, out_ref, o_sc, l_sc,
                     *, dils, tm):
    for g, (o_ref, l_ref) in enumerate(((o0_ref, l0_ref), (o1_ref, l1_ref), (o2_ref, l2_ref))):
        r = dils[g]
        for c in range(r):
            l_sc[g, pl.ds(c, tm // r, stride=r), :] = l_ref[c]
            for h in range(A_HEADS):
                o_sc[g, h, pl.ds(c, tm // r, stride=r), :] = o_ref[c, :, h * HEAD_DIM:(h + 1) * HEAD_DIM].astype(F32)
    ls = [l_sc[g] for g in range(N_GROUPS)]
    m = jnp.maximum(jnp.maximum(ls[0], ls[1]), ls[2])
    es = [jnp.exp(l - m) for l in ls]
    den = es[0] + es[1] + es[2]
    ws = [e / den for e in es]
    parts = []
    for h in range(A_HEADS):
        hs = slice(h * HEAD_DIM, (h + 1) * HEAD_DIM)
        acc = ws[0][:, h:h + 1] * o_sc[0, h]
        acc = acc + ws[1][:, h:h + 1] * o_sc[1, h]
        acc = acc + ws[2][:, h:h + 1] * o_sc[2, h]
        parts.append(acc.astype(BF16))
    oa = jnp.concatenate(parts, axis=1)
    y = _dot(oa, w_ref[...])
    out_ref[...] = _sigmoid(gp_ref[...] + bg_ref[...]) * y


def _branch_a(outs, lses, w_a, gpre, b_gate, tm):
    bsz = outs[0].shape[0]
    dils = tuple(o.shape[1] for o in outs)
    seq = dils[0] * outs[0].shape[2]
    t = bsz * seq
    per_seq = seq // tm
    wd = A_HEADS * HEAD_DIM
    cls = lambda r, width: pl.BlockSpec((None, r, tm // r, width), lambda i: (i // per_seq, 0, i % per_seq, 0))
    kern = functools.partial(_branch_a_kernel, dils=dils, tm=tm)
    return pl.pallas_call(
        kern,
        grid=(t // tm,),
        in_specs=[cls(dils[0], wd), cls(dils[1], wd), cls(dils[2], wd),
                  cls(dils[0], LANES), cls(dils[1], LANES), cls(dils[2], LANES),
                  pl.BlockSpec((wd, D_MODEL), lambda i: (0, 0)),
                  pl.BlockSpec((tm, D_MODEL), lambda i: (i, 0)),
                  pl.BlockSpec((1, D_MODEL), lambda i: (0, 0))],
        out_specs=pl.BlockSpec((tm, D_MODEL), lambda i: (i, 0)),
        out_shape=jax.ShapeDtypeStruct((t, D_MODEL), F32),
        scratch_shapes=[pltpu.VMEM((N_GROUPS, A_HEADS, tm, HEAD_DIM), F32), pltpu.VMEM((N_GROUPS, tm, LANES), F32)],
        compiler_params=_cparams(("parallel",)),
        name="branch_a",
    )(*outs, *lses, w_a, gpre, b_gate)


def _branch_b_kernel(of_ref, ob_ref, z_ref, nw_ref, w_ref, gp_ref, bg_ref, a_ref, out_ref):
    nw = nw_ref[...]
    parts = []
    for h in range(B_V_HEADS):
        hs = slice(h * B_DV, (h + 1) * B_DV)
        o = of_ref[:, hs] + ob_ref[:, hs]
        z = z_ref[:, hs].astype(F32)
        o = o * lax.rsqrt(jnp.mean(o * o, axis=1, keepdims=True) + RMS_EPS) * nw * (z * _sigmoid(z))
        parts.append(o.astype(BF16))
    ob = jnp.concatenate(parts, axis=1)
    y = _dot(ob, w_ref[...])
    out_ref[...] = (a_ref[...] + _sigmoid(gp_ref[...] + bg_ref[...]) * y).astype(out_ref.dtype)


def _branch_b(o_f, o_b, u_b2d, norm_w, w_b, gpre, b_gate, a_part, tm):
    t = a_part.shape[0]
    return pl.pallas_call(
        _branch_b_kernel,
        grid=(t // tm,),
        in_specs=[pl.BlockSpec((tm, B_VZ), lambda i: (i, 0)),
                  pl.BlockSpec((tm, B_VZ), lambda i: (i, 0)),
                  pl.BlockSpec((tm, B_VZ), lambda i: (i, (2 * B_QK + B_VZ) // B_VZ)),
                  pl.BlockSpec((1, B_DV), lambda i: (0, 0)),
                  pl.BlockSpec((B_VZ, D_MODEL), lambda i: (0, 0)),
                  pl.BlockSpec((tm, D_MODEL), lambda i: (i, 1)),
                  pl.BlockSpec((1, D_MODEL), lambda i: (0, 1)),
                  pl.BlockSpec((tm, D_MODEL), lambda i: (i, 0))],
        out_specs=pl.BlockSpec((tm, D_MODEL), lambda i: (i, 0)),
        out_shape=jax.ShapeDtypeStruct((t, D_MODEL), BF16),
        compiler_params=_cparams(("parallel",)),
        name="branch_b",
    )(o_f, o_b, u_b2d, norm_w, w_b, gpre, b_gate, a_part)


def _layer_norm(y, g, b):
    mu = jnp.mean(y, axis=1, keepdims=True)
    yc = y - mu
    var = jnp.mean(yc * yc, axis=1, keepdims=True)
    return yc * lax.rsqrt(var + LN_EPS) * g + b


def _out_ln_kernel(m_ref, w_ref, x_ref, g_ref, b_ref, out_ref):
    mix = _dot(m_ref[...], w_ref[...])
    out_ref[...] = _layer_norm(DEEPNORM_ALPHA * x_ref[...] + mix, g_ref[...], b_ref[...])


def _out_ln(merged, w_out, x2d, ln_g, ln_b, tm):
    t = merged.shape[0]
    vec = pl.BlockSpec((1, D_MODEL), lambda i: (0, 0))
    return pl.pallas_call(
        _out_ln_kernel,
        grid=(t // tm,),
        in_specs=[pl.BlockSpec((tm, D_MODEL), lambda i: (i, 0)),
                  pl.BlockSpec((D_MODEL, D_MODEL), lambda i: (0, 0)),
                  pl.BlockSpec((tm, D_MODEL), lambda i: (i, 0)), vec, vec],
        out_specs=pl.BlockSpec((tm, D_MODEL), lambda i: (i, 0)),
        out_shape=jax.ShapeDtypeStruct((t, D_MODEL), F32),
        compiler_params=_cparams(("parallel",)),
        name="out_proj_ln1",
    )(merged, w_out, x2d, ln_g, ln_b)


def _router_kernel(x_ref, rw_ref, rb_ref, idx_ref, gate_ref, rank_ref, cnt_ref, carry, *, tm):
    @pl.when(pl.program_id(0) == 0)
    def _():
        carry[...] = jnp.zeros_like(carry)

    lane = lax.broadcasted_iota(jnp.int32, (tm, LANES), 1)
    lane_f = lane.astype(F32)
    logits = jnp.dot(x_ref[...], rw_ref[...], precision=lax.Precision.HIGHEST, preferred_element_type=F32) + rb_ref[...]
    cur = jnp.where(lane < N_EXPERTS, logits, -jnp.inf)
    vals, idxs = [], []
    for _k in range(TOP_K):
        m = jnp.max(cur, axis=1, keepdims=True)
        idx = jnp.min(jnp.where(cur == m, lane_f, float(LANES)), axis=1, keepdims=True).astype(jnp.int32)
        vals.append(m)
        idxs.append(idx)
        cur = jnp.where(lane == idx, -jnp.inf, cur)
    es = [jnp.exp(v - vals[0]) for v in vals]
    den = es[0] + es[1] + es[2] + es[3]
    onehot = jnp.zeros((tm, LANES), F32)
    for idx in idxs:
        onehot = onehot + (lane == idx).astype(F32)
    ri = lax.broadcasted_iota(jnp.int32, (tm, tm), 0)
    ci = lax.broadcasted_iota(jnp.int32, (tm, tm), 1)
    before = (ci < ri).astype(BF16)
    prefix = _dot(before, onehot.astype(BF16)) + carry[0:1, :]
    idx_out = jnp.zeros((tm, LANES), jnp.int32)
    gate_out = jnp.zeros((tm, LANES), F32)
    rank_out = jnp.zeros((tm, LANES), jnp.int32)
    for k in range(TOP_K):
        rk = jnp.sum(jnp.where(lane == idxs[k], prefix, 0.0), axis=1, keepdims=True)
        idx_out = jnp.where(lane == k, idxs[k], idx_out)
        gate_out = jnp.where(lane == k, es[k] / den, gate_out)
        rank_out = jnp.where(lane == k, rk.astype(jnp.int32), rank_out)
    idx_ref[...] = idx_out
    gate_ref[...] = gate_out
    rank_ref[...] = rank_out
    total = carry[0:1, :] + jnp.sum(onehot, axis=0, keepdims=True)
    carry[...] = jnp.broadcast_to(total, carry.shape)
    cnt_ref[...] = jnp.broadcast_to(total, cnt_ref.shape)


def _router(x1, rw, rb, tm):
    t = x1.shape[0]
    row = pl.BlockSpec((tm, LANES), lambda i: (i, 0))
    kern = functools.partial(_router_kernel, tm=tm)
    return pl.pallas_call(
        kern,
        grid=(t // tm,),
        in_specs=[pl.BlockSpec((tm, D_MODEL), lambda i: (i, 0)),
                  pl.BlockSpec((D_MODEL, LANES), lambda i: (0, 0)),
                  pl.BlockSpec((1, LANES), lambda i: (0, 0))],
        out_specs=[row, row, row, pl.BlockSpec((8, LANES), lambda i: (0, 0))],
        out_shape=[jax.ShapeDtypeStruct((t, LANES), jnp.int32),
                   jax.ShapeDtypeStruct((t, LANES), F32),
                   jax.ShapeDtypeStruct((t, LANES), jnp.int32),
                   jax.ShapeDtypeStruct((8, LANES), F32)],
        scratch_shapes=[pltpu.VMEM((8, LANES), F32)],
        compiler_params=_cparams(("arbitrary",)),
        name="router",
    )(x1, rw, rb)


def _pack_rows(src_ref, dst_ref):
    half = D_MODEL // 2
    n = src_ref.shape[0]
    for s in range(ROW_SUBLANES):
        lo = pltpu.bitcast(src_ref[:, s * LANES:(s + 1) * LANES].astype(BF16).astype(F32), jnp.uint32)
        hi = pltpu.bitcast(src_ref[:, half + s * LANES:half + (s + 1) * LANES].astype(BF16).astype(F32), jnp.uint32)
        dst_ref[pl.ds(s, n, stride=ROW_SUBLANES), :] = (lo >> 16) | hi


def _unpack_words(w):
    return pltpu.bitcast(w << 16, F32), pltpu.bitcast(w & jnp.uint32(0xFFFF0000), F32)


def _tile(ref, i):
    start = i * ROW_SUBLANES
    if not isinstance(i, int):
        start = pl.multiple_of(start, ROW_SUBLANES)
    return ref.at[pl.ds(start, ROW_SUBLANES)]


def _tile_copy(src, i, dst, j, sem):
    return pltpu.make_async_copy(_tile(src, i), _tile(dst, j), sem)


def _wait_tiles(hbm, n, sem):
    pltpu.make_async_copy(hbm.at[pl.ds(0, n * ROW_SUBLANES)], hbm.at[pl.ds(0, n * ROW_SUBLANES)], sem).wait()


def _dispatch_kernel(pad_start_ref, pad_cnt_ref, dest_ref, x_ref, xs_hbm, stage, zero, sem, *, tm, nt):
    i = pl.program_id(0)

    @pl.when(i < nt)
    def _():
        _pack_rows(x_ref, stage)

        def issue(t_, carry):
            src = _tile(stage, t_)
            for k in range(TOP_K):
                pltpu.make_async_copy(src, _tile(xs_hbm, dest_ref[0, 0, t_ * TOP_K + k]), sem).start()
            return carry

        lax.fori_loop(0, tm, issue, 0, unroll=2)
        _wait_tiles(xs_hbm, tm * TOP_K, sem)

    @pl.when(i >= nt)
    def _():
        e = i - nt
        zero[...] = jnp.zeros_like(zero)
        start = pad_start_ref[e]
        cnt = pad_cnt_ref[e]

        def issue(r, carry):
            _tile_copy(zero, 0, xs_hbm, start + r, sem).start()
            return carry

        lax.fori_loop(0, cnt, issue, 0)

        def wait(r, carry):
            _tile_copy(zero, 0, xs_hbm, start, sem).wait()
            return carry

        lax.fori_loop(0, cnt, wait, 0)


def _dispatch(x1, dest, pad_from, pad_cnt, n_pad, tm):
    t = x1.shape[0]
    nt = t // tm
    kern = functools.partial(_dispatch_kernel, tm=tm, nt=nt)
    grid_spec = pltpu.PrefetchScalarGridSpec(
        num_scalar_prefetch=2,
        grid=(nt + N_EXPERTS,),
        in_specs=[pl.BlockSpec((1, 1, tm * TOP_K), lambda i, ps, pc: (jnp.minimum(i, nt - 1), 0, 0),
                               memory_space=pltpu.SMEM),
                  pl.BlockSpec((tm, D_MODEL), lambda i, ps, pc: (jnp.minimum(i, nt - 1), 0))],
        out_specs=pl.BlockSpec(memory_space=pl.ANY),
        scratch_shapes=[pltpu.VMEM((tm * ROW_SUBLANES, LANES), jnp.uint32),
                        pltpu.VMEM((ROW_SUBLANES, LANES), jnp.uint32),
                        pltpu.SemaphoreType.DMA(())],
    )
    return pl.pallas_call(
        kern,
        grid_spec=grid_spec,
        out_shape=jax.ShapeDtypeStruct((n_pad * ROW_SUBLANES, LANES), jnp.uint32),
        compiler_params=_cparams(("arbitrary",)),
        name="moe_dispatch",
    )(pad_from, pad_cnt, dest.reshape(nt, 1, tm * TOP_K), x1)


def _ffn_kernel(be_ref, nused_ref, x_ref, wg_ref, wu_ref, bg_ref, bu_ref, wd_ref, bd_ref, o_ref, acc, xb):
    i = pl.program_id(0)
    f = pl.program_id(1)
    nf = pl.num_programs(1)
    half = D_MODEL // 2

    @pl.when(f == 0)
    def _():
        acc[...] = jnp.broadcast_to(bd_ref[0], acc.shape)
        for s in range(ROW_SUBLANES):
            lo, hi = _unpack_words(x_ref[pl.ds(s, MOE_BLOCK, stride=ROW_SUBLANES), :])
            xb[:, s * LANES:(s + 1) * LANES] = lo.astype(BF16)
            xb[:, half + s * LANES:half + (s + 1) * LANES] = hi.astype(BF16)

    @pl.when(i < nused_ref[0])
    def _():
        x = xb[...]
        g = _dot(x, wg_ref[0, 0]) + bg_ref[0]
        u = _dot(x, wu_ref[0, 0]) + bu_ref[0]
        gate = jnp.minimum(g, SWIGLU_LIMIT)
        up = jnp.clip(u, -SWIGLU_LIMIT, SWIGLU_LIMIT)
        act = (up + 1.0) * gate * _sigmoid(gate * SWIGLU_ALPHA)
        acc[...] += _dot(act.astype(BF16), wd_ref[0])

    @pl.when(f == nf - 1)
    def _():
        _pack_rows(acc, o_ref)


def _expert_ffn(xs, block_expert, n_used, wgu, bg, bu, wd, bd):
    n_pad = xs.shape[0] // ROW_SUBLANES
    nb = n_pad // MOE_BLOCK
    d_ff = wgu.shape[3]
    nf = d_ff // MOE_TF
    row_blk = (MOE_BLOCK * ROW_SUBLANES, LANES)
    grid_spec = pltpu.PrefetchScalarGridSpec(
        num_scalar_prefetch=2,
        grid=(nb, nf),
        in_specs=[pl.BlockSpec(row_blk, lambda i, f, be, nu: (jnp.minimum(i, nu[0] - 1), 0)),
                  pl.BlockSpec((1, 1, D_MODEL, MOE_TF), lambda i, f, be, nu: (be[i], 0, 0, f)),
                  pl.BlockSpec((1, 1, D_MODEL, MOE_TF), lambda i, f, be, nu: (be[i], 1, 0, f)),
                  pl.BlockSpec((1, 1, MOE_TF), lambda i, f, be, nu: (be[i], 0, f)),
                  pl.BlockSpec((1, 1, MOE_TF), lambda i, f, be, nu: (be[i], 0, f)),
                  pl.BlockSpec((1, MOE_TF, D_MODEL), lambda i, f, be, nu: (be[i], f, 0)),
                  pl.BlockSpec((1, 1, D_MODEL), lambda i, f, be, nu: (be[i], 0, 0))],
        out_specs=pl.BlockSpec(row_blk, lambda i, f, be, nu: (i, 0)),
        scratch_shapes=[pltpu.VMEM((MOE_BLOCK, D_MODEL), F32), pltpu.VMEM((MOE_BLOCK, D_MODEL), BF16)],
    )
    return pl.pallas_call(
        _ffn_kernel,
        grid_spec=grid_spec,
        out_shape=jax.ShapeDtypeStruct((n_pad * ROW_SUBLANES, LANES), jnp.uint32),
        compiler_params=_cparams(("arbitrary", "arbitrary")),
        name="expert_ffn",
    )(block_expert, n_used, xs, wgu, wgu, bg, bu, wd, bd)


def _final_kernel(dest_ref, gate_ref, x_ref, p_ref, wpg_ref, bpg_ref, wple_ref, g_ref, b_ref, ys_hbm, out_ref,
                  rows, sem, *, tm):
    def issue(r, carry):
        _tile_copy(ys_hbm, dest_ref[0, 0, r], rows, r, sem).start()
        return carry

    lax.fori_loop(0, tm * TOP_K, issue, 0, unroll=8)
    x = x_ref[...]
    pg = _dot(x.astype(BF16), wpg_ref[...]) + bpg_ref[...]
    ple = _sigmoid(pg) * _dot(p_ref[...].astype(BF16), wple_ref[...])
    _wait_tiles(ys_hbm, tm * TOP_K, sem)
    gates = gate_ref[...]
    lo_parts, hi_parts = [], []
    for s in range(ROW_SUBLANES):
        acc_lo = acc_hi = None
        for k in range(TOP_K):
            lo, hi = _unpack_words(rows[pl.ds(k * ROW_SUBLANES + s, tm, stride=TOP_K * ROW_SUBLANES), :])
            g = gates[:, k:k + 1]
            acc_lo = g * lo if acc_lo is None else acc_lo + g * lo
            acc_hi = g * hi if acc_hi is None else acc_hi + g * hi
        lo_parts.append(acc_lo)
        hi_parts.append(acc_hi)
    y = jnp.concatenate(lo_parts + hi_parts, axis=1)
    out_ref[...] = _layer_norm(DEEPNORM_ALPHA * x + y + ple, g_ref[...], b_ref[...])


def _final(dest, gates, x1, p2d, w_pg, b_pg, w_ple, ln_g, ln_b, ys, tm):
    t = x1.shape[0]
    vec = pl.BlockSpec((1, D_MODEL), lambda i: (0, 0))
    kern = functools.partial(_final_kernel, tm=tm)
    return pl.pallas_call(
        kern,
        grid=(t // tm,),
        in_specs=[pl.BlockSpec((1, 1, tm * TOP_K), lambda i: (i, 0, 0), memory_space=pltpu.SMEM),
                  pl.BlockSpec((tm, LANES), lambda i: (i, 0)),
                  pl.BlockSpec((tm, D_MODEL), lambda i: (i, 0)),
                  pl.BlockSpec((tm, PLE_DIM), lambda i: (i, 0)),
                  pl.BlockSpec((D_MODEL, D_MODEL), lambda i: (0, 0)), vec,
                  pl.BlockSpec((PLE_DIM, D_MODEL), lambda i: (0, 0)), vec, vec,
                  pl.BlockSpec(memory_space=pl.ANY)],
        out_specs=pl.BlockSpec((tm, D_MODEL), lambda i: (i, 0)),
        out_shape=jax.ShapeDtypeStruct((t, D_MODEL), F32),
        scratch_shapes=[pltpu.VMEM((tm * TOP_K * ROW_SUBLANES, LANES), jnp.uint32), pltpu.SemaphoreType.DMA(())],
        compiler_params=_cparams(("arbitrary",)),
        name="combine_ple_ln2",
    )(dest.reshape(t // tm, 1, tm * TOP_K), gates, x1, p2d, w_pg, b_pg, w_ple, ln_g, ln_b, ys)


def _row_tile(t):
    return min(512, t)


def _layer(x, p, w_in, b_gate, conv_w, a_log, dt_bias, dn_norm_w, w_branch_a, w_branch_b, w_out, ln1_g, ln1_b,
           router_w, router_b, w_gate_up, b_gate_up, w_down, b_down, w_ple, w_ple_gate, b_ple_gate, ln2_g, ln2_b):
    bsz, seq, _ = x.shape
    t = bsz * seq
    tm = _row_tile(t)
    x2d = x.reshape(t, D_MODEL)
    xb = x2d.astype(BF16)

    c_a = 3 * A_QKV
    c_b = 2 * B_QK + 2 * B_VZ
    w_bf = w_in.astype(BF16)
    tp = min(IN_PROJ_TM, seq)
    u_b = _matmul(xb, w_bf[:, c_a:c_a + c_b], BF16, tp, 512)
    w_bab = jnp.pad(w_bf[:, c_a + c_b:c_a + c_b + B_GATES], ((0, 0), (0, LANES - B_GATES)))
    bab = _matmul(xb, w_bab, F32, tp, LANES)
    gpre = _matmul(xb, w_bf[:, c_a + c_b + B_GATES:], F32, tp, 512)

    slopes = _alibi_slopes()
    gw = A_HEADS * HEAD_DIM
    outs, lses = [], []
    for gi, (_win, dil) in enumerate(DILATION_GROUPS):
        w_g = jnp.concatenate([w_bf[:, part * A_QKV + gi * gw:part * A_QKV + (gi + 1) * gw] for part in range(3)], axis=1)
        o_g, l_g = _attention_group(_matmul_classes(xb, w_g, bsz, dil, tp, 512), gi, dil, slopes[gi])
        outs.append(o_g)
        lses.append(l_g)

    cw = jnp.pad(conv_w.astype(F32), ((0, 8 - CONV_W), (0, 0)))
    lane_is_g = (np.arange(LANES) % (2 * B_V_HEADS) >= B_V_HEADS) & (np.arange(LANES) < B_GATES)
    neg_a = jnp.zeros((LANES,), F32).at[B_V_HEADS:2 * B_V_HEADS].set(-jnp.exp(a_log[0].astype(F32)))
    neg_a = neg_a.at[3 * B_V_HEADS:4 * B_V_HEADS].set(-jnp.exp(a_log[1].astype(F32)))
    dtb = jnp.zeros((LANES,), F32).at[B_V_HEADS:2 * B_V_HEADS].set(dt_bias[0].astype(F32))
    dtb = dtb.at[3 * B_V_HEADS:4 * B_V_HEADS].set(dt_bias[1].astype(F32))
    gate_params = jnp.zeros((8, LANES), F32).at[0].set(neg_a).at[1].set(dtb).at[2].set(jnp.asarray(lane_is_g, F32))
    qn, kn, vn, gts = _dn_prep(u_b.reshape(bsz, seq, c_b), cw, bab.reshape(bsz, seq, LANES), gate_params)
    gts_t = gts.reshape(bsz, seq // DN_CHUNK, DN_CHUNK, LANES).transpose(0, 1, 3, 2)
    o_f, o_b = _dn_state(*_dn_intra(qn, kn, vn, gts, gts_t), seq)

    bg = b_gate.astype(F32).reshape(1, 2 * D_MODEL)
    a_part = _branch_a(outs, lses, w_branch_a.astype(BF16), gpre, bg, tm)
    merged = _branch_b(o_f.reshape(t, B_VZ), o_b.reshape(t, B_VZ), u_b, dn_norm_w.astype(F32).reshape(1, B_DV), w_branch_b.astype(BF16), gpre, bg,
                       a_part, tm)
    x1 = _out_ln(merged, w_out.astype(BF16), x2d, ln1_g.reshape(1, -1), ln1_b.reshape(1, -1), tm)

    rw = jnp.pad(router_w.astype(F32), ((0, 0), (0, LANES - N_EXPERTS)))
    rb = jnp.pad(router_b.astype(F32), (0, LANES - N_EXPERTS)).reshape(1, LANES)
    idx, gates, rank, cnt = _router(x1, rw, rb, tm)
    counts = cnt[0, :N_EXPERTS].astype(jnp.int32)
    padded = (counts + MOE_BLOCK - 1) // MOE_BLOCK * MOE_BLOCK
    pad_end = jnp.cumsum(padded)
    pad_start = pad_end - padded
    dest = pad_start[idx[:, :TOP_K]] + rank[:, :TOP_K]
    n_pad = t * TOP_K + N_EXPERTS * MOE_BLOCK
    nb = n_pad // MOE_BLOCK
    block_start = jnp.arange(nb, dtype=jnp.int32) * MOE_BLOCK
    block_expert = jnp.minimum(jnp.searchsorted(pad_end, block_start, side='right'), N_EXPERTS - 1).astype(jnp.int32)
    n_used = (pad_end[-1:] // MOE_BLOCK).astype(jnp.int32)

    xs = _dispatch(x1, dest, (pad_start + counts).astype(jnp.int32), (padded - counts).astype(jnp.int32), n_pad, tm)
    d_ff = w_down.shape[1]
    wgu = w_gate_up.reshape(N_EXPERTS, D_MODEL, d_ff, 2).transpose(0, 3, 1, 2).astype(BF16)
    bgu = b_gate_up.astype(F32).reshape(N_EXPERTS, 1, d_ff, 2)
    ys = _expert_ffn(xs, block_expert, n_used, wgu, bgu[..., 0], bgu[..., 1], w_down.astype(BF16),
                     b_down.astype(F32).reshape(N_EXPERTS, 1, D_MODEL))

    out = _final(dest, gates, x1, p.reshape(t, PLE_DIM), w_ple_gate.astype(BF16),
                 b_ple_gate.astype(F32).reshape(1, -1), w_ple.astype(BF16), ln2_g.reshape(1, -1),
                 ln2_b.reshape(1, -1), ys, min(256, t))
    return out.reshape(bsz, seq, D_MODEL)


def kernel(x, p, w_in, b_gate, conv_w, a_log, dt_bias, dn_norm_w, w_branch_a, w_branch_b, w_out, ln1_g, ln1_b,
           router_w, router_b, w_gate_up, b_gate_up, w_down, b_down, w_ple, w_ple_gate, b_ple_gate, ln2_g, ln2_b):
    assert w_in.shape[0] == DEPTH
    return _layer(x, p[0], w_in[0], b_gate[0], conv_w[0], a_log[0], dt_bias[0], dn_norm_w[0], w_branch_a[0],
                  w_branch_b[0], w_out[0], ln1_g[0], ln1_b[0], router_w[0], router_b[0], w_gate_up[0],
                  b_gate_up[0], w_down[0], b_down[0], w_ple[0], w_ple_gate[0], b_ple_gate[0], ln2_g[0], ln2_b[0])
```

```python
import functools

import numpy as np
import jax
import jax.numpy as jnp
from jax import lax
from jax.experimental import pallas as pl
from jax.experimental.pallas import tpu as pltpu

F32 = jnp.float32
BF16 = jnp.bfloat16

D_MODEL = 2048
HEAD_DIM = 128
A_HEADS = 8
DILATION_GROUPS = ((128, 1), (512, 4), (2048, 16))
N_GROUPS = 3
NEG_INF = -1e30
B_QK_HEADS = 8
B_V_HEADS = 16
B_DK = 128
B_DV = 128
CONV_W = 5
RMS_EPS = 1e-6
N_EXPERTS = 32
TOP_K = 4
SWIGLU_ALPHA = 1.702
SWIGLU_LIMIT = 7.0
PLE_DIM = 256
DEPTH = 1
DEEPNORM_ALPHA = (2 * DEPTH) ** 0.25
LN_EPS = 1e-5
A_QKV = N_GROUPS * A_HEADS * HEAD_DIM
B_QK = B_QK_HEADS * B_DK
B_VZ = B_V_HEADS * B_DV
B_GATES = 4 * B_V_HEADS

LANES = 128
N_SIDE = 64
Q_SUB = 128
DN_CHUNK = 64
DN_PAIR = B_V_HEADS // B_QK_HEADS
TRI_BASE = 16
ROW_SUBLANES = 8
IN_PROJ_TM = 1024
IN_PROJ_TN = 1024
MOE_BLOCK = 1024
MOE_TF = 512
VMEM_LIMIT = 56 * 1024 * 1024
assert D_MODEL == 2 * ROW_SUBLANES * LANES


def _cparams(sem):
    return pltpu.CompilerParams(dimension_semantics=sem, vmem_limit_bytes=VMEM_LIMIT)


def _sigmoid(x):
    return 1.0 / (1.0 + jnp.exp(-x))


def _dot(a, b):
    return jnp.dot(a, b, preferred_element_type=F32)


def _dot_nt(a, b):
    return lax.dot_general(a, b, (((1,), (1,)), ((), ())), preferred_element_type=F32)


def _dot_tn(a, b):
    return lax.dot_general(a, b, (((0,), (0,)), ((), ())), preferred_element_type=F32)


def _mm_kernel(x_ref, w_ref, o_ref):
    o_ref[...] = _dot(x_ref[...], w_ref[...]).astype(o_ref.dtype)


def _matmul(x, w, out_dtype, tm, tn):
    m, k = x.shape
    n = w.shape[1]
    return pl.pallas_call(
        _mm_kernel,
        grid=(m // tm, n // tn),
        in_specs=[pl.BlockSpec((tm, k), lambda i, j: (i, 0)),
                  pl.BlockSpec((k, tn), lambda i, j: (0, j))],
        out_specs=pl.BlockSpec((tm, tn), lambda i, j: (i, j)),
        out_shape=jax.ShapeDtypeStruct((m, n), out_dtype),
        compiler_params=_cparams(("parallel", "parallel")),
        name="in_proj",
    )(x, w)


def _mm_classes_kernel(x_ref, w_ref, o_ref, acc_ref, *, dil, rows):
    y = _dot(x_ref[...], w_ref[...])
    for j in range(acc_ref.shape[0]):
        ls = slice(j * LANES, (j + 1) * LANES)
        acc_ref[j] = y[:, ls]
        for c in range(dil):
            o_ref[0, c, :, ls] = acc_ref[j, pl.ds(c, rows, stride=dil), :].astype(o_ref.dtype)


def _matmul_classes(x, w, bsz, dil, tm, tn):
    m, k = x.shape
    n = w.shape[1]
    seq = m // bsz
    tiles_per_seq = seq // tm
    rows = tm // dil
    kern = functools.partial(_mm_classes_kernel, dil=dil, rows=rows)
    return pl.pallas_call(
        kern,
        grid=(m // tm, n // tn),
        in_specs=[pl.BlockSpec((tm, k), lambda i, j: (i, 0)),
                  pl.BlockSpec((k, tn), lambda i, j: (0, j))],
        out_specs=pl.BlockSpec((1, dil, rows, tn), lambda i, j: (i // tiles_per_seq, 0, i % tiles_per_seq, j)),
        out_shape=jax.ShapeDtypeStruct((bsz, dil, seq // dil, n), BF16),
        scratch_shapes=[pltpu.VMEM((tn // LANES, tm, LANES), F32)],
        compiler_params=_cparams(("parallel", "parallel")),
        name=f"in_proj_dil{dil}",
    )(x, w)


def _attn_kernel(q_ref, kp_ref, kc_ref, kn_ref, vp_ref, vc_ref, vn_ref, o_ref, lse_ref, kbuf, vbuf,
                 *, dil, sub_len, tl, slopes):
    i0 = pl.program_id(2) * tl
    kbuf[0:N_SIDE, :] = kp_ref[...]
    kbuf[N_SIDE:N_SIDE + tl, :] = kc_ref[...]
    kbuf[N_SIDE + tl:, :] = kn_ref[...]
    vbuf[0:N_SIDE, :] = vp_ref[...]
    vbuf[N_SIDE:N_SIDE + tl, :] = vc_ref[...]
    vbuf[N_SIDE + tl:, :] = vn_ref[...]
    span = Q_SUB + 2 * N_SIDE
    qq = lax.broadcasted_iota(jnp.int32, (Q_SUB, span), 0)
    kk = lax.broadcasted_iota(jnp.int32, (Q_SUB, span), 1)
    delta = kk - N_SIDE - qq
    absd = jnp.abs(delta)
    band = absd <= N_SIDE
    dist = (dil * absd).astype(F32)
    lane = lax.broadcasted_iota(jnp.int32, (Q_SUB, LANES), 1)
    scale = HEAD_DIM ** -0.5
    for j in range(tl // Q_SUB):
        pos = i0 + (j * Q_SUB - N_SIDE) + kk
        valid = band & (pos >= 0) & (pos < sub_len)
        lse_tile = jnp.zeros((Q_SUB, LANES), F32)
        for h in range(A_HEADS):
            hs = slice(h * HEAD_DIM, (h + 1) * HEAD_DIM)
            q = q_ref[j * Q_SUB:(j + 1) * Q_SUB, hs]
            k = kbuf[j * Q_SUB:j * Q_SUB + span, hs]
            v = vbuf[j * Q_SUB:j * Q_SUB + span, hs]
            s = _dot_nt(q, k) * scale
            s = jnp.where(valid, s - float(slopes[h]) * dist, NEG_INF)
            m = jnp.max(s, axis=1, keepdims=True)
            p = jnp.exp(s - m)
            l = jnp.sum(p, axis=1, keepdims=True)
            o = _dot(p.astype(BF16), v) / l
            o_ref[j * Q_SUB:(j + 1) * Q_SUB, hs] = o.astype(o_ref.dtype)
            lse_tile = jnp.where(lane == h, m + jnp.log(l), lse_tile)
        lse_ref[j * Q_SUB:(j + 1) * Q_SUB, :] = lse_tile


def _attention_group(qkv, gi, dil, slopes):
    bsz, _, sub_len, _ = qkv.shape
    tl = min(512, sub_len)
    assert sub_len % tl == 0 and tl % Q_SUB == 0 and sub_len % N_SIDE == 0
    width = A_HEADS * HEAD_DIM
    halo_per_tile = tl // N_SIDE
    n_halo = sub_len // N_SIDE
    prev = lambda li: jnp.maximum(li * halo_per_tile - 1, 0)
    nxt = lambda li: jnp.minimum((li + 1) * halo_per_tile, n_halo - 1)
    halo = lambda col, rowf: pl.BlockSpec((None, None, N_SIDE, width), lambda b, c, li: (b, c, rowf(li), col))
    cur = lambda col: pl.BlockSpec((None, None, tl, width), lambda b, c, li: (b, c, li, col))
    kern = functools.partial(_attn_kernel, dil=dil, sub_len=sub_len, tl=tl, slopes=tuple(float(s) for s in slopes))
    return pl.pallas_call(
        kern,
        grid=(bsz, dil, sub_len // tl),
        in_specs=[cur(0), halo(1, prev), cur(1), halo(1, nxt), halo(2, prev), cur(2), halo(2, nxt)],
        out_specs=[pl.BlockSpec((None, None, tl, width), lambda b, c, li: (b, c, li, 0)),
                   pl.BlockSpec((None, None, tl, LANES), lambda b, c, li: (b, c, li, 0))],
        out_shape=[jax.ShapeDtypeStruct((bsz, dil, sub_len, width), BF16),
                   jax.ShapeDtypeStruct((bsz, dil, sub_len, LANES), F32)],
        scratch_shapes=[pltpu.VMEM((tl + 2 * N_SIDE, width), BF16),
                        pltpu.VMEM((tl + 2 * N_SIDE, width), BF16)],
        compiler_params=_cparams(("parallel", "parallel", "parallel")),
        name=f"dilated_attn_g{gi}",
    )(qkv, qkv, qkv, qkv, qkv, qkv, qkv)


def _alibi_slopes():
    n = N_GROUPS * A_HEADS
    s = 2.0 ** (-8.0 * np.arange(1, n + 1) / n)
    return s.astype(np.float32).reshape(N_GROUPS, A_HEADS)


def _dn_prep_kernel(prev_ref, cur_ref, next_ref, cw_ref, bab_ref, gp_ref, q_ref, k_ref, v_ref, g_ref, *, ts):
    ti = pl.program_id(1)
    nt = pl.num_programs(1)
    halo = CONV_W // 2
    keep_prev = (ti > 0).astype(F32)
    keep_next = (ti < nt - 1).astype(F32)
    for c in range((2 * B_QK + B_VZ) // LANES):
        cs = slice(c * LANES, (c + 1) * LANES)
        xp = prev_ref[0, :, cs].astype(F32)[8:16] * keep_prev
        xc = cur_ref[0, :, cs].astype(F32)
        xn = next_ref[0, :, cs].astype(F32)[0:8] * keep_next
        ext = jnp.concatenate([xp, xc, xn], axis=0)
        acc = jnp.zeros((ts, LANES), F32)
        for j in range(CONV_W):
            off = 8 - halo + j
            acc = acc + ext[off:off + ts, :] * cw_ref[j:j + 1, cs]
        y = acc * _sigmoid(acc)
        if c < 2 * B_QK // LANES:
            y = y * lax.rsqrt(jnp.sum(y * y, axis=1, keepdims=True) + 1e-6)
        if c < B_QK // LANES:
            q_ref[0, :, cs] = y * (B_DK ** -0.5)
        elif c < 2 * B_QK // LANES:
            k_ref[0, :, c * LANES - B_QK:(c + 1) * LANES - B_QK] = y
        else:
            v_ref[0, :, c * LANES - 2 * B_QK:(c + 1) * LANES - 2 * B_QK] = y
    x = bab_ref[0]
    neg_a = gp_ref[0:1, :]
    dtb = gp_ref[1:2, :]
    is_g = gp_ref[2:3, :] > 0.5
    z = x + dtb
    softplus = jnp.maximum(z, 0.0) + jnp.log(1.0 + jnp.exp(-jnp.abs(z)))
    g_ref[0] = jnp.where(is_g, neg_a * softplus, _sigmoid(x))


def _dn_prep(u_b, conv_w, bab, gate_params):
    bsz, seq, _ = u_b.shape
    ts = min(256, seq)
    cq = 2 * B_QK + B_VZ
    nhalo = seq // 16
    per = ts // 16
    kern = functools.partial(_dn_prep_kernel, ts=ts)
    return pl.pallas_call(
        kern,
        grid=(bsz, seq // ts),
        in_specs=[pl.BlockSpec((1, 16, cq), lambda b, t: (b, jnp.maximum(t * per - 1, 0), 0)),
                  pl.BlockSpec((1, ts, cq), lambda b, t: (b, t, 0)),
                  pl.BlockSpec((1, 16, cq), lambda b, t: (b, jnp.minimum((t + 1) * per, nhalo - 1), 0)),
                  pl.BlockSpec((8, cq), lambda b, t: (0, 0)),
                  pl.BlockSpec((1, ts, LANES), lambda b, t: (b, t, 0)),
                  pl.BlockSpec((8, LANES), lambda b, t: (0, 0))],
        out_specs=[pl.BlockSpec((1, ts, B_QK), lambda b, t: (b, t, 0)),
                   pl.BlockSpec((1, ts, B_QK), lambda b, t: (b, t, 0)),
                   pl.BlockSpec((1, ts, B_VZ), lambda b, t: (b, t, 0)),
                   pl.BlockSpec((1, ts, LANES), lambda b, t: (b, t, 0))],
        out_shape=[jax.ShapeDtypeStruct((bsz, seq, B_QK), F32),
                   jax.ShapeDtypeStruct((bsz, seq, B_QK), F32),
                   jax.ShapeDtypeStruct((bsz, seq, B_VZ), F32),
                   jax.ShapeDtypeStruct((bsz, seq, LANES), F32)],
        compiler_params=_cparams(("parallel", "parallel")),
        name="dn_prep",
    )(u_b, u_b, u_b, conv_w, bab, gate_params)


def _dn_intra_kernel(q_ref, k_ref, v_ref, g_ref, gt_ref, u_ref, wq_ref, kq_ref, et_ref, *, c):
    n_units = 2 * DN_PAIR
    w4 = n_units * c
    hp = lax.Precision.HIGHEST
    bf = lambda t_: t_.astype(BF16)
    ii = lax.broadcasted_iota(jnp.int32, (c, w4), 0)
    ll = lax.broadcasted_iota(jnp.int32, (c, w4), 1)
    jj = ll % c
    ub = ll // c
    ub_row = ub[0:1, :]
    lo = jnp.where(ub >= DN_PAIR, jj - ii, ii - jj)
    incl = lo >= 0
    strict = lo > 0
    eye = (ii == jj).astype(F32)
    blk = (ii // TRI_BASE) == (jj // TRI_BASE)

    def pack(parts, sel):
        out = parts[n_units - 1]
        for u_ in range(n_units - 2, -1, -1):
            out = jnp.where(sel == u_, parts[u_], out)
        return out

    unit_mask = [jnp.where(ub == u_, 1.0, 0.0).astype(BF16) for u_ in range(n_units)]

    def block_diag(y16):
        return jnp.concatenate([y16 * m_ for m_ in unit_mask], axis=0)

    def mm(xs, ys):
        return [_dot(bf(x_), block_diag(bf(y_))) for x_, y_ in zip(xs, ys)]

    ri = lax.broadcasted_iota(jnp.int32, (c, c), 0)
    ci = lax.broadcasted_iota(jnp.int32, (c, c), 1)
    g_all = g_ref[0]
    gc_dir = [jnp.dot((ci <= ri).astype(F32), g_all, precision=hp, preferred_element_type=F32),
              jnp.dot((ci >= ri).astype(F32), g_all, precision=hp, preferred_element_type=F32)]
    tri4 = (lo <= 0).astype(F32)
    gcr_all = jnp.dot(gt_ref[0, 0], tri4, precision=hp, preferred_element_type=F32)
    tot_all = jnp.sum(g_all, axis=0, keepdims=True)

    pairs = list(range(B_V_HEADS // DN_PAIR))
    qs_, ks_, lms, qkms, betas, egcs, kscales, etots = [], [], [], [], [], [], [], []
    for p in pairs:
        cs = slice(p * B_DK, (p + 1) * B_DK)
        q = q_ref[0, :, cs]
        k = k_ref[0, :, cs]
        k16 = bf(k)
        k4 = jnp.concatenate([k16] * n_units, axis=0)
        gram = _dot_nt(k16, k4)
        qk = _dot_nt(bf(q), k4)
        beta_u, gc_u, gcr_u, tot_u = [], [], [], []
        for u_ in range(n_units):
            d_, e_ = divmod(u_, DN_PAIR)
            h = p * DN_PAIR + e_
            bl = d_ * 2 * B_V_HEADS + h
            gl = bl + B_V_HEADS
            beta_u.append(g_all[:, bl:bl + 1])
            gc_u.append(gc_dir[d_][:, gl:gl + 1])
            gcr_u.append(gcr_all[gl:gl + 1, :])
            tot_u.append(tot_all[:, gl:gl + 1])
        gc_p = pack(gc_u, ub)
        gcr_p = pack(gcr_u, ub_row)
        tot_p = pack(tot_u, ub_row)
        dec = jnp.where(incl, jnp.exp(jnp.where(incl, gc_p - gcr_p, 0.0)), 0.0)
        lms.append(jnp.where(strict, pack(beta_u, ub) * gram * dec, 0.0))
        qkms.append(jnp.where(incl, qk * dec, 0.0))
        qs_.append(q)
        ks_.append(k)
        betas.append(beta_u)
        egcs.append([jnp.exp(g_) for g_ in gc_u])
        kscales.append(jnp.exp(tot_p - gcr_p))
        etots.append([jnp.exp(t_) for t_ in tot_u])

    d1 = [jnp.where(blk, lm, 0.0) for lm in lms]
    d2 = mm(d1, d1)
    d4 = mm(d2, d2)
    d8 = mm(d4, d4)
    tm_ = [eye - d_ for d_ in d1]
    for dk in (d2, d4, d8):
        tm_ = [a + b for a, b in zip(tm_, mm(tm_, dk))]
    size = TRI_BASE
    while size < c:
        off = ((ii // (2 * size)) == (jj // (2 * size))) & ((ii // size) != (jj // size))
        cm = [jnp.where(off, lm, 0.0) for lm in lms]
        pc = mm(tm_, cm)
        tm_ = [a - b for a, b in zip(tm_, mm(pc, tm_))]
        size *= 2

    lane2 = lax.broadcasted_iota(jnp.int32, (1, DN_PAIR * B_DV), 1)
    for p in pairs:
        q, k = qs_[p], ks_[p]
        rows = []
        for u_ in range(n_units):
            d_, e_ = divmod(u_, DN_PAIR)
            h = p * DN_PAIR + e_
            v = v_ref[0, :, h * B_DV:(h + 1) * B_DV]
            beta = betas[p][u_]
            rows.append(jnp.concatenate([v * beta, k * (beta * egcs[p][u_])], axis=1))
        uw = _dot(block_diag(bf(tm_[p])), bf(jnp.concatenate(rows, axis=0)))
        k_t = jnp.transpose(jnp.concatenate([k] * DN_PAIR, axis=0))
        for d_ in range(2):
            u0, u1 = d_ * DN_PAIR, d_ * DN_PAIR + 1
            u_ref[d_, 0, 0, p] = jnp.concatenate([uw[u0 * c:(u0 + 1) * c, :B_DV], uw[u1 * c:(u1 + 1) * c, :B_DV]], axis=1)
            wq = jnp.concatenate([uw[u0 * c:(u0 + 1) * c, B_DV:], q * egcs[p][u0],
                                  uw[u1 * c:(u1 + 1) * c, B_DV:], q * egcs[p][u1]], axis=0)
            wq_ref[d_, 0, 0, p] = bf(wq)
            ls = slice(d_ * DN_PAIR * c, (d_ + 1) * DN_PAIR * c)
            kq = jnp.concatenate([k_t * kscales[p][:, ls], qkms[p][:, ls]], axis=0)
            kq_ref[d_, 0, 0, p] = bf(kq)
            et_ref[d_, 0, 0, p:p + 1, :] = jnp.where(lane2 < B_DV, etots[p][u0], etots[p][u1])


def _dn_intra(qn, kn, vn, gts, gts_t):
    bsz, seq, _ = qn.shape
    c = DN_CHUNK
    assert DN_PAIR * c == B_DK and seq % c == 0
    n = seq // c
    npair = B_V_HEADS // DN_PAIR
    wide = DN_PAIR * B_DV
    kern = functools.partial(_dn_intra_kernel, c=c)
    out5 = lambda r, cdim: pl.BlockSpec((2, 1, 1, npair, r, cdim), lambda b, i: (0, b, i, 0, 0, 0))
    return pl.pallas_call(
        kern,
        grid=(bsz, n),
        in_specs=[pl.BlockSpec((1, c, B_QK), lambda b, i: (b, i, 0)),
                  pl.BlockSpec((1, c, B_QK), lambda b, i: (b, i, 0)),
                  pl.BlockSpec((1, c, B_VZ), lambda b, i: (b, i, 0)),
                  pl.BlockSpec((1, c, LANES), lambda b, i: (b, i, 0)),
                  pl.BlockSpec((1, 1, LANES, c), lambda b, i: (b, i, 0, 0))],
        out_specs=[out5(c, wide), out5(2 * DN_PAIR * c, B_DK), out5(B_DK + c, DN_PAIR * c),
                   pl.BlockSpec((2, 1, 1, npair, wide), lambda b, i: (0, b, i, 0, 0))],
        out_shape=[jax.ShapeDtypeStruct((2, bsz, n, npair, c, wide), F32),
                   jax.ShapeDtypeStruct((2, bsz, n, npair, 2 * DN_PAIR * c, B_DK), BF16),
                   jax.ShapeDtypeStruct((2, bsz, n, npair, B_DK + c, DN_PAIR * c), BF16),
                   jax.ShapeDtypeStruct((2, bsz, n, npair, wide), F32)],
        compiler_params=_cparams(("parallel", "parallel")),
        name="dn_intra",
    )(qn, kn, vn, gts, gts_t)


def _dn_state_kernel(uf_ref, wqf_ref, kqf_ref, etf_ref, ub_ref, wqb_ref, kqb_ref, etb_ref, of_ref, ob_ref, s_ref, *, c):
    @pl.when(pl.program_id(1) == 0)
    def _():
        s_ref[...] = jnp.zeros_like(s_ref)

    npair = B_V_HEADS // DN_PAIR
    bf = lambda t_: t_.astype(BF16)
    chains = [(d_, p) for d_ in range(2) for p in range(npair)]
    refs = ((uf_ref, wqf_ref, kqf_ref, etf_ref, of_ref), (ub_ref, wqb_ref, kqb_ref, etb_ref, ob_ref))
    states = [s_ref[d_, p] for d_, p in chains]
    a_res = [_dot(refs[d_][1][0, 0, 0, p], bf(s_)) for (d_, p), s_ in zip(chains, states)]
    zero = jnp.zeros((c, B_DV), F32)
    b_res = []
    for (d_, p), a_ in zip(chains, a_res):
        u = refs[d_][0][0, 0, 0, p]
        v0 = u[:, :B_DV] - a_[0:c, :B_DV]
        v1 = u[:, B_DV:] - a_[2 * c:3 * c, B_DV:]
        bd_v = jnp.concatenate([jnp.concatenate([v0, zero], axis=1), jnp.concatenate([zero, v1], axis=1)], axis=0)
        b_res.append(_dot(refs[d_][2][0, 0, 0, p], bf(bd_v)))
    for (d_, p), a_, b_, s_ in zip(chains, a_res, b_res, states):
        s_ref[d_, p] = s_ * refs[d_][3][0, 0, 0, p:p + 1, :] + b_[:B_DK]
        o_ref = refs[d_][4]
        o_ref[0, :, (2 * p) * B_DV:(2 * p + 1) * B_DV] = a_[c:2 * c, :B_DV] + b_[B_DK:, :B_DV]
        o_ref[0, :, (2 * p + 1) * B_DV:(2 * p + 2) * B_DV] = a_[3 * c:4 * c, B_DV:] + b_[B_DK:, B_DV:]


def _dn_state(u_all, wq_all, kq_all, et_all, seq):
    _, bsz, n, npair, c, wide = u_all.shape
    kern = functools.partial(_dn_state_kernel, c=c)
    fwd = lambda r, cdim: pl.BlockSpec((1, 1, 1, npair, r, cdim), lambda b, i: (0, b, i, 0, 0, 0))
    bwd = lambda r, cdim: pl.BlockSpec((1, 1, 1, npair, r, cdim), lambda b, i: (1, b, n - 1 - i, 0, 0, 0))
    et_f = pl.BlockSpec((1, 1, 1, npair, wide), lambda b, i: (0, b, i, 0, 0))
    et_b = pl.BlockSpec((1, 1, 1, npair, wide), lambda b, i: (1, b, n - 1 - i, 0, 0))
    shapes = ((c, wide), (2 * DN_PAIR * c, B_DK), (B_DK + c, DN_PAIR * c))
    return pl.pallas_call(
        kern,
        grid=(bsz, n),
        in_specs=[fwd(*shapes[0]), fwd(*shapes[1]), fwd(*shapes[2]), et_f,
                  bwd(*shapes[0]), bwd(*shapes[1]), bwd(*shapes[2]), et_b],
        out_specs=[pl.BlockSpec((1, c, B_VZ), lambda b, i: (b, i, 0)),
                   pl.BlockSpec((1, c, B_VZ), lambda b, i: (b, n - 1 - i, 0))],
        out_shape=[jax.ShapeDtypeStruct((bsz, seq, B_VZ), F32), jax.ShapeDtypeStruct((bsz, seq, B_VZ), F32)],
        scratch_shapes=[pltpu.VMEM((2, npair, B_DK, wide), F32)],
        compiler_params=_cparams(("parallel", "arbitrary")),
        name="dn_state",
    )(u_all, wq_all, kq_all, et_all, u_all, wq_all, kq_all, et_all)


def _branch_a_kernel(o0_ref, o1_ref, o2_ref, l0_ref, l1_ref, l2_ref, w_ref, gp_ref, bg_ref, out_ref, o_sc, l_sc,
                     *, dils, tm):
    for g, (o_ref, l_ref) in enumerate(((o0_ref, l0_ref), (o1_ref, l1_ref), (o2_ref, l2_ref))):
        r = dils[g]
        for c in range(r):
            l_sc[g, pl.ds(c, tm // r, stride=r), :] = l_ref[c]
            for h in range(A_HEADS):
                o_sc[g, h, pl.ds(c, tm // r, stride=r), :] = o_ref[c, :, h * HEAD_DIM:(h + 1) * HEAD_DIM].astype(F32)
    ls = [l_sc[g] for g in range(N_GROUPS)]
    m = jnp.maximum(jnp.maximum(ls[0], ls[1]), ls[2])
    es = [jnp.exp(l - m) for l in ls]
    den = es[0] + es[1] + es[2]
    ws = [e / den for e in es]
    parts = []
    for h in range(A_HEADS):
        hs = slice(h * HEAD_DIM, (h + 1) * HEAD_DIM)
        acc = ws[0][:, h:h + 1] * o_sc[0, h]
        acc = acc + ws[1][:, h:h + 1] * o_sc[1, h]
        acc = acc + ws[2][:, h:h + 1] * o_sc[2, h]
        parts.append(acc.astype(BF16))
    oa = jnp.concatenate(parts, axis=1)
    y = _dot(oa, w_ref[...])
    out_ref[...] = _sigmoid(gp_ref[...] + bg_ref[...]) * y


def _branch_a(outs, lses, w_a, gpre, b_gate, tm):
    bsz = outs[0].shape[0]
    dils = tuple(o.shape[1] for o in outs)
    seq = dils[0] * outs[0].shape[2]
    t = bsz * seq
    per_seq = seq // tm
    wd = A_HEADS * HEAD_DIM
    cls = lambda r, width: pl.BlockSpec((None, r, tm // r, width), lambda i: (i // per_seq, 0, i % per_seq, 0))
    kern = functools.partial(_branch_a_kernel, dils=dils, tm=tm)
    return pl.pallas_call(
        kern,
        grid=(t // tm,),
        in_specs=[cls(dils[0], wd), cls(dils[1], wd), cls(dils[2], wd),
                  cls(dils[0], LANES), cls(dils[1], LANES), cls(dils[2], LANES),
                  pl.BlockSpec((wd, D_MODEL), lambda i: (0, 0)),
                  pl.BlockSpec((tm, D_MODEL), lambda i: (i, 0)),
                  pl.BlockSpec((1, D_MODEL), lambda i: (0, 0))],
        out_specs=pl.BlockSpec((tm, D_MODEL), lambda i: (i, 0)),
        out_shape=jax.ShapeDtypeStruct((t, D_MODEL), F32),
        scratch_shapes=[pltpu.VMEM((N_GROUPS, A_HEADS, tm, HEAD_DIM), F32), pltpu.VMEM((N_GROUPS, tm, LANES), F32)],
        compiler_params=_cparams(("parallel",)),
        name="branch_a",
    )(*outs, *lses, w_a, gpre, b_gate)


def _branch_b_kernel(of_ref, ob_ref, z_ref, nw_ref, w_ref, gp_ref, bg_ref, a_ref, out_ref):
    nw = nw_ref[...]
    parts = []
    for h in range(B_V_HEADS):
        hs = slice(h * B_DV, (h + 1) * B_DV)
        o = of_ref[:, hs] + ob_ref[:, hs]
        z = z_ref[:, hs].astype(F32)
        o = o * lax.rsqrt(jnp.mean(o * o, axis=1, keepdims=True) + RMS_EPS) * nw * (z * _sigmoid(z))
        parts.append(o.astype(BF16))
    ob = jnp.concatenate(parts, axis=1)
    y = _dot(ob, w_ref[...])
    out_ref[...] = (a_ref[...] + _sigmoid(gp_ref[...] + bg_ref[...]) * y).astype(out_ref.dtype)


def _branch_b(o_f, o_b, u_b2d, norm_w, w_b, gpre, b_gate, a_part, tm):
    t = a_part.shape[0]
    return pl.pallas_call(
        _branch_b_kernel,
        grid=(t // tm,),
        in_specs=[pl.BlockSpec((tm, B_VZ), lambda i: (i, 0)),
                  pl.BlockSpec((tm, B_VZ), lambda i: (i, 0)),
                  pl.BlockSpec((tm, B_VZ), lambda i: (i, (2 * B_QK + B_VZ) // B_VZ)),
                  pl.BlockSpec((1, B_DV), lambda i: (0, 0)),
                  pl.BlockSpec((B_VZ, D_MODEL), lambda i: (0, 0)),
                  pl.BlockSpec((tm, D_MODEL), lambda i: (i, 1)),
                  pl.BlockSpec((1, D_MODEL), lambda i: (0, 1)),
                  pl.BlockSpec((tm, D_MODEL), lambda i: (i, 0))],
        out_specs=pl.BlockSpec((tm, D_MODEL), lambda i: (i, 0)),
        out_shape=jax.ShapeDtypeStruct((t, D_MODEL), BF16),
        compiler_params=_cparams(("parallel",)),
        name="branch_b",
    )(o_f, o_b, u_b2d, norm_w, w_b, gpre, b_gate, a_part)


def _layer_norm(y, g, b):
    mu = jnp.mean(y, axis=1, keepdims=True)
    yc = y - mu
    var = jnp.mean(yc * yc, axis=1, keepdims=True)
    return yc * lax.rsqrt(var + LN_EPS) * g + b


def _out_ln_kernel(m_ref, w_ref, x_ref, g_ref, b_ref, out_ref):
    mix = _dot(m_ref[...], w_ref[...])
    out_ref[...] = _layer_norm(DEEPNORM_ALPHA * x_ref[...] + mix, g_ref[...], b_ref[...])


def _out_ln(merged, w_out, x2d, ln_g, ln_b, tm):
    t = merged.shape[0]
    vec = pl.BlockSpec((1, D_MODEL), lambda i: (0, 0))
    return pl.pallas_call(
        _out_ln_kernel,
        grid=(t // tm,),
        in_specs=[pl.BlockSpec((tm, D_MODEL), lambda i: (i, 0)),
                  pl.BlockSpec((D_MODEL, D_MODEL), lambda i: (0, 0)),
                  pl.BlockSpec((tm, D_MODEL), lambda i: (i, 0)), vec, vec],
        out_specs=pl.BlockSpec((tm, D_MODEL), lambda i: (i, 0)),
        out_shape=jax.ShapeDtypeStruct((t, D_MODEL), F32),
        compiler_params=_cparams(("parallel",)),
        name="out_proj_ln1",
    )(merged, w_out, x2d, ln_g, ln_b)


def _router_kernel(x_ref, rw_ref, rb_ref, idx_ref, gate_ref, rank_ref, cnt_ref, carry, *, tm):
    @pl.when(pl.program_id(0) == 0)
    def _():
        carry[...] = jnp.zeros_like(carry)

    lane = lax.broadcasted_iota(jnp.int32, (tm, LANES), 1)
    lane_f = lane.astype(F32)
    logits = jnp.dot(x_ref[...], rw_ref[...], precision=lax.Precision.HIGHEST, preferred_element_type=F32) + rb_ref[...]
    cur = jnp.where(lane < N_EXPERTS, logits, -jnp.inf)
    vals, idxs = [], []
    for _k in range(TOP_K):
        m = jnp.max(cur, axis=1, keepdims=True)
        idx = jnp.min(jnp.where(cur == m, lane_f, float(LANES)), axis=1, keepdims=True).astype(jnp.int32)
        vals.append(m)
        idxs.append(idx)
        cur = jnp.where(lane == idx, -jnp.inf, cur)
    es = [jnp.exp(v - vals[0]) for v in vals]
    den = es[0] + es[1] + es[2] + es[3]
    onehot = jnp.zeros((tm, LANES), F32)
    for idx in idxs:
        onehot = onehot + (lane == idx).astype(F32)
    ri = lax.broadcasted_iota(jnp.int32, (tm, tm), 0)
    ci = lax.broadcasted_iota(jnp.int32, (tm, tm), 1)
    before = (ci < ri).astype(BF16)
    prefix = _dot(before, onehot.astype(BF16)) + carry[0:1, :]
    idx_out = jnp.zeros((tm, LANES), jnp.int32)
    gate_out = jnp.zeros((tm, LANES), F32)
    rank_out = jnp.zeros((tm, LANES), jnp.int32)
    for k in range(TOP_K):
        rk = jnp.sum(jnp.where(lane == idxs[k], prefix, 0.0), axis=1, keepdims=True)
        idx_out = jnp.where(lane == k, idxs[k], idx_out)
        gate_out = jnp.where(lane == k, es[k] / den, gate_out)
        rank_out = jnp.where(lane == k, rk.astype(jnp.int32), rank_out)
    idx_ref[...] = idx_out
    gate_ref[...] = gate_out
    rank_ref[...] = rank_out
    total = carry[0:1, :] + jnp.sum(onehot, axis=0, keepdims=True)
    carry[...] = jnp.broadcast_to(total, carry.shape)
    cnt_ref[...] = jnp.broadcast_to(total, cnt_ref.shape)


def _router(x1, rw, rb, tm):
    t = x1.shape[0]
    row = pl.BlockSpec((tm, LANES), lambda i: (i, 0))
    kern = functools.partial(_router_kernel, tm=tm)
    return pl.pallas_call(
        kern,
        grid=(t // tm,),
        in_specs=[pl.BlockSpec((tm, D_MODEL), lambda i: (i, 0)),
                  pl.BlockSpec((D_MODEL, LANES), lambda i: (0, 0)),
                  pl.BlockSpec((1, LANES), lambda i: (0, 0))],
        out_specs=[row, row, row, pl.BlockSpec((8, LANES), lambda i: (0, 0))],
        out_shape=[jax.ShapeDtypeStruct((t, LANES), jnp.int32),
                   jax.ShapeDtypeStruct((t, LANES), F32),
                   jax.ShapeDtypeStruct((t, LANES), jnp.int32),
                   jax.ShapeDtypeStruct((8, LANES), F32)],
        scratch_shapes=[pltpu.VMEM((8, LANES), F32)],
        compiler_params=_cparams(("arbitrary",)),
        name="router",
    )(x1, rw, rb)


def _pack_rows(src_ref, dst_ref):
    half = D_MODEL // 2
    n = src_ref.shape[0]
    for s in range(ROW_SUBLANES):
        lo = pltpu.bitcast(src_ref[:, s * LANES:(s + 1) * LANES].astype(BF16).astype(F32), jnp.uint32)
        hi = pltpu.bitcast(src_ref[:, half + s * LANES:half + (s + 1) * LANES].astype(BF16).astype(F32), jnp.uint32)
        dst_ref[pl.ds(s, n, stride=ROW_SUBLANES), :] = (lo >> 16) | hi


def _unpack_words(w):
    return pltpu.bitcast(w << 16, F32), pltpu.bitcast(w & jnp.uint32(0xFFFF0000), F32)


def _tile(ref, i):
    start = i * ROW_SUBLANES
    if not isinstance(i, int):
        start = pl.multiple_of(start, ROW_SUBLANES)
    return ref.at[pl.ds(start, ROW_SUBLANES)]


def _tile_copy(src, i, dst, j, sem):
    return pltpu.make_async_copy(_tile(src, i), _tile(dst, j), sem)


def _wait_tiles(hbm, n, sem):
    pltpu.make_async_copy(hbm.at[pl.ds(0, n * ROW_SUBLANES)], hbm.at[pl.ds(0, n * ROW_SUBLANES)], sem).wait()


def _dispatch_kernel(pad_start_ref, pad_cnt_ref, dest_ref, x_ref, xs_hbm, stage, zero, sem, *, tm, nt):
    i = pl.program_id(0)

    @pl.when(i < nt)
    def _():
        _pack_rows(x_ref, stage)

        def issue(t_, carry):
            src = _tile(stage, t_)
            for k in range(TOP_K):
                pltpu.make_async_copy(src, _tile(xs_hbm, dest_ref[0, 0, t_ * TOP_K + k]), sem).start()
            return carry

        lax.fori_loop(0, tm, issue, 0, unroll=2)
        _wait_tiles(xs_hbm, tm * TOP_K, sem)

    @pl.when(i >= nt)
    def _():
        e = i - nt
        zero[...] = jnp.zeros_like(zero)
        start = pad_start_ref[e]
        cnt = pad_cnt_ref[e]

        def issue(r, carry):
            _tile_copy(zero, 0, xs_hbm, start + r, sem).start()
            return carry

        lax.fori_loop(0, cnt, issue, 0)

        def wait(r, carry):
            _tile_copy(zero, 0, xs_hbm, start, sem).wait()
            return carry

        lax.fori_loop(0, cnt, wait, 0)


def _dispatch(x1, dest, pad_from, pad_cnt, n_pad, tm):
    t = x1.shape[0]
    nt = t // tm
    kern = functools.partial(_dispatch_kernel, tm=tm, nt=nt)
    grid_spec = pltpu.PrefetchScalarGridSpec(
        num_scalar_prefetch=2,
        grid=(nt + pad_from.shape[0],),
        in_specs=[pl.BlockSpec((1, 1, tm * TOP_K), lambda i, ps, pc: (jnp.minimum(i, nt - 1), 0, 0),
                               memory_space=pltpu.SMEM),
                  pl.BlockSpec((tm, D_MODEL), lambda i, ps, pc: (jnp.minimum(i, nt - 1), 0))],
        out_specs=pl.BlockSpec(memory_space=pl.ANY),
        scratch_shapes=[pltpu.VMEM((tm * ROW_SUBLANES, LANES), jnp.uint32),
                        pltpu.VMEM((ROW_SUBLANES, LANES), jnp.uint32),
                        pltpu.SemaphoreType.DMA(())],
    )
    return pl.pallas_call(
        kern,
        grid_spec=grid_spec,
        out_shape=jax.ShapeDtypeStruct((n_pad * ROW_SUBLANES, LANES), jnp.uint32),
        compiler_params=_cparams(("arbitrary",)),
        name="moe_dispatch",
    )(pad_from, pad_cnt, dest.reshape(nt, 1, tm * TOP_K), x1)


def _ffn_kernel(be_ref, nused_ref, x_ref, wg_ref, wu_ref, bg_ref, bu_ref, wd_ref, bd_ref, o_ref, acc, xb):
    i = pl.program_id(0)
    f = pl.program_id(1)
    nf = pl.num_programs(1)
    half = D_MODEL // 2

    @pl.when(f == 0)
    def _():
        acc[...] = jnp.broadcast_to(bd_ref[0], acc.shape)
        for s in range(ROW_SUBLANES):
            lo, hi = _unpack_words(x_ref[pl.ds(s, MOE_BLOCK, stride=ROW_SUBLANES), :])
            xb[:, s * LANES:(s + 1) * LANES] = lo.astype(BF16)
            xb[:, half + s * LANES:half + (s + 1) * LANES] = hi.astype(BF16)

    @pl.when(i < nused_ref[0])
    def _():
        x = xb[...]
        g = _dot(x, wg_ref[0, 0]) + bg_ref[0]
        u = _dot(x, wu_ref[0, 0]) + bu_ref[0]
        gate = jnp.minimum(g, SWIGLU_LIMIT)
        up = jnp.clip(u, -SWIGLU_LIMIT, SWIGLU_LIMIT)
        act = (up + 1.0) * gate * _sigmoid(gate * SWIGLU_ALPHA)
        acc[...] += _dot(act.astype(BF16), wd_ref[0].astype(BF16))

    @pl.when(f == nf - 1)
    def _():
        _pack_rows(acc, o_ref)


def _expert_ffn(xs, block_expert, n_used, wgu, bg, bu, wd, bd):
    n_pad = xs.shape[0] // ROW_SUBLANES
    nb = n_pad // MOE_BLOCK
    d_ff = wgu.shape[3]
    nf = d_ff // MOE_TF
    row_blk = (MOE_BLOCK * ROW_SUBLANES, LANES)
    grid_spec = pltpu.PrefetchScalarGridSpec(
        num_scalar_prefetch=2,
        grid=(nb, nf),
        in_specs=[pl.BlockSpec(row_blk, lambda i, f, be, nu: (jnp.minimum(i, nu[0] - 1), 0)),
                  pl.BlockSpec((1, 1, D_MODEL, MOE_TF), lambda i, f, be, nu: (be[i], 0, 0, f)),
                  pl.BlockSpec((1, 1, D_MODEL, MOE_TF), lambda i, f, be, nu: (be[i], 1, 0, f)),
                  pl.BlockSpec((1, 1, MOE_TF), lambda i, f, be, nu: (be[i], 0, f)),
                  pl.BlockSpec((1, 1, MOE_TF), lambda i, f, be, nu: (be[i], 0, f)),
                  pl.BlockSpec((1, MOE_TF, D_MODEL), lambda i, f, be, nu: (be[i], f, 0)),
                  pl.BlockSpec((1, 1, D_MODEL), lambda i, f, be, nu: (be[i], 0, 0))],
        out_specs=pl.BlockSpec(row_blk, lambda i, f, be, nu: (i, 0)),
        scratch_shapes=[pltpu.VMEM((MOE_BLOCK, D_MODEL), F32), pltpu.VMEM((MOE_BLOCK, D_MODEL), BF16)],
    )
    return pl.pallas_call(
        _ffn_kernel,
        grid_spec=grid_spec,
        out_shape=jax.ShapeDtypeStruct((n_pad * ROW_SUBLANES, LANES), jnp.uint32),
        compiler_params=_cparams(("arbitrary", "arbitrary")),
        name="expert_ffn",
    )(block_expert, n_used, xs, wgu, wgu, bg, bu, wd, bd)


def _final_kernel(dest_ref, gate_ref, x_ref, p_ref, wpg_ref, bpg_ref, wple_ref, g_ref, b_ref, ys_hbm, out_ref,
                  rows, sem, *, tm):
    def issue(r, carry):
        _tile_copy(ys_hbm, dest_ref[0, 0, r], rows, r, sem).start()
        return carry

    lax.fori_loop(0, tm * TOP_K, issue, 0, unroll=8)
    x = x_ref[...]
    pg = _dot(x.astype(BF16), wpg_ref[...]) + bpg_ref[...]
    ple = _sigmoid(pg) * _dot(p_ref[...].astype(BF16), wple_ref[...])
    _wait_tiles(ys_hbm, tm * TOP_K, sem)
    gates = gate_ref[...]
    lo_parts, hi_parts = [], []
    for s in range(ROW_SUBLANES):
        acc_lo = acc_hi = None
        for k in range(TOP_K):
            lo, hi = _unpack_words(rows[pl.ds(k * ROW_SUBLANES + s, tm, stride=TOP_K * ROW_SUBLANES), :])
            g = gates[:, k:k + 1]
            acc_lo = g * lo if acc_lo is None else acc_lo + g * lo
            acc_hi = g * hi if acc_hi is None else acc_hi + g * hi
        lo_parts.append(acc_lo)
        hi_parts.append(acc_hi)
    y = jnp.concatenate(lo_parts + hi_parts, axis=1)
    out_ref[...] = _layer_norm(DEEPNORM_ALPHA * x + y + ple, g_ref[...], b_ref[...])


def _final(dest, gates, x1, p2d, w_pg, b_pg, w_ple, ln_g, ln_b, ys, tm):
    t = x1.shape[0]
    vec = pl.BlockSpec((1, D_MODEL), lambda i: (0, 0))
    kern = functools.partial(_final_kernel, tm=tm)
    return pl.pallas_call(
        kern,
        grid=(t // tm,),
        in_specs=[pl.BlockSpec((1, 1, tm * TOP_K), lambda i: (i, 0, 0), memory_space=pltpu.SMEM),
                  pl.BlockSpec((tm, LANES), lambda i: (i, 0)),
                  pl.BlockSpec((tm, D_MODEL), lambda i: (i, 0)),
                  pl.BlockSpec((tm, PLE_DIM), lambda i: (i, 0)),
                  pl.BlockSpec((D_MODEL, D_MODEL), lambda i: (0, 0)), vec,
                  pl.BlockSpec((PLE_DIM, D_MODEL), lambda i: (0, 0)), vec, vec,
                  pl.BlockSpec(memory_space=pl.ANY)],
        out_specs=pl.BlockSpec((tm, D_MODEL), lambda i: (i, 0)),
        out_shape=jax.ShapeDtypeStruct((t, D_MODEL), F32),
        scratch_shapes=[pltpu.VMEM((tm * TOP_K * ROW_SUBLANES, LANES), jnp.uint32), pltpu.SemaphoreType.DMA(())],
        compiler_params=_cparams(("arbitrary",)),
        name="combine_ple_ln2",
    )(dest.reshape(t // tm, 1, tm * TOP_K), gates, x1, p2d, w_pg, b_pg, w_ple, ln_g, ln_b, ys)


def _row_tile(t):
    return min(512, t)


def _layer(x, p, w_in, b_gate, conv_w, a_log, dt_bias, dn_norm_w, w_branch_a, w_branch_b, w_out, ln1_g, ln1_b,
           router_w, router_b, w_gate_up, b_gate_up, w_down, b_down, w_ple, w_ple_gate, b_ple_gate, ln2_g, ln2_b):
    bsz, seq, _ = x.shape
    t = bsz * seq
    tm = _row_tile(t)
    x2d = x.reshape(t, D_MODEL)
    xb = x2d.astype(BF16)

    c_a = 3 * A_QKV
    c_b = 2 * B_QK + 2 * B_VZ
    w_bf = w_in.astype(BF16)
    tp = min(IN_PROJ_TM, seq)
    u_b = _matmul(xb, w_bf[:, c_a:c_a + c_b], BF16, tp, IN_PROJ_TN)
    w_bab = jnp.pad(w_bf[:, c_a + c_b:c_a + c_b + B_GATES], ((0, 0), (0, LANES - B_GATES)))
    bab = _matmul(xb, w_bab, F32, tp, LANES)
    gpre = _matmul(xb, w_bf[:, c_a + c_b + B_GATES:], F32, tp, IN_PROJ_TN)

    slopes = _alibi_slopes()
    gw = A_HEADS * HEAD_DIM
    outs, lses = [], []
    for gi, (_win, dil) in enumerate(DILATION_GROUPS):
        w_g = jnp.concatenate([w_bf[:, part * A_QKV + gi * gw:part * A_QKV + (gi + 1) * gw] for part in range(3)], axis=1)
        o_g, l_g = _attention_group(_matmul_classes(xb, w_g, bsz, dil, tp, IN_PROJ_TN), gi, dil, slopes[gi])
        outs.append(o_g)
        lses.append(l_g)

    cw = jnp.pad(conv_w.astype(F32), ((0, 8 - CONV_W), (0, 0)))
    lane_is_g = (np.arange(LANES) % (2 * B_V_HEADS) >= B_V_HEADS) & (np.arange(LANES) < B_GATES)
    neg_a = jnp.zeros((LANES,), F32).at[B_V_HEADS:2 * B_V_HEADS].set(-jnp.exp(a_log[0].astype(F32)))
    neg_a = neg_a.at[3 * B_V_HEADS:4 * B_V_HEADS].set(-jnp.exp(a_log[1].astype(F32)))
    dtb = jnp.zeros((LANES,), F32).at[B_V_HEADS:2 * B_V_HEADS].set(dt_bias[0].astype(F32))
    dtb = dtb.at[3 * B_V_HEADS:4 * B_V_HEADS].set(dt_bias[1].astype(F32))
    gate_params = jnp.zeros((8, LANES), F32).at[0].set(neg_a).at[1].set(dtb).at[2].set(jnp.asarray(lane_is_g, F32))
    qn, kn, vn, gts = _dn_prep(u_b.reshape(bsz, seq, c_b), cw, bab.reshape(bsz, seq, LANES), gate_params)
    gts_t = gts.reshape(bsz, seq // DN_CHUNK, DN_CHUNK, LANES).transpose(0, 1, 3, 2)
    o_f, o_b = _dn_state(*_dn_intra(qn, kn, vn, gts, gts_t), seq)

    bg = b_gate.astype(F32).reshape(1, 2 * D_MODEL)
    a_part = _branch_a(outs, lses, w_branch_a.astype(BF16), gpre, bg, tm)
    merged = _branch_b(o_f.reshape(t, B_VZ), o_b.reshape(t, B_VZ), u_b, dn_norm_w.astype(F32).reshape(1, B_DV), w_branch_b.astype(BF16), gpre, bg,
                       a_part, tm)
    x1 = _out_ln(merged, w_out.astype(BF16), x2d, ln1_g.reshape(1, -1), ln1_b.reshape(1, -1), tm)

    rw = jnp.pad(router_w.astype(F32), ((0, 0), (0, LANES - N_EXPERTS)))
    rb = jnp.pad(router_b.astype(F32), (0, LANES - N_EXPERTS)).reshape(1, LANES)
    idx, gates, rank, cnt = _router(x1, rw, rb, tm)
    counts = cnt[0, :N_EXPERTS].astype(jnp.int32)
    padded = (counts + MOE_BLOCK - 1) // MOE_BLOCK * MOE_BLOCK
    pad_end = jnp.cumsum(padded)
    pad_start = pad_end - padded
    dest = pad_start[idx[:, :TOP_K]] + rank[:, :TOP_K]
    n_pad = t * TOP_K + N_EXPERTS * MOE_BLOCK
    nb = n_pad // MOE_BLOCK
    block_start = jnp.arange(nb, dtype=jnp.int32) * MOE_BLOCK
    block_expert = jnp.minimum(jnp.sum(pad_end[None, :] <= block_start[:, None], axis=1), N_EXPERTS - 1).astype(jnp.int32)
    n_used = (pad_end[-1:] // MOE_BLOCK).astype(jnp.int32)

    zero_from = jnp.concatenate([pad_start + counts, pad_end[-1:]]).astype(jnp.int32)
    zero_cnt = jnp.concatenate([padded - counts, n_pad - pad_end[-1:]]).astype(jnp.int32)
    xs = _dispatch(x1, dest, zero_from, zero_cnt, n_pad, tm)
    d_ff = w_down.shape[1]
    wgu = w_gate_up.reshape(N_EXPERTS, D_MODEL, d_ff, 2).transpose(0, 3, 1, 2).astype(BF16)
    bgu = b_gate_up.astype(F32).reshape(N_EXPERTS, 1, d_ff, 2)
    ys = _expert_ffn(xs, block_expert, n_used, wgu, bgu[..., 0], bgu[..., 1], w_down.astype(F32),
                     b_down.astype(F32).reshape(N_EXPERTS, 1, D_MODEL))

    out = _final(dest, gates, x1, p.reshape(t, PLE_DIM), w_ple_gate.astype(BF16),
                 b_ple_gate.astype(F32).reshape(1, -1), w_ple.astype(BF16), ln2_g.reshape(1, -1),
                 ln2_b.reshape(1, -1), ys, min(256, t))
    return out.reshape(bsz, seq, D_MODEL)


def kernel(x, p, w_in, b_gate, conv_w, a_log, dt_bias, dn_norm_w, w_branch_a, w_branch_b, w_out, ln1_g, ln1_b,
           router_w, router_b, w_gate_up, b_gate_up, w_down, b_down, w_ple, w_ple_gate, b_ple_gate, ln2_g, ln2_b):
    assert w_in.shape[0] == DEPTH
    return _layer(x, p[0], w_in[0], b_gate[0], conv_w[0], a_log[0], dt_bias[0], dn_norm_w[0], w_branch_a[0],
                  w_branch_b[0], w_out[0], ln1_g[0], ln1_b[0], router_w[0], router_b[0], w_gate_up[0],
                  b_gate_up[0], w_down[0], b_down[0], w_ple[0], w_ple_gate[0], b_ple_gate[0], ln2_g[0], ln2_b[0])
```

```python
import functools

import numpy as np
import jax
import jax.numpy as jnp
from jax import lax
from jax.experimental import pallas as pl
from jax.experimental.pallas import tpu as pltpu

F32 = jnp.float32
BF16 = jnp.bfloat16

D_MODEL = 2048
HEAD_DIM = 128
A_HEADS = 8
DILATION_GROUPS = ((128, 1), (512, 4), (2048, 16))
N_GROUPS = 3
NEG_INF = -1e30
B_QK_HEADS = 8
B_V_HEADS = 16
B_DK = 128
B_DV = 128
CONV_W = 5
RMS_EPS = 1e-6
N_EXPERTS = 32
TOP_K = 4
SWIGLU_ALPHA = 1.702
SWIGLU_LIMIT = 7.0
PLE_DIM = 256
DEPTH = 1
DEEPNORM_ALPHA = (2 * DEPTH) ** 0.25
LN_EPS = 1e-5
A_QKV = N_GROUPS * A_HEADS * HEAD_DIM
B_QK = B_QK_HEADS * B_DK
B_VZ = B_V_HEADS * B_DV
B_GATES = 4 * B_V_HEADS

LANES = 128
N_SIDE = 64
Q_SUB = 128
DN_CHUNK = 64
DN_PAIR = B_V_HEADS // B_QK_HEADS
DN_STATE_CHUNKS = 2
TRI_BASE = 16
ROW_SUBLANES = 8
IN_PROJ_TM = 1024
IN_PROJ_TN = 1024
MOE_BLOCK = 1024
MOE_TF = 512
VMEM_LIMIT = 56 * 1024 * 1024
assert D_MODEL == 2 * ROW_SUBLANES * LANES


def _cparams(sem):
    return pltpu.CompilerParams(dimension_semantics=sem, vmem_limit_bytes=VMEM_LIMIT)


def _sigmoid(x):
    return 1.0 / (1.0 + jnp.exp(-x))


def _dot(a, b):
    return jnp.dot(a, b, preferred_element_type=F32)


def _dot_nt(a, b):
    return lax.dot_general(a, b, (((1,), (1,)), ((), ())), preferred_element_type=F32)


def _dot_tn(a, b):
    return lax.dot_general(a, b, (((0,), (0,)), ((), ())), preferred_element_type=F32)


def _mm_kernel(x_ref, w_ref, o_ref):
    o_ref[...] = _dot(x_ref[...], w_ref[...]).astype(o_ref.dtype)


def _matmul(x, w, out_dtype, tm, tn):
    m, k = x.shape
    n = w.shape[1]
    return pl.pallas_call(
        _mm_kernel,
        grid=(m // tm, n // tn),
        in_specs=[pl.BlockSpec((tm, k), lambda i, j: (i, 0)),
                  pl.BlockSpec((k, tn), lambda i, j: (0, j))],
        out_specs=pl.BlockSpec((tm, tn), lambda i, j: (i, j)),
        out_shape=jax.ShapeDtypeStruct((m, n), out_dtype),
        compiler_params=_cparams(("parallel", "parallel")),
        name="in_proj",
    )(x, w)


def _mm_classes_kernel(x_ref, w_ref, o_ref, acc_ref, *, dil, rows):
    y = _dot(x_ref[...], w_ref[...])
    for j in range(acc_ref.shape[0]):
        ls = slice(j * LANES, (j + 1) * LANES)
        acc_ref[j] = y[:, ls]
        for c in range(dil):
            o_ref[0, c, :, ls] = acc_ref[j, pl.ds(c, rows, stride=dil), :].astype(o_ref.dtype)


def _matmul_classes(x, w, bsz, dil, tm, tn):
    m, k = x.shape
    n = w.shape[1]
    seq = m // bsz
    tiles_per_seq = seq // tm
    rows = tm // dil
    kern = functools.partial(_mm_classes_kernel, dil=dil, rows=rows)
    return pl.pallas_call(
        kern,
        grid=(m // tm, n // tn),
        in_specs=[pl.BlockSpec((tm, k), lambda i, j: (i, 0)),
                  pl.BlockSpec((k, tn), lambda i, j: (0, j))],
        out_specs=pl.BlockSpec((1, dil, rows, tn), lambda i, j: (i // tiles_per_seq, 0, i % tiles_per_seq, j)),
        out_shape=jax.ShapeDtypeStruct((bsz, dil, seq // dil, n), BF16),
        scratch_shapes=[pltpu.VMEM((tn // LANES, tm, LANES), F32)],
        compiler_params=_cparams(("parallel", "parallel")),
        name=f"in_proj_dil{dil}",
    )(x, w)


def _attn_kernel(q_ref, kp_ref, kc_ref, kn_ref, vp_ref, vc_ref, vn_ref, o_ref, lse_ref, kbuf, vbuf,
                 *, dil, sub_len, tl, slopes):
    i0 = pl.program_id(2) * tl
    kbuf[0:N_SIDE, :] = kp_ref[...]
    kbuf[N_SIDE:N_SIDE + tl, :] = kc_ref[...]
    kbuf[N_SIDE + tl:, :] = kn_ref[...]
    vbuf[0:N_SIDE, :] = vp_ref[...]
    vbuf[N_SIDE:N_SIDE + tl, :] = vc_ref[...]
    vbuf[N_SIDE + tl:, :] = vn_ref[...]
    span = Q_SUB + 2 * N_SIDE
    qq = lax.broadcasted_iota(jnp.int32, (Q_SUB, span), 0)
    kk = lax.broadcasted_iota(jnp.int32, (Q_SUB, span), 1)
    delta = kk - N_SIDE - qq
    absd = jnp.abs(delta)
    band = absd <= N_SIDE
    dist = (dil * absd).astype(F32)
    lane = lax.broadcasted_iota(jnp.int32, (Q_SUB, LANES), 1)
    scale = HEAD_DIM ** -0.5
    for j in range(tl // Q_SUB):
        pos = i0 + (j * Q_SUB - N_SIDE) + kk
        valid = band & (pos >= 0) & (pos < sub_len)
        lse_tile = jnp.zeros((Q_SUB, LANES), F32)
        for h in range(A_HEADS):
            hs = slice(h * HEAD_DIM, (h + 1) * HEAD_DIM)
            q = q_ref[j * Q_SUB:(j + 1) * Q_SUB, hs]
            k = kbuf[j * Q_SUB:j * Q_SUB + span, hs]
            v = vbuf[j * Q_SUB:j * Q_SUB + span, hs]
            s = _dot_nt(q, k) * scale
            s = jnp.where(valid, s - float(slopes[h]) * dist, NEG_INF)
            m = jnp.max(s, axis=1, keepdims=True)
            p = jnp.exp(s - m)
            l = jnp.sum(p, axis=1, keepdims=True)
            o = _dot(p.astype(BF16), v) / l
            o_ref[j * Q_SUB:(j + 1) * Q_SUB, hs] = o.astype(o_ref.dtype)
            lse_tile = jnp.where(lane == h, m + jnp.log(l), lse_tile)
        lse_ref[j * Q_SUB:(j + 1) * Q_SUB, :] = lse_tile


def _attention_group(qkv, gi, dil, slopes):
    bsz, _, sub_len, _ = qkv.shape
    tl = min(512, sub_len)
    assert sub_len % tl == 0 and tl % Q_SUB == 0 and sub_len % N_SIDE == 0
    width = A_HEADS * HEAD_DIM
    halo_per_tile = tl // N_SIDE
    n_halo = sub_len // N_SIDE
    prev = lambda li: jnp.maximum(li * halo_per_tile - 1, 0)
    nxt = lambda li: jnp.minimum((li + 1) * halo_per_tile, n_halo - 1)
    halo = lambda col, rowf: pl.BlockSpec((None, None, N_SIDE, width), lambda b, c, li: (b, c, rowf(li), col))
    cur = lambda col: pl.BlockSpec((None, None, tl, width), lambda b, c, li: (b, c, li, col))
    kern = functools.partial(_attn_kernel, dil=dil, sub_len=sub_len, tl=tl, slopes=tuple(float(s) for s in slopes))
    return pl.pallas_call(
        kern,
        grid=(bsz, dil, sub_len // tl),
        in_specs=[cur(0), halo(1, prev), cur(1), halo(1, nxt), halo(2, prev), cur(2), halo(2, nxt)],
        out_specs=[pl.BlockSpec((None, None, tl, width), lambda b, c, li: (b, c, li, 0)),
                   pl.BlockSpec((None, None, tl, LANES), lambda b, c, li: (b, c, li, 0))],
        out_shape=[jax.ShapeDtypeStruct((bsz, dil, sub_len, width), BF16),
                   jax.ShapeDtypeStruct((bsz, dil, sub_len, LANES), F32)],
        scratch_shapes=[pltpu.VMEM((tl + 2 * N_SIDE, width), BF16),
                        pltpu.VMEM((tl + 2 * N_SIDE, width), BF16)],
        compiler_params=_cparams(("parallel", "parallel", "parallel")),
        name=f"dilated_attn_g{gi}",
    )(qkv, qkv, qkv, qkv, qkv, qkv, qkv)


def _alibi_slopes():
    n = N_GROUPS * A_HEADS
    s = 2.0 ** (-8.0 * np.arange(1, n + 1) / n)
    return s.astype(np.float32).reshape(N_GROUPS, A_HEADS)


def _dn_prep_kernel(prev_ref, cur_ref, next_ref, cw_ref, bab_ref, gp_ref, q_ref, k_ref, v_ref, g_ref, *, ts):
    ti = pl.program_id(1)
    nt = pl.num_programs(1)
    halo = CONV_W // 2
    keep_prev = (ti > 0).astype(F32)
    keep_next = (ti < nt - 1).astype(F32)
    for c in range((2 * B_QK + B_VZ) // LANES):
        cs = slice(c * LANES, (c + 1) * LANES)
        xp = prev_ref[0, :, cs].astype(F32)[8:16] * keep_prev
        xc = cur_ref[0, :, cs].astype(F32)
        xn = next_ref[0, :, cs].astype(F32)[0:8] * keep_next
        ext = jnp.concatenate([xp, xc, xn], axis=0)
        acc = jnp.zeros((ts, LANES), F32)
        for j in range(CONV_W):
            off = 8 - halo + j
            acc = acc + ext[off:off + ts, :] * cw_ref[j:j + 1, cs]
        y = acc * _sigmoid(acc)
        if c < 2 * B_QK // LANES:
            y = y * lax.rsqrt(jnp.sum(y * y, axis=1, keepdims=True) + 1e-6)
        if c < B_QK // LANES:
            q_ref[0, :, cs] = y * (B_DK ** -0.5)
        elif c < 2 * B_QK // LANES:
            k_ref[0, :, c * LANES - B_QK:(c + 1) * LANES - B_QK] = y
        else:
            v_ref[0, :, c * LANES - 2 * B_QK:(c + 1) * LANES - 2 * B_QK] = y
    x = bab_ref[0]
    neg_a = gp_ref[0:1, :]
    dtb = gp_ref[1:2, :]
    is_g = gp_ref[2:3, :] > 0.5
    z = x + dtb
    softplus = jnp.maximum(z, 0.0) + jnp.log(1.0 + jnp.exp(-jnp.abs(z)))
    g_ref[0] = jnp.where(is_g, neg_a * softplus, _sigmoid(x))


def _dn_prep(u_b, conv_w, bab, gate_params):
    bsz, seq, _ = u_b.shape
    ts = min(256, seq)
    cq = 2 * B_QK + B_VZ
    nhalo = seq // 16
    per = ts // 16
    kern = functools.partial(_dn_prep_kernel, ts=ts)
    return pl.pallas_call(
        kern,
        grid=(bsz, seq // ts),
        in_specs=[pl.BlockSpec((1, 16, cq), lambda b, t: (b, jnp.maximum(t * per - 1, 0), 0)),
                  pl.BlockSpec((1, ts, cq), lambda b, t: (b, t, 0)),
                  pl.BlockSpec((1, 16, cq), lambda b, t: (b, jnp.minimum((t + 1) * per, nhalo - 1), 0)),
                  pl.BlockSpec((8, cq), lambda b, t: (0, 0)),
                  pl.BlockSpec((1, ts, LANES), lambda b, t: (b, t, 0)),
                  pl.BlockSpec((8, LANES), lambda b, t: (0, 0))],
        out_specs=[pl.BlockSpec((1, ts, B_QK), lambda b, t: (b, t, 0)),
                   pl.BlockSpec((1, ts, B_QK), lambda b, t: (b, t, 0)),
                   pl.BlockSpec((1, ts, B_VZ), lambda b, t: (b, t, 0)),
                   pl.BlockSpec((1, ts, LANES), lambda b, t: (b, t, 0))],
        out_shape=[jax.ShapeDtypeStruct((bsz, seq, B_QK), F32),
                   jax.ShapeDtypeStruct((bsz, seq, B_QK), F32),
                   jax.ShapeDtypeStruct((bsz, seq, B_VZ), F32),
                   jax.ShapeDtypeStruct((bsz, seq, LANES), F32)],
        compiler_params=_cparams(("parallel", "parallel")),
        name="dn_prep",
    )(u_b, u_b, u_b, conv_w, bab, gate_params)


def _dn_intra_kernel(q_ref, k_ref, v_ref, g_ref, gt_ref, u_ref, wq_ref, kq_ref, et_ref, *, c):
    n_units = 2 * DN_PAIR
    w4 = n_units * c
    hp = lax.Precision.HIGHEST
    bf = lambda t_: t_.astype(BF16)
    ii = lax.broadcasted_iota(jnp.int32, (c, w4), 0)
    ll = lax.broadcasted_iota(jnp.int32, (c, w4), 1)
    jj = ll % c
    ub = ll // c
    ub_row = ub[0:1, :]
    lo = jnp.where(ub >= DN_PAIR, jj - ii, ii - jj)
    incl = lo >= 0
    strict = lo > 0
    eye = (ii == jj).astype(F32)
    blk = (ii // TRI_BASE) == (jj // TRI_BASE)

    def pack(parts, sel):
        out = parts[n_units - 1]
        for u_ in range(n_units - 2, -1, -1):
            out = jnp.where(sel == u_, parts[u_], out)
        return out

    unit_mask = [jnp.where(ub == u_, 1.0, 0.0).astype(BF16) for u_ in range(n_units)]

    def block_diag(y16):
        return jnp.concatenate([y16 * m_ for m_ in unit_mask], axis=0)

    def mm(xs, ys):
        return [_dot(bf(x_), block_diag(bf(y_))) for x_, y_ in zip(xs, ys)]

    ri = lax.broadcasted_iota(jnp.int32, (c, c), 0)
    ci = lax.broadcasted_iota(jnp.int32, (c, c), 1)
    g_all = g_ref[0]
    gc_dir = [jnp.dot((ci <= ri).astype(F32), g_all, precision=hp, preferred_element_type=F32),
              jnp.dot((ci >= ri).astype(F32), g_all, precision=hp, preferred_element_type=F32)]
    tri4 = (lo <= 0).astype(F32)
    gcr_all = jnp.dot(gt_ref[0, 0], tri4, precision=hp, preferred_element_type=F32)
    tot_all = jnp.sum(g_all, axis=0, keepdims=True)

    pairs = list(range(B_V_HEADS // DN_PAIR))
    qs_, ks_, lms, qkms, betas, egcs, kscales, etots = [], [], [], [], [], [], [], []
    for p in pairs:
        cs = slice(p * B_DK, (p + 1) * B_DK)
        q = q_ref[0, :, cs]
        k = k_ref[0, :, cs]
        k16 = bf(k)
        k4 = jnp.concatenate([k16] * n_units, axis=0)
        gram = _dot_nt(k16, k4)
        qk = _dot_nt(bf(q), k4)
        beta_u, gc_u, gcr_u, tot_u = [], [], [], []
        for u_ in range(n_units):
            d_, e_ = divmod(u_, DN_PAIR)
            h = p * DN_PAIR + e_
            bl = d_ * 2 * B_V_HEADS + h
            gl = bl + B_V_HEADS
            beta_u.append(g_all[:, bl:bl + 1])
            gc_u.append(gc_dir[d_][:, gl:gl + 1])
            gcr_u.append(gcr_all[gl:gl + 1, :])
            tot_u.append(tot_all[:, gl:gl + 1])
        gc_p = pack(gc_u, ub)
        gcr_p = pack(gcr_u, ub_row)
        tot_p = pack(tot_u, ub_row)
        dec = jnp.where(incl, jnp.exp(jnp.where(incl, gc_p - gcr_p, 0.0)), 0.0)
        lms.append(jnp.where(strict, pack(beta_u, ub) * gram * dec, 0.0))
        qkms.append(jnp.where(incl, qk * dec, 0.0))
        qs_.append(q)
        ks_.append(k)
        betas.append(beta_u)
        egcs.append([jnp.exp(g_) for g_ in gc_u])
        kscales.append(jnp.exp(tot_p - gcr_p))
        etots.append([jnp.exp(t_) for t_ in tot_u])

    d1 = [jnp.where(blk, lm, 0.0) for lm in lms]
    d2 = mm(d1, d1)
    d4 = mm(d2, d2)
    d8 = mm(d4, d4)
    tm_ = [eye - d_ for d_ in d1]
    for dk in (d2, d4, d8):
        tm_ = [a + b for a, b in zip(tm_, mm(tm_, dk))]
    size = TRI_BASE
    while size < c:
        off = ((ii // (2 * size)) == (jj // (2 * size))) & ((ii // size) != (jj // size))
        cm = [jnp.where(off, lm, 0.0) for lm in lms]
        pc = mm(tm_, cm)
        tm_ = [a - b for a, b in zip(tm_, mm(pc, tm_))]
        size *= 2

    lane2 = lax.broadcasted_iota(jnp.int32, (1, DN_PAIR * B_DV), 1)
    for p in pairs:
        q, k = qs_[p], ks_[p]
        rows = []
        for u_ in range(n_units):
            d_, e_ = divmod(u_, DN_PAIR)
            h = p * DN_PAIR + e_
            v = v_ref[0, :, h * B_DV:(h + 1) * B_DV]
            beta = betas[p][u_]
            rows.append(jnp.concatenate([v * beta, k * (beta * egcs[p][u_])], axis=1))
        uw = _dot(block_diag(bf(tm_[p])), bf(jnp.concatenate(rows, axis=0)))
        k_t = jnp.transpose(jnp.concatenate([k] * DN_PAIR, axis=0))
        for d_ in range(2):
            u0, u1 = d_ * DN_PAIR, d_ * DN_PAIR + 1
            u_ref[d_, 0, 0, p] = jnp.concatenate([uw[u0 * c:(u0 + 1) * c, :B_DV], uw[u1 * c:(u1 + 1) * c, :B_DV]], axis=1)
            wq = jnp.concatenate([uw[u0 * c:(u0 + 1) * c, B_DV:], q * egcs[p][u0],
                                  uw[u1 * c:(u1 + 1) * c, B_DV:], q * egcs[p][u1]], axis=0)
            wq_ref[d_, 0, 0, p] = bf(wq)
            ls = slice(d_ * DN_PAIR * c, (d_ + 1) * DN_PAIR * c)
            kq = jnp.concatenate([k_t * kscales[p][:, ls], qkms[p][:, ls]], axis=0)
            kq_ref[d_, 0, 0, p] = bf(kq)
            et_ref[d_, 0, 0, p:p + 1, :] = jnp.where(lane2 < B_DV, etots[p][u0], etots[p][u1])


def _dn_intra(qn, kn, vn, gts, gts_t):
    bsz, seq, _ = qn.shape
    c = DN_CHUNK
    assert DN_PAIR * c == B_DK and seq % c == 0
    n = seq // c
    npair = B_V_HEADS // DN_PAIR
    wide = DN_PAIR * B_DV
    kern = functools.partial(_dn_intra_kernel, c=c)
    out5 = lambda r, cdim: pl.BlockSpec((2, 1, 1, npair, r, cdim), lambda b, i: (0, b, i, 0, 0, 0))
    return pl.pallas_call(
        kern,
        grid=(bsz, n),
        in_specs=[pl.BlockSpec((1, c, B_QK), lambda b, i: (b, i, 0)),
                  pl.BlockSpec((1, c, B_QK), lambda b, i: (b, i, 0)),
                  pl.BlockSpec((1, c, B_VZ), lambda b, i: (b, i, 0)),
                  pl.BlockSpec((1, c, LANES), lambda b, i: (b, i, 0)),
                  pl.BlockSpec((1, 1, LANES, c), lambda b, i: (b, i, 0, 0))],
        out_specs=[out5(c, wide), out5(2 * DN_PAIR * c, B_DK), out5(B_DK + c, DN_PAIR * c),
                   pl.BlockSpec((2, 1, 1, npair, wide), lambda b, i: (0, b, i, 0, 0))],
        out_shape=[jax.ShapeDtypeStruct((2, bsz, n, npair, c, wide), F32),
                   jax.ShapeDtypeStruct((2, bsz, n, npair, 2 * DN_PAIR * c, B_DK), BF16),
                   jax.ShapeDtypeStruct((2, bsz, n, npair, B_DK + c, DN_PAIR * c), BF16),
                   jax.ShapeDtypeStruct((2, bsz, n, npair, wide), F32)],
        compiler_params=_cparams(("parallel", "parallel")),
        name="dn_intra",
    )(qn, kn, vn, gts, gts_t)


def _dn_state_kernel(uf_ref, wqf_ref, kqf_ref, etf_ref, ub_ref, wqb_ref, kqb_ref, etb_ref, of_ref, ob_ref, s_ref,
                     *, c, per):
    @pl.when(pl.program_id(1) == 0)
    def _():
        s_ref[...] = jnp.zeros_like(s_ref)

    npair = B_V_HEADS // DN_PAIR
    bf = lambda t_: t_.astype(BF16)
    chains = [(d_, p) for d_ in range(2) for p in range(npair)]
    refs = ((uf_ref, wqf_ref, kqf_ref, etf_ref, of_ref), (ub_ref, wqb_ref, kqb_ref, etb_ref, ob_ref))
    zero = jnp.zeros((c, B_DV), F32)
    for step in range(per):
        local = (step, per - 1 - step)
        states = [s_ref[d_, p] for d_, p in chains]
        a_res = [_dot(refs[d_][1][0, 0, local[d_], p], bf(s_)) for (d_, p), s_ in zip(chains, states)]
        b_res = []
        for (d_, p), a_ in zip(chains, a_res):
            u = refs[d_][0][0, 0, local[d_], p]
            v0 = u[:, :B_DV] - a_[0:c, :B_DV]
            v1 = u[:, B_DV:] - a_[2 * c:3 * c, B_DV:]
            bd_v = jnp.concatenate([jnp.concatenate([v0, zero], axis=1), jnp.concatenate([zero, v1], axis=1)], axis=0)
            b_res.append(_dot(refs[d_][2][0, 0, local[d_], p], bf(bd_v)))
        for (d_, p), a_, b_, s_ in zip(chains, a_res, b_res, states):
            s_ref[d_, p] = s_ * refs[d_][3][0, 0, local[d_], p:p + 1, :] + b_[:B_DK]
            o_ref = refs[d_][4]
            rs = slice(local[d_] * c, (local[d_] + 1) * c)
            o_ref[0, rs, (2 * p) * B_DV:(2 * p + 1) * B_DV] = a_[c:2 * c, :B_DV] + b_[B_DK:, :B_DV]
            o_ref[0, rs, (2 * p + 1) * B_DV:(2 * p + 2) * B_DV] = a_[3 * c:4 * c, B_DV:] + b_[B_DK:, B_DV:]


def _dn_state(u_all, wq_all, kq_all, et_all, seq):
    _, bsz, n, npair, c, wide = u_all.shape
    per = DN_STATE_CHUNKS if n % DN_STATE_CHUNKS == 0 else 1
    nb = n // per
    kern = functools.partial(_dn_state_kernel, c=c, per=per)
    fwd = lambda r, cdim: pl.BlockSpec((1, 1, per, npair, r, cdim), lambda b, i: (0, b, i, 0, 0, 0))
    bwd = lambda r, cdim: pl.BlockSpec((1, 1, per, npair, r, cdim), lambda b, i: (1, b, nb - 1 - i, 0, 0, 0))
    et_f = pl.BlockSpec((1, 1, per, npair, wide), lambda b, i: (0, b, i, 0, 0))
    et_b = pl.BlockSpec((1, 1, per, npair, wide), lambda b, i: (1, b, nb - 1 - i, 0, 0))
    shapes = ((c, wide), (2 * DN_PAIR * c, B_DK), (B_DK + c, DN_PAIR * c))
    return pl.pallas_call(
        kern,
        grid=(bsz, nb),
        in_specs=[fwd(*shapes[0]), fwd(*shapes[1]), fwd(*shapes[2]), et_f,
                  bwd(*shapes[0]), bwd(*shapes[1]), bwd(*shapes[2]), et_b],
        out_specs=[pl.BlockSpec((1, per * c, B_VZ), lambda b, i: (b, i, 0)),
                   pl.BlockSpec((1, per * c, B_VZ), lambda b, i: (b, nb - 1 - i, 0))],
        out_shape=[jax.ShapeDtypeStruct((bsz, seq, B_VZ), F32), jax.ShapeDtypeStruct((bsz, seq, B_VZ), F32)],
        scratch_shapes=[pltpu.VMEM((2, npair, B_DK, wide), F32)],
        compiler_params=_cparams(("parallel", "arbitrary")),
        name="dn_state",
    )(u_all, wq_all, kq_all, et_all, u_all, wq_all, kq_all, et_all)


def _branch_a_kernel(o0_ref, o1_ref, o2_ref, l0_ref, l1_ref, l2_ref, w_ref, gp_ref, bg_ref, out_ref, o_sc, l_sc,
                     *, dils, tm):
    for g, (o_ref, l_ref) in enumerate(((o0_ref, l0_ref), (o1_ref, l1_ref), (o2_ref, l2_ref))):
        r = dils[g]
        for c in range(r):
            l_sc[g, pl.ds(c, tm // r, stride=r), :] = l_ref[c]
            for h in range(A_HEADS):
                o_sc[g, h, pl.ds(c, tm // r, stride=r), :] = o_ref[c, :, h * HEAD_DIM:(h + 1) * HEAD_DIM].astype(F32)
    ls = [l_sc[g] for g in range(N_GROUPS)]
    m = jnp.maximum(jnp.maximum(ls[0], ls[1]), ls[2])
    es = [jnp.exp(l - m) for l in ls]
    den = es[0] + es[1] + es[2]
    ws = [e / den for e in es]
    parts = []
    for h in range(A_HEADS):
        hs = slice(h * HEAD_DIM, (h + 1) * HEAD_DIM)
        acc = ws[0][:, h:h + 1] * o_sc[0, h]
        acc = acc + ws[1][:, h:h + 1] * o_sc[1, h]
        acc = acc + ws[2][:, h:h + 1] * o_sc[2, h]
        parts.append(acc.astype(BF16))
    oa = jnp.concatenate(parts, axis=1)
    y = _dot(oa, w_ref[...])
    out_ref[...] = _sigmoid(gp_ref[...] + bg_ref[...]) * y


def _branch_a(outs, lses, w_a, gpre, b_gate, tm):
    bsz = outs[0].shape[0]
    dils = tuple(o.shape[1] for o in outs)
    seq = dils[0] * outs[0].shape[2]
    t = bsz * seq
    per_seq = seq // tm
    wd = A_HEADS * HEAD_DIM
    cls = lambda r, width: pl.BlockSpec((None, r, tm // r, width), lambda i: (i // per_seq, 0, i % per_seq, 0))
    kern = functools.partial(_branch_a_kernel, dils=dils, tm=tm)
    return pl.pallas_call(
        kern,
        grid=(t // tm,),
        in_specs=[cls(dils[0], wd), cls(dils[1], wd), cls(dils[2], wd),
                  cls(dils[0], LANES), cls(dils[1], LANES), cls(dils[2], LANES),
                  pl.BlockSpec((wd, D_MODEL), lambda i: (0, 0)),
                  pl.BlockSpec((tm, D_MODEL), lambda i: (i, 0)),
                  pl.BlockSpec((1, D_MODEL), lambda i: (0, 0))],
        out_specs=pl.BlockSpec((tm, D_MODEL), lambda i: (i, 0)),
        out_shape=jax.ShapeDtypeStruct((t, D_MODEL), F32),
        scratch_shapes=[pltpu.VMEM((N_GROUPS, A_HEADS, tm, HEAD_DIM), F32), pltpu.VMEM((N_GROUPS, tm, LANES), F32)],
        compiler_params=_cparams(("parallel",)),
        name="branch_a",
    )(*outs, *lses, w_a, gpre, b_gate)


def _branch_b_kernel(of_ref, ob_ref, z_ref, nw_ref, w_ref, gp_ref, bg_ref, a_ref, out_ref):
    nw = nw_ref[...]
    parts = []
    for h in range(B_V_HEADS):
        hs = slice(h * B_DV, (h + 1) * B_DV)
        o = of_ref[:, hs] + ob_ref[:, hs]
        z = z_ref[:, hs].astype(F32)
        o = o * lax.rsqrt(jnp.mean(o * o, axis=1, keepdims=True) + RMS_EPS) * nw * (z * _sigmoid(z))
        parts.append(o.astype(BF16))
    ob = jnp.concatenate(parts, axis=1)
    y = _dot(ob, w_ref[...])
    out_ref[...] = (a_ref[...] + _sigmoid(gp_ref[...] + bg_ref[...]) * y).astype(out_ref.dtype)


def _branch_b(o_f, o_b, u_b2d, norm_w, w_b, gpre, b_gate, a_part, tm):
    t = a_part.shape[0]
    return pl.pallas_call(
        _branch_b_kernel,
        grid=(t // tm,),
        in_specs=[pl.BlockSpec((tm, B_VZ), lambda i: (i, 0)),
                  pl.BlockSpec((tm, B_VZ), lambda i: (i, 0)),
                  pl.BlockSpec((tm, B_VZ), lambda i: (i, (2 * B_QK + B_VZ) // B_VZ)),
                  pl.BlockSpec((1, B_DV), lambda i: (0, 0)),
                  pl.BlockSpec((B_VZ, D_MODEL), lambda i: (0, 0)),
                  pl.BlockSpec((tm, D_MODEL), lambda i: (i, 1)),
                  pl.BlockSpec((1, D_MODEL), lambda i: (0, 1)),
                  pl.BlockSpec((tm, D_MODEL), lambda i: (i, 0))],
        out_specs=pl.BlockSpec((tm, D_MODEL), lambda i: (i, 0)),
        out_shape=jax.ShapeDtypeStruct((t, D_MODEL), BF16),
        compiler_params=_cparams(("parallel",)),
        name="branch_b",
    )(o_f, o_b, u_b2d, norm_w, w_b, gpre, b_gate, a_part)


def _layer_norm(y, g, b):
    mu = jnp.mean(y, axis=1, keepdims=True)
    yc = y - mu
    var = jnp.mean(yc * yc, axis=1, keepdims=True)
    return yc * lax.rsqrt(var + LN_EPS) * g + b


def _out_ln_kernel(m_ref, w_ref, x_ref, g_ref, b_ref, out_ref):
    mix = _dot(m_ref[...], w_ref[...])
    out_ref[...] = _layer_norm(DEEPNORM_ALPHA * x_ref[...] + mix, g_ref[...], b_ref[...])


def _out_ln(merged, w_out, x2d, ln_g, ln_b, tm):
    t = merged.shape[0]
    vec = pl.BlockSpec((1, D_MODEL), lambda i: (0, 0))
    return pl.pallas_call(
        _out_ln_kernel,
        grid=(t // tm,),
        in_specs=[pl.BlockSpec((tm, D_MODEL), lambda i: (i, 0)),
                  pl.BlockSpec((D_MODEL, D_MODEL), lambda i: (0, 0)),
                  pl.BlockSpec((tm, D_MODEL), lambda i: (i, 0)), vec, vec],
        out_specs=pl.BlockSpec((tm, D_MODEL), lambda i: (i, 0)),
        out_shape=jax.ShapeDtypeStruct((t, D_MODEL), F32),
        compiler_params=_cparams(("parallel",)),
        name="out_proj_ln1",
    )(merged, w_out, x2d, ln_g, ln_b)


def _router_kernel(x_ref, rw_ref, rb_ref, idx_ref, gate_ref, rank_ref, cnt_ref, carry, *, tm):
    @pl.when(pl.program_id(0) == 0)
    def _():
        carry[...] = jnp.zeros_like(carry)

    lane = lax.broadcasted_iota(jnp.int32, (tm, LANES), 1)
    lane_f = lane.astype(F32)
    logits = jnp.dot(x_ref[...], rw_ref[...], precision=lax.Precision.HIGHEST, preferred_element_type=F32) + rb_ref[...]
    cur = jnp.where(lane < N_EXPERTS, logits, -jnp.inf)
    vals, idxs = [], []
    for _k in range(TOP_K):
        m = jnp.max(cur, axis=1, keepdims=True)
        idx = jnp.min(jnp.where(cur == m, lane_f, float(LANES)), axis=1, keepdims=True).astype(jnp.int32)
        vals.append(m)
        idxs.append(idx)
        cur = jnp.where(lane == idx, -jnp.inf, cur)
    es = [jnp.exp(v - vals[0]) for v in vals]
    den = es[0] + es[1] + es[2] + es[3]
    onehot = jnp.zeros((tm, LANES), F32)
    for idx in idxs:
        onehot = onehot + (lane == idx).astype(F32)
    ri = lax.broadcasted_iota(jnp.int32, (tm, tm), 0)
    ci = lax.broadcasted_iota(jnp.int32, (tm, tm), 1)
    before = (ci < ri).astype(BF16)
    prefix = _dot(before, onehot.astype(BF16)) + carry[0:1, :]
    idx_out = jnp.zeros((tm, LANES), jnp.int32)
    gate_out = jnp.zeros((tm, LANES), F32)
    rank_out = jnp.zeros((tm, LANES), jnp.int32)
    for k in range(TOP_K):
        rk = jnp.sum(jnp.where(lane == idxs[k], prefix, 0.0), axis=1, keepdims=True)
        idx_out = jnp.where(lane == k, idxs[k], idx_out)
        gate_out = jnp.where(lane == k, es[k] / den, gate_out)
        rank_out = jnp.where(lane == k, rk.astype(jnp.int32), rank_out)
    idx_ref[...] = idx_out
    gate_ref[...] = gate_out
    rank_ref[...] = rank_out
    total = carry[0:1, :] + jnp.sum(onehot, axis=0, keepdims=True)
    carry[...] = jnp.broadcast_to(total, carry.shape)
    cnt_ref[...] = jnp.broadcast_to(total, cnt_ref.shape)


def _router(x1, rw, rb, tm):
    t = x1.shape[0]
    row = pl.BlockSpec((tm, LANES), lambda i: (i, 0))
    kern = functools.partial(_router_kernel, tm=tm)
    return pl.pallas_call(
        kern,
        grid=(t // tm,),
        in_specs=[pl.BlockSpec((tm, D_MODEL), lambda i: (i, 0)),
                  pl.BlockSpec((D_MODEL, LANES), lambda i: (0, 0)),
                  pl.BlockSpec((1, LANES), lambda i: (0, 0))],
        out_specs=[row, row, row, pl.BlockSpec((8, LANES), lambda i: (0, 0))],
        out_shape=[jax.ShapeDtypeStruct((t, LANES), jnp.int32),
                   jax.ShapeDtypeStruct((t, LANES), F32),
                   jax.ShapeDtypeStruct((t, LANES), jnp.int32),
                   jax.ShapeDtypeStruct((8, LANES), F32)],
        scratch_shapes=[pltpu.VMEM((8, LANES), F32)],
        compiler_params=_cparams(("arbitrary",)),
        name="router",
    )(x1, rw, rb)


def _pack_rows(src_ref, dst_ref):
    half = D_MODEL // 2
    n = src_ref.shape[0]
    for s in range(ROW_SUBLANES):
        lo = pltpu.bitcast(src_ref[:, s * LANES:(s + 1) * LANES].astype(BF16).astype(F32), jnp.uint32)
        hi = pltpu.bitcast(src_ref[:, half + s * LANES:half + (s + 1) * LANES].astype(BF16).astype(F32), jnp.uint32)
        dst_ref[pl.ds(s, n, stride=ROW_SUBLANES), :] = (lo >> 16) | hi


def _unpack_words(w):
    return pltpu.bitcast(w << 16, F32), pltpu.bitcast(w & jnp.uint32(0xFFFF0000), F32)


def _tile(ref, i):
    start = i * ROW_SUBLANES
    if not isinstance(i, int):
        start = pl.multiple_of(start, ROW_SUBLANES)
    return ref.at[pl.ds(start, ROW_SUBLANES)]


def _tile_copy(src, i, dst, j, sem):
    return pltpu.make_async_copy(_tile(src, i), _tile(dst, j), sem)


def _wait_tiles(hbm, n, sem):
    pltpu.make_async_copy(hbm.at[pl.ds(0, n * ROW_SUBLANES)], hbm.at[pl.ds(0, n * ROW_SUBLANES)], sem).wait()


def _dispatch_kernel(pad_start_ref, pad_cnt_ref, dest_ref, x_ref, xs_hbm, stage, zero, sem, *, tm, nt):
    i = pl.program_id(0)

    @pl.when(i < nt)
    def _():
        _pack_rows(x_ref, stage)

        def issue(t_, carry):
            src = _tile(stage, t_)
            for k in range(TOP_K):
                pltpu.make_async_copy(src, _tile(xs_hbm, dest_ref[0, 0, t_ * TOP_K + k]), sem).start()
            return carry

        lax.fori_loop(0, tm, issue, 0, unroll=2)
        _wait_tiles(xs_hbm, tm * TOP_K, sem)

    @pl.when((i >= nt) & (i < nt + N_EXPERTS))
    def _():
        e = i - nt
        zero[...] = jnp.zeros_like(zero)
        start = pad_start_ref[e]
        cnt = pad_cnt_ref[e]

        def issue(r, carry):
            _tile_copy(zero, 0, xs_hbm, start + r, sem).start()
            return carry

        lax.fori_loop(0, cnt, issue, 0)

        def wait(r, carry):
            _tile_copy(zero, 0, xs_hbm, start, sem).wait()
            return carry

        lax.fori_loop(0, cnt, wait, 0)

    @pl.when(i == nt + N_EXPERTS)
    def _():
        stage[...] = jnp.zeros_like(stage)
        start = pad_start_ref[N_EXPERTS]

        def fill(j, carry):
            piece = xs_hbm.at[pl.ds(pl.multiple_of((start + j * tm) * ROW_SUBLANES, ROW_SUBLANES), tm * ROW_SUBLANES)]
            cp = pltpu.make_async_copy(stage, piece, sem)
            cp.start()
            cp.wait()
            return carry

        lax.fori_loop(0, pad_cnt_ref[N_EXPERTS] // tm, fill, 0)


def _dispatch(x1, dest, pad_from, pad_cnt, n_pad, tm):
    t = x1.shape[0]
    nt = t // tm
    assert MOE_BLOCK % tm == 0 and pad_from.shape[0] == N_EXPERTS + 1
    kern = functools.partial(_dispatch_kernel, tm=tm, nt=nt)
    grid_spec = pltpu.PrefetchScalarGridSpec(
        num_scalar_prefetch=2,
        grid=(nt + pad_from.shape[0],),
        in_specs=[pl.BlockSpec((1, 1, tm * TOP_K), lambda i, ps, pc: (jnp.minimum(i, nt - 1), 0, 0),
                               memory_space=pltpu.SMEM),
                  pl.BlockSpec((tm, D_MODEL), lambda i, ps, pc: (jnp.minimum(i, nt - 1), 0))],
        out_specs=pl.BlockSpec(memory_space=pl.ANY),
        scratch_shapes=[pltpu.VMEM((tm * ROW_SUBLANES, LANES), jnp.uint32),
                        pltpu.VMEM((ROW_SUBLANES, LANES), jnp.uint32),
                        pltpu.SemaphoreType.DMA(())],
    )
    return pl.pallas_call(
        kern,
        grid_spec=grid_spec,
        out_shape=jax.ShapeDtypeStruct((n_pad * ROW_SUBLANES, LANES), jnp.uint32),
        compiler_params=_cparams(("arbitrary",)),
        name="moe_dispatch",
    )(pad_from, pad_cnt, dest.reshape(nt, 1, tm * TOP_K), x1)


def _ffn_kernel(be_ref, nused_ref, x_ref, wg_ref, wu_ref, bg_ref, bu_ref, wd_ref, bd_ref, o_ref, acc, xb):
    i = pl.program_id(0)
    f = pl.program_id(1)
    nf = pl.num_programs(1)
    half = D_MODEL // 2

    @pl.when(f == 0)
    def _():
        acc[...] = jnp.broadcast_to(bd_ref[0], acc.shape)
        for s in range(ROW_SUBLANES):
            lo, hi = _unpack_words(x_ref[pl.ds(s, MOE_BLOCK, stride=ROW_SUBLANES), :])
            xb[:, s * LANES:(s + 1) * LANES] = lo.astype(BF16)
            xb[:, half + s * LANES:half + (s + 1) * LANES] = hi.astype(BF16)

    @pl.when(i < nused_ref[0])
    def _():
        x = xb[...]
        g = _dot(x, wg_ref[0, 0]) + bg_ref[0]
        u = _dot(x, wu_ref[0, 0]) + bu_ref[0]
        gate = jnp.minimum(g, SWIGLU_LIMIT)
        up = jnp.clip(u, -SWIGLU_LIMIT, SWIGLU_LIMIT)
        act = (up + 1.0) * gate * _sigmoid(gate * SWIGLU_ALPHA)
        acc[...] += _dot(act.astype(BF16), wd_ref[0].astype(BF16))

    @pl.when(f == nf - 1)
    def _():
        _pack_rows(acc, o_ref)


def _expert_ffn(xs, block_expert, n_used, wgu, bg, bu, wd, bd):
    n_pad = xs.shape[0] // ROW_SUBLANES
    nb = n_pad // MOE_BLOCK
    d_ff = wgu.shape[3]
    nf = d_ff // MOE_TF
    row_blk = (MOE_BLOCK * ROW_SUBLANES, LANES)
    grid_spec = pltpu.PrefetchScalarGridSpec(
        num_scalar_prefetch=2,
        grid=(nb, nf),
        in_specs=[pl.BlockSpec(row_blk, lambda i, f, be, nu: (jnp.minimum(i, nu[0] - 1), 0)),
                  pl.BlockSpec((1, 1, D_MODEL, MOE_TF), lambda i, f, be, nu: (be[i], 0, 0, f)),
                  pl.BlockSpec((1, 1, D_MODEL, MOE_TF), lambda i, f, be, nu: (be[i], 1, 0, f)),
                  pl.BlockSpec((1, 1, MOE_TF), lambda i, f, be, nu: (be[i], 0, f)),
                  pl.BlockSpec((1, 1, MOE_TF), lambda i, f, be, nu: (be[i], 0, f)),
                  pl.BlockSpec((1, MOE_TF, D_MODEL), lambda i, f, be, nu: (be[i], f, 0)),
                  pl.BlockSpec((1, 1, D_MODEL), lambda i, f, be, nu: (be[i], 0, 0))],
        out_specs=pl.BlockSpec(row_blk, lambda i, f, be, nu: (i, 0)),
        scratch_shapes=[pltpu.VMEM((MOE_BLOCK, D_MODEL), F32), pltpu.VMEM((MOE_BLOCK, D_MODEL), BF16)],
    )
    return pl.pallas_call(
        _ffn_kernel,
        grid_spec=grid_spec,
        out_shape=jax.ShapeDtypeStruct((n_pad * ROW_SUBLANES, LANES), jnp.uint32),
        compiler_params=_cparams(("arbitrary", "arbitrary")),
        name="expert_ffn",
    )(block_expert, n_used, xs, wgu, wgu, bg, bu, wd, bd)


def _final_kernel(dest_ref, gate_ref, x_ref, p_ref, wpg_ref, bpg_ref, wple_ref, g_ref, b_ref, ys_hbm, out_ref,
                  rows, sem, *, tm):
    def issue(r, carry):
        _tile_copy(ys_hbm, dest_ref[0, 0, r], rows, r, sem).start()
        return carry

    lax.fori_loop(0, tm * TOP_K, issue, 0, unroll=8)
    x = x_ref[...]
    pg = _dot(x.astype(BF16), wpg_ref[...]) + bpg_ref[...]
    ple = _sigmoid(pg) * _dot(p_ref[...].astype(BF16), wple_ref[...])
    _wait_tiles(ys_hbm, tm * TOP_K, sem)
    gates = gate_ref[...]
    lo_parts, hi_parts = [], []
    for s in range(ROW_SUBLANES):
        acc_lo = acc_hi = None
        for k in range(TOP_K):
            lo, hi = _unpack_words(rows[pl.ds(k * ROW_SUBLANES + s, tm, stride=TOP_K * ROW_SUBLANES), :])
            g = gates[:, k:k + 1]
            acc_lo = g * lo if acc_lo is None else acc_lo + g * lo
            acc_hi = g * hi if acc_hi is None else acc_hi + g * hi
        lo_parts.append(acc_lo)
        hi_parts.append(acc_hi)
    y = jnp.concatenate(lo_parts + hi_parts, axis=1)
    out_ref[...] = _layer_norm(DEEPNORM_ALPHA * x + y + ple, g_ref[...], b_ref[...])


def _final(dest, gates, x1, p2d, w_pg, b_pg, w_ple, ln_g, ln_b, ys, tm):
    t = x1.shape[0]
    vec = pl.BlockSpec((1, D_MODEL), lambda i: (0, 0))
    kern = functools.partial(_final_kernel, tm=tm)
    return pl.pallas_call(
        kern,
        grid=(t // tm,),
        in_specs=[pl.BlockSpec((1, 1, tm * TOP_K), lambda i: (i, 0, 0), memory_space=pltpu.SMEM),
                  pl.BlockSpec((tm, LANES), lambda i: (i, 0)),
                  pl.BlockSpec((tm, D_MODEL), lambda i: (i, 0)),
                  pl.BlockSpec((tm, PLE_DIM), lambda i: (i, 0)),
                  pl.BlockSpec((D_MODEL, D_MODEL), lambda i: (0, 0)), vec,
                  pl.BlockSpec((PLE_DIM, D_MODEL), lambda i: (0, 0)), vec, vec,
                  pl.BlockSpec(memory_space=pl.ANY)],
        out_specs=pl.BlockSpec((tm, D_MODEL), lambda i: (i, 0)),
        out_shape=jax.ShapeDtypeStruct((t, D_MODEL), F32),
        scratch_shapes=[pltpu.VMEM((tm * TOP_K * ROW_SUBLANES, LANES), jnp.uint32), pltpu.SemaphoreType.DMA(())],
        compiler_params=_cparams(("arbitrary",)),
        name="combine_ple_ln2",
    )(dest.reshape(t // tm, 1, tm * TOP_K), gates, x1, p2d, w_pg, b_pg, w_ple, ln_g, ln_b, ys)


def _row_tile(t):
    return min(512, t)


def _layer(x, p, w_in, b_gate, conv_w, a_log, dt_bias, dn_norm_w, w_branch_a, w_branch_b, w_out, ln1_g, ln1_b,
           router_w, router_b, w_gate_up, b_gate_up, w_down, b_down, w_ple, w_ple_gate, b_ple_gate, ln2_g, ln2_b):
    bsz, seq, _ = x.shape
    t = bsz * seq
    tm = _row_tile(t)
    x2d = x.reshape(t, D_MODEL)
    xb = x2d.astype(BF16)

    c_a = 3 * A_QKV
    c_b = 2 * B_QK + 2 * B_VZ
    w_bf = w_in.astype(BF16)
    tp = min(IN_PROJ_TM, seq)
    u_b = _matmul(xb, w_bf[:, c_a:c_a + c_b], BF16, tp, IN_PROJ_TN)
    w_bab = jnp.pad(w_bf[:, c_a + c_b:c_a + c_b + B_GATES], ((0, 0), (0, LANES - B_GATES)))
    bab = _matmul(xb, w_bab, F32, tp, LANES)
    gpre = _matmul(xb, w_bf[:, c_a + c_b + B_GATES:], F32, tp, IN_PROJ_TN)

    slopes = _alibi_slopes()
    gw = A_HEADS * HEAD_DIM
    outs, lses = [], []
    for gi, (_win, dil) in enumerate(DILATION_GROUPS):
        w_g = jnp.concatenate([w_bf[:, part * A_QKV + gi * gw:part * A_QKV + (gi + 1) * gw] for part in range(3)], axis=1)
        o_g, l_g = _attention_group(_matmul_classes(xb, w_g, bsz, dil, tp, IN_PROJ_TN), gi, dil, slopes[gi])
        outs.append(o_g)
        lses.append(l_g)

    cw = jnp.pad(conv_w.astype(F32), ((0, 8 - CONV_W), (0, 0)))
    lane_is_g = (np.arange(LANES) % (2 * B_V_HEADS) >= B_V_HEADS) & (np.arange(LANES) < B_GATES)
    neg_a = jnp.zeros((LANES,), F32).at[B_V_HEADS:2 * B_V_HEADS].set(-jnp.exp(a_log[0].astype(F32)))
    neg_a = neg_a.at[3 * B_V_HEADS:4 * B_V_HEADS].set(-jnp.exp(a_log[1].astype(F32)))
    dtb = jnp.zeros((LANES,), F32).at[B_V_HEADS:2 * B_V_HEADS].set(dt_bias[0].astype(F32))
    dtb = dtb.at[3 * B_V_HEADS:4 * B_V_HEADS].set(dt_bias[1].astype(F32))
    gate_params = jnp.zeros((8, LANES), F32).at[0].set(neg_a).at[1].set(dtb).at[2].set(jnp.asarray(lane_is_g, F32))
    qn, kn, vn, gts = _dn_prep(u_b.reshape(bsz, seq, c_b), cw, bab.reshape(bsz, seq, LANES), gate_params)
    gts_t = gts.reshape(bsz, seq // DN_CHUNK, DN_CHUNK, LANES).transpose(0, 1, 3, 2)
    o_f, o_b = _dn_state(*_dn_intra(qn, kn, vn, gts, gts_t), seq)

    bg = b_gate.astype(F32).reshape(1, 2 * D_MODEL)
    a_part = _branch_a(outs, lses, w_branch_a.astype(BF16), gpre, bg, tm)
    merged = _branch_b(o_f.reshape(t, B_VZ), o_b.reshape(t, B_VZ), u_b, dn_norm_w.astype(F32).reshape(1, B_DV), w_branch_b.astype(BF16), gpre, bg,
                       a_part, tm)
    x1 = _out_ln(merged, w_out.astype(BF16), x2d, ln1_g.reshape(1, -1), ln1_b.reshape(1, -1), tm)

    rw = jnp.pad(router_w.astype(F32), ((0, 0), (0, LANES - N_EXPERTS)))
    rb = jnp.pad(router_b.astype(F32), (0, LANES - N_EXPERTS)).reshape(1, LANES)
    idx, gates, rank, cnt = _router(x1, rw, rb, tm)
    counts = cnt[0, :N_EXPERTS].astype(jnp.int32)
    padded = (counts + MOE_BLOCK - 1) // MOE_BLOCK * MOE_BLOCK
    pad_end = jnp.cumsum(padded)
    pad_start = pad_end - padded
    dest = pad_start[idx[:, :TOP_K]] + rank[:, :TOP_K]
    n_pad = t * TOP_K + N_EXPERTS * MOE_BLOCK
    nb = n_pad // MOE_BLOCK
    block_start = jnp.arange(nb, dtype=jnp.int32) * MOE_BLOCK
    block_expert = jnp.minimum(jnp.sum(pad_end[None, :] <= block_start[:, None], axis=1), N_EXPERTS - 1).astype(jnp.int32)
    n_used = (pad_end[-1:] // MOE_BLOCK).astype(jnp.int32)

    zero_from = jnp.concatenate([pad_start + counts, pad_end[-1:]]).astype(jnp.int32)
    zero_cnt = jnp.concatenate([padded - counts, n_pad - pad_end[-1:]]).astype(jnp.int32)
    xs = _dispatch(x1, dest, zero_from, zero_cnt, n_pad, tm)
    d_ff = w_down.shape[1]
    wgu = w_gate_up.reshape(N_EXPERTS, D_MODEL, d_ff, 2).transpose(0, 3, 1, 2).astype(BF16)
    bgu = b_gate_up.astype(F32).reshape(N_EXPERTS, 1, d_ff, 2)
    ys = _expert_ffn(xs, block_expert, n_used, wgu, bgu[..., 0], bgu[..., 1], w_down.astype(F32),
                     b_down.astype(F32).reshape(N_EXPERTS, 1, D_MODEL))

    out = _final(dest, gates, x1, p.reshape(t, PLE_DIM), w_ple_gate.astype(BF16),
                 b_ple_gate.astype(F32).reshape(1, -1), w_ple.astype(BF16), ln2_g.reshape(1, -1),
                 ln2_b.reshape(1, -1), ys, tm)
    return out.reshape(bsz, seq, D_MODEL)


def kernel(x, p, w_in, b_gate, conv_w, a_log, dt_bias, dn_norm_w, w_branch_a, w_branch_b, w_out, ln1_g, ln1_b,
           router_w, router_b, w_gate_up, b_gate_up, w_down, b_down, w_ple, w_ple_gate, b_ple_gate, ln2_g, ln2_b):
    assert w_in.shape[0] == DEPTH
    return _layer(x, p[0], w_in[0], b_gate[0], conv_w[0], a_log[0], dt_bias[0], dn_norm_w[0], w_branch_a[0],
                  w_branch_b[0], w_out[0], ln1_g[0], ln1_b[0], router_w[0], router_b[0], w_gate_up[0],
                  b_gate_up[0], w_down[0], b_down[0], w_ple[0], w_ple_gate[0], b_ple_gate[0], ln2_g[0], ln2_b[0])
```

```python
import functools

import numpy as np
import jax
import jax.numpy as jnp
from jax import lax
from jax.experimental import pallas as pl
from jax.experimental.pallas import tpu as pltpu

F32 = jnp.float32
BF16 = jnp.bfloat16

D_MODEL = 2048
HEAD_DIM = 128
A_HEADS = 8
DILATION_GROUPS = ((128, 1), (512, 4), (2048, 16))
N_GROUPS = 3
NEG_INF = -1e30
B_QK_HEADS = 8
B_V_HEADS = 16
B_DK = 128
B_DV = 128
CONV_W = 5
RMS_EPS = 1e-6
N_EXPERTS = 32
TOP_K = 4
SWIGLU_ALPHA = 1.702
SWIGLU_LIMIT = 7.0
PLE_DIM = 256
DEPTH = 1
DEEPNORM_ALPHA = (2 * DEPTH) ** 0.25
LN_EPS = 1e-5
A_QKV = N_GROUPS * A_HEADS * HEAD_DIM
B_QK = B_QK_HEADS * B_DK
B_VZ = B_V_HEADS * B_DV
B_GATES = 4 * B_V_HEADS

LANES = 128
N_SIDE = 64
Q_SUB = 128
DN_CHUNK = 64
DN_PAIR = B_V_HEADS // B_QK_HEADS
DN_STATE_CHUNKS = 2
TRI_BASE = 16
ROW_SUBLANES = 8
IN_PROJ_TM = 1024
IN_PROJ_TN = 1024
MOE_BLOCK = 1024
MOE_TF = 512
COMBINE_TM = 256
VMEM_LIMIT = 56 * 1024 * 1024
assert D_MODEL == 2 * ROW_SUBLANES * LANES


def _cparams(sem):
    return pltpu.CompilerParams(dimension_semantics=sem, vmem_limit_bytes=VMEM_LIMIT)


def _sigmoid(x):
    return 1.0 / (1.0 + jnp.exp(-x))


def _dot(a, b):
    return jnp.dot(a, b, preferred_element_type=F32)


def _dot_nt(a, b):
    return lax.dot_general(a, b, (((1,), (1,)), ((), ())), preferred_element_type=F32)


def _dot_tn(a, b):
    return lax.dot_general(a, b, (((0,), (0,)), ((), ())), preferred_element_type=F32)


def _mm_kernel(x_ref, w_ref, o_ref):
    o_ref[...] = _dot(x_ref[...], w_ref[...]).astype(o_ref.dtype)


def _matmul(x, w, out_dtype, tm, tn):
    m, k = x.shape
    n = w.shape[1]
    return pl.pallas_call(
        _mm_kernel,
        grid=(m // tm, n // tn),
        in_specs=[pl.BlockSpec((tm, k), lambda i, j: (i, 0)),
                  pl.BlockSpec((k, tn), lambda i, j: (0, j))],
        out_specs=pl.BlockSpec((tm, tn), lambda i, j: (i, j)),
        out_shape=jax.ShapeDtypeStruct((m, n), out_dtype),
        compiler_params=_cparams(("parallel", "parallel")),
        name="in_proj",
    )(x, w)


def _mm_classes_kernel(x_ref, w_ref, o_ref, acc_ref, *, dil, rows):
    y = _dot(x_ref[...], w_ref[...])
    for j in range(acc_ref.shape[0]):
        ls = slice(j * LANES, (j + 1) * LANES)
        acc_ref[j] = y[:, ls]
        for c in range(dil):
            o_ref[0, c, :, ls] = acc_ref[j, pl.ds(c, rows, stride=dil), :].astype(o_ref.dtype)


def _matmul_classes(x, w, bsz, dil, tm, tn):
    m, k = x.shape
    n = w.shape[1]
    seq = m // bsz
    tiles_per_seq = seq // tm
    rows = tm // dil
    kern = functools.partial(_mm_classes_kernel, dil=dil, rows=rows)
    return pl.pallas_call(
        kern,
        grid=(m // tm, n // tn),
        in_specs=[pl.BlockSpec((tm, k), lambda i, j: (i, 0)),
                  pl.BlockSpec((k, tn), lambda i, j: (0, j))],
        out_specs=pl.BlockSpec((1, dil, rows, tn), lambda i, j: (i // tiles_per_seq, 0, i % tiles_per_seq, j)),
        out_shape=jax.ShapeDtypeStruct((bsz, dil, seq // dil, n), BF16),
        scratch_shapes=[pltpu.VMEM((tn // LANES, tm, LANES), F32)],
        compiler_params=_cparams(("parallel", "parallel")),
        name=f"in_proj_dil{dil}",
    )(x, w)


def _attn_kernel(q_ref, kp_ref, kc_ref, kn_ref, vp_ref, vc_ref, vn_ref, o_ref, lse_ref, kbuf, vbuf,
                 *, dil, sub_len, tl, slopes):
    i0 = pl.program_id(2) * tl
    kbuf[0:N_SIDE, :] = kp_ref[...]
    kbuf[N_SIDE:N_SIDE + tl, :] = kc_ref[...]
    kbuf[N_SIDE + tl:, :] = kn_ref[...]
    vbuf[0:N_SIDE, :] = vp_ref[...]
    vbuf[N_SIDE:N_SIDE + tl, :] = vc_ref[...]
    vbuf[N_SIDE + tl:, :] = vn_ref[...]
    span = Q_SUB + 2 * N_SIDE
    qq = lax.broadcasted_iota(jnp.int32, (Q_SUB, span), 0)
    kk = lax.broadcasted_iota(jnp.int32, (Q_SUB, span), 1)
    delta = kk - N_SIDE - qq
    absd = jnp.abs(delta)
    band = absd <= N_SIDE
    dist = (dil * absd).astype(F32)
    lane = lax.broadcasted_iota(jnp.int32, (Q_SUB, LANES), 1)
    scale = HEAD_DIM ** -0.5
    for j in range(tl // Q_SUB):
        pos = i0 + (j * Q_SUB - N_SIDE) + kk
        valid = band & (pos >= 0) & (pos < sub_len)
        lse_tile = jnp.zeros((Q_SUB, LANES), F32)
        for h in range(A_HEADS):
            hs = slice(h * HEAD_DIM, (h + 1) * HEAD_DIM)
            q = q_ref[j * Q_SUB:(j + 1) * Q_SUB, hs]
            k = kbuf[j * Q_SUB:j * Q_SUB + span, hs]
            v = vbuf[j * Q_SUB:j * Q_SUB + span, hs]
            s = _dot_nt(q, k) * scale
            s = jnp.where(valid, s - float(slopes[h]) * dist, NEG_INF)
            m = jnp.max(s, axis=1, keepdims=True)
            p = jnp.exp(s - m)
            l = jnp.sum(p, axis=1, keepdims=True)
            o = _dot(p.astype(BF16), v) / l
            o_ref[j * Q_SUB:(j + 1) * Q_SUB, hs] = o.astype(o_ref.dtype)
            lse_tile = jnp.where(lane == h, m + jnp.log(l), lse_tile)
        lse_ref[j * Q_SUB:(j + 1) * Q_SUB, :] = lse_tile


def _attention_group(qkv, gi, dil, slopes):
    bsz, _, sub_len, _ = qkv.shape
    tl = min(512, sub_len)
    assert sub_len % tl == 0 and tl % Q_SUB == 0 and sub_len % N_SIDE == 0
    width = A_HEADS * HEAD_DIM
    halo_per_tile = tl // N_SIDE
    n_halo = sub_len // N_SIDE
    prev = lambda li: jnp.maximum(li * halo_per_tile - 1, 0)
    nxt = lambda li: jnp.minimum((li + 1) * halo_per_tile, n_halo - 1)
    halo = lambda col, rowf: pl.BlockSpec((None, None, N_SIDE, width), lambda b, c, li: (b, c, rowf(li), col))
    cur = lambda col: pl.BlockSpec((None, None, tl, width), lambda b, c, li: (b, c, li, col))
    kern = functools.partial(_attn_kernel, dil=dil, sub_len=sub_len, tl=tl, slopes=tuple(float(s) for s in slopes))
    return pl.pallas_call(
        kern,
        grid=(bsz, dil, sub_len // tl),
        in_specs=[cur(0), halo(1, prev), cur(1), halo(1, nxt), halo(2, prev), cur(2), halo(2, nxt)],
        out_specs=[pl.BlockSpec((None, None, tl, width), lambda b, c, li: (b, c, li, 0)),
                   pl.BlockSpec((None, None, tl, LANES), lambda b, c, li: (b, c, li, 0))],
        out_shape=[jax.ShapeDtypeStruct((bsz, dil, sub_len, width), BF16),
                   jax.ShapeDtypeStruct((bsz, dil, sub_len, LANES), F32)],
        scratch_shapes=[pltpu.VMEM((tl + 2 * N_SIDE, width), BF16),
                        pltpu.VMEM((tl + 2 * N_SIDE, width), BF16)],
        compiler_params=_cparams(("parallel", "parallel", "parallel")),
        name=f"dilated_attn_g{gi}",
    )(qkv, qkv, qkv, qkv, qkv, qkv, qkv)


def _alibi_slopes():
    n = N_GROUPS * A_HEADS
    s = 2.0 ** (-8.0 * np.arange(1, n + 1) / n)
    return s.astype(np.float32).reshape(N_GROUPS, A_HEADS)


def _dn_prep_kernel(prev_ref, cur_ref, next_ref, cw_ref, bab_ref, gp_ref, q_ref, k_ref, v_ref, g_ref, *, ts):
    ti = pl.program_id(1)
    nt = pl.num_programs(1)
    halo = CONV_W // 2
    keep_prev = (ti > 0).astype(F32)
    keep_next = (ti < nt - 1).astype(F32)
    for c in range((2 * B_QK + B_VZ) // LANES):
        cs = slice(c * LANES, (c + 1) * LANES)
        xp = prev_ref[0, :, cs].astype(F32)[8:16] * keep_prev
        xc = cur_ref[0, :, cs].astype(F32)
        xn = next_ref[0, :, cs].astype(F32)[0:8] * keep_next
        ext = jnp.concatenate([xp, xc, xn], axis=0)
        acc = jnp.zeros((ts, LANES), F32)
        for j in range(CONV_W):
            off = 8 - halo + j
            acc = acc + ext[off:off + ts, :] * cw_ref[j:j + 1, cs]
        y = acc * _sigmoid(acc)
        if c < 2 * B_QK // LANES:
            y = y * lax.rsqrt(jnp.sum(y * y, axis=1, keepdims=True) + 1e-6)
        if c < B_QK // LANES:
            q_ref[0, :, cs] = y * (B_DK ** -0.5)
        elif c < 2 * B_QK // LANES:
            k_ref[0, :, c * LANES - B_QK:(c + 1) * LANES - B_QK] = y
        else:
            v_ref[0, :, c * LANES - 2 * B_QK:(c + 1) * LANES - 2 * B_QK] = y
    x = bab_ref[0]
    neg_a = gp_ref[0:1, :]
    dtb = gp_ref[1:2, :]
    is_g = gp_ref[2:3, :] > 0.5
    z = x + dtb
    softplus = jnp.maximum(z, 0.0) + jnp.log(1.0 + jnp.exp(-jnp.abs(z)))
    g_ref[0] = jnp.where(is_g, neg_a * softplus, _sigmoid(x))


def _dn_prep(u_b, conv_w, bab, gate_params):
    bsz, seq, _ = u_b.shape
    ts = min(256, seq)
    cq = 2 * B_QK + B_VZ
    nhalo = seq // 16
    per = ts // 16
    kern = functools.partial(_dn_prep_kernel, ts=ts)
    return pl.pallas_call(
        kern,
        grid=(bsz, seq // ts),
        in_specs=[pl.BlockSpec((1, 16, cq), lambda b, t: (b, jnp.maximum(t * per - 1, 0), 0)),
                  pl.BlockSpec((1, ts, cq), lambda b, t: (b, t, 0)),
                  pl.BlockSpec((1, 16, cq), lambda b, t: (b, jnp.minimum((t + 1) * per, nhalo - 1), 0)),
                  pl.BlockSpec((8, cq), lambda b, t: (0, 0)),
                  pl.BlockSpec((1, ts, LANES), lambda b, t: (b, t, 0)),
                  pl.BlockSpec((8, LANES), lambda b, t: (0, 0))],
        out_specs=[pl.BlockSpec((1, ts, B_QK), lambda b, t: (b, t, 0)),
                   pl.BlockSpec((1, ts, B_QK), lambda b, t: (b, t, 0)),
                   pl.BlockSpec((1, ts, B_VZ), lambda b, t: (b, t, 0)),
                   pl.BlockSpec((1, ts, LANES), lambda b, t: (b, t, 0))],
        out_shape=[jax.ShapeDtypeStruct((bsz, seq, B_QK), F32),
                   jax.ShapeDtypeStruct((bsz, seq, B_QK), F32),
                   jax.ShapeDtypeStruct((bsz, seq, B_VZ), F32),
                   jax.ShapeDtypeStruct((bsz, seq, LANES), F32)],
        compiler_params=_cparams(("parallel", "parallel")),
        name="dn_prep",
    )(u_b, u_b, u_b, conv_w, bab, gate_params)


def _dn_intra_kernel(q_ref, k_ref, v_ref, g_ref, gt_ref, u_ref, wq_ref, kq_ref, et_ref, *, c):
    n_units = 2 * DN_PAIR
    w4 = n_units * c
    hp = lax.Precision.HIGHEST
    bf = lambda t_: t_.astype(BF16)
    ii = lax.broadcasted_iota(jnp.int32, (c, w4), 0)
    ll = lax.broadcasted_iota(jnp.int32, (c, w4), 1)
    jj = ll % c
    ub = ll // c
    ub_row = ub[0:1, :]
    lo = jnp.where(ub >= DN_PAIR, jj - ii, ii - jj)
    incl = lo >= 0
    strict = lo > 0
    eye = (ii == jj).astype(F32)
    blk = (ii // TRI_BASE) == (jj // TRI_BASE)

    def pack(parts, sel):
        out = parts[n_units - 1]
        for u_ in range(n_units - 2, -1, -1):
            out = jnp.where(sel == u_, parts[u_], out)
        return out

    unit_mask = [jnp.where(ub == u_, 1.0, 0.0).astype(BF16) for u_ in range(n_units)]

    def block_diag(y16):
        return jnp.concatenate([y16 * m_ for m_ in unit_mask], axis=0)

    def mm(xs, ys):
        return [_dot(bf(x_), block_diag(bf(y_))) for x_, y_ in zip(xs, ys)]

    ri = lax.broadcasted_iota(jnp.int32, (c, c), 0)
    ci = lax.broadcasted_iota(jnp.int32, (c, c), 1)
    g_all = g_ref[0]
    gc_dir = [jnp.dot((ci <= ri).astype(F32), g_all, precision=hp, preferred_element_type=F32),
              jnp.dot((ci >= ri).astype(F32), g_all, precision=hp, preferred_element_type=F32)]
    tri4 = (lo <= 0).astype(F32)
    gcr_all = jnp.dot(gt_ref[0, 0], tri4, precision=hp, preferred_element_type=F32)
    tot_all = jnp.sum(g_all, axis=0, keepdims=True)

    pairs = list(range(B_V_HEADS // DN_PAIR))
    qs_, ks_, lms, qkms, betas, egcs, kscales, etots = [], [], [], [], [], [], [], []
    for p in pairs:
        cs = slice(p * B_DK, (p + 1) * B_DK)
        q = q_ref[0, :, cs]
        k = k_ref[0, :, cs]
        k16 = bf(k)
        k4 = jnp.concatenate([k16] * n_units, axis=0)
        gram = _dot_nt(k16, k4)
        qk = _dot_nt(bf(q), k4)
        beta_u, gc_u, gcr_u, tot_u = [], [], [], []
        for u_ in range(n_units):
            d_, e_ = divmod(u_, DN_PAIR)
            h = p * DN_PAIR + e_
            bl = d_ * 2 * B_V_HEADS + h
            gl = bl + B_V_HEADS
            beta_u.append(g_all[:, bl:bl + 1])
            gc_u.append(gc_dir[d_][:, gl:gl + 1])
            gcr_u.append(gcr_all[gl:gl + 1, :])
            tot_u.append(tot_all[:, gl:gl + 1])
        gc_p = pack(gc_u, ub)
        gcr_p = pack(gcr_u, ub_row)
        tot_p = pack(tot_u, ub_row)
        dec = jnp.where(incl, jnp.exp(jnp.where(incl, gc_p - gcr_p, 0.0)), 0.0)
        lms.append(jnp.where(strict, pack(beta_u, ub) * gram * dec, 0.0))
        qkms.append(jnp.where(incl, qk * dec, 0.0))
        qs_.append(q)
        ks_.append(k)
        betas.append(beta_u)
        egcs.append([jnp.exp(g_) for g_ in gc_u])
        kscales.append(jnp.exp(tot_p - gcr_p))
        etots.append([jnp.exp(t_) for t_ in tot_u])

    d1 = [jnp.where(blk, lm, 0.0) for lm in lms]
    d2 = mm(d1, d1)
    d4 = mm(d2, d2)
    d8 = mm(d4, d4)
    tm_ = [eye - d_ for d_ in d1]
    for dk in (d2, d4, d8):
        tm_ = [a + b for a, b in zip(tm_, mm(tm_, dk))]
    size = TRI_BASE
    while size < c:
        off = ((ii // (2 * size)) == (jj // (2 * size))) & ((ii // size) != (jj // size))
        cm = [jnp.where(off, lm, 0.0) for lm in lms]
        pc = mm(tm_, cm)
        tm_ = [a - b for a, b in zip(tm_, mm(pc, tm_))]
        size *= 2

    lane2 = lax.broadcasted_iota(jnp.int32, (1, DN_PAIR * B_DV), 1)
    for p in pairs:
        q, k = qs_[p], ks_[p]
        rows = []
        for u_ in range(n_units):
            d_, e_ = divmod(u_, DN_PAIR)
            h = p * DN_PAIR + e_
            v = v_ref[0, :, h * B_DV:(h + 1) * B_DV]
            beta = betas[p][u_]
            rows.append(jnp.concatenate([v * beta, k * (beta * egcs[p][u_])], axis=1))
        uw = _dot(block_diag(bf(tm_[p])), bf(jnp.concatenate(rows, axis=0)))
        k_t = jnp.transpose(jnp.concatenate([k] * DN_PAIR, axis=0))
        for d_ in range(2):
            u0, u1 = d_ * DN_PAIR, d_ * DN_PAIR + 1
            u_ref[d_, 0, 0, p] = jnp.concatenate([uw[u0 * c:(u0 + 1) * c, :B_DV], uw[u1 * c:(u1 + 1) * c, :B_DV]], axis=1)
            wq = jnp.concatenate([uw[u0 * c:(u0 + 1) * c, B_DV:], q * egcs[p][u0],
                                  uw[u1 * c:(u1 + 1) * c, B_DV:], q * egcs[p][u1]], axis=0)
            wq_ref[d_, 0, 0, p] = bf(wq)
            ls = slice(d_ * DN_PAIR * c, (d_ + 1) * DN_PAIR * c)
            kq = jnp.concatenate([k_t * kscales[p][:, ls], qkms[p][:, ls]], axis=0)
            kq_ref[d_, 0, 0, p] = bf(kq)
            et_ref[d_, 0, 0, p:p + 1, :] = jnp.where(lane2 < B_DV, etots[p][u0], etots[p][u1])


def _dn_intra(qn, kn, vn, gts, gts_t):
    bsz, seq, _ = qn.shape
    c = DN_CHUNK
    assert DN_PAIR * c == B_DK and seq % c == 0
    n = seq // c
    npair = B_V_HEADS // DN_PAIR
    wide = DN_PAIR * B_DV
    kern = functools.partial(_dn_intra_kernel, c=c)
    out5 = lambda r, cdim: pl.BlockSpec((2, 1, 1, npair, r, cdim), lambda b, i: (0, b, i, 0, 0, 0))
    return pl.pallas_call(
        kern,
        grid=(bsz, n),
        in_specs=[pl.BlockSpec((1, c, B_QK), lambda b, i: (b, i, 0)),
                  pl.BlockSpec((1, c, B_QK), lambda b, i: (b, i, 0)),
                  pl.BlockSpec((1, c, B_VZ), lambda b, i: (b, i, 0)),
                  pl.BlockSpec((1, c, LANES), lambda b, i: (b, i, 0)),
                  pl.BlockSpec((1, 1, LANES, c), lambda b, i: (b, i, 0, 0))],
        out_specs=[out5(c, wide), out5(2 * DN_PAIR * c, B_DK), out5(B_DK + c, DN_PAIR * c),
                   pl.BlockSpec((2, 1, 1, npair, wide), lambda b, i: (0, b, i, 0, 0))],
        out_shape=[jax.ShapeDtypeStruct((2, bsz, n, npair, c, wide), F32),
                   jax.ShapeDtypeStruct((2, bsz, n, npair, 2 * DN_PAIR * c, B_DK), BF16),
                   jax.ShapeDtypeStruct((2, bsz, n, npair, B_DK + c, DN_PAIR * c), BF16),
                   jax.ShapeDtypeStruct((2, bsz, n, npair, wide), F32)],
        compiler_params=_cparams(("parallel", "parallel")),
        name="dn_intra",
    )(qn, kn, vn, gts, gts_t)


def _dn_state_kernel(uf_ref, wqf_ref, kqf_ref, etf_ref, ub_ref, wqb_ref, kqb_ref, etb_ref, of_ref, ob_ref, s_ref,
                     *, c, per):
    @pl.when(pl.program_id(1) == 0)
    def _():
        s_ref[...] = jnp.zeros_like(s_ref)

    npair = B_V_HEADS // DN_PAIR
    bf = lambda t_: t_.astype(BF16)
    chains = [(d_, p) for d_ in range(2) for p in range(npair)]
    refs = ((uf_ref, wqf_ref, kqf_ref, etf_ref, of_ref), (ub_ref, wqb_ref, kqb_ref, etb_ref, ob_ref))
    zero = jnp.zeros((c, B_DV), F32)
    for step in range(per):
        local = (step, per - 1 - step)
        states = [s_ref[d_, p] for d_, p in chains]
        a_res = [_dot(refs[d_][1][0, 0, local[d_], p], bf(s_)) for (d_, p), s_ in zip(chains, states)]
        b_res = []
        for (d_, p), a_ in zip(chains, a_res):
            u = refs[d_][0][0, 0, local[d_], p]
            v0 = u[:, :B_DV] - a_[0:c, :B_DV]
            v1 = u[:, B_DV:] - a_[2 * c:3 * c, B_DV:]
            bd_v = jnp.concatenate([jnp.concatenate([v0, zero], axis=1), jnp.concatenate([zero, v1], axis=1)], axis=0)
            b_res.append(_dot(refs[d_][2][0, 0, local[d_], p], bf(bd_v)))
        for (d_, p), a_, b_, s_ in zip(chains, a_res, b_res, states):
            s_ref[d_, p] = s_ * refs[d_][3][0, 0, local[d_], p:p + 1, :] + b_[:B_DK]
            o_ref = refs[d_][4]
            rs = slice(local[d_] * c, (local[d_] + 1) * c)
            o_ref[0, rs, (2 * p) * B_DV:(2 * p + 1) * B_DV] = a_[c:2 * c, :B_DV] + b_[B_DK:, :B_DV]
            o_ref[0, rs, (2 * p + 1) * B_DV:(2 * p + 2) * B_DV] = a_[3 * c:4 * c, B_DV:] + b_[B_DK:, B_DV:]


def _dn_state(u_all, wq_all, kq_all, et_all, seq):
    _, bsz, n, npair, c, wide = u_all.shape
    per = DN_STATE_CHUNKS if n % DN_STATE_CHUNKS == 0 else 1
    nb = n // per
    kern = functools.partial(_dn_state_kernel, c=c, per=per)
    fwd = lambda r, cdim: pl.BlockSpec((1, 1, per, npair, r, cdim), lambda b, i: (0, b, i, 0, 0, 0))
    bwd = lambda r, cdim: pl.BlockSpec((1, 1, per, npair, r, cdim), lambda b, i: (1, b, nb - 1 - i, 0, 0, 0))
    et_f = pl.BlockSpec((1, 1, per, npair, wide), lambda b, i: (0, b, i, 0, 0))
    et_b = pl.BlockSpec((1, 1, per, npair, wide), lambda b, i: (1, b, nb - 1 - i, 0, 0))
    shapes = ((c, wide), (2 * DN_PAIR * c, B_DK), (B_DK + c, DN_PAIR * c))
    return pl.pallas_call(
        kern,
        grid=(bsz, nb),
        in_specs=[fwd(*shapes[0]), fwd(*shapes[1]), fwd(*shapes[2]), et_f,
                  bwd(*shapes[0]), bwd(*shapes[1]), bwd(*shapes[2]), et_b],
        out_specs=[pl.BlockSpec((1, per * c, B_VZ), lambda b, i: (b, i, 0)),
                   pl.BlockSpec((1, per * c, B_VZ), lambda b, i: (b, nb - 1 - i, 0))],
        out_shape=[jax.ShapeDtypeStruct((bsz, seq, B_VZ), F32), jax.ShapeDtypeStruct((bsz, seq, B_VZ), F32)],
        scratch_shapes=[pltpu.VMEM((2, npair, B_DK, wide), F32)],
        compiler_params=_cparams(("parallel", "arbitrary")),
        name="dn_state",
    )(u_all, wq_all, kq_all, et_all, u_all, wq_all, kq_all, et_all)


def _branch_a_kernel(o0_ref, o1_ref, o2_ref, l0_ref, l1_ref, l2_ref, w_ref, gp_ref, bg_ref, out_ref, o_sc, l_sc,
                     *, dils, tm):
    for g, (o_ref, l_ref) in enumerate(((o0_ref, l0_ref), (o1_ref, l1_ref), (o2_ref, l2_ref))):
        r = dils[g]
        for c in range(r):
            l_sc[g, pl.ds(c, tm // r, stride=r), :] = l_ref[c]
            for h in range(A_HEADS):
                o_sc[g, h, pl.ds(c, tm // r, stride=r), :] = o_ref[c, :, h * HEAD_DIM:(h + 1) * HEAD_DIM].astype(F32)
    ls = [l_sc[g] for g in range(N_GROUPS)]
    m = jnp.maximum(jnp.maximum(ls[0], ls[1]), ls[2])
    es = [jnp.exp(l - m) for l in ls]
    den = es[0] + es[1] + es[2]
    ws = [e / den for e in es]
    parts = []
    for h in range(A_HEADS):
        hs = slice(h * HEAD_DIM, (h + 1) * HEAD_DIM)
        acc = ws[0][:, h:h + 1] * o_sc[0, h]
        acc = acc + ws[1][:, h:h + 1] * o_sc[1, h]
        acc = acc + ws[2][:, h:h + 1] * o_sc[2, h]
        parts.append(acc.astype(BF16))
    oa = jnp.concatenate(parts, axis=1)
    y = _dot(oa, w_ref[...])
    out_ref[...] = _sigmoid(gp_ref[...] + bg_ref[...]) * y


def _branch_a(outs, lses, w_a, gpre, b_gate, tm):
    bsz = outs[0].shape[0]
    dils = tuple(o.shape[1] for o in outs)
    seq = dils[0] * outs[0].shape[2]
    t = bsz * seq
    per_seq = seq // tm
    wd = A_HEADS * HEAD_DIM
    cls = lambda r, width: pl.BlockSpec((None, r, tm // r, width), lambda i: (i // per_seq, 0, i % per_seq, 0))
    kern = functools.partial(_branch_a_kernel, dils=dils, tm=tm)
    return pl.pallas_call(
        kern,
        grid=(t // tm,),
        in_specs=[cls(dils[0], wd), cls(dils[1], wd), cls(dils[2], wd),
                  cls(dils[0], LANES), cls(dils[1], LANES), cls(dils[2], LANES),
                  pl.BlockSpec((wd, D_MODEL), lambda i: (0, 0)),
                  pl.BlockSpec((tm, D_MODEL), lambda i: (i, 0)),
                  pl.BlockSpec((1, D_MODEL), lambda i: (0, 0))],
        out_specs=pl.BlockSpec((tm, D_MODEL), lambda i: (i, 0)),
        out_shape=jax.ShapeDtypeStruct((t, D_MODEL), F32),
        scratch_shapes=[pltpu.VMEM((N_GROUPS, A_HEADS, tm, HEAD_DIM), F32), pltpu.VMEM((N_GROUPS, tm, LANES), F32)],
        compiler_params=_cparams(("parallel",)),
        name="branch_a",
    )(*outs, *lses, w_a, gpre, b_gate)


def _branch_b_kernel(of_ref, ob_ref, z_ref, nw_ref, w_ref, gp_ref, bg_ref, a_ref, out_ref):
    nw = nw_ref[...]
    parts = []
    for h in range(B_V_HEADS):
        hs = slice(h * B_DV, (h + 1) * B_DV)
        o = of_ref[:, hs] + ob_ref[:, hs]
        z = z_ref[:, hs].astype(F32)
        o = o * lax.rsqrt(jnp.mean(o * o, axis=1, keepdims=True) + RMS_EPS) * nw * (z * _sigmoid(z))
        parts.append(o.astype(BF16))
    ob = jnp.concatenate(parts, axis=1)
    y = _dot(ob, w_ref[...])
    out_ref[...] = (a_ref[...] + _sigmoid(gp_ref[...] + bg_ref[...]) * y).astype(out_ref.dtype)


def _branch_b(o_f, o_b, u_b2d, norm_w, w_b, gpre, b_gate, a_part, tm):
    t = a_part.shape[0]
    return pl.pallas_call(
        _branch_b_kernel,
        grid=(t // tm,),
        in_specs=[pl.BlockSpec((tm, B_VZ), lambda i: (i, 0)),
                  pl.BlockSpec((tm, B_VZ), lambda i: (i, 0)),
                  pl.BlockSpec((tm, B_VZ), lambda i: (i, (2 * B_QK + B_VZ) // B_VZ)),
                  pl.BlockSpec((1, B_DV), lambda i: (0, 0)),
                  pl.BlockSpec((B_VZ, D_MODEL), lambda i: (0, 0)),
                  pl.BlockSpec((tm, D_MODEL), lambda i: (i, 1)),
                  pl.BlockSpec((1, D_MODEL), lambda i: (0, 1)),
                  pl.BlockSpec((tm, D_MODEL), lambda i: (i, 0))],
        out_specs=pl.BlockSpec((tm, D_MODEL), lambda i: (i, 0)),
        out_shape=jax.ShapeDtypeStruct((t, D_MODEL), BF16),
        compiler_params=_cparams(("parallel",)),
        name="branch_b",
    )(o_f, o_b, u_b2d, norm_w, w_b, gpre, b_gate, a_part)


def _layer_norm(y, g, b):
    mu = jnp.mean(y, axis=1, keepdims=True)
    yc = y - mu
    var = jnp.mean(yc * yc, axis=1, keepdims=True)
    return yc * lax.rsqrt(var + LN_EPS) * g + b


def _out_ln_kernel(m_ref, w_ref, x_ref, g_ref, b_ref, out_ref):
    mix = _dot(m_ref[...], w_ref[...])
    out_ref[...] = _layer_norm(DEEPNORM_ALPHA * x_ref[...] + mix, g_ref[...], b_ref[...])


def _out_ln(merged, w_out, x2d, ln_g, ln_b, tm):
    t = merged.shape[0]
    vec = pl.BlockSpec((1, D_MODEL), lambda i: (0, 0))
    return pl.pallas_call(
        _out_ln_kernel,
        grid=(t // tm,),
        in_specs=[pl.BlockSpec((tm, D_MODEL), lambda i: (i, 0)),
                  pl.BlockSpec((D_MODEL, D_MODEL), lambda i: (0, 0)),
                  pl.BlockSpec((tm, D_MODEL), lambda i: (i, 0)), vec, vec],
        out_specs=pl.BlockSpec((tm, D_MODEL), lambda i: (i, 0)),
        out_shape=jax.ShapeDtypeStruct((t, D_MODEL), F32),
        compiler_params=_cparams(("parallel",)),
        name="out_proj_ln1",
    )(merged, w_out, x2d, ln_g, ln_b)


def _router_kernel(x_ref, rw_ref, rb_ref, idx_ref, gate_ref, rank_ref, cnt_ref, carry, *, tm):
    @pl.when(pl.program_id(0) == 0)
    def _():
        carry[...] = jnp.zeros_like(carry)

    lane = lax.broadcasted_iota(jnp.int32, (tm, LANES), 1)
    lane_f = lane.astype(F32)
    logits = jnp.dot(x_ref[...], rw_ref[...], precision=lax.Precision.HIGHEST, preferred_element_type=F32) + rb_ref[...]
    cur = jnp.where(lane < N_EXPERTS, logits, -jnp.inf)
    vals, idxs = [], []
    for _k in range(TOP_K):
        m = jnp.max(cur, axis=1, keepdims=True)
        idx = jnp.min(jnp.where(cur == m, lane_f, float(LANES)), axis=1, keepdims=True).astype(jnp.int32)
        vals.append(m)
        idxs.append(idx)
        cur = jnp.where(lane == idx, -jnp.inf, cur)
    es = [jnp.exp(v - vals[0]) for v in vals]
    den = es[0] + es[1] + es[2] + es[3]
    onehot = jnp.zeros((tm, LANES), F32)
    for idx in idxs:
        onehot = onehot + (lane == idx).astype(F32)
    ri = lax.broadcasted_iota(jnp.int32, (tm, tm), 0)
    ci = lax.broadcasted_iota(jnp.int32, (tm, tm), 1)
    before = (ci < ri).astype(BF16)
    prefix = _dot(before, onehot.astype(BF16)) + carry[0:1, :]
    idx_out = jnp.zeros((tm, LANES), jnp.int32)
    gate_out = jnp.zeros((tm, LANES), F32)
    rank_out = jnp.zeros((tm, LANES), jnp.int32)
    for k in range(TOP_K):
        rk = jnp.sum(jnp.where(lane == idxs[k], prefix, 0.0), axis=1, keepdims=True)
        idx_out = jnp.where(lane == k, idxs[k], idx_out)
        gate_out = jnp.where(lane == k, es[k] / den, gate_out)
        rank_out = jnp.where(lane == k, rk.astype(jnp.int32), rank_out)
    idx_ref[...] = idx_out
    gate_ref[...] = gate_out
    rank_ref[...] = rank_out
    total = carry[0:1, :] + jnp.sum(onehot, axis=0, keepdims=True)
    carry[...] = jnp.broadcast_to(total, carry.shape)
    cnt_ref[...] = jnp.broadcast_to(total, cnt_ref.shape)


def _router(x1, rw, rb, tm):
    t = x1.shape[0]
    row = pl.BlockSpec((tm, LANES), lambda i: (i, 0))
    kern = functools.partial(_router_kernel, tm=tm)
    return pl.pallas_call(
        kern,
        grid=(t // tm,),
        in_specs=[pl.BlockSpec((tm, D_MODEL), lambda i: (i, 0)),
                  pl.BlockSpec((D_MODEL, LANES), lambda i: (0, 0)),
                  pl.BlockSpec((1, LANES), lambda i: (0, 0))],
        out_specs=[row, row, row, pl.BlockSpec((8, LANES), lambda i: (0, 0))],
        out_shape=[jax.ShapeDtypeStruct((t, LANES), jnp.int32),
                   jax.ShapeDtypeStruct((t, LANES), F32),
                   jax.ShapeDtypeStruct((t, LANES), jnp.int32),
                   jax.ShapeDtypeStruct((8, LANES), F32)],
        scratch_shapes=[pltpu.VMEM((8, LANES), F32)],
        compiler_params=_cparams(("arbitrary",)),
        name="router",
    )(x1, rw, rb)


def _pack_rows(src_ref, dst_ref):
    half = D_MODEL // 2
    n = src_ref.shape[0]
    for s in range(ROW_SUBLANES):
        lo = pltpu.bitcast(src_ref[:, s * LANES:(s + 1) * LANES].astype(BF16).astype(F32), jnp.uint32)
        hi = pltpu.bitcast(src_ref[:, half + s * LANES:half + (s + 1) * LANES].astype(BF16).astype(F32), jnp.uint32)
        dst_ref[pl.ds(s, n, stride=ROW_SUBLANES), :] = (lo >> 16) | hi


def _unpack_words(w):
    return pltpu.bitcast(w << 16, F32), pltpu.bitcast(w & jnp.uint32(0xFFFF0000), F32)


def _tile(ref, i):
    start = i * ROW_SUBLANES
    if not isinstance(i, int):
        start = pl.multiple_of(start, ROW_SUBLANES)
    return ref.at[pl.ds(start, ROW_SUBLANES)]


def _tile_copy(src, i, dst, j, sem):
    return pltpu.make_async_copy(_tile(src, i), _tile(dst, j), sem)


def _wait_tiles(hbm, n, sem):
    pltpu.make_async_copy(hbm.at[pl.ds(0, n * ROW_SUBLANES)], hbm.at[pl.ds(0, n * ROW_SUBLANES)], sem).wait()


def _dispatch_kernel(pad_start_ref, pad_cnt_ref, dest_ref, x_ref, xs_hbm, stage, zero, sem, *, tm, nt):
    i = pl.program_id(0)

    @pl.when(i < nt)
    def _():
        _pack_rows(x_ref, stage)

        def issue(t_, carry):
            src = _tile(stage, t_)
            for k in range(TOP_K):
                pltpu.make_async_copy(src, _tile(xs_hbm, dest_ref[0, 0, t_ * TOP_K + k]), sem).start()
            return carry

        lax.fori_loop(0, tm, issue, 0, unroll=2)
        _wait_tiles(xs_hbm, tm * TOP_K, sem)

    @pl.when((i >= nt) & (i < nt + N_EXPERTS))
    def _():
        e = i - nt
        zero[...] = jnp.zeros_like(zero)
        start = pad_start_ref[e]
        cnt = pad_cnt_ref[e]

        def issue(r, carry):
            _tile_copy(zero, 0, xs_hbm, start + r, sem).start()
            return carry

        lax.fori_loop(0, cnt, issue, 0)

        def wait(r, carry):
            _tile_copy(zero, 0, xs_hbm, start, sem).wait()
            return carry

        lax.fori_loop(0, cnt, wait, 0)

    @pl.when(i == nt + N_EXPERTS)
    def _():
        stage[...] = jnp.zeros_like(stage)
        start = pad_start_ref[N_EXPERTS]

        def fill(j, carry):
            piece = xs_hbm.at[pl.ds(pl.multiple_of((start + j * tm) * ROW_SUBLANES, ROW_SUBLANES), tm * ROW_SUBLANES)]
            cp = pltpu.make_async_copy(stage, piece, sem)
            cp.start()
            cp.wait()
            return carry

        lax.fori_loop(0, pad_cnt_ref[N_EXPERTS] // tm, fill, 0)


def _dispatch(x1, dest, pad_from, pad_cnt, n_pad, tm):
    t = x1.shape[0]
    nt = t // tm
    assert MOE_BLOCK % tm == 0 and pad_from.shape[0] == N_EXPERTS + 1
    kern = functools.partial(_dispatch_kernel, tm=tm, nt=nt)
    grid_spec = pltpu.PrefetchScalarGridSpec(
        num_scalar_prefetch=2,
        grid=(nt + pad_from.shape[0],),
        in_specs=[pl.BlockSpec((1, 1, tm * TOP_K), lambda i, ps, pc: (jnp.minimum(i, nt - 1), 0, 0),
                               memory_space=pltpu.SMEM),
                  pl.BlockSpec((tm, D_MODEL), lambda i, ps, pc: (jnp.minimum(i, nt - 1), 0))],
        out_specs=pl.BlockSpec(memory_space=pl.ANY),
        scratch_shapes=[pltpu.VMEM((tm * ROW_SUBLANES, LANES), jnp.uint32),
                        pltpu.VMEM((ROW_SUBLANES, LANES), jnp.uint32),
                        pltpu.SemaphoreType.DMA(())],
    )
    return pl.pallas_call(
        kern,
        grid_spec=grid_spec,
        out_shape=jax.ShapeDtypeStruct((n_pad * ROW_SUBLANES, LANES), jnp.uint32),
        compiler_params=_cparams(("arbitrary",)),
        name="moe_dispatch",
    )(pad_from, pad_cnt, dest.reshape(nt, 1, tm * TOP_K), x1)


def _ffn_kernel(be_ref, nused_ref, rows_ref, x_ref, wg_ref, wu_ref, bg_ref, bu_ref, wd_ref, bd_ref, o_ref, acc, xb):
    i = pl.program_id(0)
    f = pl.program_id(1)
    nf = pl.num_programs(1)
    half = D_MODEL // 2

    @pl.when(f == 0)
    def _():
        acc[...] = jnp.broadcast_to(bd_ref[0], acc.shape)
        for s in range(ROW_SUBLANES):
            lo, hi = _unpack_words(x_ref[pl.ds(s, MOE_BLOCK, stride=ROW_SUBLANES), :])
            xb[:, s * LANES:(s + 1) * LANES] = lo.astype(BF16)
            xb[:, half + s * LANES:half + (s + 1) * LANES] = hi.astype(BF16)

    def compute(nrows):
        x = xb[0:nrows, :]
        g = _dot(x, wg_ref[0, 0]) + bg_ref[0]
        u = _dot(x, wu_ref[0, 0]) + bu_ref[0]
        gate = jnp.minimum(g, SWIGLU_LIMIT)
        up = jnp.clip(u, -SWIGLU_LIMIT, SWIGLU_LIMIT)
        act = (up + 1.0) * gate * _sigmoid(gate * SWIGLU_ALPHA)
        acc[0:nrows, :] += _dot(act.astype(BF16), wd_ref[0].astype(BF16))

    rows = rows_ref[i]
    pl.when(rows > MOE_BLOCK // 2)(lambda: compute(MOE_BLOCK))
    pl.when((rows > 0) & (rows <= MOE_BLOCK // 2))(lambda: compute(MOE_BLOCK // 2))

    @pl.when(f == nf - 1)
    def _():
        _pack_rows(acc, o_ref)


def _expert_ffn(xs, block_expert, n_used, block_rows, wgu, bg, bu, wd, bd):
    n_pad = xs.shape[0] // ROW_SUBLANES
    nb = n_pad // MOE_BLOCK
    d_ff = wgu.shape[3]
    nf = d_ff // MOE_TF
    row_blk = (MOE_BLOCK * ROW_SUBLANES, LANES)
    grid_spec = pltpu.PrefetchScalarGridSpec(
        num_scalar_prefetch=3,
        grid=(nb, nf),
        in_specs=[pl.BlockSpec(row_blk, lambda i, f, be, nu, rw: (jnp.minimum(i, nu[0] - 1), 0)),
                  pl.BlockSpec((1, 1, D_MODEL, MOE_TF), lambda i, f, be, nu, rw: (be[i], 0, 0, f)),
                  pl.BlockSpec((1, 1, D_MODEL, MOE_TF), lambda i, f, be, nu, rw: (be[i], 1, 0, f)),
                  pl.BlockSpec((1, 1, MOE_TF), lambda i, f, be, nu, rw: (be[i], 0, f)),
                  pl.BlockSpec((1, 1, MOE_TF), lambda i, f, be, nu, rw: (be[i], 0, f)),
                  pl.BlockSpec((1, MOE_TF, D_MODEL), lambda i, f, be, nu, rw: (be[i], f, 0)),
                  pl.BlockSpec((1, 1, D_MODEL), lambda i, f, be, nu, rw: (be[i], 0, 0))],
        out_specs=pl.BlockSpec(row_blk, lambda i, f, be, nu, rw: (i, 0)),
        scratch_shapes=[pltpu.VMEM((MOE_BLOCK, D_MODEL), F32), pltpu.VMEM((MOE_BLOCK, D_MODEL), BF16)],
    )
    return pl.pallas_call(
        _ffn_kernel,
        grid_spec=grid_spec,
        out_shape=jax.ShapeDtypeStruct((n_pad * ROW_SUBLANES, LANES), jnp.uint32),
        compiler_params=_cparams(("arbitrary", "arbitrary")),
        name="expert_ffn",
    )(block_expert, n_used, block_rows, xs, wgu, wgu, bg, bu, wd, bd)


def _final_kernel(dest_ref, gate_ref, x_ref, p_ref, wpg_ref, bpg_ref, wple_ref, g_ref, b_ref, ys_hbm, out_ref,
                  rows, sem, *, tm):
    def issue(r, carry):
        _tile_copy(ys_hbm, dest_ref[0, 0, r], rows, r, sem).start()
        return carry

    lax.fori_loop(0, tm * TOP_K, issue, 0, unroll=8)
    x = x_ref[...]
    pg = _dot(x.astype(BF16), wpg_ref[...]) + bpg_ref[...]
    ple = _sigmoid(pg) * _dot(p_ref[...].astype(BF16), wple_ref[...])
    _wait_tiles(ys_hbm, tm * TOP_K, sem)
    gates = gate_ref[...]
    lo_parts, hi_parts = [], []
    for s in range(ROW_SUBLANES):
        acc_lo = acc_hi = None
        for k in range(TOP_K):
            lo, hi = _unpack_words(rows[pl.ds(k * ROW_SUBLANES + s, tm, stride=TOP_K * ROW_SUBLANES), :])
            g = gates[:, k:k + 1]
            acc_lo = g * lo if acc_lo is None else acc_lo + g * lo
            acc_hi = g * hi if acc_hi is None else acc_hi + g * hi
        lo_parts.append(acc_lo)
        hi_parts.append(acc_hi)
    y = jnp.concatenate(lo_parts + hi_parts, axis=1)
    out_ref[...] = _layer_norm(DEEPNORM_ALPHA * x + y + ple, g_ref[...], b_ref[...])


def _final(dest, gates, x1, p2d, w_pg, b_pg, w_ple, ln_g, ln_b, ys, tm):
    t = x1.shape[0]
    vec = pl.BlockSpec((1, D_MODEL), lambda i: (0, 0))
    kern = functools.partial(_final_kernel, tm=tm)
    return pl.pallas_call(
        kern,
        grid=(t // tm,),
        in_specs=[pl.BlockSpec((1, 1, tm * TOP_K), lambda i: (i, 0, 0), memory_space=pltpu.SMEM),
                  pl.BlockSpec((tm, LANES), lambda i: (i, 0)),
                  pl.BlockSpec((tm, D_MODEL), lambda i: (i, 0)),
                  pl.BlockSpec((tm, PLE_DIM), lambda i: (i, 0)),
                  pl.BlockSpec((D_MODEL, D_MODEL), lambda i: (0, 0)), vec,
                  pl.BlockSpec((PLE_DIM, D_MODEL), lambda i: (0, 0)), vec, vec,
                  pl.BlockSpec(memory_space=pl.ANY)],
        out_specs=pl.BlockSpec((tm, D_MODEL), lambda i: (i, 0)),
        out_shape=jax.ShapeDtypeStruct((t, D_MODEL), F32),
        scratch_shapes=[pltpu.VMEM((tm * TOP_K * ROW_SUBLANES, LANES), jnp.uint32), pltpu.SemaphoreType.DMA(())],
        compiler_params=_cparams(("arbitrary",)),
        name="combine_ple_ln2",
    )(dest.reshape(t // tm, 1, tm * TOP_K), gates, x1, p2d, w_pg, b_pg, w_ple, ln_g, ln_b, ys)


def _row_tile(t):
    return min(512, t)


def _layer(x, p, w_in, b_gate, conv_w, a_log, dt_bias, dn_norm_w, w_branch_a, w_branch_b, w_out, ln1_g, ln1_b,
           router_w, router_b, w_gate_up, b_gate_up, w_down, b_down, w_ple, w_ple_gate, b_ple_gate, ln2_g, ln2_b):
    bsz, seq, _ = x.shape
    t = bsz * seq
    tm = _row_tile(t)
    x2d = x.reshape(t, D_MODEL)
    xb = x2d.astype(BF16)

    c_a = 3 * A_QKV
    c_b = 2 * B_QK + 2 * B_VZ
    w_bf = w_in.astype(BF16)
    tp = min(IN_PROJ_TM, seq)
    u_b = _matmul(xb, w_bf[:, c_a:c_a + c_b], BF16, tp, IN_PROJ_TN)
    w_bab = jnp.pad(w_bf[:, c_a + c_b:c_a + c_b + B_GATES], ((0, 0), (0, LANES - B_GATES)))
    bab = _matmul(xb, w_bab, F32, tp, LANES)
    gpre = _matmul(xb, w_bf[:, c_a + c_b + B_GATES:], F32, tp, IN_PROJ_TN)

    slopes = _alibi_slopes()
    gw = A_HEADS * HEAD_DIM
    outs, lses = [], []
    for gi, (_win, dil) in enumerate(DILATION_GROUPS):
        w_g = jnp.concatenate([w_bf[:, part * A_QKV + gi * gw:part * A_QKV + (gi + 1) * gw] for part in range(3)], axis=1)
        o_g, l_g = _attention_group(_matmul_classes(xb, w_g, bsz, dil, tp, IN_PROJ_TN), gi, dil, slopes[gi])
        outs.append(o_g)
        lses.append(l_g)

    cw = jnp.pad(conv_w.astype(F32), ((0, 8 - CONV_W), (0, 0)))
    lane_is_g = (np.arange(LANES) % (2 * B_V_HEADS) >= B_V_HEADS) & (np.arange(LANES) < B_GATES)
    neg_a = jnp.zeros((LANES,), F32).at[B_V_HEADS:2 * B_V_HEADS].set(-jnp.exp(a_log[0].astype(F32)))
    neg_a = neg_a.at[3 * B_V_HEADS:4 * B_V_HEADS].set(-jnp.exp(a_log[1].astype(F32)))
    dtb = jnp.zeros((LANES,), F32).at[B_V_HEADS:2 * B_V_HEADS].set(dt_bias[0].astype(F32))
    dtb = dtb.at[3 * B_V_HEADS:4 * B_V_HEADS].set(dt_bias[1].astype(F32))
    gate_params = jnp.zeros((8, LANES), F32).at[0].set(neg_a).at[1].set(dtb).at[2].set(jnp.asarray(lane_is_g, F32))
    qn, kn, vn, gts = _dn_prep(u_b.reshape(bsz, seq, c_b), cw, bab.reshape(bsz, seq, LANES), gate_params)
    gts_t = gts.reshape(bsz, seq // DN_CHUNK, DN_CHUNK, LANES).transpose(0, 1, 3, 2)
    o_f, o_b = _dn_state(*_dn_intra(qn, kn, vn, gts, gts_t), seq)

    bg = b_gate.astype(F32).reshape(1, 2 * D_MODEL)
    a_part = _branch_a(outs, lses, w_branch_a.astype(BF16), gpre, bg, tm)
    merged = _branch_b(o_f.reshape(t, B_VZ), o_b.reshape(t, B_VZ), u_b, dn_norm_w.astype(F32).reshape(1, B_DV), w_branch_b.astype(BF16), gpre, bg,
                       a_part, tm)
    x1 = _out_ln(merged, w_out.astype(BF16), x2d, ln1_g.reshape(1, -1), ln1_b.reshape(1, -1), tm)

    rw = jnp.pad(router_w.astype(F32), ((0, 0), (0, LANES - N_EXPERTS)))
    rb = jnp.pad(router_b.astype(F32), (0, LANES - N_EXPERTS)).reshape(1, LANES)
    idx, gates, rank, cnt = _router(x1, rw, rb, tm)
    counts = cnt[0, :N_EXPERTS].astype(jnp.int32)
    padded = (counts + MOE_BLOCK - 1) // MOE_BLOCK * MOE_BLOCK
    pad_end = jnp.cumsum(padded)
    pad_start = pad_end - padded
    dest = pad_start[idx[:, :TOP_K]] + rank[:, :TOP_K]
    n_pad = t * TOP_K + N_EXPERTS * MOE_BLOCK
    nb = n_pad // MOE_BLOCK
    block_start = jnp.arange(nb, dtype=jnp.int32) * MOE_BLOCK
    block_expert = jnp.minimum(jnp.sum(pad_end[None, :] <= block_start[:, None], axis=1), N_EXPERTS - 1).astype(jnp.int32)
    n_used = (pad_end[-1:] // MOE_BLOCK).astype(jnp.int32)

    zero_from = jnp.concatenate([pad_start + counts, pad_end[-1:]]).astype(jnp.int32)
    zero_cnt = jnp.concatenate([padded - counts, n_pad - pad_end[-1:]]).astype(jnp.int32)
    xs = _dispatch(x1, dest, zero_from, zero_cnt, n_pad, tm)
    d_ff = w_down.shape[1]
    wgu = w_gate_up.reshape(N_EXPERTS, D_MODEL, d_ff, 2).transpose(0, 3, 1, 2).astype(BF16)
    bgu = b_gate_up.astype(F32).reshape(N_EXPERTS, 1, d_ff, 2)
    seg_end = (pad_start + counts)[block_expert]
    block_rows = jnp.where(block_start < pad_end[-1], jnp.clip(seg_end - block_start, 0, MOE_BLOCK), 0).astype(jnp.int32)
    ys = _expert_ffn(xs, block_expert, n_used, block_rows, wgu, bgu[..., 0], bgu[..., 1], w_down.astype(F32),
                     b_down.astype(F32).reshape(N_EXPERTS, 1, D_MODEL))

    out = _final(dest, gates, x1, p.reshape(t, PLE_DIM), w_ple_gate.astype(BF16),
                 b_ple_gate.astype(F32).reshape(1, -1), w_ple.astype(BF16), ln2_g.reshape(1, -1),
                 ln2_b.reshape(1, -1), ys, min(COMBINE_TM, t))
    return out.reshape(bsz, seq, D_MODEL)


def kernel(x, p, w_in, b_gate, conv_w, a_log, dt_bias, dn_norm_w, w_branch_a, w_branch_b, w_out, ln1_g, ln1_b,
           router_w, router_b, w_gate_up, b_gate_up, w_down, b_down, w_ple, w_ple_gate, b_ple_gate, ln2_g, ln2_b):
    assert w_in.shape[0] == DEPTH
    return _layer(x, p[0], w_in[0], b_gate[0], conv_w[0], a_log[0], dt_bias[0], dn_norm_w[0], w_branch_a[0],
                  w_branch_b[0], w_out[0], ln1_g[0], ln1_b[0], router_w[0], router_b[0], w_gate_up[0],
                  b_gate_up[0], w_down[0], b_down[0], w_ple[0], w_ple_gate[0], b_ple_gate[0], ln2_g[0], ln2_b[0])
```

```python
import functools

import numpy as np
import jax
import jax.numpy as jnp
from jax import lax
from jax.experimental import pallas as pl
from jax.experimental.pallas import tpu as pltpu

F32 = jnp.float32
BF16 = jnp.bfloat16

D_MODEL = 2048
HEAD_DIM = 128
A_HEADS = 8
DILATION_GROUPS = ((128, 1), (512, 4), (2048, 16))
N_GROUPS = 3
NEG_INF = -1e30
B_QK_HEADS = 8
B_V_HEADS = 16
B_DK = 128
B_DV = 128
CONV_W = 5
RMS_EPS = 1e-6
N_EXPERTS = 32
TOP_K = 4
SWIGLU_ALPHA = 1.702
SWIGLU_LIMIT = 7.0
PLE_DIM = 256
DEPTH = 1
DEEPNORM_ALPHA = (2 * DEPTH) ** 0.25
LN_EPS = 1e-5
A_QKV = N_GROUPS * A_HEADS * HEAD_DIM
B_QK = B_QK_HEADS * B_DK
B_VZ = B_V_HEADS * B_DV
B_GATES = 4 * B_V_HEADS

LANES = 128
N_SIDE = 64
Q_SUB = 128
DN_CHUNK = 64
DN_PAIR = B_V_HEADS // B_QK_HEADS
DN_STATE_CHUNKS = 2
TRI_BASE = 16
ROW_SUBLANES = 8
IN_PROJ_TM = 1024
IN_PROJ_TN = 1024
MOE_BLOCK = 1024
MOE_TF = 512
COMBINE_TM = 256
VMEM_LIMIT = 56 * 1024 * 1024
assert D_MODEL == 2 * ROW_SUBLANES * LANES


def _cparams(sem):
    return pltpu.CompilerParams(dimension_semantics=sem, vmem_limit_bytes=VMEM_LIMIT)


def _sigmoid(x):
    return 1.0 / (1.0 + jnp.exp(-x))


def _dot(a, b):
    return jnp.dot(a, b, preferred_element_type=F32)


def _dot_nt(a, b):
    return lax.dot_general(a, b, (((1,), (1,)), ((), ())), preferred_element_type=F32)


def _dot_tn(a, b):
    return lax.dot_general(a, b, (((0,), (0,)), ((), ())), preferred_element_type=F32)


def _mm_kernel(x_ref, w_ref, o_ref):
    o_ref[...] = _dot(x_ref[...], w_ref[...]).astype(o_ref.dtype)


def _matmul(x, w, out_dtype, tm, tn):
    m, k = x.shape
    n = w.shape[1]
    return pl.pallas_call(
        _mm_kernel,
        grid=(m // tm, n // tn),
        in_specs=[pl.BlockSpec((tm, k), lambda i, j: (i, 0)),
                  pl.BlockSpec((k, tn), lambda i, j: (0, j))],
        out_specs=pl.BlockSpec((tm, tn), lambda i, j: (i, j)),
        out_shape=jax.ShapeDtypeStruct((m, n), out_dtype),
        compiler_params=_cparams(("parallel", "parallel")),
        name="in_proj",
    )(x, w)


def _mm_classes_kernel(x_ref, w_ref, o_ref, acc_ref, *, dil, rows):
    y = _dot(x_ref[...], w_ref[...])
    for j in range(acc_ref.shape[0]):
        ls = slice(j * LANES, (j + 1) * LANES)
        acc_ref[j] = y[:, ls]
        for c in range(dil):
            o_ref[0, c, :, ls] = acc_ref[j, pl.ds(c, rows, stride=dil), :].astype(o_ref.dtype)


def _matmul_classes(x, w, bsz, dil, tm, tn):
    m, k = x.shape
    n = w.shape[1]
    seq = m // bsz
    tiles_per_seq = seq // tm
    rows = tm // dil
    kern = functools.partial(_mm_classes_kernel, dil=dil, rows=rows)
    return pl.pallas_call(
        kern,
        grid=(m // tm, n // tn),
        in_specs=[pl.BlockSpec((tm, k), lambda i, j: (i, 0)),
                  pl.BlockSpec((k, tn), lambda i, j: (0, j))],
        out_specs=pl.BlockSpec((1, dil, rows, tn), lambda i, j: (i // tiles_per_seq, 0, i % tiles_per_seq, j)),
        out_shape=jax.ShapeDtypeStruct((bsz, dil, seq // dil, n), BF16),
        scratch_shapes=[pltpu.VMEM((tn // LANES, tm, LANES), F32)],
        compiler_params=_cparams(("parallel", "parallel")),
        name=f"in_proj_dil{dil}",
    )(x, w)


def _attn_kernel(q_ref, kp_ref, kc_ref, kn_ref, vp_ref, vc_ref, vn_ref, o_ref, lse_ref, kbuf, vbuf,
                 *, dil, sub_len, tl, slopes):
    i0 = pl.program_id(2) * tl
    kbuf[0:N_SIDE, :] = kp_ref[...]
    kbuf[N_SIDE:N_SIDE + tl, :] = kc_ref[...]
    kbuf[N_SIDE + tl:, :] = kn_ref[...]
    vbuf[0:N_SIDE, :] = vp_ref[...]
    vbuf[N_SIDE:N_SIDE + tl, :] = vc_ref[...]
    vbuf[N_SIDE + tl:, :] = vn_ref[...]
    span = Q_SUB + 2 * N_SIDE
    qq = lax.broadcasted_iota(jnp.int32, (Q_SUB, span), 0)
    kk = lax.broadcasted_iota(jnp.int32, (Q_SUB, span), 1)
    delta = kk - N_SIDE - qq
    absd = jnp.abs(delta)
    band = absd <= N_SIDE
    dist = (dil * absd).astype(F32)
    lane = lax.broadcasted_iota(jnp.int32, (Q_SUB, LANES), 1)
    scale = HEAD_DIM ** -0.5
    for j in range(tl // Q_SUB):
        pos = i0 + (j * Q_SUB - N_SIDE) + kk
        valid = band & (pos >= 0) & (pos < sub_len)
        lse_tile = jnp.zeros((Q_SUB, LANES), F32)
        for h in range(A_HEADS):
            hs = slice(h * HEAD_DIM, (h + 1) * HEAD_DIM)
            q = q_ref[j * Q_SUB:(j + 1) * Q_SUB, hs]
            k = kbuf[j * Q_SUB:j * Q_SUB + span, hs]
            v = vbuf[j * Q_SUB:j * Q_SUB + span, hs]
            s = _dot_nt(q, k) * scale
            s = jnp.where(valid, s - float(slopes[h]) * dist, NEG_INF)
            m = jnp.max(s, axis=1, keepdims=True)
            p = jnp.exp(s - m)
            l = jnp.sum(p, axis=1, keepdims=True)
            o = _dot(p.astype(BF16), v) / l
            o_ref[j * Q_SUB:(j + 1) * Q_SUB, hs] = o.astype(o_ref.dtype)
            lse_tile = jnp.where(lane == h, m + jnp.log(l), lse_tile)
        lse_ref[j * Q_SUB:(j + 1) * Q_SUB, :] = lse_tile


def _attention_group(qkv, gi, dil, slopes):
    bsz, _, sub_len, _ = qkv.shape
    tl = min(512, sub_len)
    assert sub_len % tl == 0 and tl % Q_SUB == 0 and sub_len % N_SIDE == 0
    width = A_HEADS * HEAD_DIM
    halo_per_tile = tl // N_SIDE
    n_halo = sub_len // N_SIDE
    prev = lambda li: jnp.maximum(li * halo_per_tile - 1, 0)
    nxt = lambda li: jnp.minimum((li + 1) * halo_per_tile, n_halo - 1)
    halo = lambda col, rowf: pl.BlockSpec((None, None, N_SIDE, width), lambda b, c, li: (b, c, rowf(li), col))
    cur = lambda col: pl.BlockSpec((None, None, tl, width), lambda b, c, li: (b, c, li, col))
    kern = functools.partial(_attn_kernel, dil=dil, sub_len=sub_len, tl=tl, slopes=tuple(float(s) for s in slopes))
    return pl.pallas_call(
        kern,
        grid=(bsz, dil, sub_len // tl),
        in_specs=[cur(0), halo(1, prev), cur(1), halo(1, nxt), halo(2, prev), cur(2), halo(2, nxt)],
        out_specs=[pl.BlockSpec((None, None, tl, width), lambda b, c, li: (b, c, li, 0)),
                   pl.BlockSpec((None, None, tl, LANES), lambda b, c, li: (b, c, li, 0))],
        out_shape=[jax.ShapeDtypeStruct((bsz, dil, sub_len, width), BF16),
                   jax.ShapeDtypeStruct((bsz, dil, sub_len, LANES), F32)],
        scratch_shapes=[pltpu.VMEM((tl + 2 * N_SIDE, width), BF16),
                        pltpu.VMEM((tl + 2 * N_SIDE, width), BF16)],
        compiler_params=_cparams(("parallel", "parallel", "parallel")),
        name=f"dilated_attn_g{gi}",
    )(qkv, qkv, qkv, qkv, qkv, qkv, qkv)


def _alibi_slopes():
    n = N_GROUPS * A_HEADS
    s = 2.0 ** (-8.0 * np.arange(1, n + 1) / n)
    return s.astype(np.float32).reshape(N_GROUPS, A_HEADS)


def _dn_prep_kernel(prev_ref, cur_ref, next_ref, cw_ref, bab_ref, gp_ref, q_ref, k_ref, v_ref, g_ref, *, ts):
    ti = pl.program_id(1)
    nt = pl.num_programs(1)
    halo = CONV_W // 2
    keep_prev = (ti > 0).astype(F32)
    keep_next = (ti < nt - 1).astype(F32)
    for c in range((2 * B_QK + B_VZ) // LANES):
        cs = slice(c * LANES, (c + 1) * LANES)
        xp = prev_ref[0, :, cs].astype(F32)[8:16] * keep_prev
        xc = cur_ref[0, :, cs].astype(F32)
        xn = next_ref[0, :, cs].astype(F32)[0:8] * keep_next
        ext = jnp.concatenate([xp, xc, xn], axis=0)
        acc = jnp.zeros((ts, LANES), F32)
        for j in range(CONV_W):
            off = 8 - halo + j
            acc = acc + ext[off:off + ts, :] * cw_ref[j:j + 1, cs]
        y = acc * _sigmoid(acc)
        if c < 2 * B_QK // LANES:
            y = y * lax.rsqrt(jnp.sum(y * y, axis=1, keepdims=True) + 1e-6)
        if c < B_QK // LANES:
            q_ref[0, :, cs] = y * (B_DK ** -0.5)
        elif c < 2 * B_QK // LANES:
            k_ref[0, :, c * LANES - B_QK:(c + 1) * LANES - B_QK] = y
        else:
            v_ref[0, :, c * LANES - 2 * B_QK:(c + 1) * LANES - 2 * B_QK] = y
    x = bab_ref[0]
    neg_a = gp_ref[0:1, :]
    dtb = gp_ref[1:2, :]
    is_g = gp_ref[2:3, :] > 0.5
    z = x + dtb
    softplus = jnp.maximum(z, 0.0) + jnp.log(1.0 + jnp.exp(-jnp.abs(z)))
    g_ref[0] = jnp.where(is_g, neg_a * softplus, _sigmoid(x))


def _dn_prep(u_b, conv_w, bab, gate_params):
    bsz, seq, _ = u_b.shape
    ts = min(256, seq)
    cq = 2 * B_QK + B_VZ
    nhalo = seq // 16
    per = ts // 16
    kern = functools.partial(_dn_prep_kernel, ts=ts)
    return pl.pallas_call(
        kern,
        grid=(bsz, seq // ts),
        in_specs=[pl.BlockSpec((1, 16, cq), lambda b, t: (b, jnp.maximum(t * per - 1, 0), 0)),
                  pl.BlockSpec((1, ts, cq), lambda b, t: (b, t, 0)),
                  pl.BlockSpec((1, 16, cq), lambda b, t: (b, jnp.minimum((t + 1) * per, nhalo - 1), 0)),
                  pl.BlockSpec((8, cq), lambda b, t: (0, 0)),
                  pl.BlockSpec((1, ts, LANES), lambda b, t: (b, t, 0)),
                  pl.BlockSpec((8, LANES), lambda b, t: (0, 0))],
        out_specs=[pl.BlockSpec((1, ts, B_QK), lambda b, t: (b, t, 0)),
                   pl.BlockSpec((1, ts, B_QK), lambda b, t: (b, t, 0)),
                   pl.BlockSpec((1, ts, B_VZ), lambda b, t: (b, t, 0)),
                   pl.BlockSpec((1, ts, LANES), lambda b, t: (b, t, 0))],
        out_shape=[jax.ShapeDtypeStruct((bsz, seq, B_QK), F32),
                   jax.ShapeDtypeStruct((bsz, seq, B_QK), F32),
                   jax.ShapeDtypeStruct((bsz, seq, B_VZ), F32),
                   jax.ShapeDtypeStruct((bsz, seq, LANES), F32)],
        compiler_params=_cparams(("parallel", "parallel")),
        name="dn_prep",
    )(u_b, u_b, u_b, conv_w, bab, gate_params)


def _dn_intra_kernel(q_ref, k_ref, v_ref, g_ref, gt_ref, u_ref, wq_ref, kq_ref, et_ref, *, c):
    n_units = 2 * DN_PAIR
    w4 = n_units * c
    hp = lax.Precision.HIGHEST
    bf = lambda t_: t_.astype(BF16)
    ii = lax.broadcasted_iota(jnp.int32, (c, w4), 0)
    ll = lax.broadcasted_iota(jnp.int32, (c, w4), 1)
    jj = ll % c
    ub = ll // c
    ub_row = ub[0:1, :]
    lo = jnp.where(ub >= DN_PAIR, jj - ii, ii - jj)
    incl = lo >= 0
    strict = lo > 0
    eye = (ii == jj).astype(F32)
    blk = (ii // TRI_BASE) == (jj // TRI_BASE)

    def pack(parts, sel):
        out = parts[n_units - 1]
        for u_ in range(n_units - 2, -1, -1):
            out = jnp.where(sel == u_, parts[u_], out)
        return out

    unit_mask = [jnp.where(ub == u_, 1.0, 0.0).astype(BF16) for u_ in range(n_units)]

    def block_diag(y16):
        return jnp.concatenate([y16 * m_ for m_ in unit_mask], axis=0)

    def mm(xs, ys):
        return [_dot(bf(x_), block_diag(bf(y_))) for x_, y_ in zip(xs, ys)]

    ri = lax.broadcasted_iota(jnp.int32, (c, c), 0)
    ci = lax.broadcasted_iota(jnp.int32, (c, c), 1)
    g_all = g_ref[0]
    gc_dir = [jnp.dot((ci <= ri).astype(F32), g_all, precision=hp, preferred_element_type=F32),
              jnp.dot((ci >= ri).astype(F32), g_all, precision=hp, preferred_element_type=F32)]
    tri4 = (lo <= 0).astype(F32)
    gcr_all = jnp.dot(gt_ref[0, 0], tri4, precision=hp, preferred_element_type=F32)
    tot_all = jnp.sum(g_all, axis=0, keepdims=True)

    pairs = list(range(B_V_HEADS // DN_PAIR))
    qs_, ks_, lms, qkms, betas, egcs, kscales, etots = [], [], [], [], [], [], [], []
    for p in pairs:
        cs = slice(p * B_DK, (p + 1) * B_DK)
        q = q_ref[0, :, cs]
        k = k_ref[0, :, cs]
        k16 = bf(k)
        k4 = jnp.concatenate([k16] * n_units, axis=0)
        gram = _dot_nt(k16, k4)
        qk = _dot_nt(bf(q), k4)
        beta_u, gc_u, gcr_u, tot_u = [], [], [], []
        for u_ in range(n_units):
            d_, e_ = divmod(u_, DN_PAIR)
            h = p * DN_PAIR + e_
            bl = d_ * 2 * B_V_HEADS + h
            gl = bl + B_V_HEADS
            beta_u.append(g_all[:, bl:bl + 1])
            gc_u.append(gc_dir[d_][:, gl:gl + 1])
            gcr_u.append(gcr_all[gl:gl + 1, :])
            tot_u.append(tot_all[:, gl:gl + 1])
        gc_p = pack(gc_u, ub)
        gcr_p = pack(gcr_u, ub_row)
        tot_p = pack(tot_u, ub_row)
        dec = jnp.where(incl, jnp.exp(jnp.where(incl, gc_p - gcr_p, 0.0)), 0.0)
        lms.append(jnp.where(strict, pack(beta_u, ub) * gram * dec, 0.0))
        qkms.append(jnp.where(incl, qk * dec, 0.0))
        qs_.append(q)
        ks_.append(k)
        betas.append(beta_u)
        egcs.append([jnp.exp(g_) for g_ in gc_u])
        kscales.append(jnp.exp(tot_p - gcr_p))
        etots.append([jnp.exp(t_) for t_ in tot_u])

    d1 = [jnp.where(blk, lm, 0.0) for lm in lms]
    d2 = mm(d1, d1)
    d4 = mm(d2, d2)
    d8 = mm(d4, d4)
    tm_ = [eye - d_ for d_ in d1]
    for dk in (d2, d4, d8):
        tm_ = [a + b for a, b in zip(tm_, mm(tm_, dk))]
    size = TRI_BASE
    while size < c:
        off = ((ii // (2 * size)) == (jj // (2 * size))) & ((ii // size) != (jj // size))
        cm = [jnp.where(off, lm, 0.0) for lm in lms]
        pc = mm(tm_, cm)
        tm_ = [a - b for a, b in zip(tm_, mm(pc, tm_))]
        size *= 2

    lane2 = lax.broadcasted_iota(jnp.int32, (1, DN_PAIR * B_DV), 1)
    for p in pairs:
        q, k = qs_[p], ks_[p]
        rows = []
        for u_ in range(n_units):
            d_, e_ = divmod(u_, DN_PAIR)
            h = p * DN_PAIR + e_
            v = v_ref[0, :, h * B_DV:(h + 1) * B_DV]
            beta = betas[p][u_]
            rows.append(jnp.concatenate([v * beta, k * (beta * egcs[p][u_])], axis=1))
        uw = _dot(block_diag(bf(tm_[p])), bf(jnp.concatenate(rows, axis=0)))
        k_t = jnp.transpose(jnp.concatenate([k] * DN_PAIR, axis=0))
        for d_ in range(2):
            u0, u1 = d_ * DN_PAIR, d_ * DN_PAIR + 1
            u_ref[d_, 0, 0, p] = jnp.concatenate([uw[u0 * c:(u0 + 1) * c, :B_DV], uw[u1 * c:(u1 + 1) * c, :B_DV]], axis=1)
            wq = jnp.concatenate([uw[u0 * c:(u0 + 1) * c, B_DV:], q * egcs[p][u0],
                                  uw[u1 * c:(u1 + 1) * c, B_DV:], q * egcs[p][u1]], axis=0)
            wq_ref[d_, 0, 0, p] = bf(wq)
            ls = slice(d_ * DN_PAIR * c, (d_ + 1) * DN_PAIR * c)
            kq = jnp.concatenate([k_t * kscales[p][:, ls], qkms[p][:, ls]], axis=0)
            kq_ref[d_, 0, 0, p] = bf(kq)
            et_ref[d_, 0, 0, p:p + 1, :] = jnp.where(lane2 < B_DV, etots[p][u0], etots[p][u1])


def _dn_intra(qn, kn, vn, gts, gts_t):
    bsz, seq, _ = qn.shape
    c = DN_CHUNK
    assert DN_PAIR * c == B_DK and seq % c == 0
    n = seq // c
    npair = B_V_HEADS // DN_PAIR
    wide = DN_PAIR * B_DV
    kern = functools.partial(_dn_intra_kernel, c=c)
    out5 = lambda r, cdim: pl.BlockSpec((2, 1, 1, npair, r, cdim), lambda b, i: (0, b, i, 0, 0, 0))
    return pl.pallas_call(
        kern,
        grid=(bsz, n),
        in_specs=[pl.BlockSpec((1, c, B_QK), lambda b, i: (b, i, 0)),
                  pl.BlockSpec((1, c, B_QK), lambda b, i: (b, i, 0)),
                  pl.BlockSpec((1, c, B_VZ), lambda b, i: (b, i, 0)),
                  pl.BlockSpec((1, c, LANES), lambda b, i: (b, i, 0)),
                  pl.BlockSpec((1, 1, LANES, c), lambda b, i: (b, i, 0, 0))],
        out_specs=[out5(c, wide), out5(2 * DN_PAIR * c, B_DK), out5(B_DK + c, DN_PAIR * c),
                   pl.BlockSpec((2, 1, 1, npair, wide), lambda b, i: (0, b, i, 0, 0))],
        out_shape=[jax.ShapeDtypeStruct((2, bsz, n, npair, c, wide), F32),
                   jax.ShapeDtypeStruct((2, bsz, n, npair, 2 * DN_PAIR * c, B_DK), BF16),
                   jax.ShapeDtypeStruct((2, bsz, n, npair, B_DK + c, DN_PAIR * c), BF16),
                   jax.ShapeDtypeStruct((2, bsz, n, npair, wide), F32)],
        compiler_params=_cparams(("parallel", "parallel")),
        name="dn_intra",
    )(qn, kn, vn, gts, gts_t)


def _dn_state_kernel(uf_ref, wqf_ref, kqf_ref, etf_ref, ub_ref, wqb_ref, kqb_ref, etb_ref, of_ref, ob_ref, s_ref,
                     *, c, per):
    @pl.when(pl.program_id(1) == 0)
    def _():
        s_ref[...] = jnp.zeros_like(s_ref)

    npair = B_V_HEADS // DN_PAIR
    bf = lambda t_: t_.astype(BF16)
    chains = [(d_, p) for d_ in range(2) for p in range(npair)]
    refs = ((uf_ref, wqf_ref, kqf_ref, etf_ref, of_ref), (ub_ref, wqb_ref, kqb_ref, etb_ref, ob_ref))
    zero = jnp.zeros((c, B_DV), F32)
    for step in range(per):
        local = (step, per - 1 - step)
        states = [s_ref[d_, p] for d_, p in chains]
        a_res = [_dot(refs[d_][1][0, 0, local[d_], p], bf(s_)) for (d_, p), s_ in zip(chains, states)]
        b_res = []
        for (d_, p), a_ in zip(chains, a_res):
            u = refs[d_][0][0, 0, local[d_], p]
            v0 = u[:, :B_DV] - a_[0:c, :B_DV]
            v1 = u[:, B_DV:] - a_[2 * c:3 * c, B_DV:]
            bd_v = jnp.concatenate([jnp.concatenate([v0, zero], axis=1), jnp.concatenate([zero, v1], axis=1)], axis=0)
            b_res.append(_dot(refs[d_][2][0, 0, local[d_], p], bf(bd_v)))
        for (d_, p), a_, b_, s_ in zip(chains, a_res, b_res, states):
            s_ref[d_, p] = s_ * refs[d_][3][0, 0, local[d_], p:p + 1, :] + b_[:B_DK]
            o_ref = refs[d_][4]
            rs = slice(local[d_] * c, (local[d_] + 1) * c)
            o_ref[0, rs, (2 * p) * B_DV:(2 * p + 1) * B_DV] = a_[c:2 * c, :B_DV] + b_[B_DK:, :B_DV]
            o_ref[0, rs, (2 * p + 1) * B_DV:(2 * p + 2) * B_DV] = a_[3 * c:4 * c, B_DV:] + b_[B_DK:, B_DV:]


def _dn_state(u_all, wq_all, kq_all, et_all, seq):
    _, bsz, n, npair, c, wide = u_all.shape
    per = DN_STATE_CHUNKS if n % DN_STATE_CHUNKS == 0 else 1
    nb = n // per
    kern = functools.partial(_dn_state_kernel, c=c, per=per)
    fwd = lambda r, cdim: pl.BlockSpec((1, 1, per, npair, r, cdim), lambda b, i: (0, b, i, 0, 0, 0))
    bwd = lambda r, cdim: pl.BlockSpec((1, 1, per, npair, r, cdim), lambda b, i: (1, b, nb - 1 - i, 0, 0, 0))
    et_f = pl.BlockSpec((1, 1, per, npair, wide), lambda b, i: (0, b, i, 0, 0))
    et_b = pl.BlockSpec((1, 1, per, npair, wide), lambda b, i: (1, b, nb - 1 - i, 0, 0))
    shapes = ((c, wide), (2 * DN_PAIR * c, B_DK), (B_DK + c, DN_PAIR * c))
    return pl.pallas_call(
        kern,
        grid=(bsz, nb),
        in_specs=[fwd(*shapes[0]), fwd(*shapes[1]), fwd(*shapes[2]), et_f,
                  bwd(*shapes[0]), bwd(*shapes[1]), bwd(*shapes[2]), et_b],
        out_specs=[pl.BlockSpec((1, per * c, B_VZ), lambda b, i: (b, i, 0)),
                   pl.BlockSpec((1, per * c, B_VZ), lambda b, i: (b, nb - 1 - i, 0))],
        out_shape=[jax.ShapeDtypeStruct((bsz, seq, B_VZ), F32), jax.ShapeDtypeStruct((bsz, seq, B_VZ), F32)],
        scratch_shapes=[pltpu.VMEM((2, npair, B_DK, wide), F32)],
        compiler_params=_cparams(("parallel", "arbitrary")),
        name="dn_state",
    )(u_all, wq_all, kq_all, et_all, u_all, wq_all, kq_all, et_all)


def _branch_a_kernel(o0_ref, o1_ref, o2_ref, l0_ref, l1_ref, l2_ref, w_ref, gp_ref, bg_ref, out_ref, o_sc, l_sc,
                     *, dils, tm):
    for g, (o_ref, l_ref) in enumerate(((o0_ref, l0_ref), (o1_ref, l1_ref), (o2_ref, l2_ref))):
        r = dils[g]
        for c in range(r):
            l_sc[g, pl.ds(c, tm // r, stride=r), :] = l_ref[c]
            for h in range(A_HEADS):
                o_sc[g, h, pl.ds(c, tm // r, stride=r), :] = o_ref[c, :, h * HEAD_DIM:(h + 1) * HEAD_DIM].astype(F32)
    ls = [l_sc[g] for g in range(N_GROUPS)]
    m = jnp.maximum(jnp.maximum(ls[0], ls[1]), ls[2])
    es = [jnp.exp(l - m) for l in ls]
    den = es[0] + es[1] + es[2]
    ws = [e / den for e in es]
    parts = []
    for h in range(A_HEADS):
        hs = slice(h * HEAD_DIM, (h + 1) * HEAD_DIM)
        acc = ws[0][:, h:h + 1] * o_sc[0, h]
        acc = acc + ws[1][:, h:h + 1] * o_sc[1, h]
        acc = acc + ws[2][:, h:h + 1] * o_sc[2, h]
        parts.append(acc.astype(BF16))
    oa = jnp.concatenate(parts, axis=1)
    y = _dot(oa, w_ref[...])
    out_ref[...] = _sigmoid(gp_ref[...] + bg_ref[...]) * y


def _branch_a(outs, lses, w_a, gpre, b_gate, tm):
    bsz = outs[0].shape[0]
    dils = tuple(o.shape[1] for o in outs)
    seq = dils[0] * outs[0].shape[2]
    t = bsz * seq
    per_seq = seq // tm
    wd = A_HEADS * HEAD_DIM
    cls = lambda r, width: pl.BlockSpec((None, r, tm // r, width), lambda i: (i // per_seq, 0, i % per_seq, 0))
    kern = functools.partial(_branch_a_kernel, dils=dils, tm=tm)
    return pl.pallas_call(
        kern,
        grid=(t // tm,),
        in_specs=[cls(dils[0], wd), cls(dils[1], wd), cls(dils[2], wd),
                  cls(dils[0], LANES), cls(dils[1], LANES), cls(dils[2], LANES),
                  pl.BlockSpec((wd, D_MODEL), lambda i: (0, 0)),
                  pl.BlockSpec((tm, D_MODEL), lambda i: (i, 0)),
                  pl.BlockSpec((1, D_MODEL), lambda i: (0, 0))],
        out_specs=pl.BlockSpec((tm, D_MODEL), lambda i: (i, 0)),
        out_shape=jax.ShapeDtypeStruct((t, D_MODEL), F32),
        scratch_shapes=[pltpu.VMEM((N_GROUPS, A_HEADS, tm, HEAD_DIM), F32), pltpu.VMEM((N_GROUPS, tm, LANES), F32)],
        compiler_params=_cparams(("parallel",)),
        name="branch_a",
    )(*outs, *lses, w_a, gpre, b_gate)


def _branch_b_kernel(of_ref, ob_ref, z_ref, nw_ref, w_ref, gp_ref, bg_ref, a_ref, out_ref):
    nw = nw_ref[...]
    parts = []
    for h in range(B_V_HEADS):
        hs = slice(h * B_DV, (h + 1) * B_DV)
        o = of_ref[:, hs] + ob_ref[:, hs]
        z = z_ref[:, hs].astype(F32)
        o = o * lax.rsqrt(jnp.mean(o * o, axis=1, keepdims=True) + RMS_EPS) * nw * (z * _sigmoid(z))
        parts.append(o.astype(BF16))
    ob = jnp.concatenate(parts, axis=1)
    y = _dot(ob, w_ref[...])
    out_ref[...] = (a_ref[...] + _sigmoid(gp_ref[...] + bg_ref[...]) * y).astype(out_ref.dtype)


def _branch_b(o_f, o_b, u_b2d, norm_w, w_b, gpre, b_gate, a_part, tm):
    t = a_part.shape[0]
    return pl.pallas_call(
        _branch_b_kernel,
        grid=(t // tm,),
        in_specs=[pl.BlockSpec((tm, B_VZ), lambda i: (i, 0)),
                  pl.BlockSpec((tm, B_VZ), lambda i: (i, 0)),
                  pl.BlockSpec((tm, B_VZ), lambda i: (i, (2 * B_QK + B_VZ) // B_VZ)),
                  pl.BlockSpec((1, B_DV), lambda i: (0, 0)),
                  pl.BlockSpec((B_VZ, D_MODEL), lambda i: (0, 0)),
                  pl.BlockSpec((tm, D_MODEL), lambda i: (i, 1)),
                  pl.BlockSpec((1, D_MODEL), lambda i: (0, 1)),
                  pl.BlockSpec((tm, D_MODEL), lambda i: (i, 0))],
        out_specs=pl.BlockSpec((tm, D_MODEL), lambda i: (i, 0)),
        out_shape=jax.ShapeDtypeStruct((t, D_MODEL), BF16),
        compiler_params=_cparams(("parallel",)),
        name="branch_b",
    )(o_f, o_b, u_b2d, norm_w, w_b, gpre, b_gate, a_part)


def _layer_norm(y, g, b):
    mu = jnp.mean(y, axis=1, keepdims=True)
    yc = y - mu
    var = jnp.mean(yc * yc, axis=1, keepdims=True)
    return yc * lax.rsqrt(var + LN_EPS) * g + b


def _out_ln_kernel(m_ref, w_ref, x_ref, g_ref, b_ref, out_ref):
    mix = _dot(m_ref[...], w_ref[...])
    out_ref[...] = _layer_norm(DEEPNORM_ALPHA * x_ref[...] + mix, g_ref[...], b_ref[...])


def _out_ln(merged, w_out, x2d, ln_g, ln_b, tm):
    t = merged.shape[0]
    vec = pl.BlockSpec((1, D_MODEL), lambda i: (0, 0))
    return pl.pallas_call(
        _out_ln_kernel,
        grid=(t // tm,),
        in_specs=[pl.BlockSpec((tm, D_MODEL), lambda i: (i, 0)),
                  pl.BlockSpec((D_MODEL, D_MODEL), lambda i: (0, 0)),
                  pl.BlockSpec((tm, D_MODEL), lambda i: (i, 0)), vec, vec],
        out_specs=pl.BlockSpec((tm, D_MODEL), lambda i: (i, 0)),
        out_shape=jax.ShapeDtypeStruct((t, D_MODEL), F32),
        compiler_params=_cparams(("parallel",)),
        name="out_proj_ln1",
    )(merged, w_out, x2d, ln_g, ln_b)


def _router_kernel(x_ref, rw_ref, rb_ref, idx_ref, gate_ref, rank_ref, cnt_ref, carry, *, tm):
    @pl.when(pl.program_id(0) == 0)
    def _():
        carry[...] = jnp.zeros_like(carry)

    lane = lax.broadcasted_iota(jnp.int32, (tm, LANES), 1)
    lane_f = lane.astype(F32)
    logits = jnp.dot(x_ref[...], rw_ref[...], precision=lax.Precision.HIGHEST, preferred_element_type=F32) + rb_ref[...]
    cur = jnp.where(lane < N_EXPERTS, logits, -jnp.inf)
    vals, idxs = [], []
    for _k in range(TOP_K):
        m = jnp.max(cur, axis=1, keepdims=True)
        idx = jnp.min(jnp.where(cur == m, lane_f, float(LANES)), axis=1, keepdims=True).astype(jnp.int32)
        vals.append(m)
        idxs.append(idx)
        cur = jnp.where(lane == idx, -jnp.inf, cur)
    es = [jnp.exp(v - vals[0]) for v in vals]
    den = es[0] + es[1] + es[2] + es[3]
    onehot = jnp.zeros((tm, LANES), F32)
    for idx in idxs:
        onehot = onehot + (lane == idx).astype(F32)
    ri = lax.broadcasted_iota(jnp.int32, (tm, tm), 0)
    ci = lax.broadcasted_iota(jnp.int32, (tm, tm), 1)
    before = (ci < ri).astype(BF16)
    prefix = _dot(before, onehot.astype(BF16)) + carry[0:1, :]
    idx_out = jnp.zeros((tm, LANES), jnp.int32)
    gate_out = jnp.zeros((tm, LANES), F32)
    rank_out = jnp.zeros((tm, LANES), jnp.int32)
    for k in range(TOP_K):
        rk = jnp.sum(jnp.where(lane == idxs[k], prefix, 0.0), axis=1, keepdims=True)
        idx_out = jnp.where(lane == k, idxs[k], idx_out)
        gate_out = jnp.where(lane == k, es[k] / den, gate_out)
        rank_out = jnp.where(lane == k, rk.astype(jnp.int32), rank_out)
    idx_ref[...] = idx_out
    gate_ref[...] = gate_out
    rank_ref[...] = rank_out
    total = carry[0:1, :] + jnp.sum(onehot, axis=0, keepdims=True)
    carry[...] = jnp.broadcast_to(total, carry.shape)
    cnt_ref[...] = jnp.broadcast_to(total, cnt_ref.shape)


def _router(x1, rw, rb, tm):
    t = x1.shape[0]
    row = pl.BlockSpec((tm, LANES), lambda i: (i, 0))
    kern = functools.partial(_router_kernel, tm=tm)
    return pl.pallas_call(
        kern,
        grid=(t // tm,),
        in_specs=[pl.BlockSpec((tm, D_MODEL), lambda i: (i, 0)),
                  pl.BlockSpec((D_MODEL, LANES), lambda i: (0, 0)),
                  pl.BlockSpec((1, LANES), lambda i: (0, 0))],
        out_specs=[row, row, row, pl.BlockSpec((8, LANES), lambda i: (0, 0))],
        out_shape=[jax.ShapeDtypeStruct((t, LANES), jnp.int32),
                   jax.ShapeDtypeStruct((t, LANES), F32),
                   jax.ShapeDtypeStruct((t, LANES), jnp.int32),
                   jax.ShapeDtypeStruct((8, LANES), F32)],
        scratch_shapes=[pltpu.VMEM((8, LANES), F32)],
        compiler_params=_cparams(("arbitrary",)),
        name="router",
    )(x1, rw, rb)


def _pack_rows(src_ref, dst_ref):
    half = D_MODEL // 2
    n = src_ref.shape[0]
    for s in range(ROW_SUBLANES):
        lo = pltpu.bitcast(src_ref[:, s * LANES:(s + 1) * LANES].astype(BF16).astype(F32), jnp.uint32)
        hi = pltpu.bitcast(src_ref[:, half + s * LANES:half + (s + 1) * LANES].astype(BF16).astype(F32), jnp.uint32)
        dst_ref[pl.ds(s, n, stride=ROW_SUBLANES), :] = (lo >> 16) | hi


def _unpack_words(w):
    return pltpu.bitcast(w << 16, F32), pltpu.bitcast(w & jnp.uint32(0xFFFF0000), F32)


def _tile(ref, i):
    start = i * ROW_SUBLANES
    if not isinstance(i, int):
        start = pl.multiple_of(start, ROW_SUBLANES)
    return ref.at[pl.ds(start, ROW_SUBLANES)]


def _tile_copy(src, i, dst, j, sem):
    return pltpu.make_async_copy(_tile(src, i), _tile(dst, j), sem)


def _wait_tiles(hbm, n, sem):
    pltpu.make_async_copy(hbm.at[pl.ds(0, n * ROW_SUBLANES)], hbm.at[pl.ds(0, n * ROW_SUBLANES)], sem).wait()


def _dispatch_kernel(pad_start_ref, pad_cnt_ref, dest_ref, x_ref, xs_hbm, stage, zero, sem, *, tm, nt):
    i = pl.program_id(0)

    @pl.when(i < nt)
    def _():
        _pack_rows(x_ref, stage)

        def issue(t_, carry):
            src = _tile(stage, t_)
            for k in range(TOP_K):
                pltpu.make_async_copy(src, _tile(xs_hbm, dest_ref[0, 0, t_ * TOP_K + k]), sem).start()
            return carry

        lax.fori_loop(0, tm, issue, 0, unroll=2)
        _wait_tiles(xs_hbm, tm * TOP_K, sem)

    @pl.when((i >= nt) & (i < nt + N_EXPERTS))
    def _():
        e = i - nt
        zero[...] = jnp.zeros_like(zero)
        start = pad_start_ref[e]
        cnt = pad_cnt_ref[e]

        def issue(r, carry):
            _tile_copy(zero, 0, xs_hbm, start + r, sem).start()
            return carry

        lax.fori_loop(0, cnt, issue, 0)

        def wait(r, carry):
            _tile_copy(zero, 0, xs_hbm, start, sem).wait()
            return carry

        lax.fori_loop(0, cnt, wait, 0)

    @pl.when(i == nt + N_EXPERTS)
    def _():
        stage[...] = jnp.zeros_like(stage)
        start = pad_start_ref[N_EXPERTS]

        def fill(j, carry):
            piece = xs_hbm.at[pl.ds(pl.multiple_of((start + j * tm) * ROW_SUBLANES, ROW_SUBLANES), tm * ROW_SUBLANES)]
            cp = pltpu.make_async_copy(stage, piece, sem)
            cp.start()
            cp.wait()
            return carry

        lax.fori_loop(0, pad_cnt_ref[N_EXPERTS] // tm, fill, 0)


def _dispatch(x1, dest, pad_from, pad_cnt, n_pad, tm):
    t = x1.shape[0]
    nt = t // tm
    assert MOE_BLOCK % tm == 0 and pad_from.shape[0] == N_EXPERTS + 1
    kern = functools.partial(_dispatch_kernel, tm=tm, nt=nt)
    grid_spec = pltpu.PrefetchScalarGridSpec(
        num_scalar_prefetch=2,
        grid=(nt + pad_from.shape[0],),
        in_specs=[pl.BlockSpec((1, 1, tm * TOP_K), lambda i, ps, pc: (jnp.minimum(i, nt - 1), 0, 0),
                               memory_space=pltpu.SMEM),
                  pl.BlockSpec((tm, D_MODEL), lambda i, ps, pc: (jnp.minimum(i, nt - 1), 0))],
        out_specs=pl.BlockSpec(memory_space=pl.ANY),
        scratch_shapes=[pltpu.VMEM((tm * ROW_SUBLANES, LANES), jnp.uint32),
                        pltpu.VMEM((ROW_SUBLANES, LANES), jnp.uint32),
                        pltpu.SemaphoreType.DMA(())],
    )
    return pl.pallas_call(
        kern,
        grid_spec=grid_spec,
        out_shape=jax.ShapeDtypeStruct((n_pad * ROW_SUBLANES, LANES), jnp.uint32),
        compiler_params=_cparams(("arbitrary",)),
        name="moe_dispatch",
    )(pad_from, pad_cnt, dest.reshape(nt, 1, tm * TOP_K), x1)


GU_GROUP = 2 * LANES


def _regroup_kernel(w_ref, o_ref):
    ri = lax.broadcasted_iota(jnp.int32, (GU_GROUP, GU_GROUP), 0)
    ci = lax.broadcasted_iota(jnp.int32, (GU_GROUP, GU_GROUP), 1)
    src = jnp.where(ci < LANES, 2 * ci, 2 * (ci - LANES) + 1)
    perm = jnp.where(ri == src, 1.0, 0.0).astype(BF16)
    for c in range(w_ref.shape[2] // GU_GROUP):
        cs = slice(c * GU_GROUP, (c + 1) * GU_GROUP)
        o_ref[0, :, cs] = _dot(w_ref[0, :, cs].astype(BF16), perm).astype(BF16)


def _regroup_gate_up(w_gate_up):
    e, d, n2 = w_gate_up.shape
    rt = 512
    return pl.pallas_call(
        _regroup_kernel,
        grid=(e, d // rt),
        in_specs=[pl.BlockSpec((1, rt, n2), lambda i, j: (i, j, 0))],
        out_specs=pl.BlockSpec((1, rt, n2), lambda i, j: (i, j, 0)),
        out_shape=jax.ShapeDtypeStruct((e, d, n2), BF16),
        compiler_params=_cparams(("parallel", "parallel")),
        name="regroup_gate_up",
    )(w_gate_up)

def _ffn_kernel(be_ref, nused_ref, rows_ref, x_ref, wgu_ref, bgu_ref, wd_ref, bd_ref, o_ref, acc, xb):
    i = pl.program_id(0)
    f = pl.program_id(1)
    nf = pl.num_programs(1)
    half = D_MODEL // 2

    @pl.when(f == 0)
    def _():
        acc[...] = jnp.broadcast_to(bd_ref[0], acc.shape)
        for s in range(ROW_SUBLANES):
            lo, hi = _unpack_words(x_ref[pl.ds(s, MOE_BLOCK, stride=ROW_SUBLANES), :])
            xb[:, s * LANES:(s + 1) * LANES] = lo.astype(BF16)
            xb[:, half + s * LANES:half + (s + 1) * LANES] = hi.astype(BF16)

    def compute(nrows):
        gu = _dot(xb[0:nrows, :], wgu_ref[0]) + bgu_ref[0]
        acts = []
        for m in range(MOE_TF // LANES):
            gate = jnp.minimum(gu[:, m * GU_GROUP:m * GU_GROUP + LANES], SWIGLU_LIMIT)
            up = jnp.clip(gu[:, m * GU_GROUP + LANES:(m + 1) * GU_GROUP], -SWIGLU_LIMIT, SWIGLU_LIMIT)
            acts.append(((up + 1.0) * gate * _sigmoid(gate * SWIGLU_ALPHA)).astype(BF16))
        acc[0:nrows, :] += _dot(jnp.concatenate(acts, axis=1), wd_ref[0].astype(BF16))

    rows = rows_ref[i]
    pl.when(rows > MOE_BLOCK // 2)(lambda: compute(MOE_BLOCK))
    pl.when((rows > 0) & (rows <= MOE_BLOCK // 2))(lambda: compute(MOE_BLOCK // 2))

    @pl.when(f == nf - 1)
    def _():
        _pack_rows(acc, o_ref)


def _expert_ffn(xs, block_expert, n_used, block_rows, wgu, bgu, wd, bd):
    n_pad = xs.shape[0] // ROW_SUBLANES
    nb = n_pad // MOE_BLOCK
    d_ff = wgu.shape[2] // 2
    nf = d_ff // MOE_TF
    row_blk = (MOE_BLOCK * ROW_SUBLANES, LANES)
    grid_spec = pltpu.PrefetchScalarGridSpec(
        num_scalar_prefetch=3,
        grid=(nb, nf),
        in_specs=[pl.BlockSpec(row_blk, lambda i, f, be, nu, rw: (jnp.minimum(i, nu[0] - 1), 0)),
                  pl.BlockSpec((1, D_MODEL, 2 * MOE_TF), lambda i, f, be, nu, rw: (be[i], 0, f)),
                  pl.BlockSpec((1, 1, 2 * MOE_TF), lambda i, f, be, nu, rw: (be[i], 0, f)),
                  pl.BlockSpec((1, MOE_TF, D_MODEL), lambda i, f, be, nu, rw: (be[i], f, 0)),
                  pl.BlockSpec((1, 1, D_MODEL), lambda i, f, be, nu, rw: (be[i], 0, 0))],
        out_specs=pl.BlockSpec(row_blk, lambda i, f, be, nu, rw: (i, 0)),
        scratch_shapes=[pltpu.VMEM((MOE_BLOCK, D_MODEL), F32), pltpu.VMEM((MOE_BLOCK, D_MODEL), BF16)],
    )
    return pl.pallas_call(
        _ffn_kernel,
        grid_spec=grid_spec,
        out_shape=jax.ShapeDtypeStruct((n_pad * ROW_SUBLANES, LANES), jnp.uint32),
        compiler_params=_cparams(("arbitrary", "arbitrary")),
        name="expert_ffn",
    )(block_expert, n_used, block_rows, xs, wgu, bgu, wd, bd)


def _final_kernel(dest_ref, gate_ref, x_ref, p_ref, wpg_ref, bpg_ref, wple_ref, g_ref, b_ref, ys_hbm, out_ref,
                  rows, sem, *, tm):
    def issue(r, carry):
        _tile_copy(ys_hbm, dest_ref[0, 0, r], rows, r, sem).start()
        return carry

    lax.fori_loop(0, tm * TOP_K, issue, 0, unroll=8)
    x = x_ref[...]
    pg = _dot(x.astype(BF16), wpg_ref[...]) + bpg_ref[...]
    ple = _sigmoid(pg) * _dot(p_ref[...].astype(BF16), wple_ref[...])
    _wait_tiles(ys_hbm, tm * TOP_K, sem)
    gates = gate_ref[...]
    lo_parts, hi_parts = [], []
    for s in range(ROW_SUBLANES):
        acc_lo = acc_hi = None
        for k in range(TOP_K):
            lo, hi = _unpack_words(rows[pl.ds(k * ROW_SUBLANES + s, tm, stride=TOP_K * ROW_SUBLANES), :])
            g = gates[:, k:k + 1]
            acc_lo = g * lo if acc_lo is None else acc_lo + g * lo
            acc_hi = g * hi if acc_hi is None else acc_hi + g * hi
        lo_parts.append(acc_lo)
        hi_parts.append(acc_hi)
    y = jnp.concatenate(lo_parts + hi_parts, axis=1)
    out_ref[...] = _layer_norm(DEEPNORM_ALPHA * x + y + ple, g_ref[...], b_ref[...])


def _final(dest, gates, x1, p2d, w_pg, b_pg, w_ple, ln_g, ln_b, ys, tm):
    t = x1.shape[0]
    vec = pl.BlockSpec((1, D_MODEL), lambda i: (0, 0))
    kern = functools.partial(_final_kernel, tm=tm)
    return pl.pallas_call(
        kern,
        grid=(t // tm,),
        in_specs=[pl.BlockSpec((1, 1, tm * TOP_K), lambda i: (i, 0, 0), memory_space=pltpu.SMEM),
                  pl.BlockSpec((tm, LANES), lambda i: (i, 0)),
                  pl.BlockSpec((tm, D_MODEL), lambda i: (i, 0)),
                  pl.BlockSpec((tm, PLE_DIM), lambda i: (i, 0)),
                  pl.BlockSpec((D_MODEL, D_MODEL), lambda i: (0, 0)), vec,
                  pl.BlockSpec((PLE_DIM, D_MODEL), lambda i: (0, 0)), vec, vec,
                  pl.BlockSpec(memory_space=pl.ANY)],
        out_specs=pl.BlockSpec((tm, D_MODEL), lambda i: (i, 0)),
        out_shape=jax.ShapeDtypeStruct((t, D_MODEL), F32),
        scratch_shapes=[pltpu.VMEM((tm * TOP_K * ROW_SUBLANES, LANES), jnp.uint32), pltpu.SemaphoreType.DMA(())],
        compiler_params=_cparams(("arbitrary",)),
        name="combine_ple_ln2",
    )(dest.reshape(t // tm, 1, tm * TOP_K), gates, x1, p2d, w_pg, b_pg, w_ple, ln_g, ln_b, ys)


def _row_tile(t):
    return min(512, t)


def _layer(x, p, w_in, b_gate, conv_w, a_log, dt_bias, dn_norm_w, w_branch_a, w_branch_b, w_out, ln1_g, ln1_b,
           router_w, router_b, w_gate_up, b_gate_up, w_down, b_down, w_ple, w_ple_gate, b_ple_gate, ln2_g, ln2_b):
    bsz, seq, _ = x.shape
    t = bsz * seq
    tm = _row_tile(t)
    x2d = x.reshape(t, D_MODEL)
    xb = x2d.astype(BF16)

    c_a = 3 * A_QKV
    c_b = 2 * B_QK + 2 * B_VZ
    w_bf = w_in.astype(BF16)
    tp = min(IN_PROJ_TM, seq)
    u_b = _matmul(xb, w_bf[:, c_a:c_a + c_b], BF16, tp, IN_PROJ_TN)
    w_bab = jnp.pad(w_bf[:, c_a + c_b:c_a + c_b + B_GATES], ((0, 0), (0, LANES - B_GATES)))
    bab = _matmul(xb, w_bab, F32, tp, LANES)
    gpre = _matmul(xb, w_bf[:, c_a + c_b + B_GATES:], F32, tp, IN_PROJ_TN)

    slopes = _alibi_slopes()
    gw = A_HEADS * HEAD_DIM
    outs, lses = [], []
    for gi, (_win, dil) in enumerate(DILATION_GROUPS):
        w_g = jnp.concatenate([w_bf[:, part * A_QKV + gi * gw:part * A_QKV + (gi + 1) * gw] for part in range(3)], axis=1)
        o_g, l_g = _attention_group(_matmul_classes(xb, w_g, bsz, dil, tp, IN_PROJ_TN), gi, dil, slopes[gi])
        outs.append(o_g)
        lses.append(l_g)

    cw = jnp.pad(conv_w.astype(F32), ((0, 8 - CONV_W), (0, 0)))
    lane_is_g = (np.arange(LANES) % (2 * B_V_HEADS) >= B_V_HEADS) & (np.arange(LANES) < B_GATES)
    neg_a = jnp.zeros((LANES,), F32).at[B_V_HEADS:2 * B_V_HEADS].set(-jnp.exp(a_log[0].astype(F32)))
    neg_a = neg_a.at[3 * B_V_HEADS:4 * B_V_HEADS].set(-jnp.exp(a_log[1].astype(F32)))
    dtb = jnp.zeros((LANES,), F32).at[B_V_HEADS:2 * B_V_HEADS].set(dt_bias[0].astype(F32))
    dtb = dtb.at[3 * B_V_HEADS:4 * B_V_HEADS].set(dt_bias[1].astype(F32))
    gate_params = jnp.zeros((8, LANES), F32).at[0].set(neg_a).at[1].set(dtb).at[2].set(jnp.asarray(lane_is_g, F32))
    qn, kn, vn, gts = _dn_prep(u_b.reshape(bsz, seq, c_b), cw, bab.reshape(bsz, seq, LANES), gate_params)
    gts_t = gts.reshape(bsz, seq // DN_CHUNK, DN_CHUNK, LANES).transpose(0, 1, 3, 2)
    o_f, o_b = _dn_state(*_dn_intra(qn, kn, vn, gts, gts_t), seq)

    bg = b_gate.astype(F32).reshape(1, 2 * D_MODEL)
    a_part = _branch_a(outs, lses, w_branch_a.astype(BF16), gpre, bg, tm)
    merged = _branch_b(o_f.reshape(t, B_VZ), o_b.reshape(t, B_VZ), u_b, dn_norm_w.astype(F32).reshape(1, B_DV), w_branch_b.astype(BF16), gpre, bg,
                       a_part, tm)
    x1 = _out_ln(merged, w_out.astype(BF16), x2d, ln1_g.reshape(1, -1), ln1_b.reshape(1, -1), tm)

    rw = jnp.pad(router_w.astype(F32), ((0, 0), (0, LANES - N_EXPERTS)))
    rb = jnp.pad(router_b.astype(F32), (0, LANES - N_EXPERTS)).reshape(1, LANES)
    idx, gates, rank, cnt = _router(x1, rw, rb, tm)
    counts = cnt[0, :N_EXPERTS].astype(jnp.int32)
    padded = (counts + MOE_BLOCK - 1) // MOE_BLOCK * MOE_BLOCK
    pad_end = jnp.cumsum(padded)
    pad_start = pad_end - padded
    dest = pad_start[idx[:, :TOP_K]] + rank[:, :TOP_K]
    n_pad = t * TOP_K + N_EXPERTS * MOE_BLOCK
    nb = n_pad // MOE_BLOCK
    block_start = jnp.arange(nb, dtype=jnp.int32) * MOE_BLOCK
    block_expert = jnp.minimum(jnp.sum(pad_end[None, :] <= block_start[:, None], axis=1), N_EXPERTS - 1).astype(jnp.int32)
    n_used = (pad_end[-1:] // MOE_BLOCK).astype(jnp.int32)

    zero_from = jnp.concatenate([pad_start + counts, pad_end[-1:]]).astype(jnp.int32)
    zero_cnt = jnp.concatenate([padded - counts, n_pad - pad_end[-1:]]).astype(jnp.int32)
    xs = _dispatch(x1, dest, zero_from, zero_cnt, n_pad, tm)
    d_ff = w_down.shape[1]
    wgu = _regroup_gate_up(w_gate_up.astype(F32))
    bgu = b_gate_up.astype(F32).reshape(N_EXPERTS, d_ff // LANES, LANES, 2).transpose(0, 1, 3, 2)
    bgu = bgu.reshape(N_EXPERTS, 1, 2 * d_ff)
    seg_end = (pad_start + counts)[block_expert]
    block_rows = jnp.where(block_start < pad_end[-1], jnp.clip(seg_end - block_start, 0, MOE_BLOCK), 0).astype(jnp.int32)
    ys = _expert_ffn(xs, block_expert, n_used, block_rows, wgu, bgu, w_down.astype(F32),
                     b_down.astype(F32).reshape(N_EXPERTS, 1, D_MODEL))

    out = _final(dest, gates, x1, p.reshape(t, PLE_DIM), w_ple_gate.astype(BF16),
                 b_ple_gate.astype(F32).reshape(1, -1), w_ple.astype(BF16), ln2_g.reshape(1, -1),
                 ln2_b.reshape(1, -1), ys, min(COMBINE_TM, t))
    return out.reshape(bsz, seq, D_MODEL)


def kernel(x, p, w_in, b_gate, conv_w, a_log, dt_bias, dn_norm_w, w_branch_a, w_branch_b, w_out, ln1_g, ln1_b,
           router_w, router_b, w_gate_up, b_gate_up, w_down, b_down, w_ple, w_ple_gate, b_ple_gate, ln2_g, ln2_b):
    assert w_in.shape[0] == DEPTH
    return _layer(x, p[0], w_in[0], b_gate[0], conv_w[0], a_log[0], dt_bias[0], dn_norm_w[0], w_branch_a[0],
                  w_branch_b[0], w_out[0], ln1_g[0], ln1_b[0], router_w[0], router_b[0], w_gate_up[0],
                  b_gate_up[0], w_down[0], b_down[0], w_ple[0], w_ple_gate[0], b_ple_gate[0], ln2_g[0], ln2_b[0])
```

```python
import functools

import numpy as np
import jax
import jax.numpy as jnp
from jax import lax
from jax.experimental import pallas as pl
from jax.experimental.pallas import tpu as pltpu

F32 = jnp.float32
BF16 = jnp.bfloat16

D_MODEL = 2048
HEAD_DIM = 128
A_HEADS = 8
DILATION_GROUPS = ((128, 1), (512, 4), (2048, 16))
N_GROUPS = 3
NEG_INF = -1e30
B_QK_HEADS = 8
B_V_HEADS = 16
B_DK = 128
B_DV = 128
CONV_W = 5
RMS_EPS = 1e-6
N_EXPERTS = 32
TOP_K = 4
SWIGLU_ALPHA = 1.702
SWIGLU_LIMIT = 7.0
PLE_DIM = 256
DEPTH = 1
DEEPNORM_ALPHA = (2 * DEPTH) ** 0.25
LN_EPS = 1e-5
A_QKV = N_GROUPS * A_HEADS * HEAD_DIM
B_QK = B_QK_HEADS * B_DK
B_VZ = B_V_HEADS * B_DV
B_GATES = 4 * B_V_HEADS

LANES = 128
N_SIDE = 64
Q_SUB = 128
DN_CHUNK = 64
DN_PAIR = B_V_HEADS // B_QK_HEADS
DN_INTRA_CHUNKS = 2
DN_STATE_CHUNKS = 2
TRI_BASE = 16
ROW_SUBLANES = 8
IN_PROJ_TM = 1024
IN_PROJ_TN = 1024
MOE_BLOCK = 1024
MOE_TF = 512
COMBINE_TM = 256
VMEM_LIMIT = 56 * 1024 * 1024
assert D_MODEL == 2 * ROW_SUBLANES * LANES


def _cparams(sem):
    return pltpu.CompilerParams(dimension_semantics=sem, vmem_limit_bytes=VMEM_LIMIT)


def _sigmoid(x):
    return 1.0 / (1.0 + jnp.exp(-x))


def _dot(a, b):
    return jnp.dot(a, b, preferred_element_type=F32)


def _dot_nt(a, b):
    return lax.dot_general(a, b, (((1,), (1,)), ((), ())), preferred_element_type=F32)


def _dot_tn(a, b):
    return lax.dot_general(a, b, (((0,), (0,)), ((), ())), preferred_element_type=F32)


def _mm_kernel(x_ref, w_ref, o_ref):
    o_ref[...] = _dot(x_ref[...], w_ref[...]).astype(o_ref.dtype)


def _matmul(x, w, out_dtype, tm, tn):
    m, k = x.shape
    n = w.shape[1]
    return pl.pallas_call(
        _mm_kernel,
        grid=(m // tm, n // tn),
        in_specs=[pl.BlockSpec((tm, k), lambda i, j: (i, 0)),
                  pl.BlockSpec((k, tn), lambda i, j: (0, j))],
        out_specs=pl.BlockSpec((tm, tn), lambda i, j: (i, j)),
        out_shape=jax.ShapeDtypeStruct((m, n), out_dtype),
        compiler_params=_cparams(("parallel", "parallel")),
        name="in_proj",
    )(x, w)


def _mm_classes_kernel(x_ref, w_ref, o_ref, acc_ref, *, dil, rows):
    y = _dot(x_ref[...], w_ref[...])
    for j in range(acc_ref.shape[0]):
        ls = slice(j * LANES, (j + 1) * LANES)
        acc_ref[j] = y[:, ls]
        for c in range(dil):
            o_ref[0, c, :, ls] = acc_ref[j, pl.ds(c, rows, stride=dil), :].astype(o_ref.dtype)


def _matmul_classes(x, w, bsz, dil, tm, tn):
    m, k = x.shape
    n = w.shape[1]
    seq = m // bsz
    tiles_per_seq = seq // tm
    rows = tm // dil
    kern = functools.partial(_mm_classes_kernel, dil=dil, rows=rows)
    return pl.pallas_call(
        kern,
        grid=(m // tm, n // tn),
        in_specs=[pl.BlockSpec((tm, k), lambda i, j: (i, 0)),
                  pl.BlockSpec((k, tn), lambda i, j: (0, j))],
        out_specs=pl.BlockSpec((1, dil, rows, tn), lambda i, j: (i // tiles_per_seq, 0, i % tiles_per_seq, j)),
        out_shape=jax.ShapeDtypeStruct((bsz, dil, seq // dil, n), BF16),
        scratch_shapes=[pltpu.VMEM((tn // LANES, tm, LANES), F32)],
        compiler_params=_cparams(("parallel", "parallel")),
        name=f"in_proj_dil{dil}",
    )(x, w)


def _attn_kernel(q_ref, kp_ref, kc_ref, kn_ref, vp_ref, vc_ref, vn_ref, o_ref, lse_ref, kbuf, vbuf,
                 *, dil, sub_len, tl, slopes):
    i0 = pl.program_id(2) * tl
    kbuf[0:N_SIDE, :] = kp_ref[...]
    kbuf[N_SIDE:N_SIDE + tl, :] = kc_ref[...]
    kbuf[N_SIDE + tl:, :] = kn_ref[...]
    vbuf[0:N_SIDE, :] = vp_ref[...]
    vbuf[N_SIDE:N_SIDE + tl, :] = vc_ref[...]
    vbuf[N_SIDE + tl:, :] = vn_ref[...]
    span = Q_SUB + 2 * N_SIDE
    qq = lax.broadcasted_iota(jnp.int32, (Q_SUB, span), 0)
    kk = lax.broadcasted_iota(jnp.int32, (Q_SUB, span), 1)
    delta = kk - N_SIDE - qq
    absd = jnp.abs(delta)
    band = absd <= N_SIDE
    dist = (dil * absd).astype(F32)
    lane = lax.broadcasted_iota(jnp.int32, (Q_SUB, LANES), 1)
    scale = HEAD_DIM ** -0.5
    for j in range(tl // Q_SUB):
        pos = i0 + (j * Q_SUB - N_SIDE) + kk
        valid = band & (pos >= 0) & (pos < sub_len)
        lse_tile = jnp.zeros((Q_SUB, LANES), F32)
        for h in range(A_HEADS):
            hs = slice(h * HEAD_DIM, (h + 1) * HEAD_DIM)
            q = q_ref[j * Q_SUB:(j + 1) * Q_SUB, hs]
            k = kbuf[j * Q_SUB:j * Q_SUB + span, hs]
            v = vbuf[j * Q_SUB:j * Q_SUB + span, hs]
            s = _dot_nt(q, k) * scale
            s = jnp.where(valid, s - float(slopes[h]) * dist, NEG_INF)
            m = jnp.max(s, axis=1, keepdims=True)
            p = jnp.exp(s - m)
            l = jnp.sum(p, axis=1, keepdims=True)
            o = _dot(p.astype(BF16), v) / l
            o_ref[j * Q_SUB:(j + 1) * Q_SUB, hs] = o.astype(o_ref.dtype)
            lse_tile = jnp.where(lane == h, m + jnp.log(l), lse_tile)
        lse_ref[j * Q_SUB:(j + 1) * Q_SUB, :] = lse_tile


def _attention_group(qkv, gi, dil, slopes):
    bsz, _, sub_len, _ = qkv.shape
    tl = min(512, sub_len)
    assert sub_len % tl == 0 and tl % Q_SUB == 0 and sub_len % N_SIDE == 0
    width = A_HEADS * HEAD_DIM
    halo_per_tile = tl // N_SIDE
    n_halo = sub_len // N_SIDE
    prev = lambda li: jnp.maximum(li * halo_per_tile - 1, 0)
    nxt = lambda li: jnp.minimum((li + 1) * halo_per_tile, n_halo - 1)
    halo = lambda col, rowf: pl.BlockSpec((None, None, N_SIDE, width), lambda b, c, li: (b, c, rowf(li), col))
    cur = lambda col: pl.BlockSpec((None, None, tl, width), lambda b, c, li: (b, c, li, col))
    kern = functools.partial(_attn_kernel, dil=dil, sub_len=sub_len, tl=tl, slopes=tuple(float(s) for s in slopes))
    return pl.pallas_call(
        kern,
        grid=(bsz, dil, sub_len // tl),
        in_specs=[cur(0), halo(1, prev), cur(1), halo(1, nxt), halo(2, prev), cur(2), halo(2, nxt)],
        out_specs=[pl.BlockSpec((None, None, tl, width), lambda b, c, li: (b, c, li, 0)),
                   pl.BlockSpec((None, None, tl, LANES), lambda b, c, li: (b, c, li, 0))],
        out_shape=[jax.ShapeDtypeStruct((bsz, dil, sub_len, width), BF16),
                   jax.ShapeDtypeStruct((bsz, dil, sub_len, LANES), F32)],
        scratch_shapes=[pltpu.VMEM((tl + 2 * N_SIDE, width), BF16),
                        pltpu.VMEM((tl + 2 * N_SIDE, width), BF16)],
        compiler_params=_cparams(("parallel", "parallel", "parallel")),
        name=f"dilated_attn_g{gi}",
    )(qkv, qkv, qkv, qkv, qkv, qkv, qkv)


def _alibi_slopes():
    n = N_GROUPS * A_HEADS
    s = 2.0 ** (-8.0 * np.arange(1, n + 1) / n)
    return s.astype(np.float32).reshape(N_GROUPS, A_HEADS)


def _dn_prep_kernel(prev_ref, cur_ref, next_ref, cw_ref, bab_ref, gp_ref, q_ref, k_ref, v_ref, g_ref, *, ts):
    ti = pl.program_id(1)
    nt = pl.num_programs(1)
    halo = CONV_W // 2
    keep_prev = (ti > 0).astype(F32)
    keep_next = (ti < nt - 1).astype(F32)
    for c in range((2 * B_QK + B_VZ) // LANES):
        cs = slice(c * LANES, (c + 1) * LANES)
        xp = prev_ref[0, :, cs].astype(F32)[8:16] * keep_prev
        xc = cur_ref[0, :, cs].astype(F32)
        xn = next_ref[0, :, cs].astype(F32)[0:8] * keep_next
        ext = jnp.concatenate([xp, xc, xn], axis=0)
        acc = jnp.zeros((ts, LANES), F32)
        for j in range(CONV_W):
            off = 8 - halo + j
            acc = acc + ext[off:off + ts, :] * cw_ref[j:j + 1, cs]
        y = acc * _sigmoid(acc)
        if c < 2 * B_QK // LANES:
            y = y * lax.rsqrt(jnp.sum(y * y, axis=1, keepdims=True) + 1e-6)
        if c < B_QK // LANES:
            q_ref[0, :, cs] = y * (B_DK ** -0.5)
        elif c < 2 * B_QK // LANES:
            k_ref[0, :, c * LANES - B_QK:(c + 1) * LANES - B_QK] = y
        else:
            v_ref[0, :, c * LANES - 2 * B_QK:(c + 1) * LANES - 2 * B_QK] = y
    x = bab_ref[0]
    neg_a = gp_ref[0:1, :]
    dtb = gp_ref[1:2, :]
    is_g = gp_ref[2:3, :] > 0.5
    z = x + dtb
    softplus = jnp.maximum(z, 0.0) + jnp.log(1.0 + jnp.exp(-jnp.abs(z)))
    g_ref[0] = jnp.where(is_g, neg_a * softplus, _sigmoid(x))


def _dn_prep(u_b, conv_w, bab, gate_params):
    bsz, seq, _ = u_b.shape
    ts = min(256, seq)
    cq = 2 * B_QK + B_VZ
    nhalo = seq // 16
    per = ts // 16
    kern = functools.partial(_dn_prep_kernel, ts=ts)
    return pl.pallas_call(
        kern,
        grid=(bsz, seq // ts),
        in_specs=[pl.BlockSpec((1, 16, cq), lambda b, t: (b, jnp.maximum(t * per - 1, 0), 0)),
                  pl.BlockSpec((1, ts, cq), lambda b, t: (b, t, 0)),
                  pl.BlockSpec((1, 16, cq), lambda b, t: (b, jnp.minimum((t + 1) * per, nhalo - 1), 0)),
                  pl.BlockSpec((8, cq), lambda b, t: (0, 0)),
                  pl.BlockSpec((1, ts, LANES), lambda b, t: (b, t, 0)),
                  pl.BlockSpec((8, LANES), lambda b, t: (0, 0))],
        out_specs=[pl.BlockSpec((1, ts, B_QK), lambda b, t: (b, t, 0)),
                   pl.BlockSpec((1, ts, B_QK), lambda b, t: (b, t, 0)),
                   pl.BlockSpec((1, ts, B_VZ), lambda b, t: (b, t, 0)),
                   pl.BlockSpec((1, ts, LANES), lambda b, t: (b, t, 0))],
        out_shape=[jax.ShapeDtypeStruct((bsz, seq, B_QK), F32),
                   jax.ShapeDtypeStruct((bsz, seq, B_QK), F32),
                   jax.ShapeDtypeStruct((bsz, seq, B_VZ), F32),
                   jax.ShapeDtypeStruct((bsz, seq, LANES), F32)],
        compiler_params=_cparams(("parallel", "parallel")),
        name="dn_prep",
    )(u_b, u_b, u_b, conv_w, bab, gate_params)


def _dn_intra_kernel(q_ref, k_ref, v_ref, g_ref, gt_ref, u_ref, wq_ref, kq_ref, et_ref, *, c, per):
    n_units = 2 * DN_PAIR
    w4 = n_units * c
    hp = lax.Precision.HIGHEST
    bf = lambda t_: t_.astype(BF16)
    ii = lax.broadcasted_iota(jnp.int32, (c, w4), 0)
    ll = lax.broadcasted_iota(jnp.int32, (c, w4), 1)
    jj = ll % c
    ub = ll // c
    ub_row = ub[0:1, :]
    lo = jnp.where(ub >= DN_PAIR, jj - ii, ii - jj)
    incl = lo >= 0
    strict = lo > 0
    eye = (ii == jj).astype(F32)
    blk = (ii // TRI_BASE) == (jj // TRI_BASE)

    def pack(parts, sel):
        out = parts[n_units - 1]
        for u_ in range(n_units - 2, -1, -1):
            out = jnp.where(sel == u_, parts[u_], out)
        return out

    unit_mask = [jnp.where(ub == u_, 1.0, 0.0).astype(BF16) for u_ in range(n_units)]

    def block_diag(y16):
        return jnp.concatenate([y16 * m_ for m_ in unit_mask], axis=0)

    def mm(xs, ys):
        return [_dot(bf(x_), block_diag(bf(y_))) for x_, y_ in zip(xs, ys)]

    ri = lax.broadcasted_iota(jnp.int32, (c, c), 0)
    ci = lax.broadcasted_iota(jnp.int32, (c, c), 1)
    tri4 = (lo <= 0).astype(F32)
    tri_f = (ci <= ri).astype(F32)
    tri_b = (ci >= ri).astype(F32)
    items = [(cc, p) for cc in range(per) for p in range(B_V_HEADS // DN_PAIR)]
    qs_, ks_, lms, qkms, betas, egcs, kscales, etots = [], [], [], [], [], [], [], []
    for cc, p in items:
        rows_cc = slice(cc * c, (cc + 1) * c)
        if p == 0:
            g_all = g_ref[0, rows_cc, :]
            gc_dir = [jnp.dot(tri_f, g_all, precision=hp, preferred_element_type=F32),
                      jnp.dot(tri_b, g_all, precision=hp, preferred_element_type=F32)]
            gcr_all = jnp.dot(gt_ref[0, cc], tri4, precision=hp, preferred_element_type=F32)
            tot_all = jnp.sum(g_all, axis=0, keepdims=True)
        cs = slice(p * B_DK, (p + 1) * B_DK)
        q = q_ref[0, rows_cc, cs]
        k = k_ref[0, rows_cc, cs]
        k16 = bf(k)
        k4 = jnp.concatenate([k16] * n_units, axis=0)
        gram = _dot_nt(k16, k4)
        qk = _dot_nt(bf(q), k4)
        beta_u, gc_u, gcr_u, tot_u = [], [], [], []
        for u_ in range(n_units):
            d_, e_ = divmod(u_, DN_PAIR)
            h = p * DN_PAIR + e_
            bl = d_ * 2 * B_V_HEADS + h
            gl = bl + B_V_HEADS
            beta_u.append(g_all[:, bl:bl + 1])
            gc_u.append(gc_dir[d_][:, gl:gl + 1])
            gcr_u.append(gcr_all[gl:gl + 1, :])
            tot_u.append(tot_all[:, gl:gl + 1])
        gc_p = pack(gc_u, ub)
        gcr_p = pack(gcr_u, ub_row)
        tot_p = pack(tot_u, ub_row)
        dec = jnp.where(incl, jnp.exp(jnp.where(incl, gc_p - gcr_p, 0.0)), 0.0)
        lms.append(jnp.where(strict, pack(beta_u, ub) * gram * dec, 0.0))
        qkms.append(jnp.where(incl, qk * dec, 0.0))
        qs_.append(q)
        ks_.append(k)
        betas.append(beta_u)
        egcs.append([jnp.exp(g_) for g_ in gc_u])
        kscales.append(jnp.exp(tot_p - gcr_p))
        etots.append([jnp.exp(t_) for t_ in tot_u])

    d1 = [jnp.where(blk, lm, 0.0) for lm in lms]
    d2 = mm(d1, d1)
    d4 = mm(d2, d2)
    d8 = mm(d4, d4)
    tm_ = [eye - d_ for d_ in d1]
    for dk in (d2, d4, d8):
        tm_ = [a + b for a, b in zip(tm_, mm(tm_, dk))]
    size = TRI_BASE
    while size < c:
        off = ((ii // (2 * size)) == (jj // (2 * size))) & ((ii // size) != (jj // size))
        cm = [jnp.where(off, lm, 0.0) for lm in lms]
        pc = mm(tm_, cm)
        tm_ = [a - b for a, b in zip(tm_, mm(pc, tm_))]
        size *= 2

    lane2 = lax.broadcasted_iota(jnp.int32, (1, DN_PAIR * B_DV), 1)
    for it, (cc, p) in enumerate(items):
        q, k = qs_[it], ks_[it]
        rows = []
        for u_ in range(n_units):
            d_, e_ = divmod(u_, DN_PAIR)
            h = p * DN_PAIR + e_
            v = v_ref[0, cc * c:(cc + 1) * c, h * B_DV:(h + 1) * B_DV]
            beta = betas[it][u_]
            rows.append(jnp.concatenate([v * beta, k * (beta * egcs[it][u_])], axis=1))
        uw = _dot(block_diag(bf(tm_[it])), bf(jnp.concatenate(rows, axis=0)))
        k_t = jnp.transpose(jnp.concatenate([k] * DN_PAIR, axis=0))
        for d_ in range(2):
            u0, u1 = d_ * DN_PAIR, d_ * DN_PAIR + 1
            u_ref[d_, 0, cc, p] = jnp.concatenate([uw[u0 * c:(u0 + 1) * c, :B_DV], uw[u1 * c:(u1 + 1) * c, :B_DV]], axis=1)
            wq = jnp.concatenate([uw[u0 * c:(u0 + 1) * c, B_DV:], q * egcs[it][u0],
                                  uw[u1 * c:(u1 + 1) * c, B_DV:], q * egcs[it][u1]], axis=0)
            wq_ref[d_, 0, cc, p] = bf(wq)
            ls = slice(d_ * DN_PAIR * c, (d_ + 1) * DN_PAIR * c)
            kq = jnp.concatenate([k_t * kscales[it][:, ls], qkms[it][:, ls]], axis=0)
            kq_ref[d_, 0, cc, p] = bf(kq)
            et_ref[d_, 0, cc, p:p + 1, :] = jnp.where(lane2 < B_DV, etots[it][u0], etots[it][u1])


def _dn_intra(qn, kn, vn, gts, gts_t):
    bsz, seq, _ = qn.shape
    c = DN_CHUNK
    assert DN_PAIR * c == B_DK and seq % c == 0
    n = seq // c
    npair = B_V_HEADS // DN_PAIR
    wide = DN_PAIR * B_DV
    per = DN_INTRA_CHUNKS if n % DN_INTRA_CHUNKS == 0 else 1
    kern = functools.partial(_dn_intra_kernel, c=c, per=per)
    out5 = lambda r, cdim: pl.BlockSpec((2, 1, per, npair, r, cdim), lambda b, i: (0, b, i, 0, 0, 0))
    return pl.pallas_call(
        kern,
        grid=(bsz, n // per),
        in_specs=[pl.BlockSpec((1, per * c, B_QK), lambda b, i: (b, i, 0)),
                  pl.BlockSpec((1, per * c, B_QK), lambda b, i: (b, i, 0)),
                  pl.BlockSpec((1, per * c, B_VZ), lambda b, i: (b, i, 0)),
                  pl.BlockSpec((1, per * c, LANES), lambda b, i: (b, i, 0)),
                  pl.BlockSpec((1, per, LANES, c), lambda b, i: (b, i, 0, 0))],
        out_specs=[out5(c, wide), out5(2 * DN_PAIR * c, B_DK), out5(B_DK + c, DN_PAIR * c),
                   pl.BlockSpec((2, 1, per, npair, wide), lambda b, i: (0, b, i, 0, 0))],
        out_shape=[jax.ShapeDtypeStruct((2, bsz, n, npair, c, wide), F32),
                   jax.ShapeDtypeStruct((2, bsz, n, npair, 2 * DN_PAIR * c, B_DK), BF16),
                   jax.ShapeDtypeStruct((2, bsz, n, npair, B_DK + c, DN_PAIR * c), BF16),
                   jax.ShapeDtypeStruct((2, bsz, n, npair, wide), F32)],
        compiler_params=_cparams(("parallel", "parallel")),
        name="dn_intra",
    )(qn, kn, vn, gts, gts_t)


def _dn_state_kernel(uf_ref, wqf_ref, kqf_ref, etf_ref, ub_ref, wqb_ref, kqb_ref, etb_ref, of_ref, ob_ref, s_ref,
                     *, c, per):
    @pl.when(pl.program_id(1) == 0)
    def _():
        s_ref[...] = jnp.zeros_like(s_ref)

    npair = B_V_HEADS // DN_PAIR
    bf = lambda t_: t_.astype(BF16)
    chains = [(d_, p) for d_ in range(2) for p in range(npair)]
    refs = ((uf_ref, wqf_ref, kqf_ref, etf_ref, of_ref), (ub_ref, wqb_ref, kqb_ref, etb_ref, ob_ref))
    zero = jnp.zeros((c, B_DV), F32)
    for step in range(per):
        local = (step, per - 1 - step)
        states = [s_ref[d_, p] for d_, p in chains]
        a_res = [_dot(refs[d_][1][0, 0, local[d_], p], bf(s_)) for (d_, p), s_ in zip(chains, states)]
        b_res = []
        for (d_, p), a_ in zip(chains, a_res):
            u = refs[d_][0][0, 0, local[d_], p]
            v0 = u[:, :B_DV] - a_[0:c, :B_DV]
            v1 = u[:, B_DV:] - a_[2 * c:3 * c, B_DV:]
            bd_v = jnp.concatenate([jnp.concatenate([v0, zero], axis=1), jnp.concatenate([zero, v1], axis=1)], axis=0)
            b_res.append(_dot(refs[d_][2][0, 0, local[d_], p], bf(bd_v)))
        for (d_, p), a_, b_, s_ in zip(chains, a_res, b_res, states):
            s_ref[d_, p] = s_ * refs[d_][3][0, 0, local[d_], p:p + 1, :] + b_[:B_DK]
            o_ref = refs[d_][4]
            rs = slice(local[d_] * c, (local[d_] + 1) * c)
            o_ref[0, rs, (2 * p) * B_DV:(2 * p + 1) * B_DV] = a_[c:2 * c, :B_DV] + b_[B_DK:, :B_DV]
            o_ref[0, rs, (2 * p + 1) * B_DV:(2 * p + 2) * B_DV] = a_[3 * c:4 * c, B_DV:] + b_[B_DK:, B_DV:]


def _dn_state(u_all, wq_all, kq_all, et_all, seq):
    _, bsz, n, npair, c, wide = u_all.shape
    per = DN_STATE_CHUNKS if n % DN_STATE_CHUNKS == 0 else 1
    nb = n // per
    kern = functools.partial(_dn_state_kernel, c=c, per=per)
    fwd = lambda r, cdim: pl.BlockSpec((1, 1, per, npair, r, cdim), lambda b, i: (0, b, i, 0, 0, 0))
    bwd = lambda r, cdim: pl.BlockSpec((1, 1, per, npair, r, cdim), lambda b, i: (1, b, nb - 1 - i, 0, 0, 0))
    et_f = pl.BlockSpec((1, 1, per, npair, wide), lambda b, i: (0, b, i, 0, 0))
    et_b = pl.BlockSpec((1, 1, per, npair, wide), lambda b, i: (1, b, nb - 1 - i, 0, 0))
    shapes = ((c, wide), (2 * DN_PAIR * c, B_DK), (B_DK + c, DN_PAIR * c))
    return pl.pallas_call(
        kern,
        grid=(bsz, nb),
        in_specs=[fwd(*shapes[0]), fwd(*shapes[1]), fwd(*shapes[2]), et_f,
                  bwd(*shapes[0]), bwd(*shapes[1]), bwd(*shapes[2]), et_b],
        out_specs=[pl.BlockSpec((1, per * c, B_VZ), lambda b, i: (b, i, 0)),
                   pl.BlockSpec((1, per * c, B_VZ), lambda b, i: (b, nb - 1 - i, 0))],
        out_shape=[jax.ShapeDtypeStruct((bsz, seq, B_VZ), F32), jax.ShapeDtypeStruct((bsz, seq, B_VZ), F32)],
        scratch_shapes=[pltpu.VMEM((2, npair, B_DK, wide), F32)],
        compiler_params=_cparams(("parallel", "arbitrary")),
        name="dn_state",
    )(u_all, wq_all, kq_all, et_all, u_all, wq_all, kq_all, et_all)


def _branch_a_kernel(o0_ref, o1_ref, o2_ref, l0_ref, l1_ref, l2_ref, w_ref, gp_ref, bg_ref, out_ref, o_sc, l_sc,
                     *, dils, tm):
    for g, (o_ref, l_ref) in enumerate(((o0_ref, l0_ref), (o1_ref, l1_ref), (o2_ref, l2_ref))):
        r = dils[g]
        for c in range(r):
            l_sc[g, pl.ds(c, tm // r, stride=r), :] = l_ref[c]
            for h in range(A_HEADS):
                o_sc[g, h, pl.ds(c, tm // r, stride=r), :] = o_ref[c, :, h * HEAD_DIM:(h + 1) * HEAD_DIM].astype(F32)
    ls = [l_sc[g] for g in range(N_GROUPS)]
    m = jnp.maximum(jnp.maximum(ls[0], ls[1]), ls[2])
    es = [jnp.exp(l - m) for l in ls]
    den = es[0] + es[1] + es[2]
    ws = [e / den for e in es]
    parts = []
    for h in range(A_HEADS):
        hs = slice(h * HEAD_DIM, (h + 1) * HEAD_DIM)
        acc = ws[0][:, h:h + 1] * o_sc[0, h]
        acc = acc + ws[1][:, h:h + 1] * o_sc[1, h]
        acc = acc + ws[2][:, h:h + 1] * o_sc[2, h]
        parts.append(acc.astype(BF16))
    oa = jnp.concatenate(parts, axis=1)
    y = _dot(oa, w_ref[...])
    out_ref[...] = _sigmoid(gp_ref[...] + bg_ref[...]) * y


def _branch_a(outs, lses, w_a, gpre, b_gate, tm):
    bsz = outs[0].shape[0]
    dils = tuple(o.shape[1] for o in outs)
    seq = dils[0] * outs[0].shape[2]
    t = bsz * seq
    per_seq = seq // tm
    wd = A_HEADS * HEAD_DIM
    cls = lambda r, width: pl.BlockSpec((None, r, tm // r, width), lambda i: (i // per_seq, 0, i % per_seq, 0))
    kern = functools.partial(_branch_a_kernel, dils=dils, tm=tm)
    return pl.pallas_call(
        kern,
        grid=(t // tm,),
        in_specs=[cls(dils[0], wd), cls(dils[1], wd), cls(dils[2], wd),
                  cls(dils[0], LANES), cls(dils[1], LANES), cls(dils[2], LANES),
                  pl.BlockSpec((wd, D_MODEL), lambda i: (0, 0)),
                  pl.BlockSpec((tm, D_MODEL), lambda i: (i, 0)),
                  pl.BlockSpec((1, D_MODEL), lambda i: (0, 0))],
        out_specs=pl.BlockSpec((tm, D_MODEL), lambda i: (i, 0)),
        out_shape=jax.ShapeDtypeStruct((t, D_MODEL), F32),
        scratch_shapes=[pltpu.VMEM((N_GROUPS, A_HEADS, tm, HEAD_DIM), F32), pltpu.VMEM((N_GROUPS, tm, LANES), F32)],
        compiler_params=_cparams(("parallel",)),
        name="branch_a",
    )(*outs, *lses, w_a, gpre, b_gate)


def _branch_b_kernel(of_ref, ob_ref, z_ref, nw_ref, w_ref, gp_ref, bg_ref, a_ref, out_ref):
    nw = nw_ref[...]
    parts = []
    for h in range(B_V_HEADS):
        hs = slice(h * B_DV, (h + 1) * B_DV)
        o = of_ref[:, hs] + ob_ref[:, hs]
        z = z_ref[:, hs].astype(F32)
        o = o * lax.rsqrt(jnp.mean(o * o, axis=1, keepdims=True) + RMS_EPS) * nw * (z * _sigmoid(z))
        parts.append(o.astype(BF16))
    ob = jnp.concatenate(parts, axis=1)
    y = _dot(ob, w_ref[...])
    out_ref[...] = (a_ref[...] + _sigmoid(gp_ref[...] + bg_ref[...]) * y).astype(out_ref.dtype)


def _branch_b(o_f, o_b, u_b2d, norm_w, w_b, gpre, b_gate, a_part, tm):
    t = a_part.shape[0]
    return pl.pallas_call(
        _branch_b_kernel,
        grid=(t // tm,),
        in_specs=[pl.BlockSpec((tm, B_VZ), lambda i: (i, 0)),
                  pl.BlockSpec((tm, B_VZ), lambda i: (i, 0)),
                  pl.BlockSpec((tm, B_VZ), lambda i: (i, (2 * B_QK + B_VZ) // B_VZ)),
                  pl.BlockSpec((1, B_DV), lambda i: (0, 0)),
                  pl.BlockSpec((B_VZ, D_MODEL), lambda i: (0, 0)),
                  pl.BlockSpec((tm, D_MODEL), lambda i: (i, 1)),
                  pl.BlockSpec((1, D_MODEL), lambda i: (0, 1)),
                  pl.BlockSpec((tm, D_MODEL), lambda i: (i, 0))],
        out_specs=pl.BlockSpec((tm, D_MODEL), lambda i: (i, 0)),
        out_shape=jax.ShapeDtypeStruct((t, D_MODEL), BF16),
        compiler_params=_cparams(("parallel",)),
        name="branch_b",
    )(o_f, o_b, u_b2d, norm_w, w_b, gpre, b_gate, a_part)


def _layer_norm(y, g, b):
    mu = jnp.mean(y, axis=1, keepdims=True)
    yc = y - mu
    var = jnp.mean(yc * yc, axis=1, keepdims=True)
    return yc * lax.rsqrt(var + LN_EPS) * g + b


def _out_ln_kernel(m_ref, w_ref, x_ref, g_ref, b_ref, out_ref):
    mix = _dot(m_ref[...], w_ref[...])
    out_ref[...] = _layer_norm(DEEPNORM_ALPHA * x_ref[...] + mix, g_ref[...], b_ref[...])


def _out_ln(merged, w_out, x2d, ln_g, ln_b, tm):
    t = merged.shape[0]
    vec = pl.BlockSpec((1, D_MODEL), lambda i: (0, 0))
    return pl.pallas_call(
        _out_ln_kernel,
        grid=(t // tm,),
        in_specs=[pl.BlockSpec((tm, D_MODEL), lambda i: (i, 0)),
                  pl.BlockSpec((D_MODEL, D_MODEL), lambda i: (0, 0)),
                  pl.BlockSpec((tm, D_MODEL), lambda i: (i, 0)), vec, vec],
        out_specs=pl.BlockSpec((tm, D_MODEL), lambda i: (i, 0)),
        out_shape=jax.ShapeDtypeStruct((t, D_MODEL), F32),
        compiler_params=_cparams(("parallel",)),
        name="out_proj_ln1",
    )(merged, w_out, x2d, ln_g, ln_b)


def _router_kernel(x_ref, rw_ref, rb_ref, idx_ref, gate_ref, rank_ref, cnt_ref, carry, *, tm):
    @pl.when(pl.program_id(0) == 0)
    def _():
        carry[...] = jnp.zeros_like(carry)

    lane = lax.broadcasted_iota(jnp.int32, (tm, LANES), 1)
    lane_f = lane.astype(F32)
    logits = jnp.dot(x_ref[...], rw_ref[...], precision=lax.Precision.HIGHEST, preferred_element_type=F32) + rb_ref[...]
    cur = jnp.where(lane < N_EXPERTS, logits, -jnp.inf)
    vals, idxs = [], []
    for _k in range(TOP_K):
        m = jnp.max(cur, axis=1, keepdims=True)
        idx = jnp.min(jnp.where(cur == m, lane_f, float(LANES)), axis=1, keepdims=True).astype(jnp.int32)
        vals.append(m)
        idxs.append(idx)
        cur = jnp.where(lane == idx, -jnp.inf, cur)
    es = [jnp.exp(v - vals[0]) for v in vals]
    den = es[0] + es[1] + es[2] + es[3]
    onehot = jnp.zeros((tm, LANES), F32)
    for idx in idxs:
        onehot = onehot + (lane == idx).astype(F32)
    ri = lax.broadcasted_iota(jnp.int32, (tm, tm), 0)
    ci = lax.broadcasted_iota(jnp.int32, (tm, tm), 1)
    before = (ci < ri).astype(BF16)
    prefix = _dot(before, onehot.astype(BF16)) + carry[0:1, :]
    idx_out = jnp.zeros((tm, LANES), jnp.int32)
    gate_out = jnp.zeros((tm, LANES), F32)
    rank_out = jnp.zeros((tm, LANES), jnp.int32)
    for k in range(TOP_K):
        rk = jnp.sum(jnp.where(lane == idxs[k], prefix, 0.0), axis=1, keepdims=True)
        idx_out = jnp.where(lane == k, idxs[k], idx_out)
        gate_out = jnp.where(lane == k, es[k] / den, gate_out)
        rank_out = jnp.where(lane == k, rk.astype(jnp.int32), rank_out)
    idx_ref[...] = idx_out
    gate_ref[...] = gate_out
    rank_ref[...] = rank_out
    total = carry[0:1, :] + jnp.sum(onehot, axis=0, keepdims=True)
    carry[...] = jnp.broadcast_to(total, carry.shape)
    cnt_ref[...] = jnp.broadcast_to(total, cnt_ref.shape)


def _router(x1, rw, rb, tm):
    t = x1.shape[0]
    row = pl.BlockSpec((tm, LANES), lambda i: (i, 0))
    kern = functools.partial(_router_kernel, tm=tm)
    return pl.pallas_call(
        kern,
        grid=(t // tm,),
        in_specs=[pl.BlockSpec((tm, D_MODEL), lambda i: (i, 0)),
                  pl.BlockSpec((D_MODEL, LANES), lambda i: (0, 0)),
                  pl.BlockSpec((1, LANES), lambda i: (0, 0))],
        out_specs=[row, row, row, pl.BlockSpec((8, LANES), lambda i: (0, 0))],
        out_shape=[jax.ShapeDtypeStruct((t, LANES), jnp.int32),
                   jax.ShapeDtypeStruct((t, LANES), F32),
                   jax.ShapeDtypeStruct((t, LANES), jnp.int32),
                   jax.ShapeDtypeStruct((8, LANES), F32)],
        scratch_shapes=[pltpu.VMEM((8, LANES), F32)],
        compiler_params=_cparams(("arbitrary",)),
        name="router",
    )(x1, rw, rb)


def _pack_rows(src_ref, dst_ref):
    half = D_MODEL // 2
    n = src_ref.shape[0]
    for s in range(ROW_SUBLANES):
        lo = pltpu.bitcast(src_ref[:, s * LANES:(s + 1) * LANES].astype(BF16).astype(F32), jnp.uint32)
        hi = pltpu.bitcast(src_ref[:, half + s * LANES:half + (s + 1) * LANES].astype(BF16).astype(F32), jnp.uint32)
        dst_ref[pl.ds(s, n, stride=ROW_SUBLANES), :] = (lo >> 16) | hi


def _unpack_words(w):
    return pltpu.bitcast(w << 16, F32), pltpu.bitcast(w & jnp.uint32(0xFFFF0000), F32)


def _tile(ref, i):
    start = i * ROW_SUBLANES
    if not isinstance(i, int):
        start = pl.multiple_of(start, ROW_SUBLANES)
    return ref.at[pl.ds(start, ROW_SUBLANES)]


def _tile_copy(src, i, dst, j, sem):
    return pltpu.make_async_copy(_tile(src, i), _tile(dst, j), sem)


def _wait_tiles(hbm, n, sem):
    pltpu.make_async_copy(hbm.at[pl.ds(0, n * ROW_SUBLANES)], hbm.at[pl.ds(0, n * ROW_SUBLANES)], sem).wait()


def _dispatch_kernel(pad_start_ref, pad_cnt_ref, dest_ref, x_ref, xs_hbm, stage, zero, sem, *, tm, nt):
    i = pl.program_id(0)

    @pl.when(i < nt)
    def _():
        _pack_rows(x_ref, stage)

        def issue(t_, carry):
            src = _tile(stage, t_)
            for k in range(TOP_K):
                pltpu.make_async_copy(src, _tile(xs_hbm, dest_ref[0, 0, t_ * TOP_K + k]), sem).start()
            return carry

        lax.fori_loop(0, tm, issue, 0, unroll=2)
        _wait_tiles(xs_hbm, tm * TOP_K, sem)

    @pl.when((i >= nt) & (i < nt + N_EXPERTS))
    def _():
        e = i - nt
        zero[...] = jnp.zeros_like(zero)
        start = pad_start_ref[e]
        cnt = pad_cnt_ref[e]

        def issue(r, carry):
            _tile_copy(zero, 0, xs_hbm, start + r, sem).start()
            return carry

        lax.fori_loop(0, cnt, issue, 0)

        def wait(r, carry):
            _tile_copy(zero, 0, xs_hbm, start, sem).wait()
            return carry

        lax.fori_loop(0, cnt, wait, 0)

    @pl.when(i == nt + N_EXPERTS)
    def _():
        stage[...] = jnp.zeros_like(stage)
        start = pad_start_ref[N_EXPERTS]

        def fill(j, carry):
            piece = xs_hbm.at[pl.ds(pl.multiple_of((start + j * tm) * ROW_SUBLANES, ROW_SUBLANES), tm * ROW_SUBLANES)]
            cp = pltpu.make_async_copy(stage, piece, sem)
            cp.start()
            cp.wait()
            return carry

        lax.fori_loop(0, pad_cnt_ref[N_EXPERTS] // tm, fill, 0)


def _dispatch(x1, dest, pad_from, pad_cnt, n_pad, tm):
    t = x1.shape[0]
    nt = t // tm
    assert MOE_BLOCK % tm == 0 and pad_from.shape[0] == N_EXPERTS + 1
    kern = functools.partial(_dispatch_kernel, tm=tm, nt=nt)
    grid_spec = pltpu.PrefetchScalarGridSpec(
        num_scalar_prefetch=2,
        grid=(nt + pad_from.shape[0],),
        in_specs=[pl.BlockSpec((1, 1, tm * TOP_K), lambda i, ps, pc: (jnp.minimum(i, nt - 1), 0, 0),
                               memory_space=pltpu.SMEM),
                  pl.BlockSpec((tm, D_MODEL), lambda i, ps, pc: (jnp.minimum(i, nt - 1), 0))],
        out_specs=pl.BlockSpec(memory_space=pl.ANY),
        scratch_shapes=[pltpu.VMEM((tm * ROW_SUBLANES, LANES), jnp.uint32),
                        pltpu.VMEM((ROW_SUBLANES, LANES), jnp.uint32),
                        pltpu.SemaphoreType.DMA(())],
    )
    return pl.pallas_call(
        kern,
        grid_spec=grid_spec,
        out_shape=jax.ShapeDtypeStruct((n_pad * ROW_SUBLANES, LANES), jnp.uint32),
        compiler_params=_cparams(("arbitrary",)),
        name="moe_dispatch",
    )(pad_from, pad_cnt, dest.reshape(nt, 1, tm * TOP_K), x1)


GU_GROUP = 2 * LANES


def _regroup_kernel(w_ref, o_ref):
    ri = lax.broadcasted_iota(jnp.int32, (GU_GROUP, GU_GROUP), 0)
    ci = lax.broadcasted_iota(jnp.int32, (GU_GROUP, GU_GROUP), 1)
    src = jnp.where(ci < LANES, 2 * ci, 2 * (ci - LANES) + 1)
    perm = jnp.where(ri == src, 1.0, 0.0).astype(BF16)
    for c in range(w_ref.shape[2] // GU_GROUP):
        cs = slice(c * GU_GROUP, (c + 1) * GU_GROUP)
        o_ref[0, :, cs] = _dot(w_ref[0, :, cs].astype(BF16), perm).astype(BF16)


def _regroup_gate_up(w_gate_up):
    e, d, n2 = w_gate_up.shape
    rt = 512
    return pl.pallas_call(
        _regroup_kernel,
        grid=(e, d // rt),
        in_specs=[pl.BlockSpec((1, rt, n2), lambda i, j: (i, j, 0))],
        out_specs=pl.BlockSpec((1, rt, n2), lambda i, j: (i, j, 0)),
        out_shape=jax.ShapeDtypeStruct((e, d, n2), BF16),
        compiler_params=_cparams(("parallel", "parallel")),
        name="regroup_gate_up",
    )(w_gate_up)

def _ffn_kernel(be_ref, nused_ref, rows_ref, x_ref, wgu_ref, bgu_ref, wd_ref, bd_ref, o_ref, acc, xb):
    i = pl.program_id(0)
    f = pl.program_id(1)
    nf = pl.num_programs(1)
    half = D_MODEL // 2

    @pl.when(f == 0)
    def _():
        acc[...] = jnp.broadcast_to(bd_ref[0], acc.shape)
        for s in range(ROW_SUBLANES):
            lo, hi = _unpack_words(x_ref[pl.ds(s, MOE_BLOCK, stride=ROW_SUBLANES), :])
            xb[:, s * LANES:(s + 1) * LANES] = lo.astype(BF16)
            xb[:, half + s * LANES:half + (s + 1) * LANES] = hi.astype(BF16)

    def compute(nrows):
        gu = _dot(xb[0:nrows, :], wgu_ref[0]) + bgu_ref[0]
        acts = []
        for m in range(MOE_TF // LANES):
            gate = jnp.minimum(gu[:, m * GU_GROUP:m * GU_GROUP + LANES], SWIGLU_LIMIT)
            up = jnp.clip(gu[:, m * GU_GROUP + LANES:(m + 1) * GU_GROUP], -SWIGLU_LIMIT, SWIGLU_LIMIT)
            acts.append(((up + 1.0) * gate * _sigmoid(gate * SWIGLU_ALPHA)).astype(BF16))
        acc[0:nrows, :] += _dot(jnp.concatenate(acts, axis=1), wd_ref[0].astype(BF16))

    rows = rows_ref[i]
    pl.when(rows > MOE_BLOCK // 2)(lambda: compute(MOE_BLOCK))
    pl.when((rows > 0) & (rows <= MOE_BLOCK // 2))(lambda: compute(MOE_BLOCK // 2))

    @pl.when(f == nf - 1)
    def _():
        _pack_rows(acc, o_ref)


def _expert_ffn(xs, block_expert, n_used, block_rows, wgu, bgu, wd, bd):
    n_pad = xs.shape[0] // ROW_SUBLANES
    nb = n_pad // MOE_BLOCK
    d_ff = wgu.shape[2] // 2
    nf = d_ff // MOE_TF
    row_blk = (MOE_BLOCK * ROW_SUBLANES, LANES)
    grid_spec = pltpu.PrefetchScalarGridSpec(
        num_scalar_prefetch=3,
        grid=(nb, nf),
        in_specs=[pl.BlockSpec(row_blk, lambda i, f, be, nu, rw: (jnp.minimum(i, nu[0] - 1), 0)),
                  pl.BlockSpec((1, D_MODEL, 2 * MOE_TF), lambda i, f, be, nu, rw: (be[i], 0, f)),
                  pl.BlockSpec((1, 1, 2 * MOE_TF), lambda i, f, be, nu, rw: (be[i], 0, f)),
                  pl.BlockSpec((1, MOE_TF, D_MODEL), lambda i, f, be, nu, rw: (be[i], f, 0)),
                  pl.BlockSpec((1, 1, D_MODEL), lambda i, f, be, nu, rw: (be[i], 0, 0))],
        out_specs=pl.BlockSpec(row_blk, lambda i, f, be, nu, rw: (i, 0)),
        scratch_shapes=[pltpu.VMEM((MOE_BLOCK, D_MODEL), F32), pltpu.VMEM((MOE_BLOCK, D_MODEL), BF16)],
    )
    return pl.pallas_call(
        _ffn_kernel,
        grid_spec=grid_spec,
        out_shape=jax.ShapeDtypeStruct((n_pad * ROW_SUBLANES, LANES), jnp.uint32),
        compiler_params=_cparams(("arbitrary", "arbitrary")),
        name="expert_ffn",
    )(block_expert, n_used, block_rows, xs, wgu, bgu, wd, bd)


def _final_kernel(dest_ref, gate_ref, x_ref, p_ref, wpg_ref, bpg_ref, wple_ref, g_ref, b_ref, ys_hbm, out_ref,
                  rows, sem, *, tm):
    def issue(t_, carry):
        for k in range(TOP_K):
            _tile_copy(ys_hbm, dest_ref[0, 0, t_ * TOP_K + k], rows, k * tm + t_, sem).start()
        return carry

    lax.fori_loop(0, tm, issue, 0, unroll=2)
    x = x_ref[...]
    pg = _dot(x.astype(BF16), wpg_ref[...]) + bpg_ref[...]
    ple = _sigmoid(pg) * _dot(p_ref[...].astype(BF16), wple_ref[...])
    _wait_tiles(ys_hbm, tm * TOP_K, sem)
    gates = gate_ref[...]
    lo_parts, hi_parts = [], []
    for s in range(ROW_SUBLANES):
        acc_lo = acc_hi = None
        for k in range(TOP_K):
            lo, hi = _unpack_words(rows[pl.ds(k * tm * ROW_SUBLANES + s, tm, stride=ROW_SUBLANES), :])
            g = gates[:, k:k + 1]
            acc_lo = g * lo if acc_lo is None else acc_lo + g * lo
            acc_hi = g * hi if acc_hi is None else acc_hi + g * hi
        lo_parts.append(acc_lo)
        hi_parts.append(acc_hi)
    y = jnp.concatenate(lo_parts + hi_parts, axis=1)
    out_ref[...] = _layer_norm(DEEPNORM_ALPHA * x + y + ple, g_ref[...], b_ref[...])


def _final(dest, gates, x1, p2d, w_pg, b_pg, w_ple, ln_g, ln_b, ys, tm):
    t = x1.shape[0]
    vec = pl.BlockSpec((1, D_MODEL), lambda i: (0, 0))
    kern = functools.partial(_final_kernel, tm=tm)
    return pl.pallas_call(
        kern,
        grid=(t // tm,),
        in_specs=[pl.BlockSpec((1, 1, tm * TOP_K), lambda i: (i, 0, 0), memory_space=pltpu.SMEM),
                  pl.BlockSpec((tm, LANES), lambda i: (i, 0)),
                  pl.BlockSpec((tm, D_MODEL), lambda i: (i, 0)),
                  pl.BlockSpec((tm, PLE_DIM), lambda i: (i, 0)),
                  pl.BlockSpec((D_MODEL, D_MODEL), lambda i: (0, 0)), vec,
                  pl.BlockSpec((PLE_DIM, D_MODEL), lambda i: (0, 0)), vec, vec,
                  pl.BlockSpec(memory_space=pl.ANY)],
        out_specs=pl.BlockSpec((tm, D_MODEL), lambda i: (i, 0)),
        out_shape=jax.ShapeDtypeStruct((t, D_MODEL), F32),
        scratch_shapes=[pltpu.VMEM((tm * TOP_K * ROW_SUBLANES, LANES), jnp.uint32), pltpu.SemaphoreType.DMA(())],
        compiler_params=_cparams(("arbitrary",)),
        name="combine_ple_ln2",
    )(dest.reshape(t // tm, 1, tm * TOP_K), gates, x1, p2d, w_pg, b_pg, w_ple, ln_g, ln_b, ys)


def _row_tile(t):
    return min(512, t)


def _layer(x, p, w_in, b_gate, conv_w, a_log, dt_bias, dn_norm_w, w_branch_a, w_branch_b, w_out, ln1_g, ln1_b,
           router_w, router_b, w_gate_up, b_gate_up, w_down, b_down, w_ple, w_ple_gate, b_ple_gate, ln2_g, ln2_b):
    bsz, seq, _ = x.shape
    t = bsz * seq
    tm = _row_tile(t)
    x2d = x.reshape(t, D_MODEL)
    xb = x2d.astype(BF16)

    c_a = 3 * A_QKV
    c_b = 2 * B_QK + 2 * B_VZ
    w_bf = w_in.astype(BF16)
    tp = min(IN_PROJ_TM, seq)
    u_b = _matmul(xb, w_bf[:, c_a:c_a + c_b], BF16, tp, IN_PROJ_TN)
    w_bab = jnp.pad(w_bf[:, c_a + c_b:c_a + c_b + B_GATES], ((0, 0), (0, LANES - B_GATES)))
    bab = _matmul(xb, w_bab, F32, tp, LANES)
    gpre = _matmul(xb, w_bf[:, c_a + c_b + B_GATES:], F32, tp, IN_PROJ_TN)

    slopes = _alibi_slopes()
    gw = A_HEADS * HEAD_DIM
    outs, lses = [], []
    for gi, (_win, dil) in enumerate(DILATION_GROUPS):
        w_g = jnp.concatenate([w_bf[:, part * A_QKV + gi * gw:part * A_QKV + (gi + 1) * gw] for part in range(3)], axis=1)
        o_g, l_g = _attention_group(_matmul_classes(xb, w_g, bsz, dil, tp, IN_PROJ_TN), gi, dil, slopes[gi])
        outs.append(o_g)
        lses.append(l_g)

    cw = jnp.pad(conv_w.astype(F32), ((0, 8 - CONV_W), (0, 0)))
    lane_is_g = (np.arange(LANES) % (2 * B_V_HEADS) >= B_V_HEADS) & (np.arange(LANES) < B_GATES)
    neg_a = jnp.zeros((LANES,), F32).at[B_V_HEADS:2 * B_V_HEADS].set(-jnp.exp(a_log[0].astype(F32)))
    neg_a = neg_a.at[3 * B_V_HEADS:4 * B_V_HEADS].set(-jnp.exp(a_log[1].astype(F32)))
    dtb = jnp.zeros((LANES,), F32).at[B_V_HEADS:2 * B_V_HEADS].set(dt_bias[0].astype(F32))
    dtb = dtb.at[3 * B_V_HEADS:4 * B_V_HEADS].set(dt_bias[1].astype(F32))
    gate_params = jnp.zeros((8, LANES), F32).at[0].set(neg_a).at[1].set(dtb).at[2].set(jnp.asarray(lane_is_g, F32))
    qn, kn, vn, gts = _dn_prep(u_b.reshape(bsz, seq, c_b), cw, bab.reshape(bsz, seq, LANES), gate_params)
    gts_t = gts.reshape(bsz, seq // DN_CHUNK, DN_CHUNK, LANES).transpose(0, 1, 3, 2)
    o_f, o_b = _dn_state(*_dn_intra(qn, kn, vn, gts, gts_t), seq)

    bg = b_gate.astype(F32).reshape(1, 2 * D_MODEL)
    a_part = _branch_a(outs, lses, w_branch_a.astype(BF16), gpre, bg, tm)
    merged = _branch_b(o_f.reshape(t, B_VZ), o_b.reshape(t, B_VZ), u_b, dn_norm_w.astype(F32).reshape(1, B_DV), w_branch_b.astype(BF16), gpre, bg,
                       a_part, tm)
    x1 = _out_ln(merged, w_out.astype(BF16), x2d, ln1_g.reshape(1, -1), ln1_b.reshape(1, -1), tm)

    rw = jnp.pad(router_w.astype(F32), ((0, 0), (0, LANES - N_EXPERTS)))
    rb = jnp.pad(router_b.astype(F32), (0, LANES - N_EXPERTS)).reshape(1, LANES)
    idx, gates, rank, cnt = _router(x1, rw, rb, tm)
    counts = cnt[0, :N_EXPERTS].astype(jnp.int32)
    padded = (counts + MOE_BLOCK - 1) // MOE_BLOCK * MOE_BLOCK
    pad_end = jnp.cumsum(padded)
    pad_start = pad_end - padded
    dest = pad_start[idx[:, :TOP_K]] + rank[:, :TOP_K]
    n_pad = t * TOP_K + N_EXPERTS * MOE_BLOCK
    nb = n_pad // MOE_BLOCK
    block_start = jnp.arange(nb, dtype=jnp.int32) * MOE_BLOCK
    block_expert = jnp.minimum(jnp.sum(pad_end[None, :] <= block_start[:, None], axis=1), N_EXPERTS - 1).astype(jnp.int32)
    n_used = (pad_end[-1:] // MOE_BLOCK).astype(jnp.int32)

    zero_from = jnp.concatenate([pad_start + counts, pad_end[-1:]]).astype(jnp.int32)
    zero_cnt = jnp.concatenate([padded - counts, n_pad - pad_end[-1:]]).astype(jnp.int32)
    xs = _dispatch(x1, dest, zero_from, zero_cnt, n_pad, tm)
    d_ff = w_down.shape[1]
    wgu = _regroup_gate_up(w_gate_up.astype(F32))
    bgu = b_gate_up.astype(F32).reshape(N_EXPERTS, d_ff // LANES, LANES, 2).transpose(0, 1, 3, 2)
    bgu = bgu.reshape(N_EXPERTS, 1, 2 * d_ff)
    seg_end = (pad_start + counts)[block_expert]
    block_rows = jnp.where(block_start < pad_end[-1], jnp.clip(seg_end - block_start, 0, MOE_BLOCK), 0).astype(jnp.int32)
    ys = _expert_ffn(xs, block_expert, n_used, block_rows, wgu, bgu, w_down.astype(F32),
                     b_down.astype(F32).reshape(N_EXPERTS, 1, D_MODEL))

    out = _final(dest, gates, x1, p.reshape(t, PLE_DIM), w_ple_gate.astype(BF16),
                 b_ple_gate.astype(F32).reshape(1, -1), w_ple.astype(BF16), ln2_g.reshape(1, -1),
                 ln2_b.reshape(1, -1), ys, min(COMBINE_TM, t))
    return out.reshape(bsz, seq, D_MODEL)


def kernel(x, p, w_in, b_gate, conv_w, a_log, dt_bias, dn_norm_w, w_branch_a, w_branch_b, w_out, ln1_g, ln1_b,
           router_w, router_b, w_gate_up, b_gate_up, w_down, b_down, w_ple, w_ple_gate, b_ple_gate, ln2_g, ln2_b):
    assert w_in.shape[0] == DEPTH
    return _layer(x, p[0], w_in[0], b_gate[0], conv_w[0], a_log[0], dt_bias[0], dn_norm_w[0], w_branch_a[0],
                  w_branch_b[0], w_out[0], ln1_g[0], ln1_b[0], router_w[0], router_b[0], w_gate_up[0],
                  b_gate_up[0], w_down[0], b_down[0], w_ple[0], w_ple_gate[0], b_ple_gate[0], ln2_g[0], ln2_b[0])
```

```python
import functools

import numpy as np
import jax
import jax.numpy as jnp
from jax import lax
from jax.experimental import pallas as pl
from jax.experimental.pallas import tpu as pltpu

F32 = jnp.float32
BF16 = jnp.bfloat16

D_MODEL = 2048
HEAD_DIM = 128
A_HEADS = 8
DILATION_GROUPS = ((128, 1), (512, 4), (2048, 16))
N_GROUPS = 3
NEG_INF = -1e30
B_QK_HEADS = 8
B_V_HEADS = 16
B_DK = 128
B_DV = 128
CONV_W = 5
RMS_EPS = 1e-6
N_EXPERTS = 32
TOP_K = 4
SWIGLU_ALPHA = 1.702
SWIGLU_LIMIT = 7.0
PLE_DIM = 256
DEPTH = 1
DEEPNORM_ALPHA = (2 * DEPTH) ** 0.25
LN_EPS = 1e-5
A_QKV = N_GROUPS * A_HEADS * HEAD_DIM
B_QK = B_QK_HEADS * B_DK
B_VZ = B_V_HEADS * B_DV
B_GATES = 4 * B_V_HEADS

LANES = 128
N_SIDE = 64
Q_SUB = 128
DN_CHUNK = 64
DN_PAIR = B_V_HEADS // B_QK_HEADS
DN_INTRA_CHUNKS = 2
DN_STATE_CHUNKS = 2
TRI_BASE = 16
ROW_SUBLANES = 8
IN_PROJ_TM = 1024
IN_PROJ_TN = 1024
MOE_BLOCK = 1024
MOE_TF = 512
COMBINE_TM = 256
VMEM_LIMIT = 56 * 1024 * 1024
assert D_MODEL == 2 * ROW_SUBLANES * LANES


def _cparams(sem):
    return pltpu.CompilerParams(dimension_semantics=sem, vmem_limit_bytes=VMEM_LIMIT)


def _sigmoid(x):
    return 1.0 / (1.0 + jnp.exp(-x))


def _dot(a, b):
    return jnp.dot(a, b, preferred_element_type=F32)


def _dot_nt(a, b):
    return lax.dot_general(a, b, (((1,), (1,)), ((), ())), preferred_element_type=F32)


def _dot_tn(a, b):
    return lax.dot_general(a, b, (((0,), (0,)), ((), ())), preferred_element_type=F32)


def _mm_kernel(x_ref, w_ref, o_ref):
    o_ref[...] = _dot(x_ref[...], w_ref[...]).astype(o_ref.dtype)


def _matmul(x, w, out_dtype, tm, tn):
    m, k = x.shape
    n = w.shape[1]
    return pl.pallas_call(
        _mm_kernel,
        grid=(m // tm, n // tn),
        in_specs=[pl.BlockSpec((tm, k), lambda i, j: (i, 0)),
                  pl.BlockSpec((k, tn), lambda i, j: (0, j))],
        out_specs=pl.BlockSpec((tm, tn), lambda i, j: (i, j)),
        out_shape=jax.ShapeDtypeStruct((m, n), out_dtype),
        compiler_params=_cparams(("parallel", "parallel")),
        name="in_proj",
    )(x, w)


def _mm_classes_kernel(x_ref, w_ref, o_ref, acc_ref, *, dil, rows):
    y = _dot(x_ref[...], w_ref[...])
    for j in range(acc_ref.shape[0]):
        ls = slice(j * LANES, (j + 1) * LANES)
        acc_ref[j] = y[:, ls]
        for c in range(dil):
            o_ref[0, c, :, ls] = acc_ref[j, pl.ds(c, rows, stride=dil), :].astype(o_ref.dtype)


def _matmul_classes(x, w, bsz, dil, tm, tn):
    m, k = x.shape
    n = w.shape[1]
    seq = m // bsz
    tiles_per_seq = seq // tm
    rows = tm // dil
    kern = functools.partial(_mm_classes_kernel, dil=dil, rows=rows)
    return pl.pallas_call(
        kern,
        grid=(m // tm, n // tn),
        in_specs=[pl.BlockSpec((tm, k), lambda i, j: (i, 0)),
                  pl.BlockSpec((k, tn), lambda i, j: (0, j))],
        out_specs=pl.BlockSpec((1, dil, rows, tn), lambda i, j: (i // tiles_per_seq, 0, i % tiles_per_seq, j)),
        out_shape=jax.ShapeDtypeStruct((bsz, dil, seq // dil, n), BF16),
        scratch_shapes=[pltpu.VMEM((tn // LANES, tm, LANES), F32)],
        compiler_params=_cparams(("parallel", "parallel")),
        name=f"in_proj_dil{dil}",
    )(x, w)


def _attn_kernel(q_ref, kp_ref, kc_ref, kn_ref, vp_ref, vc_ref, vn_ref, o_ref, lse_ref, kbuf, vbuf,
                 *, dil, sub_len, tl, slopes):
    i0 = pl.program_id(2) * tl
    kbuf[0:N_SIDE, :] = kp_ref[...]
    kbuf[N_SIDE:N_SIDE + tl, :] = kc_ref[...]
    kbuf[N_SIDE + tl:, :] = kn_ref[...]
    vbuf[0:N_SIDE, :] = vp_ref[...]
    vbuf[N_SIDE:N_SIDE + tl, :] = vc_ref[...]
    vbuf[N_SIDE + tl:, :] = vn_ref[...]
    span = Q_SUB + 2 * N_SIDE
    qq = lax.broadcasted_iota(jnp.int32, (Q_SUB, span), 0)
    kk = lax.broadcasted_iota(jnp.int32, (Q_SUB, span), 1)
    delta = kk - N_SIDE - qq
    absd = jnp.abs(delta)
    band = absd <= N_SIDE
    dist = (dil * absd).astype(F32)
    lane = lax.broadcasted_iota(jnp.int32, (Q_SUB, LANES), 1)
    scale = HEAD_DIM ** -0.5
    for j in range(tl // Q_SUB):
        pos = i0 + (j * Q_SUB - N_SIDE) + kk
        valid = band & (pos >= 0) & (pos < sub_len)
        lse_tile = jnp.zeros((Q_SUB, LANES), F32)
        for h in range(A_HEADS):
            hs = slice(h * HEAD_DIM, (h + 1) * HEAD_DIM)
            q = q_ref[j * Q_SUB:(j + 1) * Q_SUB, hs]
            k = kbuf[j * Q_SUB:j * Q_SUB + span, hs]
            v = vbuf[j * Q_SUB:j * Q_SUB + span, hs]
            s = _dot_nt(q, k) * scale
            s = jnp.where(valid, s - float(slopes[h]) * dist, NEG_INF)
            m = jnp.max(s, axis=1, keepdims=True)
            p = jnp.exp(s - m)
            l = jnp.sum(p, axis=1, keepdims=True)
            o = _dot(p.astype(BF16), v) / l
            o_ref[j * Q_SUB:(j + 1) * Q_SUB, hs] = o.astype(o_ref.dtype)
            lse_tile = jnp.where(lane == h, m + jnp.log(l), lse_tile)
        lse_ref[j * Q_SUB:(j + 1) * Q_SUB, :] = lse_tile


def _attention_group(qkv, gi, dil, slopes):
    bsz, _, sub_len, _ = qkv.shape
    tl = min(512, sub_len)
    assert sub_len % tl == 0 and tl % Q_SUB == 0 and sub_len % N_SIDE == 0
    width = A_HEADS * HEAD_DIM
    halo_per_tile = tl // N_SIDE
    n_halo = sub_len // N_SIDE
    prev = lambda li: jnp.maximum(li * halo_per_tile - 1, 0)
    nxt = lambda li: jnp.minimum((li + 1) * halo_per_tile, n_halo - 1)
    halo = lambda col, rowf: pl.BlockSpec((None, None, N_SIDE, width), lambda b, c, li: (b, c, rowf(li), col))
    cur = lambda col: pl.BlockSpec((None, None, tl, width), lambda b, c, li: (b, c, li, col))
    kern = functools.partial(_attn_kernel, dil=dil, sub_len=sub_len, tl=tl, slopes=tuple(float(s) for s in slopes))
    return pl.pallas_call(
        kern,
        grid=(bsz, dil, sub_len // tl),
        in_specs=[cur(0), halo(1, prev), cur(1), halo(1, nxt), halo(2, prev), cur(2), halo(2, nxt)],
        out_specs=[pl.BlockSpec((None, None, tl, width), lambda b, c, li: (b, c, li, 0)),
                   pl.BlockSpec((None, None, tl, LANES), lambda b, c, li: (b, c, li, 0))],
        out_shape=[jax.ShapeDtypeStruct((bsz, dil, sub_len, width), BF16),
                   jax.ShapeDtypeStruct((bsz, dil, sub_len, LANES), F32)],
        scratch_shapes=[pltpu.VMEM((tl + 2 * N_SIDE, width), BF16),
                        pltpu.VMEM((tl + 2 * N_SIDE, width), BF16)],
        compiler_params=_cparams(("parallel", "parallel", "parallel")),
        name=f"dilated_attn_g{gi}",
    )(qkv, qkv, qkv, qkv, qkv, qkv, qkv)


def _alibi_slopes():
    n = N_GROUPS * A_HEADS
    s = 2.0 ** (-8.0 * np.arange(1, n + 1) / n)
    return s.astype(np.float32).reshape(N_GROUPS, A_HEADS)


def _dn_prep_kernel(prev_ref, cur_ref, next_ref, cw_ref, bab_ref, gp_ref, q_ref, k_ref, v_ref, g_ref, *, ts):
    ti = pl.program_id(1)
    nt = pl.num_programs(1)
    halo = CONV_W // 2
    keep_prev = (ti > 0).astype(F32)
    keep_next = (ti < nt - 1).astype(F32)
    for c in range((2 * B_QK + B_VZ) // LANES):
        cs = slice(c * LANES, (c + 1) * LANES)
        xp = prev_ref[0, :, cs].astype(F32)[8:16] * keep_prev
        xc = cur_ref[0, :, cs].astype(F32)
        xn = next_ref[0, :, cs].astype(F32)[0:8] * keep_next
        ext = jnp.concatenate([xp, xc, xn], axis=0)
        acc = jnp.zeros((ts, LANES), F32)
        for j in range(CONV_W):
            off = 8 - halo + j
            acc = acc + ext[off:off + ts, :] * cw_ref[j:j + 1, cs]
        y = acc * _sigmoid(acc)
        if c < 2 * B_QK // LANES:
            y = y * lax.rsqrt(jnp.sum(y * y, axis=1, keepdims=True) + 1e-6)
        if c < B_QK // LANES:
            q_ref[0, :, cs] = y * (B_DK ** -0.5)
        elif c < 2 * B_QK // LANES:
            k_ref[0, :, c * LANES - B_QK:(c + 1) * LANES - B_QK] = y
        else:
            v_ref[0, :, c * LANES - 2 * B_QK:(c + 1) * LANES - 2 * B_QK] = y
    x = bab_ref[0]
    neg_a = gp_ref[0:1, :]
    dtb = gp_ref[1:2, :]
    is_g = gp_ref[2:3, :] > 0.5
    z = x + dtb
    softplus = jnp.maximum(z, 0.0) + jnp.log(1.0 + jnp.exp(-jnp.abs(z)))
    g_ref[0] = jnp.where(is_g, neg_a * softplus, _sigmoid(x))


def _dn_prep(u_b, conv_w, bab, gate_params):
    bsz, seq, _ = u_b.shape
    ts = min(256, seq)
    cq = 2 * B_QK + B_VZ
    nhalo = seq // 16
    per = ts // 16
    kern = functools.partial(_dn_prep_kernel, ts=ts)
    return pl.pallas_call(
        kern,
        grid=(bsz, seq // ts),
        in_specs=[pl.BlockSpec((1, 16, cq), lambda b, t: (b, jnp.maximum(t * per - 1, 0), 0)),
                  pl.BlockSpec((1, ts, cq), lambda b, t: (b, t, 0)),
                  pl.BlockSpec((1, 16, cq), lambda b, t: (b, jnp.minimum((t + 1) * per, nhalo - 1), 0)),
                  pl.BlockSpec((8, cq), lambda b, t: (0, 0)),
                  pl.BlockSpec((1, ts, LANES), lambda b, t: (b, t, 0)),
                  pl.BlockSpec((8, LANES), lambda b, t: (0, 0))],
        out_specs=[pl.BlockSpec((1, ts, B_QK), lambda b, t: (b, t, 0)),
                   pl.BlockSpec((1, ts, B_QK), lambda b, t: (b, t, 0)),
                   pl.BlockSpec((1, ts, B_VZ), lambda b, t: (b, t, 0)),
                   pl.BlockSpec((1, ts, LANES), lambda b, t: (b, t, 0))],
        out_shape=[jax.ShapeDtypeStruct((bsz, seq, B_QK), F32),
                   jax.ShapeDtypeStruct((bsz, seq, B_QK), F32),
                   jax.ShapeDtypeStruct((bsz, seq, B_VZ), F32),
                   jax.ShapeDtypeStruct((bsz, seq, LANES), F32)],
        compiler_params=_cparams(("parallel", "parallel")),
        name="dn_prep",
    )(u_b, u_b, u_b, conv_w, bab, gate_params)


def _dn_intra_kernel(q_ref, k_ref, v_ref, g_ref, gt_ref, u_ref, wq_ref, kq_ref, et_ref, *, c, per):
    n_units = 2 * DN_PAIR
    w4 = n_units * c
    hp = lax.Precision.HIGHEST
    bf = lambda t_: t_.astype(BF16)
    ii = lax.broadcasted_iota(jnp.int32, (c, w4), 0)
    ll = lax.broadcasted_iota(jnp.int32, (c, w4), 1)
    jj = ll % c
    ub = ll // c
    ub_row = ub[0:1, :]
    lo = jnp.where(ub >= DN_PAIR, jj - ii, ii - jj)
    incl = lo >= 0
    strict = lo > 0
    eye = (ii == jj).astype(F32)
    blk = (ii // TRI_BASE) == (jj // TRI_BASE)

    def pack(parts, sel):
        out = parts[n_units - 1]
        for u_ in range(n_units - 2, -1, -1):
            out = jnp.where(sel == u_, parts[u_], out)
        return out

    unit_mask = [jnp.where(ub == u_, 1.0, 0.0).astype(BF16) for u_ in range(n_units)]

    def block_diag(y16):
        return jnp.concatenate([y16 * m_ for m_ in unit_mask], axis=0)

    def mm(xs, ys):
        return [_dot(bf(x_), block_diag(bf(y_))) for x_, y_ in zip(xs, ys)]

    ri = lax.broadcasted_iota(jnp.int32, (c, c), 0)
    ci = lax.broadcasted_iota(jnp.int32, (c, c), 1)
    tri4 = (lo <= 0).astype(F32)
    tri_f = (ci <= ri).astype(F32)
    tri_b = (ci >= ri).astype(F32)
    items = [(cc, p) for cc in range(per) for p in range(B_V_HEADS // DN_PAIR)]
    qs_, ks_, lms, qkms, betas, egcs, kscales, etots = [], [], [], [], [], [], [], []
    for cc, p in items:
        rows_cc = slice(cc * c, (cc + 1) * c)
        if p == 0:
            g_all = g_ref[0, rows_cc, :]
            gc_dir = [jnp.dot(tri_f, g_all, precision=hp, preferred_element_type=F32),
                      jnp.dot(tri_b, g_all, precision=hp, preferred_element_type=F32)]
            gcr_all = jnp.dot(gt_ref[0, cc], tri4, precision=hp, preferred_element_type=F32)
            tot_all = jnp.sum(g_all, axis=0, keepdims=True)
        cs = slice(p * B_DK, (p + 1) * B_DK)
        q = q_ref[0, rows_cc, cs]
        k = k_ref[0, rows_cc, cs]
        k16 = bf(k)
        k4 = jnp.concatenate([k16] * n_units, axis=0)
        gram = _dot_nt(k16, k4)
        qk = _dot_nt(bf(q), k4)
        beta_u, gc_u, gcr_u, tot_u = [], [], [], []
        for u_ in range(n_units):
            d_, e_ = divmod(u_, DN_PAIR)
            h = p * DN_PAIR + e_
            bl = d_ * 2 * B_V_HEADS + h
            gl = bl + B_V_HEADS
            beta_u.append(g_all[:, bl:bl + 1])
            gc_u.append(gc_dir[d_][:, gl:gl + 1])
            gcr_u.append(gcr_all[gl:gl + 1, :])
            tot_u.append(tot_all[:, gl:gl + 1])
        gc_p = pack(gc_u, ub)
        gcr_p = pack(gcr_u, ub_row)
        tot_p = pack(tot_u, ub_row)
        dec = jnp.where(incl, jnp.exp(jnp.where(incl, gc_p - gcr_p, 0.0)), 0.0)
        lms.append(jnp.where(strict, pack(beta_u, ub) * gram * dec, 0.0))
        qkms.append(jnp.where(incl, qk * dec, 0.0))
        qs_.append(q)
        ks_.append(k)
        betas.append(beta_u)
        egcs.append([jnp.exp(g_) for g_ in gc_u])
        kscales.append(jnp.exp(tot_p - gcr_p))
        etots.append([jnp.exp(t_) for t_ in tot_u])

    d1 = [jnp.where(blk, lm, 0.0) for lm in lms]
    d2 = mm(d1, d1)
    d4 = mm(d2, d2)
    d8 = mm(d4, d4)
    tm_ = [eye - d_ for d_ in d1]
    for dk in (d2, d4, d8):
        tm_ = [a + b for a, b in zip(tm_, mm(tm_, dk))]
    size = TRI_BASE
    while size < c:
        off = ((ii // (2 * size)) == (jj // (2 * size))) & ((ii // size) != (jj // size))
        cm = [jnp.where(off, lm, 0.0) for lm in lms]
        pc = mm(tm_, cm)
        tm_ = [a - b for a, b in zip(tm_, mm(pc, tm_))]
        size *= 2

    lane2 = lax.broadcasted_iota(jnp.int32, (1, DN_PAIR * B_DV), 1)
    for it, (cc, p) in enumerate(items):
        q, k = qs_[it], ks_[it]
        rows = []
        for u_ in range(n_units):
            d_, e_ = divmod(u_, DN_PAIR)
            h = p * DN_PAIR + e_
            v = v_ref[0, cc * c:(cc + 1) * c, h * B_DV:(h + 1) * B_DV]
            beta = betas[it][u_]
            rows.append(jnp.concatenate([v * beta, k * (beta * egcs[it][u_])], axis=1))
        uw = _dot(block_diag(bf(tm_[it])), bf(jnp.concatenate(rows, axis=0)))
        k_t = jnp.transpose(jnp.concatenate([k] * DN_PAIR, axis=0))
        for d_ in range(2):
            u0, u1 = d_ * DN_PAIR, d_ * DN_PAIR + 1
            u_ref[d_, 0, cc, p] = jnp.concatenate([uw[u0 * c:(u0 + 1) * c, :B_DV], uw[u1 * c:(u1 + 1) * c, :B_DV]], axis=1)
            wq = jnp.concatenate([uw[u0 * c:(u0 + 1) * c, B_DV:], q * egcs[it][u0],
                                  uw[u1 * c:(u1 + 1) * c, B_DV:], q * egcs[it][u1]], axis=0)
            wq_ref[d_, 0, cc, p] = bf(wq)
            ls = slice(d_ * DN_PAIR * c, (d_ + 1) * DN_PAIR * c)
            kq = jnp.concatenate([k_t * kscales[it][:, ls], qkms[it][:, ls]], axis=0)
            kq_ref[d_, 0, cc, p] = bf(kq)
            et_ref[d_, 0, cc, p:p + 1, :] = jnp.where(lane2 < B_DV, etots[it][u0], etots[it][u1])


def _dn_intra(qn, kn, vn, gts, gts_t):
    bsz, seq, _ = qn.shape
    c = DN_CHUNK
    assert DN_PAIR * c == B_DK and seq % c == 0
    n = seq // c
    npair = B_V_HEADS // DN_PAIR
    wide = DN_PAIR * B_DV
    per = DN_INTRA_CHUNKS if n % DN_INTRA_CHUNKS == 0 else 1
    kern = functools.partial(_dn_intra_kernel, c=c, per=per)
    out5 = lambda r, cdim: pl.BlockSpec((2, 1, per, npair, r, cdim), lambda b, i: (0, b, i, 0, 0, 0))
    return pl.pallas_call(
        kern,
        grid=(bsz, n // per),
        in_specs=[pl.BlockSpec((1, per * c, B_QK), lambda b, i: (b, i, 0)),
                  pl.BlockSpec((1, per * c, B_QK), lambda b, i: (b, i, 0)),
                  pl.BlockSpec((1, per * c, B_VZ), lambda b, i: (b, i, 0)),
                  pl.BlockSpec((1, per * c, LANES), lambda b, i: (b, i, 0)),
                  pl.BlockSpec((1, per, LANES, c), lambda b, i: (b, i, 0, 0))],
        out_specs=[out5(c, wide), out5(2 * DN_PAIR * c, B_DK), out5(B_DK + c, DN_PAIR * c),
                   pl.BlockSpec((2, 1, per, npair, wide), lambda b, i: (0, b, i, 0, 0))],
        out_shape=[jax.ShapeDtypeStruct((2, bsz, n, npair, c, wide), F32),
                   jax.ShapeDtypeStruct((2, bsz, n, npair, 2 * DN_PAIR * c, B_DK), BF16),
                   jax.ShapeDtypeStruct((2, bsz, n, npair, B_DK + c, DN_PAIR * c), BF16),
                   jax.ShapeDtypeStruct((2, bsz, n, npair, wide), F32)],
        compiler_params=_cparams(("parallel", "parallel")),
        name="dn_intra",
    )(qn, kn, vn, gts, gts_t)


def _dn_state_kernel(uf_ref, wqf_ref, kqf_ref, etf_ref, ub_ref, wqb_ref, kqb_ref, etb_ref, of_ref, ob_ref, s_ref,
                     *, c, per):
    @pl.when(pl.program_id(1) == 0)
    def _():
        s_ref[...] = jnp.zeros_like(s_ref)

    npair = B_V_HEADS // DN_PAIR
    bf = lambda t_: t_.astype(BF16)
    chains = [(d_, p) for d_ in range(2) for p in range(npair)]
    refs = ((uf_ref, wqf_ref, kqf_ref, etf_ref, of_ref), (ub_ref, wqb_ref, kqb_ref, etb_ref, ob_ref))
    zero = jnp.zeros((c, B_DV), F32)
    for step in range(per):
        local = (step, per - 1 - step)
        states = [s_ref[d_, p] for d_, p in chains]
        a_res = [_dot(refs[d_][1][0, 0, local[d_], p], bf(s_)) for (d_, p), s_ in zip(chains, states)]
        b_res = []
        for (d_, p), a_ in zip(chains, a_res):
            u = refs[d_][0][0, 0, local[d_], p]
            v0 = u[:, :B_DV] - a_[0:c, :B_DV]
            v1 = u[:, B_DV:] - a_[2 * c:3 * c, B_DV:]
            bd_v = jnp.concatenate([jnp.concatenate([v0, zero], axis=1), jnp.concatenate([zero, v1], axis=1)], axis=0)
            b_res.append(_dot(refs[d_][2][0, 0, local[d_], p], bf(bd_v)))
        for (d_, p), a_, b_, s_ in zip(chains, a_res, b_res, states):
            s_ref[d_, p] = s_ * refs[d_][3][0, 0, local[d_], p:p + 1, :] + b_[:B_DK]
            o_ref = refs[d_][4]
            rs = slice(local[d_] * c, (local[d_] + 1) * c)
            o_ref[0, rs, (2 * p) * B_DV:(2 * p + 1) * B_DV] = a_[c:2 * c, :B_DV] + b_[B_DK:, :B_DV]
            o_ref[0, rs, (2 * p + 1) * B_DV:(2 * p + 2) * B_DV] = a_[3 * c:4 * c, B_DV:] + b_[B_DK:, B_DV:]


def _dn_state(u_all, wq_all, kq_all, et_all, seq):
    _, bsz, n, npair, c, wide = u_all.shape
    per = DN_STATE_CHUNKS if n % DN_STATE_CHUNKS == 0 else 1
    nb = n // per
    kern = functools.partial(_dn_state_kernel, c=c, per=per)
    fwd = lambda r, cdim: pl.BlockSpec((1, 1, per, npair, r, cdim), lambda b, i: (0, b, i, 0, 0, 0))
    bwd = lambda r, cdim: pl.BlockSpec((1, 1, per, npair, r, cdim), lambda b, i: (1, b, nb - 1 - i, 0, 0, 0))
    et_f = pl.BlockSpec((1, 1, per, npair, wide), lambda b, i: (0, b, i, 0, 0))
    et_b = pl.BlockSpec((1, 1, per, npair, wide), lambda b, i: (1, b, nb - 1 - i, 0, 0))
    shapes = ((c, wide), (2 * DN_PAIR * c, B_DK), (B_DK + c, DN_PAIR * c))
    return pl.pallas_call(
        kern,
        grid=(bsz, nb),
        in_specs=[fwd(*shapes[0]), fwd(*shapes[1]), fwd(*shapes[2]), et_f,
                  bwd(*shapes[0]), bwd(*shapes[1]), bwd(*shapes[2]), et_b],
        out_specs=[pl.BlockSpec((1, per * c, B_VZ), lambda b, i: (b, i, 0)),
                   pl.BlockSpec((1, per * c, B_VZ), lambda b, i: (b, nb - 1 - i, 0))],
        out_shape=[jax.ShapeDtypeStruct((bsz, seq, B_VZ), F32), jax.ShapeDtypeStruct((bsz, seq, B_VZ), F32)],
        scratch_shapes=[pltpu.VMEM((2, npair, B_DK, wide), F32)],
        compiler_params=_cparams(("parallel", "arbitrary")),
        name="dn_state",
    )(u_all, wq_all, kq_all, et_all, u_all, wq_all, kq_all, et_all)


def _branch_a_kernel(o0_ref, o1_ref, o2_ref, l0_ref, l1_ref, l2_ref, w_ref, gp_ref, bg_ref, out_ref, o_sc, l_sc,
                     *, dils, tm):
    for g, (o_ref, l_ref) in enumerate(((o0_ref, l0_ref), (o1_ref, l1_ref), (o2_ref, l2_ref))):
        r = dils[g]
        for c in range(r):
            l_sc[g, pl.ds(c, tm // r, stride=r), :] = l_ref[c]
            for h in range(A_HEADS):
                o_sc[g, h, pl.ds(c, tm // r, stride=r), :] = o_ref[c, :, h * HEAD_DIM:(h + 1) * HEAD_DIM].astype(F32)
    ls = [l_sc[g] for g in range(N_GROUPS)]
    m = jnp.maximum(jnp.maximum(ls[0], ls[1]), ls[2])
    es = [jnp.exp(l - m) for l in ls]
    den = es[0] + es[1] + es[2]
    ws = [e / den for e in es]
    parts = []
    for h in range(A_HEADS):
        hs = slice(h * HEAD_DIM, (h + 1) * HEAD_DIM)
        acc = ws[0][:, h:h + 1] * o_sc[0, h]
        acc = acc + ws[1][:, h:h + 1] * o_sc[1, h]
        acc = acc + ws[2][:, h:h + 1] * o_sc[2, h]
        parts.append(acc.astype(BF16))
    oa = jnp.concatenate(parts, axis=1)
    y = _dot(oa, w_ref[...])
    out_ref[...] = _sigmoid(gp_ref[...] + bg_ref[...]) * y


def _branch_a(outs, lses, w_a, gpre, b_gate, tm):
    bsz = outs[0].shape[0]
    dils = tuple(o.shape[1] for o in outs)
    seq = dils[0] * outs[0].shape[2]
    t = bsz * seq
    per_seq = seq // tm
    wd = A_HEADS * HEAD_DIM
    cls = lambda r, width: pl.BlockSpec((None, r, tm // r, width), lambda i: (i // per_seq, 0, i % per_seq, 0))
    kern = functools.partial(_branch_a_kernel, dils=dils, tm=tm)
    return pl.pallas_call(
        kern,
        grid=(t // tm,),
        in_specs=[cls(dils[0], wd), cls(dils[1], wd), cls(dils[2], wd),
                  cls(dils[0], LANES), cls(dils[1], LANES), cls(dils[2], LANES),
                  pl.BlockSpec((wd, D_MODEL), lambda i: (0, 0)),
                  pl.BlockSpec((tm, D_MODEL), lambda i: (i, 0)),
                  pl.BlockSpec((1, D_MODEL), lambda i: (0, 0))],
        out_specs=pl.BlockSpec((tm, D_MODEL), lambda i: (i, 0)),
        out_shape=jax.ShapeDtypeStruct((t, D_MODEL), F32),
        scratch_shapes=[pltpu.VMEM((N_GROUPS, A_HEADS, tm, HEAD_DIM), F32), pltpu.VMEM((N_GROUPS, tm, LANES), F32)],
        compiler_params=_cparams(("parallel",)),
        name="branch_a",
    )(*outs, *lses, w_a, gpre, b_gate)


def _branch_b_kernel(of_ref, ob_ref, z_ref, nw_ref, w_ref, gp_ref, bg_ref, a_ref, out_ref):
    nw = nw_ref[...]
    parts = []
    for h in range(B_V_HEADS):
        hs = slice(h * B_DV, (h + 1) * B_DV)
        o = of_ref[:, hs] + ob_ref[:, hs]
        z = z_ref[:, hs].astype(F32)
        o = o * lax.rsqrt(jnp.mean(o * o, axis=1, keepdims=True) + RMS_EPS) * nw * (z * _sigmoid(z))
        parts.append(o.astype(BF16))
    ob = jnp.concatenate(parts, axis=1)
    y = _dot(ob, w_ref[...])
    out_ref[...] = (a_ref[...] + _sigmoid(gp_ref[...] + bg_ref[...]) * y).astype(out_ref.dtype)


def _branch_b(o_f, o_b, u_b2d, norm_w, w_b, gpre, b_gate, a_part, tm):
    t = a_part.shape[0]
    return pl.pallas_call(
        _branch_b_kernel,
        grid=(t // tm,),
        in_specs=[pl.BlockSpec((tm, B_VZ), lambda i: (i, 0)),
                  pl.BlockSpec((tm, B_VZ), lambda i: (i, 0)),
                  pl.BlockSpec((tm, B_VZ), lambda i: (i, (2 * B_QK + B_VZ) // B_VZ)),
                  pl.BlockSpec((1, B_DV), lambda i: (0, 0)),
                  pl.BlockSpec((B_VZ, D_MODEL), lambda i: (0, 0)),
                  pl.BlockSpec((tm, D_MODEL), lambda i: (i, 1)),
                  pl.BlockSpec((1, D_MODEL), lambda i: (0, 1)),
                  pl.BlockSpec((tm, D_MODEL), lambda i: (i, 0))],
        out_specs=pl.BlockSpec((tm, D_MODEL), lambda i: (i, 0)),
        out_shape=jax.ShapeDtypeStruct((t, D_MODEL), BF16),
        compiler_params=_cparams(("parallel",)),
        name="branch_b",
    )(o_f, o_b, u_b2d, norm_w, w_b, gpre, b_gate, a_part)


def _layer_norm(y, g, b):
    mu = jnp.mean(y, axis=1, keepdims=True)
    yc = y - mu
    var = jnp.mean(yc * yc, axis=1, keepdims=True)
    return yc * lax.rsqrt(var + LN_EPS) * g + b


def _out_ln_kernel(m_ref, w_ref, x_ref, g_ref, b_ref, out_ref):
    mix = _dot(m_ref[...], w_ref[...])
    out_ref[...] = _layer_norm(DEEPNORM_ALPHA * x_ref[...] + mix, g_ref[...], b_ref[...])


def _out_ln(merged, w_out, x2d, ln_g, ln_b, tm):
    t = merged.shape[0]
    vec = pl.BlockSpec((1, D_MODEL), lambda i: (0, 0))
    return pl.pallas_call(
        _out_ln_kernel,
        grid=(t // tm,),
        in_specs=[pl.BlockSpec((tm, D_MODEL), lambda i: (i, 0)),
                  pl.BlockSpec((D_MODEL, D_MODEL), lambda i: (0, 0)),
                  pl.BlockSpec((tm, D_MODEL), lambda i: (i, 0)), vec, vec],
        out_specs=pl.BlockSpec((tm, D_MODEL), lambda i: (i, 0)),
        out_shape=jax.ShapeDtypeStruct((t, D_MODEL), F32),
        compiler_params=_cparams(("parallel",)),
        name="out_proj_ln1",
    )(merged, w_out, x2d, ln_g, ln_b)


def _router_kernel(x_ref, rwh_ref, rwl_ref, rb_ref, idx_ref, gate_ref, rank_ref, cnt_ref, carry, *, tm):
    @pl.when(pl.program_id(0) == 0)
    def _():
        carry[...] = jnp.zeros_like(carry)

    lane = lax.broadcasted_iota(jnp.int32, (tm, LANES), 1)
    lane_f = lane.astype(F32)
    x = x_ref[...]
    x_hi = x.astype(BF16)
    x_lo = (x - x_hi.astype(F32)).astype(BF16)
    logits = _dot(x_hi, rwh_ref[...]) + (_dot(x_lo, rwh_ref[...]) + _dot(x_hi, rwl_ref[...])) + rb_ref[...]
    cur = jnp.where(lane < N_EXPERTS, logits, -jnp.inf)
    vals, idxs = [], []
    for _k in range(TOP_K):
        m = jnp.max(cur, axis=1, keepdims=True)
        idx = jnp.min(jnp.where(cur == m, lane_f, float(LANES)), axis=1, keepdims=True).astype(jnp.int32)
        vals.append(m)
        idxs.append(idx)
        cur = jnp.where(lane == idx, -jnp.inf, cur)
    es = [jnp.exp(v - vals[0]) for v in vals]
    den = es[0] + es[1] + es[2] + es[3]
    onehot = jnp.zeros((tm, LANES), F32)
    for idx in idxs:
        onehot = onehot + (lane == idx).astype(F32)
    ri = lax.broadcasted_iota(jnp.int32, (tm, tm), 0)
    ci = lax.broadcasted_iota(jnp.int32, (tm, tm), 1)
    before = (ci < ri).astype(BF16)
    prefix = _dot(before, onehot.astype(BF16)) + carry[0:1, :]
    idx_out = jnp.zeros((tm, LANES), jnp.int32)
    gate_out = jnp.zeros((tm, LANES), F32)
    rank_out = jnp.zeros((tm, LANES), jnp.int32)
    for k in range(TOP_K):
        rk = jnp.sum(jnp.where(lane == idxs[k], prefix, 0.0), axis=1, keepdims=True)
        idx_out = jnp.where(lane == k, idxs[k], idx_out)
        gate_out = jnp.where(lane == k, es[k] / den, gate_out)
        rank_out = jnp.where(lane == k, rk.astype(jnp.int32), rank_out)
    idx_ref[...] = idx_out
    gate_ref[...] = gate_out
    rank_ref[...] = rank_out
    total = carry[0:1, :] + jnp.sum(onehot, axis=0, keepdims=True)
    carry[...] = jnp.broadcast_to(total, carry.shape)
    cnt_ref[...] = jnp.broadcast_to(total, cnt_ref.shape)


def _router(x1, rw, rb, tm):
    rw_hi = rw.astype(BF16)
    rw_lo = (rw - rw_hi.astype(F32)).astype(BF16)
    t = x1.shape[0]
    row = pl.BlockSpec((tm, LANES), lambda i: (i, 0))
    kern = functools.partial(_router_kernel, tm=tm)
    return pl.pallas_call(
        kern,
        grid=(t // tm,),
        in_specs=[pl.BlockSpec((tm, D_MODEL), lambda i: (i, 0)),
                  pl.BlockSpec((D_MODEL, LANES), lambda i: (0, 0)),
                  pl.BlockSpec((D_MODEL, LANES), lambda i: (0, 0)),
                  pl.BlockSpec((1, LANES), lambda i: (0, 0))],
        out_specs=[row, row, row, pl.BlockSpec((8, LANES), lambda i: (0, 0))],
        out_shape=[jax.ShapeDtypeStruct((t, LANES), jnp.int32),
                   jax.ShapeDtypeStruct((t, LANES), F32),
                   jax.ShapeDtypeStruct((t, LANES), jnp.int32),
                   jax.ShapeDtypeStruct((8, LANES), F32)],
        scratch_shapes=[pltpu.VMEM((8, LANES), F32)],
        compiler_params=_cparams(("arbitrary",)),
        name="router",
    )(x1, rw_hi, rw_lo, rb)


def _pack_rows(src_ref, dst_ref):
    half = D_MODEL // 2
    n = src_ref.shape[0]
    for s in range(ROW_SUBLANES):
        lo = pltpu.bitcast(src_ref[:, s * LANES:(s + 1) * LANES].astype(BF16).astype(F32), jnp.uint32)
        hi = pltpu.bitcast(src_ref[:, half + s * LANES:half + (s + 1) * LANES].astype(BF16).astype(F32), jnp.uint32)
        dst_ref[pl.ds(s, n, stride=ROW_SUBLANES), :] = (lo >> 16) | hi


def _unpack_words(w):
    return pltpu.bitcast(w << 16, F32), pltpu.bitcast(w & jnp.uint32(0xFFFF0000), F32)


def _tile(ref, i):
    start = i * ROW_SUBLANES
    if not isinstance(i, int):
        start = pl.multiple_of(start, ROW_SUBLANES)
    return ref.at[pl.ds(start, ROW_SUBLANES)]


def _tile_copy(src, i, dst, j, sem):
    return pltpu.make_async_copy(_tile(src, i), _tile(dst, j), sem)


def _wait_tiles(hbm, n, sem):
    pltpu.make_async_copy(hbm.at[pl.ds(0, n * ROW_SUBLANES)], hbm.at[pl.ds(0, n * ROW_SUBLANES)], sem).wait()


def _dispatch_kernel(pad_start_ref, pad_cnt_ref, dest_ref, x_ref, xs_hbm, stage, zero, sem, *, tm, nt):
    i = pl.program_id(0)

    @pl.when(i < nt)
    def _():
        _pack_rows(x_ref, stage)

        def issue(t_, carry):
            src = _tile(stage, t_)
            for k in range(TOP_K):
                pltpu.make_async_copy(src, _tile(xs_hbm, dest_ref[0, 0, t_ * TOP_K + k]), sem).start()
            return carry

        lax.fori_loop(0, tm, issue, 0, unroll=2)
        _wait_tiles(xs_hbm, tm * TOP_K, sem)

    @pl.when((i >= nt) & (i < nt + N_EXPERTS))
    def _():
        e = i - nt
        zero[...] = jnp.zeros_like(zero)
        start = pad_start_ref[e]
        cnt = pad_cnt_ref[e]

        def issue(r, carry):
            _tile_copy(zero, 0, xs_hbm, start + r, sem).start()
            return carry

        lax.fori_loop(0, cnt, issue, 0)

        def wait(r, carry):
            _tile_copy(zero, 0, xs_hbm, start, sem).wait()
            return carry

        lax.fori_loop(0, cnt, wait, 0)

    @pl.when(i == nt + N_EXPERTS)
    def _():
        stage[...] = jnp.zeros_like(stage)
        start = pad_start_ref[N_EXPERTS]

        def fill(j, carry):
            piece = xs_hbm.at[pl.ds(pl.multiple_of((start + j * tm) * ROW_SUBLANES, ROW_SUBLANES), tm * ROW_SUBLANES)]
            cp = pltpu.make_async_copy(stage, piece, sem)
            cp.start()
            cp.wait()
            return carry

        lax.fori_loop(0, pad_cnt_ref[N_EXPERTS] // tm, fill, 0)


def _dispatch(x1, dest, pad_from, pad_cnt, n_pad, tm):
    t = x1.shape[0]
    nt = t // tm
    assert MOE_BLOCK % tm == 0 and pad_from.shape[0] == N_EXPERTS + 1
    kern = functools.partial(_dispatch_kernel, tm=tm, nt=nt)
    grid_spec = pltpu.PrefetchScalarGridSpec(
        num_scalar_prefetch=2,
        grid=(nt + pad_from.shape[0],),
        in_specs=[pl.BlockSpec((1, 1, tm * TOP_K), lambda i, ps, pc: (jnp.minimum(i, nt - 1), 0, 0),
                               memory_space=pltpu.SMEM),
                  pl.BlockSpec((tm, D_MODEL), lambda i, ps, pc: (jnp.minimum(i, nt - 1), 0))],
        out_specs=pl.BlockSpec(memory_space=pl.ANY),
        scratch_shapes=[pltpu.VMEM((tm * ROW_SUBLANES, LANES), jnp.uint32),
                        pltpu.VMEM((ROW_SUBLANES, LANES), jnp.uint32),
                        pltpu.SemaphoreType.DMA(())],
    )
    return pl.pallas_call(
        kern,
        grid_spec=grid_spec,
        out_shape=jax.ShapeDtypeStruct((n_pad * ROW_SUBLANES, LANES), jnp.uint32),
        compiler_params=_cparams(("arbitrary",)),
        name="moe_dispatch",
    )(pad_from, pad_cnt, dest.reshape(nt, 1, tm * TOP_K), x1)


GU_GROUP = 2 * LANES


def _regroup_kernel(w_ref, o_ref):
    ri = lax.broadcasted_iota(jnp.int32, (GU_GROUP, GU_GROUP), 0)
    ci = lax.broadcasted_iota(jnp.int32, (GU_GROUP, GU_GROUP), 1)
    src = jnp.where(ci < LANES, 2 * ci, 2 * (ci - LANES) + 1)
    perm = jnp.where(ri == src, 1.0, 0.0).astype(BF16)
    for c in range(w_ref.shape[2] // GU_GROUP):
        cs = slice(c * GU_GROUP, (c + 1) * GU_GROUP)
        o_ref[0, :, cs] = _dot(w_ref[0, :, cs].astype(BF16), perm).astype(BF16)


def _regroup_gate_up(w_gate_up):
    e, d, n2 = w_gate_up.shape
    rt = 512
    return pl.pallas_call(
        _regroup_kernel,
        grid=(e, d // rt),
        in_specs=[pl.BlockSpec((1, rt, n2), lambda i, j: (i, j, 0))],
        out_specs=pl.BlockSpec((1, rt, n2), lambda i, j: (i, j, 0)),
        out_shape=jax.ShapeDtypeStruct((e, d, n2), BF16),
        compiler_params=_cparams(("parallel", "parallel")),
        name="regroup_gate_up",
    )(w_gate_up)

def _ffn_kernel(be_ref, nused_ref, rows_ref, x_ref, wgu_ref, bgu_ref, wd_ref, bd_ref, o_ref, acc, xb):
    i = pl.program_id(0)
    f = pl.program_id(1)
    nf = pl.num_programs(1)
    half = D_MODEL // 2

    @pl.when(f == 0)
    def _():
        acc[...] = jnp.broadcast_to(bd_ref[0], acc.shape)
        for s in range(ROW_SUBLANES):
            lo, hi = _unpack_words(x_ref[pl.ds(s, MOE_BLOCK, stride=ROW_SUBLANES), :])
            xb[:, s * LANES:(s + 1) * LANES] = lo.astype(BF16)
            xb[:, half + s * LANES:half + (s + 1) * LANES] = hi.astype(BF16)

    def compute(nrows):
        gu = _dot(xb[0:nrows, :], wgu_ref[0]) + bgu_ref[0]
        acts = []
        for m in range(MOE_TF // LANES):
            gate = jnp.minimum(gu[:, m * GU_GROUP:m * GU_GROUP + LANES], SWIGLU_LIMIT)
            up = jnp.clip(gu[:, m * GU_GROUP + LANES:(m + 1) * GU_GROUP], -SWIGLU_LIMIT, SWIGLU_LIMIT)
            acts.append(((up + 1.0) * gate * _sigmoid(gate * SWIGLU_ALPHA)).astype(BF16))
        acc[0:nrows, :] += _dot(jnp.concatenate(acts, axis=1), wd_ref[0].astype(BF16))

    rows = rows_ref[i]
    pl.when(rows > MOE_BLOCK // 2)(lambda: compute(MOE_BLOCK))
    pl.when((rows > MOE_BLOCK // 4) & (rows <= MOE_BLOCK // 2))(lambda: compute(MOE_BLOCK // 2))
    pl.when((rows > 0) & (rows <= MOE_BLOCK // 4))(lambda: compute(MOE_BLOCK // 4))

    @pl.when(f == nf - 1)
    def _():
        _pack_rows(acc, o_ref)


def _expert_ffn(xs, block_expert, n_used, block_rows, wgu, bgu, wd, bd):
    n_pad = xs.shape[0] // ROW_SUBLANES
    nb = n_pad // MOE_BLOCK
    d_ff = wgu.shape[2] // 2
    nf = d_ff // MOE_TF
    row_blk = (MOE_BLOCK * ROW_SUBLANES, LANES)
    grid_spec = pltpu.PrefetchScalarGridSpec(
        num_scalar_prefetch=3,
        grid=(nb, nf),
        in_specs=[pl.BlockSpec(row_blk, lambda i, f, be, nu, rw: (jnp.minimum(i, nu[0] - 1), 0)),
                  pl.BlockSpec((1, D_MODEL, 2 * MOE_TF), lambda i, f, be, nu, rw: (be[i], 0, f)),
                  pl.BlockSpec((1, 1, 2 * MOE_TF), lambda i, f, be, nu, rw: (be[i], 0, f)),
                  pl.BlockSpec((1, MOE_TF, D_MODEL), lambda i, f, be, nu, rw: (be[i], f, 0)),
                  pl.BlockSpec((1, 1, D_MODEL), lambda i, f, be, nu, rw: (be[i], 0, 0))],
        out_specs=pl.BlockSpec(row_blk, lambda i, f, be, nu, rw: (i, 0)),
        scratch_shapes=[pltpu.VMEM((MOE_BLOCK, D_MODEL), F32), pltpu.VMEM((MOE_BLOCK, D_MODEL), BF16)],
    )
    return pl.pallas_call(
        _ffn_kernel,
        grid_spec=grid_spec,
        out_shape=jax.ShapeDtypeStruct((n_pad * ROW_SUBLANES, LANES), jnp.uint32),
        compiler_params=_cparams(("arbitrary", "arbitrary")),
        name="expert_ffn",
    )(block_expert, n_used, block_rows, xs, wgu, bgu, wd, bd)


def _final_kernel(dest_ref, gate_ref, x_ref, p_ref, wpg_ref, bpg_ref, wple_ref, g_ref, b_ref, ys_hbm, out_ref,
                  rows, sem, *, tm):
    def issue(t_, carry):
        for k in range(TOP_K):
            _tile_copy(ys_hbm, dest_ref[0, 0, t_ * TOP_K + k], rows, k * tm + t_, sem).start()
        return carry

    lax.fori_loop(0, tm, issue, 0, unroll=2)
    x = x_ref[...]
    pg = _dot(x.astype(BF16), wpg_ref[...]) + bpg_ref[...]
    ple = _sigmoid(pg) * _dot(p_ref[...].astype(BF16), wple_ref[...])
    _wait_tiles(ys_hbm, tm * TOP_K, sem)
    gates = gate_ref[...]
    lo_parts, hi_parts = [], []
    for s in range(ROW_SUBLANES):
        acc_lo = acc_hi = None
        for k in range(TOP_K):
            lo, hi = _unpack_words(rows[pl.ds(k * tm * ROW_SUBLANES + s, tm, stride=ROW_SUBLANES), :])
            g = gates[:, k:k + 1]
            acc_lo = g * lo if acc_lo is None else acc_lo + g * lo
            acc_hi = g * hi if acc_hi is None else acc_hi + g * hi
        lo_parts.append(acc_lo)
        hi_parts.append(acc_hi)
    y = jnp.concatenate(lo_parts + hi_parts, axis=1)
    out_ref[...] = _layer_norm(DEEPNORM_ALPHA * x + y + ple, g_ref[...], b_ref[...])


def _final(dest, gates, x1, p2d, w_pg, b_pg, w_ple, ln_g, ln_b, ys, tm):
    t = x1.shape[0]
    vec = pl.BlockSpec((1, D_MODEL), lambda i: (0, 0))
    kern = functools.partial(_final_kernel, tm=tm)
    return pl.pallas_call(
        kern,
        grid=(t // tm,),
        in_specs=[pl.BlockSpec((1, 1, tm * TOP_K), lambda i: (i, 0, 0), memory_space=pltpu.SMEM),
                  pl.BlockSpec((tm, LANES), lambda i: (i, 0)),
                  pl.BlockSpec((tm, D_MODEL), lambda i: (i, 0)),
                  pl.BlockSpec((tm, PLE_DIM), lambda i: (i, 0)),
                  pl.BlockSpec((D_MODEL, D_MODEL), lambda i: (0, 0)), vec,
                  pl.BlockSpec((PLE_DIM, D_MODEL), lambda i: (0, 0)), vec, vec,
                  pl.BlockSpec(memory_space=pl.ANY)],
        out_specs=pl.BlockSpec((tm, D_MODEL), lambda i: (i, 0)),
        out_shape=jax.ShapeDtypeStruct((t, D_MODEL), F32),
        scratch_shapes=[pltpu.VMEM((tm * TOP_K * ROW_SUBLANES, LANES), jnp.uint32), pltpu.SemaphoreType.DMA(())],
        compiler_params=_cparams(("arbitrary",)),
        name="combine_ple_ln2",
    )(dest.reshape(t // tm, 1, tm * TOP_K), gates, x1, p2d, w_pg, b_pg, w_ple, ln_g, ln_b, ys)


def _row_tile(t):
    return min(512, t)


def _layer(x, p, w_in, b_gate, conv_w, a_log, dt_bias, dn_norm_w, w_branch_a, w_branch_b, w_out, ln1_g, ln1_b,
           router_w, router_b, w_gate_up, b_gate_up, w_down, b_down, w_ple, w_ple_gate, b_ple_gate, ln2_g, ln2_b):
    bsz, seq, _ = x.shape
    t = bsz * seq
    tm = _row_tile(t)
    x2d = x.reshape(t, D_MODEL)
    xb = x2d.astype(BF16)

    c_a = 3 * A_QKV
    c_b = 2 * B_QK + 2 * B_VZ
    w_bf = w_in.astype(BF16)
    tp = min(IN_PROJ_TM, seq)
    u_b = _matmul(xb, w_bf[:, c_a:c_a + c_b], BF16, tp, IN_PROJ_TN)
    w_bab = jnp.pad(w_bf[:, c_a + c_b:c_a + c_b + B_GATES], ((0, 0), (0, LANES - B_GATES)))
    bab = _matmul(xb, w_bab, F32, tp, LANES)
    gpre = _matmul(xb, w_bf[:, c_a + c_b + B_GATES:], F32, tp, IN_PROJ_TN)

    slopes = _alibi_slopes()
    gw = A_HEADS * HEAD_DIM
    outs, lses = [], []
    for gi, (_win, dil) in enumerate(DILATION_GROUPS):
        w_g = jnp.concatenate([w_bf[:, part * A_QKV + gi * gw:part * A_QKV + (gi + 1) * gw] for part in range(3)], axis=1)
        o_g, l_g = _attention_group(_matmul_classes(xb, w_g, bsz, dil, tp, IN_PROJ_TN), gi, dil, slopes[gi])
        outs.append(o_g)
        lses.append(l_g)

    cw = jnp.pad(conv_w.astype(F32), ((0, 8 - CONV_W), (0, 0)))
    lane_is_g = (np.arange(LANES) % (2 * B_V_HEADS) >= B_V_HEADS) & (np.arange(LANES) < B_GATES)
    neg_a = jnp.zeros((LANES,), F32).at[B_V_HEADS:2 * B_V_HEADS].set(-jnp.exp(a_log[0].astype(F32)))
    neg_a = neg_a.at[3 * B_V_HEADS:4 * B_V_HEADS].set(-jnp.exp(a_log[1].astype(F32)))
    dtb = jnp.zeros((LANES,), F32).at[B_V_HEADS:2 * B_V_HEADS].set(dt_bias[0].astype(F32))
    dtb = dtb.at[3 * B_V_HEADS:4 * B_V_HEADS].set(dt_bias[1].astype(F32))
    gate_params = jnp.zeros((8, LANES), F32).at[0].set(neg_a).at[1].set(dtb).at[2].set(jnp.asarray(lane_is_g, F32))
    qn, kn, vn, gts = _dn_prep(u_b.reshape(bsz, seq, c_b), cw, bab.reshape(bsz, seq, LANES), gate_params)
    gts_t = gts.reshape(bsz, seq // DN_CHUNK, DN_CHUNK, LANES).transpose(0, 1, 3, 2)
    o_f, o_b = _dn_state(*_dn_intra(qn, kn, vn, gts, gts_t), seq)

    bg = b_gate.astype(F32).reshape(1, 2 * D_MODEL)
    a_part = _branch_a(outs, lses, w_branch_a.astype(BF16), gpre, bg, tm)
    merged = _branch_b(o_f.reshape(t, B_VZ), o_b.reshape(t, B_VZ), u_b, dn_norm_w.astype(F32).reshape(1, B_DV), w_branch_b.astype(BF16), gpre, bg,
                       a_part, tm)
    x1 = _out_ln(merged, w_out.astype(BF16), x2d, ln1_g.reshape(1, -1), ln1_b.reshape(1, -1), tm)

    rw = jnp.pad(router_w.astype(F32), ((0, 0), (0, LANES - N_EXPERTS)))
    rb = jnp.pad(router_b.astype(F32), (0, LANES - N_EXPERTS)).reshape(1, LANES)
    idx, gates, rank, cnt = _router(x1, rw, rb, tm)
    counts = cnt[0, :N_EXPERTS].astype(jnp.int32)
    padded = (counts + MOE_BLOCK - 1) // MOE_BLOCK * MOE_BLOCK
    pad_end = jnp.cumsum(padded)
    pad_start = pad_end - padded
    dest = pad_start[idx[:, :TOP_K]] + rank[:, :TOP_K]
    n_pad = t * TOP_K + N_EXPERTS * MOE_BLOCK
    nb = n_pad // MOE_BLOCK
    block_start = jnp.arange(nb, dtype=jnp.int32) * MOE_BLOCK
    block_expert = jnp.minimum(jnp.sum(pad_end[None, :] <= block_start[:, None], axis=1), N_EXPERTS - 1).astype(jnp.int32)
    n_used = (pad_end[-1:] // MOE_BLOCK).astype(jnp.int32)

    zero_from = jnp.concatenate([pad_start + counts, pad_end[-1:]]).astype(jnp.int32)
    zero_cnt = jnp.concatenate([padded - counts, n_pad - pad_end[-1:]]).astype(jnp.int32)
    xs = _dispatch(x1, dest, zero_from, zero_cnt, n_pad, tm)
    d_ff = w_down.shape[1]
    wgu = _regroup_gate_up(w_gate_up.astype(F32))
    bgu = b_gate_up.astype(F32).reshape(N_EXPERTS, d_ff // LANES, LANES, 2).transpose(0, 1, 3, 2)
    bgu = bgu.reshape(N_EXPERTS, 1, 2 * d_ff)
    seg_end = (pad_start + counts)[block_expert]
    block_rows = jnp.where(block_start < pad_end[-1], jnp.clip(seg_end - block_start, 0, MOE_BLOCK), 0).astype(jnp.int32)
    ys = _expert_ffn(xs, block_expert, n_used, block_rows, wgu, bgu, w_down.astype(F32),
                     b_down.astype(F32).reshape(N_EXPERTS, 1, D_MODEL))

    out = _final(dest, gates, x1, p.reshape(t, PLE_DIM), w_ple_gate.astype(BF16),
                 b_ple_gate.astype(F32).reshape(1, -1), w_ple.astype(BF16), ln2_g.reshape(1, -1),
                 ln2_b.reshape(1, -1), ys, min(COMBINE_TM, t))
    return out.reshape(bsz, seq, D_MODEL)


def kernel(x, p, w_in, b_gate, conv_w, a_log, dt_bias, dn_norm_w, w_branch_a, w_branch_b, w_out, ln1_g, ln1_b,
           router_w, router_b, w_gate_up, b_gate_up, w_down, b_down, w_ple, w_ple_gate, b_ple_gate, ln2_g, ln2_b):
    assert w_in.shape[0] == DEPTH
    return _layer(x, p[0], w_in[0], b_gate[0], conv_w[0], a_log[0], dt_bias[0], dn_norm_w[0], w_branch_a[0],
                  w_branch_b[0], w_out[0], ln1_g[0], ln1_b[0], router_w[0], router_b[0], w_gate_up[0],
                  b_gate_up[0], w_down[0], b_down[0], w_ple[0], w_ple_gate[0], b_ple_gate[0], ln2_g[0], ln2_b[0])
```

```python
import functools

import numpy as np
import jax
import jax.numpy as jnp
from jax import lax
from jax.experimental import pallas as pl
from jax.experimental.pallas import tpu as pltpu

F32 = jnp.float32
BF16 = jnp.bfloat16

D_MODEL = 2048
HEAD_DIM = 128
A_HEADS = 8
DILATION_GROUPS = ((128, 1), (512, 4), (2048, 16))
N_GROUPS = 3
NEG_INF = -1e30
B_QK_HEADS = 8
B_V_HEADS = 16
B_DK = 128
B_DV = 128
CONV_W = 5
RMS_EPS = 1e-6
N_EXPERTS = 32
TOP_K = 4
SWIGLU_ALPHA = 1.702
SWIGLU_LIMIT = 7.0
PLE_DIM = 256
DEPTH = 1
DEEPNORM_ALPHA = (2 * DEPTH) ** 0.25
LN_EPS = 1e-5
A_QKV = N_GROUPS * A_HEADS * HEAD_DIM
B_QK = B_QK_HEADS * B_DK
B_VZ = B_V_HEADS * B_DV
B_GATES = 4 * B_V_HEADS

LANES = 128
N_SIDE = 64
Q_SUB = 128
DN_CHUNK = 64
DN_PAIR = B_V_HEADS // B_QK_HEADS
DN_INTRA_CHUNKS = 2
DN_STATE_CHUNKS = 2
TRI_BASE = 16
ROW_SUBLANES = 8
IN_PROJ_TM = 1024
IN_PROJ_TN = 1024
MOE_BLOCK = 1024
MOE_TF = 512
COMBINE_TM = 256
VMEM_LIMIT = 56 * 1024 * 1024
assert D_MODEL == 2 * ROW_SUBLANES * LANES


def _cparams(sem):
    return pltpu.CompilerParams(dimension_semantics=sem, vmem_limit_bytes=VMEM_LIMIT)


def _sigmoid(x):
    return 1.0 / (1.0 + jnp.exp(-x))


def _dot(a, b):
    return jnp.dot(a, b, preferred_element_type=F32)


def _dot_nt(a, b):
    return lax.dot_general(a, b, (((1,), (1,)), ((), ())), preferred_element_type=F32)


def _dot_tn(a, b):
    return lax.dot_general(a, b, (((0,), (0,)), ((), ())), preferred_element_type=F32)


def _mm_kernel(x_ref, w_ref, o_ref):
    o_ref[...] = _dot(x_ref[...], w_ref[...]).astype(o_ref.dtype)


def _matmul(x, w, out_dtype, tm, tn):
    m, k = x.shape
    n = w.shape[1]
    return pl.pallas_call(
        _mm_kernel,
        grid=(m // tm, n // tn),
        in_specs=[pl.BlockSpec((tm, k), lambda i, j: (i, 0)),
                  pl.BlockSpec((k, tn), lambda i, j: (0, j))],
        out_specs=pl.BlockSpec((tm, tn), lambda i, j: (i, j)),
        out_shape=jax.ShapeDtypeStruct((m, n), out_dtype),
        compiler_params=_cparams(("parallel", "parallel")),
        name="in_proj",
    )(x, w)


def _mm_classes_kernel(x_ref, w_ref, o_ref, acc_ref, *, dil, rows):
    y = _dot(x_ref[...], w_ref[...])
    for j in range(acc_ref.shape[0]):
        ls = slice(j * LANES, (j + 1) * LANES)
        acc_ref[j] = y[:, ls]
        for c in range(dil):
            o_ref[0, c, :, ls] = acc_ref[j, pl.ds(c, rows, stride=dil), :].astype(o_ref.dtype)


def _matmul_classes(x, w, bsz, dil, tm, tn):
    m, k = x.shape
    n = w.shape[1]
    seq = m // bsz
    tiles_per_seq = seq // tm
    rows = tm // dil
    kern = functools.partial(_mm_classes_kernel, dil=dil, rows=rows)
    return pl.pallas_call(
        kern,
        grid=(m // tm, n // tn),
        in_specs=[pl.BlockSpec((tm, k), lambda i, j: (i, 0)),
                  pl.BlockSpec((k, tn), lambda i, j: (0, j))],
        out_specs=pl.BlockSpec((1, dil, rows, tn), lambda i, j: (i // tiles_per_seq, 0, i % tiles_per_seq, j)),
        out_shape=jax.ShapeDtypeStruct((bsz, dil, seq // dil, n), BF16),
        scratch_shapes=[pltpu.VMEM((tn // LANES, tm, LANES), F32)],
        compiler_params=_cparams(("parallel", "parallel")),
        name=f"in_proj_dil{dil}",
    )(x, w)


def _attn_kernel(q_ref, kp_ref, kc_ref, kn_ref, vp_ref, vc_ref, vn_ref, o_ref, lse_ref, kbuf, vbuf,
                 *, dil, sub_len, tl, slopes):
    i0 = pl.program_id(2) * tl
    kbuf[0:N_SIDE, :] = kp_ref[...]
    kbuf[N_SIDE:N_SIDE + tl, :] = kc_ref[...]
    kbuf[N_SIDE + tl:, :] = kn_ref[...]
    vbuf[0:N_SIDE, :] = vp_ref[...]
    vbuf[N_SIDE:N_SIDE + tl, :] = vc_ref[...]
    vbuf[N_SIDE + tl:, :] = vn_ref[...]
    span = Q_SUB + 2 * N_SIDE
    qq = lax.broadcasted_iota(jnp.int32, (Q_SUB, span), 0)
    kk = lax.broadcasted_iota(jnp.int32, (Q_SUB, span), 1)
    delta = kk - N_SIDE - qq
    absd = jnp.abs(delta)
    band = absd <= N_SIDE
    dist = (dil * absd).astype(F32)
    lane = lax.broadcasted_iota(jnp.int32, (Q_SUB, LANES), 1)
    scale = HEAD_DIM ** -0.5
    for j in range(tl // Q_SUB):
        pos = i0 + (j * Q_SUB - N_SIDE) + kk
        valid = band & (pos >= 0) & (pos < sub_len)
        lse_tile = jnp.zeros((Q_SUB, LANES), F32)
        for h in range(A_HEADS):
            hs = slice(h * HEAD_DIM, (h + 1) * HEAD_DIM)
            q = q_ref[j * Q_SUB:(j + 1) * Q_SUB, hs]
            k = kbuf[j * Q_SUB:j * Q_SUB + span, hs]
            v = vbuf[j * Q_SUB:j * Q_SUB + span, hs]
            s = _dot_nt(q, k) * scale
            s = jnp.where(valid, s - float(slopes[h]) * dist, NEG_INF)
            m = jnp.max(s, axis=1, keepdims=True)
            p = jnp.exp(s - m)
            l = jnp.sum(p, axis=1, keepdims=True)
            o = _dot(p.astype(BF16), v) / l
            o_ref[j * Q_SUB:(j + 1) * Q_SUB, hs] = o.astype(o_ref.dtype)
            lse_tile = jnp.where(lane == h, m + jnp.log(l), lse_tile)
        lse_ref[j * Q_SUB:(j + 1) * Q_SUB, :] = lse_tile


def _attention_group(qkv, gi, dil, slopes):
    bsz, _, sub_len, _ = qkv.shape
    tl = min(512, sub_len)
    assert sub_len % tl == 0 and tl % Q_SUB == 0 and sub_len % N_SIDE == 0
    width = A_HEADS * HEAD_DIM
    halo_per_tile = tl // N_SIDE
    n_halo = sub_len // N_SIDE
    prev = lambda li: jnp.maximum(li * halo_per_tile - 1, 0)
    nxt = lambda li: jnp.minimum((li + 1) * halo_per_tile, n_halo - 1)
    halo = lambda col, rowf: pl.BlockSpec((None, None, N_SIDE, width), lambda b, c, li: (b, c, rowf(li), col))
    cur = lambda col: pl.BlockSpec((None, None, tl, width), lambda b, c, li: (b, c, li, col))
    kern = functools.partial(_attn_kernel, dil=dil, sub_len=sub_len, tl=tl, slopes=tuple(float(s) for s in slopes))
    return pl.pallas_call(
        kern,
        grid=(bsz, dil, sub_len // tl),
        in_specs=[cur(0), halo(1, prev), cur(1), halo(1, nxt), halo(2, prev), cur(2), halo(2, nxt)],
        out_specs=[pl.BlockSpec((None, None, tl, width), lambda b, c, li: (b, c, li, 0)),
                   pl.BlockSpec((None, None, tl, LANES), lambda b, c, li: (b, c, li, 0))],
        out_shape=[jax.ShapeDtypeStruct((bsz, dil, sub_len, width), BF16),
                   jax.ShapeDtypeStruct((bsz, dil, sub_len, LANES), F32)],
        scratch_shapes=[pltpu.VMEM((tl + 2 * N_SIDE, width), BF16),
                        pltpu.VMEM((tl + 2 * N_SIDE, width), BF16)],
        compiler_params=_cparams(("parallel", "parallel", "parallel")),
        name=f"dilated_attn_g{gi}",
    )(qkv, qkv, qkv, qkv, qkv, qkv, qkv)


def _alibi_slopes():
    n = N_GROUPS * A_HEADS
    s = 2.0 ** (-8.0 * np.arange(1, n + 1) / n)
    return s.astype(np.float32).reshape(N_GROUPS, A_HEADS)


def _dn_prep_kernel(prev_ref, cur_ref, next_ref, cw_ref, bab_ref, gp_ref, q_ref, k_ref, v_ref, g_ref, *, ts):
    ti = pl.program_id(1)
    nt = pl.num_programs(1)
    halo = CONV_W // 2
    keep_prev = (ti > 0).astype(F32)
    keep_next = (ti < nt - 1).astype(F32)
    for c in range((2 * B_QK + B_VZ) // LANES):
        cs = slice(c * LANES, (c + 1) * LANES)
        xp = prev_ref[0, :, cs].astype(F32)[8:16] * keep_prev
        xc = cur_ref[0, :, cs].astype(F32)
        xn = next_ref[0, :, cs].astype(F32)[0:8] * keep_next
        ext = jnp.concatenate([xp, xc, xn], axis=0)
        acc = jnp.zeros((ts, LANES), F32)
        for j in range(CONV_W):
            off = 8 - halo + j
            acc = acc + ext[off:off + ts, :] * cw_ref[j:j + 1, cs]
        y = acc * _sigmoid(acc)
        if c < 2 * B_QK // LANES:
            y = y * lax.rsqrt(jnp.sum(y * y, axis=1, keepdims=True) + 1e-6)
        if c < B_QK // LANES:
            q_ref[0, :, cs] = y * (B_DK ** -0.5)
        elif c < 2 * B_QK // LANES:
            k_ref[0, :, c * LANES - B_QK:(c + 1) * LANES - B_QK] = y
        else:
            v_ref[0, :, c * LANES - 2 * B_QK:(c + 1) * LANES - 2 * B_QK] = y
    x = bab_ref[0]
    neg_a = gp_ref[0:1, :]
    dtb = gp_ref[1:2, :]
    is_g = gp_ref[2:3, :] > 0.5
    z = x + dtb
    softplus = jnp.maximum(z, 0.0) + jnp.log(1.0 + jnp.exp(-jnp.abs(z)))
    g_ref[0] = jnp.where(is_g, neg_a * softplus, _sigmoid(x))


def _dn_prep(u_b, conv_w, bab, gate_params):
    bsz, seq, _ = u_b.shape
    ts = min(256, seq)
    cq = 2 * B_QK + B_VZ
    nhalo = seq // 16
    per = ts // 16
    kern = functools.partial(_dn_prep_kernel, ts=ts)
    return pl.pallas_call(
        kern,
        grid=(bsz, seq // ts),
        in_specs=[pl.BlockSpec((1, 16, cq), lambda b, t: (b, jnp.maximum(t * per - 1, 0), 0)),
                  pl.BlockSpec((1, ts, cq), lambda b, t: (b, t, 0)),
                  pl.BlockSpec((1, 16, cq), lambda b, t: (b, jnp.minimum((t + 1) * per, nhalo - 1), 0)),
                  pl.BlockSpec((8, cq), lambda b, t: (0, 0)),
                  pl.BlockSpec((1, ts, LANES), lambda b, t: (b, t, 0)),
                  pl.BlockSpec((8, LANES), lambda b, t: (0, 0))],
        out_specs=[pl.BlockSpec((1, ts, B_QK), lambda b, t: (b, t, 0)),
                   pl.BlockSpec((1, ts, B_QK), lambda b, t: (b, t, 0)),
                   pl.BlockSpec((1, ts, B_VZ), lambda b, t: (b, t, 0)),
                   pl.BlockSpec((1, ts, LANES), lambda b, t: (b, t, 0))],
        out_shape=[jax.ShapeDtypeStruct((bsz, seq, B_QK), F32),
                   jax.ShapeDtypeStruct((bsz, seq, B_QK), F32),
                   jax.ShapeDtypeStruct((bsz, seq, B_VZ), F32),
                   jax.ShapeDtypeStruct((bsz, seq, LANES), F32)],
        compiler_params=_cparams(("parallel", "parallel")),
        name="dn_prep",
    )(u_b, u_b, u_b, conv_w, bab, gate_params)


def _dn_intra_kernel(q_ref, k_ref, v_ref, g_ref, gt_ref, u_ref, wq_ref, kq_ref, et_ref, *, c, per):
    n_units = 2 * DN_PAIR
    w4 = n_units * c
    hp = lax.Precision.HIGHEST
    bf = lambda t_: t_.astype(BF16)
    ii = lax.broadcasted_iota(jnp.int32, (c, w4), 0)
    ll = lax.broadcasted_iota(jnp.int32, (c, w4), 1)
    jj = ll % c
    ub = ll // c
    ub_row = ub[0:1, :]
    lo = jnp.where(ub >= DN_PAIR, jj - ii, ii - jj)
    incl = lo >= 0
    strict = lo > 0
    eye = (ii == jj).astype(F32)
    blk = (ii // TRI_BASE) == (jj // TRI_BASE)

    def pack(parts, sel):
        out = parts[n_units - 1]
        for u_ in range(n_units - 2, -1, -1):
            out = jnp.where(sel == u_, parts[u_], out)
        return out

    unit_mask = [jnp.where(ub == u_, 1.0, 0.0).astype(BF16) for u_ in range(n_units)]

    def block_diag(y16):
        return jnp.concatenate([y16 * m_ for m_ in unit_mask], axis=0)

    def mm(xs, ys):
        return [_dot(bf(x_), block_diag(bf(y_))) for x_, y_ in zip(xs, ys)]

    ri = lax.broadcasted_iota(jnp.int32, (c, c), 0)
    ci = lax.broadcasted_iota(jnp.int32, (c, c), 1)
    tri4 = (lo <= 0).astype(F32)
    tri_f = (ci <= ri).astype(F32)
    tri_b = (ci >= ri).astype(F32)
    items = [(cc, p) for cc in range(per) for p in range(B_V_HEADS // DN_PAIR)]
    qs_, ks_, lms, qkms, betas, egcs, kscales, etots = [], [], [], [], [], [], [], []
    for cc, p in items:
        rows_cc = slice(cc * c, (cc + 1) * c)
        if p == 0:
            g_all = g_ref[0, rows_cc, :]
            gc_dir = [jnp.dot(tri_f, g_all, precision=hp, preferred_element_type=F32),
                      jnp.dot(tri_b, g_all, precision=hp, preferred_element_type=F32)]
            gcr_all = jnp.dot(gt_ref[0, cc], tri4, precision=hp, preferred_element_type=F32)
            tot_all = jnp.sum(g_all, axis=0, keepdims=True)
        cs = slice(p * B_DK, (p + 1) * B_DK)
        q = q_ref[0, rows_cc, cs]
        k = k_ref[0, rows_cc, cs]
        k16 = bf(k)
        k4 = jnp.concatenate([k16] * n_units, axis=0)
        gram = _dot_nt(k16, k4)
        qk = _dot_nt(bf(q), k4)
        beta_u, gc_u, gcr_u, tot_u = [], [], [], []
        for u_ in range(n_units):
            d_, e_ = divmod(u_, DN_PAIR)
            h = p * DN_PAIR + e_
            bl = d_ * 2 * B_V_HEADS + h
            gl = bl + B_V_HEADS
            beta_u.append(g_all[:, bl:bl + 1])
            gc_u.append(gc_dir[d_][:, gl:gl + 1])
            gcr_u.append(gcr_all[gl:gl + 1, :])
            tot_u.append(tot_all[:, gl:gl + 1])
        gc_p = pack(gc_u, ub)
        gcr_p = pack(gcr_u, ub_row)
        tot_p = pack(tot_u, ub_row)
        dec = jnp.where(incl, jnp.exp(jnp.where(incl, gc_p - gcr_p, 0.0)), 0.0)
        lms.append(jnp.where(strict, pack(beta_u, ub) * gram * dec, 0.0))
        qkms.append(jnp.where(incl, qk * dec, 0.0))
        qs_.append(q)
        ks_.append(k)
        betas.append(beta_u)
        egcs.append([jnp.exp(g_) for g_ in gc_u])
        kscales.append(jnp.exp(tot_p - gcr_p))
        etots.append([jnp.exp(t_) for t_ in tot_u])

    d1 = [jnp.where(blk, lm, 0.0) for lm in lms]
    d2 = mm(d1, d1)
    d4 = mm(d2, d2)
    d8 = mm(d4, d4)
    tm_ = [eye - d_ for d_ in d1]
    for dk in (d2, d4, d8):
        tm_ = [a + b for a, b in zip(tm_, mm(tm_, dk))]
    size = TRI_BASE
    while size < c:
        off = ((ii // (2 * size)) == (jj // (2 * size))) & ((ii // size) != (jj // size))
        cm = [jnp.where(off, lm, 0.0) for lm in lms]
        pc = mm(tm_, cm)
        tm_ = [a - b for a, b in zip(tm_, mm(pc, tm_))]
        size *= 2

    lane2 = lax.broadcasted_iota(jnp.int32, (1, DN_PAIR * B_DV), 1)
    for it, (cc, p) in enumerate(items):
        q, k = qs_[it], ks_[it]
        rows = []
        for u_ in range(n_units):
            d_, e_ = divmod(u_, DN_PAIR)
            h = p * DN_PAIR + e_
            v = v_ref[0, cc * c:(cc + 1) * c, h * B_DV:(h + 1) * B_DV]
            beta = betas[it][u_]
            rows.append(jnp.concatenate([v * beta, k * (beta * egcs[it][u_])], axis=1))
        uw = _dot(block_diag(bf(tm_[it])), bf(jnp.concatenate(rows, axis=0)))
        k_t = jnp.transpose(jnp.concatenate([k] * DN_PAIR, axis=0))
        for d_ in range(2):
            u0, u1 = d_ * DN_PAIR, d_ * DN_PAIR + 1
            u_ref[d_, 0, cc, p] = jnp.concatenate([uw[u0 * c:(u0 + 1) * c, :B_DV], uw[u1 * c:(u1 + 1) * c, :B_DV]], axis=1)
            wq = jnp.concatenate([uw[u0 * c:(u0 + 1) * c, B_DV:], q * egcs[it][u0],
                                  uw[u1 * c:(u1 + 1) * c, B_DV:], q * egcs[it][u1]], axis=0)
            wq_ref[d_, 0, cc, p] = bf(wq)
            ls = slice(d_ * DN_PAIR * c, (d_ + 1) * DN_PAIR * c)
            kq = jnp.concatenate([k_t * kscales[it][:, ls], qkms[it][:, ls]], axis=0)
            kq_ref[d_, 0, cc, p] = bf(kq)
            et_ref[d_, 0, cc, p:p + 1, :] = jnp.where(lane2 < B_DV, etots[it][u0], etots[it][u1])


def _dn_intra(qn, kn, vn, gts, gts_t):
    bsz, seq, _ = qn.shape
    c = DN_CHUNK
    assert DN_PAIR * c == B_DK and seq % c == 0
    n = seq // c
    npair = B_V_HEADS // DN_PAIR
    wide = DN_PAIR * B_DV
    per = DN_INTRA_CHUNKS if n % DN_INTRA_CHUNKS == 0 else 1
    kern = functools.partial(_dn_intra_kernel, c=c, per=per)
    out5 = lambda r, cdim: pl.BlockSpec((2, 1, per, npair, r, cdim), lambda b, i: (0, b, i, 0, 0, 0))
    return pl.pallas_call(
        kern,
        grid=(bsz, n // per),
        in_specs=[pl.BlockSpec((1, per * c, B_QK), lambda b, i: (b, i, 0)),
                  pl.BlockSpec((1, per * c, B_QK), lambda b, i: (b, i, 0)),
                  pl.BlockSpec((1, per * c, B_VZ), lambda b, i: (b, i, 0)),
                  pl.BlockSpec((1, per * c, LANES), lambda b, i: (b, i, 0)),
                  pl.BlockSpec((1, per, LANES, c), lambda b, i: (b, i, 0, 0))],
        out_specs=[out5(c, wide), out5(2 * DN_PAIR * c, B_DK), out5(B_DK + c, DN_PAIR * c),
                   pl.BlockSpec((2, 1, per, npair, wide), lambda b, i: (0, b, i, 0, 0))],
        out_shape=[jax.ShapeDtypeStruct((2, bsz, n, npair, c, wide), F32),
                   jax.ShapeDtypeStruct((2, bsz, n, npair, 2 * DN_PAIR * c, B_DK), BF16),
                   jax.ShapeDtypeStruct((2, bsz, n, npair, B_DK + c, DN_PAIR * c), BF16),
                   jax.ShapeDtypeStruct((2, bsz, n, npair, wide), F32)],
        compiler_params=_cparams(("parallel", "parallel")),
        name="dn_intra",
    )(qn, kn, vn, gts, gts_t)


def _dn_state_kernel(uf_ref, wqf_ref, kqf_ref, etf_ref, ub_ref, wqb_ref, kqb_ref, etb_ref, of_ref, ob_ref, s_ref,
                     *, c, per):
    @pl.when(pl.program_id(1) == 0)
    def _():
        s_ref[...] = jnp.zeros_like(s_ref)

    npair = B_V_HEADS // DN_PAIR
    bf = lambda t_: t_.astype(BF16)
    chains = [(d_, p) for d_ in range(2) for p in range(npair)]
    refs = ((uf_ref, wqf_ref, kqf_ref, etf_ref, of_ref), (ub_ref, wqb_ref, kqb_ref, etb_ref, ob_ref))
    zero = jnp.zeros((c, B_DV), F32)
    for step in range(per):
        local = (step, per - 1 - step)
        states = [s_ref[d_, p] for d_, p in chains]
        a_res = [_dot(refs[d_][1][0, 0, local[d_], p], bf(s_)) for (d_, p), s_ in zip(chains, states)]
        b_res = []
        for (d_, p), a_ in zip(chains, a_res):
            u = refs[d_][0][0, 0, local[d_], p]
            v0 = u[:, :B_DV] - a_[0:c, :B_DV]
            v1 = u[:, B_DV:] - a_[2 * c:3 * c, B_DV:]
            bd_v = jnp.concatenate([jnp.concatenate([v0, zero], axis=1), jnp.concatenate([zero, v1], axis=1)], axis=0)
            b_res.append(_dot(refs[d_][2][0, 0, local[d_], p], bf(bd_v)))
        for (d_, p), a_, b_, s_ in zip(chains, a_res, b_res, states):
            s_ref[d_, p] = s_ * refs[d_][3][0, 0, local[d_], p:p + 1, :] + b_[:B_DK]
            o_ref = refs[d_][4]
            rs = slice(local[d_] * c, (local[d_] + 1) * c)
            o_ref[0, rs, (2 * p) * B_DV:(2 * p + 1) * B_DV] = a_[c:2 * c, :B_DV] + b_[B_DK:, :B_DV]
            o_ref[0, rs, (2 * p + 1) * B_DV:(2 * p + 2) * B_DV] = a_[3 * c:4 * c, B_DV:] + b_[B_DK:, B_DV:]


def _dn_state(u_all, wq_all, kq_all, et_all, seq):
    _, bsz, n, npair, c, wide = u_all.shape
    per = DN_STATE_CHUNKS if n % DN_STATE_CHUNKS == 0 else 1
    nb = n // per
    kern = functools.partial(_dn_state_kernel, c=c, per=per)
    fwd = lambda r, cdim: pl.BlockSpec((1, 1, per, npair, r, cdim), lambda b, i: (0, b, i, 0, 0, 0))
    bwd = lambda r, cdim: pl.BlockSpec((1, 1, per, npair, r, cdim), lambda b, i: (1, b, nb - 1 - i, 0, 0, 0))
    et_f = pl.BlockSpec((1, 1, per, npair, wide), lambda b, i: (0, b, i, 0, 0))
    et_b = pl.BlockSpec((1, 1, per, npair, wide), lambda b, i: (1, b, nb - 1 - i, 0, 0))
    shapes = ((c, wide), (2 * DN_PAIR * c, B_DK), (B_DK + c, DN_PAIR * c))
    return pl.pallas_call(
        kern,
        grid=(bsz, nb),
        in_specs=[fwd(*shapes[0]), fwd(*shapes[1]), fwd(*shapes[2]), et_f,
                  bwd(*shapes[0]), bwd(*shapes[1]), bwd(*shapes[2]), et_b],
        out_specs=[pl.BlockSpec((1, per * c, B_VZ), lambda b, i: (b, i, 0)),
                   pl.BlockSpec((1, per * c, B_VZ), lambda b, i: (b, nb - 1 - i, 0))],
        out_shape=[jax.ShapeDtypeStruct((bsz, seq, B_VZ), F32), jax.ShapeDtypeStruct((bsz, seq, B_VZ), F32)],
        scratch_shapes=[pltpu.VMEM((2, npair, B_DK, wide), F32)],
        compiler_params=_cparams(("parallel", "arbitrary")),
        name="dn_state",
    )(u_all, wq_all, kq_all, et_all, u_all, wq_all, kq_all, et_all)


def _branch_a_kernel(o0_ref, o1_ref, o2_ref, l0_ref, l1_ref, l2_ref, w_ref, gp_ref, bg_ref, out_ref, l_sc,
                     *, dils, tm):
    tok = lax.broadcasted_iota(jnp.int32, (tm, tm), 0)
    src = lax.broadcasted_iota(jnp.int32, (tm, tm), 1)
    o_tok = []
    for g, (o_ref, l_ref) in enumerate(((o0_ref, l0_ref), (o1_ref, l1_ref), (o2_ref, l2_ref))):
        r = dils[g]
        rows = tm // r
        for c in range(r):
            l_sc[g, pl.ds(c, rows, stride=r), :] = l_ref[c]
        if r == 1:
            o_tok.append(o_ref[0].astype(F32))
        else:
            perm = jnp.where(tok == (src % rows) * r + src // rows, 1.0, 0.0).astype(BF16)
            o_tok.append(_dot(perm, jnp.concatenate([o_ref[c] for c in range(r)], axis=0)))
    ls = [l_sc[g] for g in range(N_GROUPS)]
    m = jnp.maximum(jnp.maximum(ls[0], ls[1]), ls[2])
    es = [jnp.exp(l - m) for l in ls]
    den = es[0] + es[1] + es[2]
    ws = [e / den for e in es]
    parts = []
    for h in range(A_HEADS):
        hs = slice(h * HEAD_DIM, (h + 1) * HEAD_DIM)
        acc = ws[0][:, h:h + 1] * o_tok[0][:, hs]
        acc = acc + ws[1][:, h:h + 1] * o_tok[1][:, hs]
        acc = acc + ws[2][:, h:h + 1] * o_tok[2][:, hs]
        parts.append(acc.astype(BF16))
    oa = jnp.concatenate(parts, axis=1)
    y = _dot(oa, w_ref[...])
    out_ref[...] = _sigmoid(gp_ref[...] + bg_ref[...]) * y


def _branch_a(outs, lses, w_a, gpre, b_gate, tm):
    bsz = outs[0].shape[0]
    dils = tuple(o.shape[1] for o in outs)
    seq = dils[0] * outs[0].shape[2]
    t = bsz * seq
    per_seq = seq // tm
    wd = A_HEADS * HEAD_DIM
    cls = lambda r, width: pl.BlockSpec((None, r, tm // r, width), lambda i: (i // per_seq, 0, i % per_seq, 0))
    kern = functools.partial(_branch_a_kernel, dils=dils, tm=tm)
    return pl.pallas_call(
        kern,
        grid=(t // tm,),
        in_specs=[cls(dils[0], wd), cls(dils[1], wd), cls(dils[2], wd),
                  cls(dils[0], LANES), cls(dils[1], LANES), cls(dils[2], LANES),
                  pl.BlockSpec((wd, D_MODEL), lambda i: (0, 0)),
                  pl.BlockSpec((tm, D_MODEL), lambda i: (i, 0)),
                  pl.BlockSpec((1, D_MODEL), lambda i: (0, 0))],
        out_specs=pl.BlockSpec((tm, D_MODEL), lambda i: (i, 0)),
        out_shape=jax.ShapeDtypeStruct((t, D_MODEL), F32),
        scratch_shapes=[pltpu.VMEM((N_GROUPS, tm, LANES), F32)],
        compiler_params=_cparams(("parallel",)),
        name="branch_a",
    )(*outs, *lses, w_a, gpre, b_gate)


def _branch_b_kernel(of_ref, ob_ref, z_ref, nw_ref, w_ref, gp_ref, bg_ref, a_ref, out_ref):
    nw = nw_ref[...]
    parts = []
    for h in range(B_V_HEADS):
        hs = slice(h * B_DV, (h + 1) * B_DV)
        o = of_ref[:, hs] + ob_ref[:, hs]
        z = z_ref[:, hs].astype(F32)
        o = o * lax.rsqrt(jnp.mean(o * o, axis=1, keepdims=True) + RMS_EPS) * nw * (z * _sigmoid(z))
        parts.append(o.astype(BF16))
    ob = jnp.concatenate(parts, axis=1)
    y = _dot(ob, w_ref[...])
    out_ref[...] = (a_ref[...] + _sigmoid(gp_ref[...] + bg_ref[...]) * y).astype(out_ref.dtype)


def _branch_b(o_f, o_b, u_b2d, norm_w, w_b, gpre, b_gate, a_part, tm):
    t = a_part.shape[0]
    return pl.pallas_call(
        _branch_b_kernel,
        grid=(t // tm,),
        in_specs=[pl.BlockSpec((tm, B_VZ), lambda i: (i, 0)),
                  pl.BlockSpec((tm, B_VZ), lambda i: (i, 0)),
                  pl.BlockSpec((tm, B_VZ), lambda i: (i, (2 * B_QK + B_VZ) // B_VZ)),
                  pl.BlockSpec((1, B_DV), lambda i: (0, 0)),
                  pl.BlockSpec((B_VZ, D_MODEL), lambda i: (0, 0)),
                  pl.BlockSpec((tm, D_MODEL), lambda i: (i, 1)),
                  pl.BlockSpec((1, D_MODEL), lambda i: (0, 1)),
                  pl.BlockSpec((tm, D_MODEL), lambda i: (i, 0))],
        out_specs=pl.BlockSpec((tm, D_MODEL), lambda i: (i, 0)),
        out_shape=jax.ShapeDtypeStruct((t, D_MODEL), BF16),
        compiler_params=_cparams(("parallel",)),
        name="branch_b",
    )(o_f, o_b, u_b2d, norm_w, w_b, gpre, b_gate, a_part)


def _layer_norm(y, g, b):
    mu = jnp.mean(y, axis=1, keepdims=True)
    yc = y - mu
    var = jnp.mean(yc * yc, axis=1, keepdims=True)
    return yc * lax.rsqrt(var + LN_EPS) * g + b


def _out_ln_kernel(m_ref, w_ref, x_ref, g_ref, b_ref, out_ref):
    mix = _dot(m_ref[...], w_ref[...])
    out_ref[...] = _layer_norm(DEEPNORM_ALPHA * x_ref[...] + mix, g_ref[...], b_ref[...])


def _out_ln(merged, w_out, x2d, ln_g, ln_b, tm):
    t = merged.shape[0]
    vec = pl.BlockSpec((1, D_MODEL), lambda i: (0, 0))
    return pl.pallas_call(
        _out_ln_kernel,
        grid=(t // tm,),
        in_specs=[pl.BlockSpec((tm, D_MODEL), lambda i: (i, 0)),
                  pl.BlockSpec((D_MODEL, D_MODEL), lambda i: (0, 0)),
                  pl.BlockSpec((tm, D_MODEL), lambda i: (i, 0)), vec, vec],
        out_specs=pl.BlockSpec((tm, D_MODEL), lambda i: (i, 0)),
        out_shape=jax.ShapeDtypeStruct((t, D_MODEL), F32),
        compiler_params=_cparams(("parallel",)),
        name="out_proj_ln1",
    )(merged, w_out, x2d, ln_g, ln_b)


def _router_kernel(x_ref, rwh_ref, rwl_ref, rb_ref, idx_ref, gate_ref, rank_ref, cnt_ref, carry, *, tm):
    @pl.when(pl.program_id(0) == 0)
    def _():
        carry[...] = jnp.zeros_like(carry)

    lane = lax.broadcasted_iota(jnp.int32, (tm, LANES), 1)
    lane_f = lane.astype(F32)
    x = x_ref[...]
    x_hi = x.astype(BF16)
    x_lo = (x - x_hi.astype(F32)).astype(BF16)
    logits = _dot(x_hi, rwh_ref[...]) + (_dot(x_lo, rwh_ref[...]) + _dot(x_hi, rwl_ref[...])) + rb_ref[...]
    cur = jnp.where(lane < N_EXPERTS, logits, -jnp.inf)
    vals, idxs = [], []
    for _k in range(TOP_K):
        m = jnp.max(cur, axis=1, keepdims=True)
        idx = jnp.min(jnp.where(cur == m, lane_f, float(LANES)), axis=1, keepdims=True).astype(jnp.int32)
        vals.append(m)
        idxs.append(idx)
        cur = jnp.where(lane == idx, -jnp.inf, cur)
    es = [jnp.exp(v - vals[0]) for v in vals]
    den = es[0] + es[1] + es[2] + es[3]
    onehot = jnp.zeros((tm, LANES), F32)
    for idx in idxs:
        onehot = onehot + (lane == idx).astype(F32)
    ri = lax.broadcasted_iota(jnp.int32, (tm, tm), 0)
    ci = lax.broadcasted_iota(jnp.int32, (tm, tm), 1)
    before = (ci < ri).astype(BF16)
    prefix = _dot(before, onehot.astype(BF16)) + carry[0:1, :]
    idx_out = jnp.zeros((tm, LANES), jnp.int32)
    gate_out = jnp.zeros((tm, LANES), F32)
    rank_out = jnp.zeros((tm, LANES), jnp.int32)
    for k in range(TOP_K):
        rk = jnp.sum(jnp.where(lane == idxs[k], prefix, 0.0), axis=1, keepdims=True)
        idx_out = jnp.where(lane == k, idxs[k], idx_out)
        gate_out = jnp.where(lane == k, es[k] / den, gate_out)
        rank_out = jnp.where(lane == k, rk.astype(jnp.int32), rank_out)
    idx_ref[...] = idx_out
    gate_ref[...] = gate_out
    rank_ref[...] = rank_out
    total = carry[0:1, :] + jnp.sum(onehot, axis=0, keepdims=True)
    carry[...] = jnp.broadcast_to(total, carry.shape)
    cnt_ref[...] = jnp.broadcast_to(total, cnt_ref.shape)


def _router(x1, rw, rb, tm):
    rw_hi = rw.astype(BF16)
    rw_lo = (rw - rw_hi.astype(F32)).astype(BF16)
    t = x1.shape[0]
    row = pl.BlockSpec((tm, LANES), lambda i: (i, 0))
    kern = functools.partial(_router_kernel, tm=tm)
    return pl.pallas_call(
        kern,
        grid=(t // tm,),
        in_specs=[pl.BlockSpec((tm, D_MODEL), lambda i: (i, 0)),
                  pl.BlockSpec((D_MODEL, LANES), lambda i: (0, 0)),
                  pl.BlockSpec((D_MODEL, LANES), lambda i: (0, 0)),
                  pl.BlockSpec((1, LANES), lambda i: (0, 0))],
        out_specs=[row, row, row, pl.BlockSpec((8, LANES), lambda i: (0, 0))],
        out_shape=[jax.ShapeDtypeStruct((t, LANES), jnp.int32),
                   jax.ShapeDtypeStruct((t, LANES), F32),
                   jax.ShapeDtypeStruct((t, LANES), jnp.int32),
                   jax.ShapeDtypeStruct((8, LANES), F32)],
        scratch_shapes=[pltpu.VMEM((8, LANES), F32)],
        compiler_params=_cparams(("arbitrary",)),
        name="router",
    )(x1, rw_hi, rw_lo, rb)


def _pack_rows(src_ref, dst_ref):
    half = D_MODEL // 2
    n = src_ref.shape[0]
    for s in range(ROW_SUBLANES):
        lo = pltpu.bitcast(src_ref[:, s * LANES:(s + 1) * LANES].astype(BF16).astype(F32), jnp.uint32)
        hi = pltpu.bitcast(src_ref[:, half + s * LANES:half + (s + 1) * LANES].astype(BF16).astype(F32), jnp.uint32)
        dst_ref[pl.ds(s, n, stride=ROW_SUBLANES), :] = (lo >> 16) | hi


def _unpack_words(w):
    return pltpu.bitcast(w << 16, F32), pltpu.bitcast(w & jnp.uint32(0xFFFF0000), F32)


def _tile(ref, i):
    start = i * ROW_SUBLANES
    if not isinstance(i, int):
        start = pl.multiple_of(start, ROW_SUBLANES)
    return ref.at[pl.ds(start, ROW_SUBLANES)]


def _tile_copy(src, i, dst, j, sem):
    return pltpu.make_async_copy(_tile(src, i), _tile(dst, j), sem)


def _wait_tiles(hbm, n, sem):
    pltpu.make_async_copy(hbm.at[pl.ds(0, n * ROW_SUBLANES)], hbm.at[pl.ds(0, n * ROW_SUBLANES)], sem).wait()


def _dispatch_kernel(pad_start_ref, pad_cnt_ref, dest_ref, x_ref, xs_hbm, stage, zero, sem, *, tm, nt):
    i = pl.program_id(0)

    @pl.when(i < nt)
    def _():
        _pack_rows(x_ref, stage)

        def issue(t_, carry):
            src = _tile(stage, t_)
            for k in range(TOP_K):
                pltpu.make_async_copy(src, _tile(xs_hbm, dest_ref[0, 0, t_ * TOP_K + k]), sem).start()
            return carry

        lax.fori_loop(0, tm, issue, 0, unroll=2)
        _wait_tiles(xs_hbm, tm * TOP_K, sem)

    @pl.when((i >= nt) & (i < nt + N_EXPERTS))
    def _():
        e = i - nt
        zero[...] = jnp.zeros_like(zero)
        start = pad_start_ref[e]
        cnt = pad_cnt_ref[e]

        def issue(r, carry):
            _tile_copy(zero, 0, xs_hbm, start + r, sem).start()
            return carry

        lax.fori_loop(0, cnt, issue, 0)

        def wait(r, carry):
            _tile_copy(zero, 0, xs_hbm, start, sem).wait()
            return carry

        lax.fori_loop(0, cnt, wait, 0)

    @pl.when(i == nt + N_EXPERTS)
    def _():
        stage[...] = jnp.zeros_like(stage)
        start = pad_start_ref[N_EXPERTS]

        def fill(j, carry):
            piece = xs_hbm.at[pl.ds(pl.multiple_of((start + j * tm) * ROW_SUBLANES, ROW_SUBLANES), tm * ROW_SUBLANES)]
            cp = pltpu.make_async_copy(stage, piece, sem)
            cp.start()
            cp.wait()
            return carry

        lax.fori_loop(0, pad_cnt_ref[N_EXPERTS] // tm, fill, 0)


def _dispatch(x1, dest, pad_from, pad_cnt, n_pad, tm):
    t = x1.shape[0]
    nt = t // tm
    assert MOE_BLOCK % tm == 0 and pad_from.shape[0] == N_EXPERTS + 1
    kern = functools.partial(_dispatch_kernel, tm=tm, nt=nt)
    grid_spec = pltpu.PrefetchScalarGridSpec(
        num_scalar_prefetch=2,
        grid=(nt + pad_from.shape[0],),
        in_specs=[pl.BlockSpec((1, 1, tm * TOP_K), lambda i, ps, pc: (jnp.minimum(i, nt - 1), 0, 0),
                               memory_space=pltpu.SMEM),
                  pl.BlockSpec((tm, D_MODEL), lambda i, ps, pc: (jnp.minimum(i, nt - 1), 0))],
        out_specs=pl.BlockSpec(memory_space=pl.ANY),
        scratch_shapes=[pltpu.VMEM((tm * ROW_SUBLANES, LANES), jnp.uint32),
                        pltpu.VMEM((ROW_SUBLANES, LANES), jnp.uint32),
                        pltpu.SemaphoreType.DMA(())],
    )
    return pl.pallas_call(
        kern,
        grid_spec=grid_spec,
        out_shape=jax.ShapeDtypeStruct((n_pad * ROW_SUBLANES, LANES), jnp.uint32),
        compiler_params=_cparams(("arbitrary",)),
        name="moe_dispatch",
    )(pad_from, pad_cnt, dest.reshape(nt, 1, tm * TOP_K), x1)


GU_GROUP = 2 * LANES


def _regroup_kernel(w_ref, o_ref):
    ri = lax.broadcasted_iota(jnp.int32, (GU_GROUP, GU_GROUP), 0)
    ci = lax.broadcasted_iota(jnp.int32, (GU_GROUP, GU_GROUP), 1)
    src = jnp.where(ci < LANES, 2 * ci, 2 * (ci - LANES) + 1)
    perm = jnp.where(ri == src, 1.0, 0.0).astype(BF16)
    for c in range(w_ref.shape[2] // GU_GROUP):
        cs = slice(c * GU_GROUP, (c + 1) * GU_GROUP)
        o_ref[0, :, cs] = _dot(w_ref[0, :, cs].astype(BF16), perm).astype(BF16)


def _regroup_gate_up(w_gate_up):
    e, d, n2 = w_gate_up.shape
    rt = 512
    return pl.pallas_call(
        _regroup_kernel,
        grid=(e, d // rt),
        in_specs=[pl.BlockSpec((1, rt, n2), lambda i, j: (i, j, 0))],
        out_specs=pl.BlockSpec((1, rt, n2), lambda i, j: (i, j, 0)),
        out_shape=jax.ShapeDtypeStruct((e, d, n2), BF16),
        compiler_params=_cparams(("parallel", "parallel")),
        name="regroup_gate_up",
    )(w_gate_up)

def _ffn_kernel(be_ref, nused_ref, rows_ref, x_ref, wgu_ref, bgu_ref, wd_ref, bd_ref, o_ref, acc, xb):
    i = pl.program_id(0)
    f = pl.program_id(1)
    nf = pl.num_programs(1)
    half = D_MODEL // 2

    @pl.when(f == 0)
    def _():
        acc[...] = jnp.broadcast_to(bd_ref[0], acc.shape)
        for s in range(ROW_SUBLANES):
            lo, hi = _unpack_words(x_ref[pl.ds(s, MOE_BLOCK, stride=ROW_SUBLANES), :])
            xb[:, s * LANES:(s + 1) * LANES] = lo.astype(BF16)
            xb[:, half + s * LANES:half + (s + 1) * LANES] = hi.astype(BF16)

    def compute(nrows):
        gu = _dot(xb[0:nrows, :], wgu_ref[0]) + bgu_ref[0]
        acts = []
        for m in range(MOE_TF // LANES):
            gate = jnp.minimum(gu[:, m * GU_GROUP:m * GU_GROUP + LANES], SWIGLU_LIMIT)
            up = jnp.clip(gu[:, m * GU_GROUP + LANES:(m + 1) * GU_GROUP], -SWIGLU_LIMIT, SWIGLU_LIMIT)
            acts.append(((up + 1.0) * gate * _sigmoid(gate * SWIGLU_ALPHA)).astype(BF16))
        acc[0:nrows, :] += _dot(jnp.concatenate(acts, axis=1), wd_ref[0].astype(BF16))

    rows = rows_ref[i]
    pl.when(rows > MOE_BLOCK // 2)(lambda: compute(MOE_BLOCK))
    pl.when((rows > MOE_BLOCK // 4) & (rows <= MOE_BLOCK // 2))(lambda: compute(MOE_BLOCK // 2))
    pl.when((rows > 0) & (rows <= MOE_BLOCK // 4))(lambda: compute(MOE_BLOCK // 4))

    @pl.when(f == nf - 1)
    def _():
        _pack_rows(acc, o_ref)


def _expert_ffn(xs, block_expert, n_used, block_rows, wgu, bgu, wd, bd):
    n_pad = xs.shape[0] // ROW_SUBLANES
    nb = n_pad // MOE_BLOCK
    d_ff = wgu.shape[2] // 2
    nf = d_ff // MOE_TF
    row_blk = (MOE_BLOCK * ROW_SUBLANES, LANES)
    grid_spec = pltpu.PrefetchScalarGridSpec(
        num_scalar_prefetch=3,
        grid=(nb, nf),
        in_specs=[pl.BlockSpec(row_blk, lambda i, f, be, nu, rw: (jnp.minimum(i, nu[0] - 1), 0)),
                  pl.BlockSpec((1, D_MODEL, 2 * MOE_TF), lambda i, f, be, nu, rw: (be[i], 0, f)),
                  pl.BlockSpec((1, 1, 2 * MOE_TF), lambda i, f, be, nu, rw: (be[i], 0, f)),
                  pl.BlockSpec((1, MOE_TF, D_MODEL), lambda i, f, be, nu, rw: (be[i], f, 0)),
                  pl.BlockSpec((1, 1, D_MODEL), lambda i, f, be, nu, rw: (be[i], 0, 0))],
        out_specs=pl.BlockSpec(row_blk, lambda i, f, be, nu, rw: (i, 0)),
        scratch_shapes=[pltpu.VMEM((MOE_BLOCK, D_MODEL), F32), pltpu.VMEM((MOE_BLOCK, D_MODEL), BF16)],
    )
    return pl.pallas_call(
        _ffn_kernel,
        grid_spec=grid_spec,
        out_shape=jax.ShapeDtypeStruct((n_pad * ROW_SUBLANES, LANES), jnp.uint32),
        compiler_params=_cparams(("arbitrary", "arbitrary")),
        name="expert_ffn",
    )(block_expert, n_used, block_rows, xs, wgu, bgu, wd, bd)


def _final_kernel(dest_ref, gate_ref, x_ref, p_ref, wpg_ref, bpg_ref, wple_ref, g_ref, b_ref, ys_hbm, out_ref,
                  rows, sem, *, tm):
    def issue(t_, carry):
        for k in range(TOP_K):
            _tile_copy(ys_hbm, dest_ref[0, 0, t_ * TOP_K + k], rows, k * tm + t_, sem).start()
        return carry

    lax.fori_loop(0, tm, issue, 0, unroll=2)
    x = x_ref[...]
    pg = _dot(x.astype(BF16), wpg_ref[...]) + bpg_ref[...]
    ple = _sigmoid(pg) * _dot(p_ref[...].astype(BF16), wple_ref[...])
    _wait_tiles(ys_hbm, tm * TOP_K, sem)
    gates = gate_ref[...]
    lo_parts, hi_parts = [], []
    for s in range(ROW_SUBLANES):
        acc_lo = acc_hi = None
        for k in range(TOP_K):
            lo, hi = _unpack_words(rows[pl.ds(k * tm * ROW_SUBLANES + s, tm, stride=ROW_SUBLANES), :])
            g = gates[:, k:k + 1]
            acc_lo = g * lo if acc_lo is None else acc_lo + g * lo
            acc_hi = g * hi if acc_hi is None else acc_hi + g * hi
        lo_parts.append(acc_lo)
        hi_parts.append(acc_hi)
    y = jnp.concatenate(lo_parts + hi_parts, axis=1)
    out_ref[...] = _layer_norm(DEEPNORM_ALPHA * x + y + ple, g_ref[...], b_ref[...])


def _final(dest, gates, x1, p2d, w_pg, b_pg, w_ple, ln_g, ln_b, ys, tm):
    t = x1.shape[0]
    vec = pl.BlockSpec((1, D_MODEL), lambda i: (0, 0))
    kern = functools.partial(_final_kernel, tm=tm)
    return pl.pallas_call(
        kern,
        grid=(t // tm,),
        in_specs=[pl.BlockSpec((1, 1, tm * TOP_K), lambda i: (i, 0, 0), memory_space=pltpu.SMEM),
                  pl.BlockSpec((tm, LANES), lambda i: (i, 0)),
                  pl.BlockSpec((tm, D_MODEL), lambda i: (i, 0)),
                  pl.BlockSpec((tm, PLE_DIM), lambda i: (i, 0)),
                  pl.BlockSpec((D_MODEL, D_MODEL), lambda i: (0, 0)), vec,
                  pl.BlockSpec((PLE_DIM, D_MODEL), lambda i: (0, 0)), vec, vec,
                  pl.BlockSpec(memory_space=pl.ANY)],
        out_specs=pl.BlockSpec((tm, D_MODEL), lambda i: (i, 0)),
        out_shape=jax.ShapeDtypeStruct((t, D_MODEL), F32),
        scratch_shapes=[pltpu.VMEM((tm * TOP_K * ROW_SUBLANES, LANES), jnp.uint32), pltpu.SemaphoreType.DMA(())],
        compiler_params=_cparams(("arbitrary",)),
        name="combine_ple_ln2",
    )(dest.reshape(t // tm, 1, tm * TOP_K), gates, x1, p2d, w_pg, b_pg, w_ple, ln_g, ln_b, ys)


def _row_tile(t):
    return min(512, t)


def _layer(x, p, w_in, b_gate, conv_w, a_log, dt_bias, dn_norm_w, w_branch_a, w_branch_b, w_out, ln1_g, ln1_b,
           router_w, router_b, w_gate_up, b_gate_up, w_down, b_down, w_ple, w_ple_gate, b_ple_gate, ln2_g, ln2_b):
    bsz, seq, _ = x.shape
    t = bsz * seq
    tm = _row_tile(t)
    x2d = x.reshape(t, D_MODEL)
    xb = x2d.astype(BF16)

    c_a = 3 * A_QKV
    c_b = 2 * B_QK + 2 * B_VZ
    w_bf = w_in.astype(BF16)
    tp = min(IN_PROJ_TM, seq)
    u_b = _matmul(xb, w_bf[:, c_a:c_a + c_b], BF16, tp, IN_PROJ_TN)
    w_bab = jnp.pad(w_bf[:, c_a + c_b:c_a + c_b + B_GATES], ((0, 0), (0, LANES - B_GATES)))
    bab = _matmul(xb, w_bab, F32, tp, LANES)
    gpre = _matmul(xb, w_bf[:, c_a + c_b + B_GATES:], F32, tp, IN_PROJ_TN)

    slopes = _alibi_slopes()
    gw = A_HEADS * HEAD_DIM
    outs, lses = [], []
    for gi, (_win, dil) in enumerate(DILATION_GROUPS):
        w_g = jnp.concatenate([w_bf[:, part * A_QKV + gi * gw:part * A_QKV + (gi + 1) * gw] for part in range(3)], axis=1)
        o_g, l_g = _attention_group(_matmul_classes(xb, w_g, bsz, dil, tp, IN_PROJ_TN), gi, dil, slopes[gi])
        outs.append(o_g)
        lses.append(l_g)

    cw = jnp.pad(conv_w.astype(F32), ((0, 8 - CONV_W), (0, 0)))
    lane_is_g = (np.arange(LANES) % (2 * B_V_HEADS) >= B_V_HEADS) & (np.arange(LANES) < B_GATES)
    neg_a = jnp.zeros((LANES,), F32).at[B_V_HEADS:2 * B_V_HEADS].set(-jnp.exp(a_log[0].astype(F32)))
    neg_a = neg_a.at[3 * B_V_HEADS:4 * B_V_HEADS].set(-jnp.exp(a_log[1].astype(F32)))
    dtb = jnp.zeros((LANES,), F32).at[B_V_HEADS:2 * B_V_HEADS].set(dt_bias[0].astype(F32))
    dtb = dtb.at[3 * B_V_HEADS:4 * B_V_HEADS].set(dt_bias[1].astype(F32))
    gate_params = jnp.zeros((8, LANES), F32).at[0].set(neg_a).at[1].set(dtb).at[2].set(jnp.asarray(lane_is_g, F32))
    qn, kn, vn, gts = _dn_prep(u_b.reshape(bsz, seq, c_b), cw, bab.reshape(bsz, seq, LANES), gate_params)
    gts_t = gts.reshape(bsz, seq // DN_CHUNK, DN_CHUNK, LANES).transpose(0, 1, 3, 2)
    o_f, o_b = _dn_state(*_dn_intra(qn, kn, vn, gts, gts_t), seq)

    bg = b_gate.astype(F32).reshape(1, 2 * D_MODEL)
    a_part = _branch_a(outs, lses, w_branch_a.astype(BF16), gpre, bg, tm)
    merged = _branch_b(o_f.reshape(t, B_VZ), o_b.reshape(t, B_VZ), u_b, dn_norm_w.astype(F32).reshape(1, B_DV), w_branch_b.astype(BF16), gpre, bg,
                       a_part, tm)
    x1 = _out_ln(merged, w_out.astype(BF16), x2d, ln1_g.reshape(1, -1), ln1_b.reshape(1, -1), tm)

    rw = jnp.pad(router_w.astype(F32), ((0, 0), (0, LANES - N_EXPERTS)))
    rb = jnp.pad(router_b.astype(F32), (0, LANES - N_EXPERTS)).reshape(1, LANES)
    idx, gates, rank, cnt = _router(x1, rw, rb, tm)
    counts = cnt[0, :N_EXPERTS].astype(jnp.int32)
    padded = (counts + MOE_BLOCK - 1) // MOE_BLOCK * MOE_BLOCK
    pad_end = jnp.cumsum(padded)
    pad_start = pad_end - padded
    dest = pad_start[idx[:, :TOP_K]] + rank[:, :TOP_K]
    n_pad = t * TOP_K + N_EXPERTS * MOE_BLOCK
    nb = n_pad // MOE_BLOCK
    block_start = jnp.arange(nb, dtype=jnp.int32) * MOE_BLOCK
    block_expert = jnp.minimum(jnp.sum(pad_end[None, :] <= block_start[:, None], axis=1), N_EXPERTS - 1).astype(jnp.int32)
    n_used = (pad_end[-1:] // MOE_BLOCK).astype(jnp.int32)

    zero_from = jnp.concatenate([pad_start + counts, pad_end[-1:]]).astype(jnp.int32)
    zero_cnt = jnp.concatenate([padded - counts, n_pad - pad_end[-1:]]).astype(jnp.int32)
    xs = _dispatch(x1, dest, zero_from, zero_cnt, n_pad, tm)
    d_ff = w_down.shape[1]
    wgu = _regroup_gate_up(w_gate_up.astype(F32))
    bgu = b_gate_up.astype(F32).reshape(N_EXPERTS, d_ff // LANES, LANES, 2).transpose(0, 1, 3, 2)
    bgu = bgu.reshape(N_EXPERTS, 1, 2 * d_ff)
    seg_end = (pad_start + counts)[block_expert]
    block_rows = jnp.where(block_start < pad_end[-1], jnp.clip(seg_end - block_start, 0, MOE_BLOCK), 0).astype(jnp.int32)
    ys = _expert_ffn(xs, block_expert, n_used, block_rows, wgu, bgu, w_down.astype(F32),
                     b_down.astype(F32).reshape(N_EXPERTS, 1, D_MODEL))

    out = _final(dest, gates, x1, p.reshape(t, PLE_DIM), w_ple_gate.astype(BF16),
                 b_ple_gate.astype(F32).reshape(1, -1), w_ple.astype(BF16), ln2_g.reshape(1, -1),
                 ln2_b.reshape(1, -1), ys, min(COMBINE_TM, t))
    return out.reshape(bsz, seq, D_MODEL)


def kernel(x, p, w_in, b_gate, conv_w, a_log, dt_bias, dn_norm_w, w_branch_a, w_branch_b, w_out, ln1_g, ln1_b,
           router_w, router_b, w_gate_up, b_gate_up, w_down, b_down, w_ple, w_ple_gate, b_ple_gate, ln2_g, ln2_b):
    assert w_in.shape[0] == DEPTH
    return _layer(x, p[0], w_in[0], b_gate[0], conv_w[0], a_log[0], dt_bias[0], dn_norm_w[0], w_branch_a[0],
                  w_branch_b[0], w_out[0], ln1_g[0], ln1_b[0], router_w[0], router_b[0], w_gate_up[0],
                  b_gate_up[0], w_down[0], b_down[0], w_ple[0], w_ple_gate[0], b_ple_gate[0], ln2_g[0], ln2_b[0])
```

```python
import functools

import numpy as np
import jax
import jax.numpy as jnp
from jax import lax
from jax.experimental import pallas as pl
from jax.experimental.pallas import tpu as pltpu

F32 = jnp.float32
BF16 = jnp.bfloat16

D_MODEL = 2048
HEAD_DIM = 128
A_HEADS = 8
DILATION_GROUPS = ((128, 1), (512, 4), (2048, 16))
N_GROUPS = 3
NEG_INF = -1e30
B_QK_HEADS = 8
B_V_HEADS = 16
B_DK = 128
B_DV = 128
CONV_W = 5
RMS_EPS = 1e-6
N_EXPERTS = 32
TOP_K = 4
SWIGLU_ALPHA = 1.702
SWIGLU_LIMIT = 7.0
PLE_DIM = 256
DEPTH = 1
DEEPNORM_ALPHA = (2 * DEPTH) ** 0.25
LN_EPS = 1e-5
A_QKV = N_GROUPS * A_HEADS * HEAD_DIM
B_QK = B_QK_HEADS * B_DK
B_VZ = B_V_HEADS * B_DV
B_GATES = 4 * B_V_HEADS

LANES = 128
N_SIDE = 64
Q_SUB = 128
DN_CHUNK = 64
DN_PAIR = B_V_HEADS // B_QK_HEADS
DN_INTRA_CHUNKS = 2
DN_STATE_CHUNKS = 4
TRI_BASE = 16
ROW_SUBLANES = 8
IN_PROJ_TM = 1024
IN_PROJ_TN = 1024
MOE_BLOCK = 1024
MOE_TF = 512
COMBINE_TM = 256
VMEM_LIMIT = 56 * 1024 * 1024
assert D_MODEL == 2 * ROW_SUBLANES * LANES


def _cparams(sem):
    return pltpu.CompilerParams(dimension_semantics=sem, vmem_limit_bytes=VMEM_LIMIT)


def _sigmoid(x):
    return 1.0 / (1.0 + jnp.exp(-x))


def _dot(a, b):
    return jnp.dot(a, b, preferred_element_type=F32)


def _dot_nt(a, b):
    return lax.dot_general(a, b, (((1,), (1,)), ((), ())), preferred_element_type=F32)


def _dot_tn(a, b):
    return lax.dot_general(a, b, (((0,), (0,)), ((), ())), preferred_element_type=F32)


def _mm_kernel(x_ref, w_ref, o_ref):
    o_ref[...] = _dot(x_ref[...], w_ref[...]).astype(o_ref.dtype)


def _matmul(x, w, out_dtype, tm, tn):
    m, k = x.shape
    n = w.shape[1]
    return pl.pallas_call(
        _mm_kernel,
        grid=(m // tm, n // tn),
        in_specs=[pl.BlockSpec((tm, k), lambda i, j: (i, 0)),
                  pl.BlockSpec((k, tn), lambda i, j: (0, j))],
        out_specs=pl.BlockSpec((tm, tn), lambda i, j: (i, j)),
        out_shape=jax.ShapeDtypeStruct((m, n), out_dtype),
        compiler_params=_cparams(("parallel", "parallel")),
        name="in_proj",
    )(x, w)


def _mm_classes_kernel(x_ref, w_ref, o_ref, acc_ref, *, dil, rows):
    y = _dot(x_ref[...], w_ref[...])
    for j in range(acc_ref.shape[0]):
        ls = slice(j * LANES, (j + 1) * LANES)
        acc_ref[j] = y[:, ls]
        for c in range(dil):
            o_ref[0, c, :, ls] = acc_ref[j, pl.ds(c, rows, stride=dil), :].astype(o_ref.dtype)


def _matmul_classes(x, w, bsz, dil, tm, tn):
    m, k = x.shape
    n = w.shape[1]
    seq = m // bsz
    tiles_per_seq = seq // tm
    rows = tm // dil
    kern = functools.partial(_mm_classes_kernel, dil=dil, rows=rows)
    return pl.pallas_call(
        kern,
        grid=(m // tm, n // tn),
        in_specs=[pl.BlockSpec((tm, k), lambda i, j: (i, 0)),
                  pl.BlockSpec((k, tn), lambda i, j: (0, j))],
        out_specs=pl.BlockSpec((1, dil, rows, tn), lambda i, j: (i // tiles_per_seq, 0, i % tiles_per_seq, j)),
        out_shape=jax.ShapeDtypeStruct((bsz, dil, seq // dil, n), BF16),
        scratch_shapes=[pltpu.VMEM((tn // LANES, tm, LANES), F32)],
        compiler_params=_cparams(("parallel", "parallel")),
        name=f"in_proj_dil{dil}",
    )(x, w)


def _attn_kernel(q_ref, kp_ref, kc_ref, kn_ref, vp_ref, vc_ref, vn_ref, o_ref, lse_ref, kbuf, vbuf,
                 *, dil, sub_len, tl, slopes):
    i0 = pl.program_id(2) * tl
    kbuf[0:N_SIDE, :] = kp_ref[...]
    kbuf[N_SIDE:N_SIDE + tl, :] = kc_ref[...]
    kbuf[N_SIDE + tl:, :] = kn_ref[...]
    vbuf[0:N_SIDE, :] = vp_ref[...]
    vbuf[N_SIDE:N_SIDE + tl, :] = vc_ref[...]
    vbuf[N_SIDE + tl:, :] = vn_ref[...]
    span = Q_SUB + 2 * N_SIDE
    qq = lax.broadcasted_iota(jnp.int32, (Q_SUB, span), 0)
    kk = lax.broadcasted_iota(jnp.int32, (Q_SUB, span), 1)
    delta = kk - N_SIDE - qq
    absd = jnp.abs(delta)
    band = absd <= N_SIDE
    dist = (dil * absd).astype(F32)
    lane = lax.broadcasted_iota(jnp.int32, (Q_SUB, LANES), 1)
    scale = HEAD_DIM ** -0.5
    for j in range(tl // Q_SUB):
        pos = i0 + (j * Q_SUB - N_SIDE) + kk
        valid = band & (pos >= 0) & (pos < sub_len)
        lse_tile = jnp.zeros((Q_SUB, LANES), F32)
        for h in range(A_HEADS):
            hs = slice(h * HEAD_DIM, (h + 1) * HEAD_DIM)
            q = q_ref[j * Q_SUB:(j + 1) * Q_SUB, hs]
            k = kbuf[j * Q_SUB:j * Q_SUB + span, hs]
            v = vbuf[j * Q_SUB:j * Q_SUB + span, hs]
            s = _dot_nt(q, k) * scale
            s = jnp.where(valid, s - float(slopes[h]) * dist, NEG_INF)
            m = jnp.max(s, axis=1, keepdims=True)
            p = jnp.exp(s - m)
            l = jnp.sum(p, axis=1, keepdims=True)
            o = _dot(p.astype(BF16), v) / l
            o_ref[j * Q_SUB:(j + 1) * Q_SUB, hs] = o.astype(o_ref.dtype)
            lse_tile = jnp.where(lane == h, m + jnp.log(l), lse_tile)
        lse_ref[j * Q_SUB:(j + 1) * Q_SUB, :] = lse_tile


def _attention_group(qkv, gi, dil, slopes):
    bsz, _, sub_len, _ = qkv.shape
    tl = min(512, sub_len)
    assert sub_len % tl == 0 and tl % Q_SUB == 0 and sub_len % N_SIDE == 0
    width = A_HEADS * HEAD_DIM
    halo_per_tile = tl // N_SIDE
    n_halo = sub_len // N_SIDE
    prev = lambda li: jnp.maximum(li * halo_per_tile - 1, 0)
    nxt = lambda li: jnp.minimum((li + 1) * halo_per_tile, n_halo - 1)
    halo = lambda col, rowf: pl.BlockSpec((None, None, N_SIDE, width), lambda b, c, li: (b, c, rowf(li), col))
    cur = lambda col: pl.BlockSpec((None, None, tl, width), lambda b, c, li: (b, c, li, col))
    kern = functools.partial(_attn_kernel, dil=dil, sub_len=sub_len, tl=tl, slopes=tuple(float(s) for s in slopes))
    return pl.pallas_call(
        kern,
        grid=(bsz, dil, sub_len // tl),
        in_specs=[cur(0), halo(1, prev), cur(1), halo(1, nxt), halo(2, prev), cur(2), halo(2, nxt)],
        out_specs=[pl.BlockSpec((None, None, tl, width), lambda b, c, li: (b, c, li, 0)),
                   pl.BlockSpec((None, None, tl, LANES), lambda b, c, li: (b, c, li, 0))],
        out_shape=[jax.ShapeDtypeStruct((bsz, dil, sub_len, width), BF16),
                   jax.ShapeDtypeStruct((bsz, dil, sub_len, LANES), F32)],
        scratch_shapes=[pltpu.VMEM((tl + 2 * N_SIDE, width), BF16),
                        pltpu.VMEM((tl + 2 * N_SIDE, width), BF16)],
        compiler_params=_cparams(("parallel", "parallel", "parallel")),
        name=f"dilated_attn_g{gi}",
    )(qkv, qkv, qkv, qkv, qkv, qkv, qkv)


def _alibi_slopes():
    n = N_GROUPS * A_HEADS
    s = 2.0 ** (-8.0 * np.arange(1, n + 1) / n)
    return s.astype(np.float32).reshape(N_GROUPS, A_HEADS)


def _dn_prep_kernel(prev_ref, cur_ref, next_ref, cw_ref, bab_ref, gp_ref, q_ref, k_ref, v_ref, g_ref, *, ts):
    ti = pl.program_id(1)
    nt = pl.num_programs(1)
    halo = CONV_W // 2
    keep_prev = (ti > 0).astype(F32)
    keep_next = (ti < nt - 1).astype(F32)
    for c in range((2 * B_QK + B_VZ) // LANES):
        cs = slice(c * LANES, (c + 1) * LANES)
        xp = prev_ref[0, :, cs].astype(F32)[8:16] * keep_prev
        xc = cur_ref[0, :, cs].astype(F32)
        xn = next_ref[0, :, cs].astype(F32)[0:8] * keep_next
        ext = jnp.concatenate([xp, xc, xn], axis=0)
        acc = jnp.zeros((ts, LANES), F32)
        for j in range(CONV_W):
            off = 8 - halo + j
            acc = acc + ext[off:off + ts, :] * cw_ref[j:j + 1, cs]
        y = acc * _sigmoid(acc)
        if c < 2 * B_QK // LANES:
            y = y * lax.rsqrt(jnp.sum(y * y, axis=1, keepdims=True) + 1e-6)
        if c < B_QK // LANES:
            q_ref[0, :, cs] = y * (B_DK ** -0.5)
        elif c < 2 * B_QK // LANES:
            k_ref[0, :, c * LANES - B_QK:(c + 1) * LANES - B_QK] = y
        else:
            v_ref[0, :, c * LANES - 2 * B_QK:(c + 1) * LANES - 2 * B_QK] = y
    x = bab_ref[0]
    neg_a = gp_ref[0:1, :]
    dtb = gp_ref[1:2, :]
    is_g = gp_ref[2:3, :] > 0.5
    z = x + dtb
    softplus = jnp.maximum(z, 0.0) + jnp.log(1.0 + jnp.exp(-jnp.abs(z)))
    g_ref[0] = jnp.where(is_g, neg_a * softplus, _sigmoid(x))


def _dn_prep(u_b, conv_w, bab, gate_params):
    bsz, seq, _ = u_b.shape
    ts = min(256, seq)
    cq = 2 * B_QK + B_VZ
    nhalo = seq // 16
    per = ts // 16
    kern = functools.partial(_dn_prep_kernel, ts=ts)
    return pl.pallas_call(
        kern,
        grid=(bsz, seq // ts),
        in_specs=[pl.BlockSpec((1, 16, cq), lambda b, t: (b, jnp.maximum(t * per - 1, 0), 0)),
                  pl.BlockSpec((1, ts, cq), lambda b, t: (b, t, 0)),
                  pl.BlockSpec((1, 16, cq), lambda b, t: (b, jnp.minimum((t + 1) * per, nhalo - 1), 0)),
                  pl.BlockSpec((8, cq), lambda b, t: (0, 0)),
                  pl.BlockSpec((1, ts, LANES), lambda b, t: (b, t, 0)),
                  pl.BlockSpec((8, LANES), lambda b, t: (0, 0))],
        out_specs=[pl.BlockSpec((1, ts, B_QK), lambda b, t: (b, t, 0)),
                   pl.BlockSpec((1, ts, B_QK), lambda b, t: (b, t, 0)),
                   pl.BlockSpec((1, ts, B_VZ), lambda b, t: (b, t, 0)),
                   pl.BlockSpec((1, ts, LANES), lambda b, t: (b, t, 0))],
        out_shape=[jax.ShapeDtypeStruct((bsz, seq, B_QK), F32),
                   jax.ShapeDtypeStruct((bsz, seq, B_QK), F32),
                   jax.ShapeDtypeStruct((bsz, seq, B_VZ), F32),
                   jax.ShapeDtypeStruct((bsz, seq, LANES), F32)],
        compiler_params=_cparams(("parallel", "parallel")),
        name="dn_prep",
    )(u_b, u_b, u_b, conv_w, bab, gate_params)


def _dn_intra_kernel(q_ref, k_ref, v_ref, g_ref, gt_ref, u_ref, wq_ref, kq_ref, et_ref, *, c, per):
    n_units = 2 * DN_PAIR
    w4 = n_units * c
    hp = lax.Precision.HIGHEST
    bf = lambda t_: t_.astype(BF16)
    ii = lax.broadcasted_iota(jnp.int32, (c, w4), 0)
    ll = lax.broadcasted_iota(jnp.int32, (c, w4), 1)
    jj = ll % c
    ub = ll // c
    ub_row = ub[0:1, :]
    lo = jnp.where(ub >= DN_PAIR, jj - ii, ii - jj)
    incl = lo >= 0
    strict = lo > 0
    eye = (ii == jj).astype(F32)
    blk = (ii // TRI_BASE) == (jj // TRI_BASE)

    def pack(parts, sel):
        out = parts[n_units - 1]
        for u_ in range(n_units - 2, -1, -1):
            out = jnp.where(sel == u_, parts[u_], out)
        return out

    unit_mask = [jnp.where(ub == u_, 1.0, 0.0).astype(BF16) for u_ in range(n_units)]

    def block_diag(y16):
        return jnp.concatenate([y16 * m_ for m_ in unit_mask], axis=0)

    def mm(xs, ys):
        return [_dot(bf(x_), block_diag(bf(y_))) for x_, y_ in zip(xs, ys)]

    ri = lax.broadcasted_iota(jnp.int32, (c, c), 0)
    ci = lax.broadcasted_iota(jnp.int32, (c, c), 1)
    tri4 = (lo <= 0).astype(F32)
    tri_f = (ci <= ri).astype(F32)
    tri_b = (ci >= ri).astype(F32)
    items = [(cc, p) for cc in range(per) for p in range(B_V_HEADS // DN_PAIR)]
    qs_, ks_, lms, qkms, betas, egcs, kscales, etots = [], [], [], [], [], [], [], []
    for cc, p in items:
        rows_cc = slice(cc * c, (cc + 1) * c)
        if p == 0:
            g_all = g_ref[0, rows_cc, :]
            gc_dir = [jnp.dot(tri_f, g_all, precision=hp, preferred_element_type=F32),
                      jnp.dot(tri_b, g_all, precision=hp, preferred_element_type=F32)]
            gcr_all = jnp.dot(gt_ref[0, cc], tri4, precision=hp, preferred_element_type=F32)
            tot_all = jnp.sum(g_all, axis=0, keepdims=True)
        cs = slice(p * B_DK, (p + 1) * B_DK)
        q = q_ref[0, rows_cc, cs]
        k = k_ref[0, rows_cc, cs]
        k16 = bf(k)
        k4 = jnp.concatenate([k16] * n_units, axis=0)
        gram = _dot_nt(k16, k4)
        qk = _dot_nt(bf(q), k4)
        beta_u, gc_u, gcr_u, tot_u = [], [], [], []
        for u_ in range(n_units):
            d_, e_ = divmod(u_, DN_PAIR)
            h = p * DN_PAIR + e_
            bl = d_ * 2 * B_V_HEADS + h
            gl = bl + B_V_HEADS
            beta_u.append(g_all[:, bl:bl + 1])
            gc_u.append(gc_dir[d_][:, gl:gl + 1])
            gcr_u.append(gcr_all[gl:gl + 1, :])
            tot_u.append(tot_all[:, gl:gl + 1])
        gc_p = pack(gc_u, ub)
        gcr_p = pack(gcr_u, ub_row)
        tot_p = pack(tot_u, ub_row)
        dec = jnp.where(incl, jnp.exp(jnp.where(incl, gc_p - gcr_p, 0.0)), 0.0)
        lms.append(jnp.where(strict, pack(beta_u, ub) * gram * dec, 0.0))
        qkms.append(jnp.where(incl, qk * dec, 0.0))
        qs_.append(q)
        ks_.append(k)
        betas.append(beta_u)
        egcs.append([jnp.exp(g_) for g_ in gc_u])
        kscales.append(jnp.exp(tot_p - gcr_p))
        etots.append([jnp.exp(t_) for t_ in tot_u])

    d1 = [jnp.where(blk, lm, 0.0) for lm in lms]
    d2 = mm(d1, d1)
    d4 = mm(d2, d2)
    d8 = mm(d4, d4)
    tm_ = [eye - d_ for d_ in d1]
    for dk in (d2, d4, d8):
        tm_ = [a + b for a, b in zip(tm_, mm(tm_, dk))]
    size = TRI_BASE
    while size < c:
        off = ((ii // (2 * size)) == (jj // (2 * size))) & ((ii // size) != (jj // size))
        cm = [jnp.where(off, lm, 0.0) for lm in lms]
        pc = mm(tm_, cm)
        tm_ = [a - b for a, b in zip(tm_, mm(pc, tm_))]
        size *= 2

    lane2 = lax.broadcasted_iota(jnp.int32, (1, DN_PAIR * B_DV), 1)
    for it, (cc, p) in enumerate(items):
        q, k = qs_[it], ks_[it]
        rows = []
        for u_ in range(n_units):
            d_, e_ = divmod(u_, DN_PAIR)
            h = p * DN_PAIR + e_
            v = v_ref[0, cc * c:(cc + 1) * c, h * B_DV:(h + 1) * B_DV]
            beta = betas[it][u_]
            rows.append(jnp.concatenate([v * beta, k * (beta * egcs[it][u_])], axis=1))
        uw = _dot(block_diag(bf(tm_[it])), bf(jnp.concatenate(rows, axis=0)))
        k_t = jnp.transpose(jnp.concatenate([k] * DN_PAIR, axis=0))
        for d_ in range(2):
            u0, u1 = d_ * DN_PAIR, d_ * DN_PAIR + 1
            u_ref[d_, 0, cc, p] = jnp.concatenate([uw[u0 * c:(u0 + 1) * c, :B_DV], uw[u1 * c:(u1 + 1) * c, :B_DV]], axis=1)
            wq = jnp.concatenate([uw[u0 * c:(u0 + 1) * c, B_DV:], q * egcs[it][u0],
                                  uw[u1 * c:(u1 + 1) * c, B_DV:], q * egcs[it][u1]], axis=0)
            wq_ref[d_, 0, cc, p] = bf(wq)
            ls = slice(d_ * DN_PAIR * c, (d_ + 1) * DN_PAIR * c)
            kq = jnp.concatenate([k_t * kscales[it][:, ls], qkms[it][:, ls]], axis=0)
            kq_ref[d_, 0, cc, p] = bf(kq)
            et_ref[d_, 0, cc, p:p + 1, :] = jnp.where(lane2 < B_DV, etots[it][u0], etots[it][u1])


def _dn_intra(qn, kn, vn, gts, gts_t):
    bsz, seq, _ = qn.shape
    c = DN_CHUNK
    assert DN_PAIR * c == B_DK and seq % c == 0
    n = seq // c
    npair = B_V_HEADS // DN_PAIR
    wide = DN_PAIR * B_DV
    per = DN_INTRA_CHUNKS if n % DN_INTRA_CHUNKS == 0 else 1
    kern = functools.partial(_dn_intra_kernel, c=c, per=per)
    out5 = lambda r, cdim: pl.BlockSpec((2, 1, per, npair, r, cdim), lambda b, i: (0, b, i, 0, 0, 0))
    return pl.pallas_call(
        kern,
        grid=(bsz, n // per),
        in_specs=[pl.BlockSpec((1, per * c, B_QK), lambda b, i: (b, i, 0)),
                  pl.BlockSpec((1, per * c, B_QK), lambda b, i: (b, i, 0)),
                  pl.BlockSpec((1, per * c, B_VZ), lambda b, i: (b, i, 0)),
                  pl.BlockSpec((1, per * c, LANES), lambda b, i: (b, i, 0)),
                  pl.BlockSpec((1, per, LANES, c), lambda b, i: (b, i, 0, 0))],
        out_specs=[out5(c, wide), out5(2 * DN_PAIR * c, B_DK), out5(B_DK + c, DN_PAIR * c),
                   pl.BlockSpec((2, 1, per, npair, wide), lambda b, i: (0, b, i, 0, 0))],
        out_shape=[jax.ShapeDtypeStruct((2, bsz, n, npair, c, wide), F32),
                   jax.ShapeDtypeStruct((2, bsz, n, npair, 2 * DN_PAIR * c, B_DK), BF16),
                   jax.ShapeDtypeStruct((2, bsz, n, npair, B_DK + c, DN_PAIR * c), BF16),
                   jax.ShapeDtypeStruct((2, bsz, n, npair, wide), F32)],
        compiler_params=_cparams(("parallel", "parallel")),
        name="dn_intra",
    )(qn, kn, vn, gts, gts_t)


def _dn_state_kernel(uf_ref, wqf_ref, kqf_ref, etf_ref, ub_ref, wqb_ref, kqb_ref, etb_ref, of_ref, ob_ref, s_ref,
                     *, c, per):
    @pl.when(pl.program_id(1) == 0)
    def _():
        s_ref[...] = jnp.zeros_like(s_ref)

    npair = B_V_HEADS // DN_PAIR
    bf = lambda t_: t_.astype(BF16)
    chains = [(d_, p) for d_ in range(2) for p in range(npair)]
    refs = ((uf_ref, wqf_ref, kqf_ref, etf_ref, of_ref), (ub_ref, wqb_ref, kqb_ref, etb_ref, ob_ref))
    zero = jnp.zeros((c, B_DV), F32)
    for step in range(per):
        local = (step, per - 1 - step)
        states = [s_ref[d_, p] for d_, p in chains]
        a_res = [_dot(refs[d_][1][0, 0, local[d_], p], bf(s_)) for (d_, p), s_ in zip(chains, states)]
        b_res = []
        for (d_, p), a_ in zip(chains, a_res):
            u = refs[d_][0][0, 0, local[d_], p]
            v0 = u[:, :B_DV] - a_[0:c, :B_DV]
            v1 = u[:, B_DV:] - a_[2 * c:3 * c, B_DV:]
            bd_v = jnp.concatenate([jnp.concatenate([v0, zero], axis=1), jnp.concatenate([zero, v1], axis=1)], axis=0)
            b_res.append(_dot(refs[d_][2][0, 0, local[d_], p], bf(bd_v)))
        for (d_, p), a_, b_, s_ in zip(chains, a_res, b_res, states):
            s_ref[d_, p] = s_ * refs[d_][3][0, 0, local[d_], p:p + 1, :] + b_[:B_DK]
            o_ref = refs[d_][4]
            rs = slice(local[d_] * c, (local[d_] + 1) * c)
            o_ref[0, rs, (2 * p) * B_DV:(2 * p + 1) * B_DV] = a_[c:2 * c, :B_DV] + b_[B_DK:, :B_DV]
            o_ref[0, rs, (2 * p + 1) * B_DV:(2 * p + 2) * B_DV] = a_[3 * c:4 * c, B_DV:] + b_[B_DK:, B_DV:]


def _dn_state(u_all, wq_all, kq_all, et_all, seq):
    _, bsz, n, npair, c, wide = u_all.shape
    per = DN_STATE_CHUNKS if n % DN_STATE_CHUNKS == 0 else 1
    nb = n // per
    kern = functools.partial(_dn_state_kernel, c=c, per=per)
    fwd = lambda r, cdim: pl.BlockSpec((1, 1, per, npair, r, cdim), lambda b, i: (0, b, i, 0, 0, 0))
    bwd = lambda r, cdim: pl.BlockSpec((1, 1, per, npair, r, cdim), lambda b, i: (1, b, nb - 1 - i, 0, 0, 0))
    et_f = pl.BlockSpec((1, 1, per, npair, wide), lambda b, i: (0, b, i, 0, 0))
    et_b = pl.BlockSpec((1, 1, per, npair, wide), lambda b, i: (1, b, nb - 1 - i, 0, 0))
    shapes = ((c, wide), (2 * DN_PAIR * c, B_DK), (B_DK + c, DN_PAIR * c))
    return pl.pallas_call(
        kern,
        grid=(bsz, nb),
        in_specs=[fwd(*shapes[0]), fwd(*shapes[1]), fwd(*shapes[2]), et_f,
                  bwd(*shapes[0]), bwd(*shapes[1]), bwd(*shapes[2]), et_b],
        out_specs=[pl.BlockSpec((1, per * c, B_VZ), lambda b, i: (b, i, 0)),
                   pl.BlockSpec((1, per * c, B_VZ), lambda b, i: (b, nb - 1 - i, 0))],
        out_shape=[jax.ShapeDtypeStruct((bsz, seq, B_VZ), F32), jax.ShapeDtypeStruct((bsz, seq, B_VZ), F32)],
        scratch_shapes=[pltpu.VMEM((2, npair, B_DK, wide), F32)],
        compiler_params=_cparams(("parallel", "arbitrary")),
        name="dn_state",
    )(u_all, wq_all, kq_all, et_all, u_all, wq_all, kq_all, et_all)


def _branch_a_kernel(o0_ref, o1_ref, o2_ref, l0_ref, l1_ref, l2_ref, w_ref, gp_ref, bg_ref, out_ref, l_sc,
                     *, dils, tm):
    tok = lax.broadcasted_iota(jnp.int32, (tm, tm), 0)
    src = lax.broadcasted_iota(jnp.int32, (tm, tm), 1)
    o_tok = []
    for g, (o_ref, l_ref) in enumerate(((o0_ref, l0_ref), (o1_ref, l1_ref), (o2_ref, l2_ref))):
        r = dils[g]
        rows = tm // r
        for c in range(r):
            l_sc[g, pl.ds(c, rows, stride=r), :] = l_ref[c]
        if r == 1:
            o_tok.append(o_ref[0].astype(F32))
        else:
            perm = jnp.where(tok == (src % rows) * r + src // rows, 1.0, 0.0).astype(BF16)
            o_tok.append(_dot(perm, jnp.concatenate([o_ref[c] for c in range(r)], axis=0)))
    ls = [l_sc[g] for g in range(N_GROUPS)]
    m = jnp.maximum(jnp.maximum(ls[0], ls[1]), ls[2])
    es = [jnp.exp(l - m) for l in ls]
    den = es[0] + es[1] + es[2]
    ws = [e / den for e in es]
    parts = []
    for h in range(A_HEADS):
        hs = slice(h * HEAD_DIM, (h + 1) * HEAD_DIM)
        acc = ws[0][:, h:h + 1] * o_tok[0][:, hs]
        acc = acc + ws[1][:, h:h + 1] * o_tok[1][:, hs]
        acc = acc + ws[2][:, h:h + 1] * o_tok[2][:, hs]
        parts.append(acc.astype(BF16))
    oa = jnp.concatenate(parts, axis=1)
    y = _dot(oa, w_ref[...])
    out_ref[...] = _sigmoid(gp_ref[...] + bg_ref[...]) * y


def _branch_a(outs, lses, w_a, gpre, b_gate, tm):
    bsz = outs[0].shape[0]
    dils = tuple(o.shape[1] for o in outs)
    seq = dils[0] * outs[0].shape[2]
    t = bsz * seq
    per_seq = seq // tm
    wd = A_HEADS * HEAD_DIM
    cls = lambda r, width: pl.BlockSpec((None, r, tm // r, width), lambda i: (i // per_seq, 0, i % per_seq, 0))
    kern = functools.partial(_branch_a_kernel, dils=dils, tm=tm)
    return pl.pallas_call(
        kern,
        grid=(t // tm,),
        in_specs=[cls(dils[0], wd), cls(dils[1], wd), cls(dils[2], wd),
                  cls(dils[0], LANES), cls(dils[1], LANES), cls(dils[2], LANES),
                  pl.BlockSpec((wd, D_MODEL), lambda i: (0, 0)),
                  pl.BlockSpec((tm, D_MODEL), lambda i: (i, 0)),
                  pl.BlockSpec((1, D_MODEL), lambda i: (0, 0))],
        out_specs=pl.BlockSpec((tm, D_MODEL), lambda i: (i, 0)),
        out_shape=jax.ShapeDtypeStruct((t, D_MODEL), F32),
        scratch_shapes=[pltpu.VMEM((N_GROUPS, tm, LANES), F32)],
        compiler_params=_cparams(("parallel",)),
        name="branch_a",
    )(*outs, *lses, w_a, gpre, b_gate)


def _branch_b_kernel(of_ref, ob_ref, z_ref, nw_ref, w_ref, gp_ref, bg_ref, a_ref, out_ref):
    nw = nw_ref[...]
    parts = []
    for h in range(B_V_HEADS):
        hs = slice(h * B_DV, (h + 1) * B_DV)
        o = of_ref[:, hs] + ob_ref[:, hs]
        z = z_ref[:, hs].astype(F32)
        o = o * lax.rsqrt(jnp.mean(o * o, axis=1, keepdims=True) + RMS_EPS) * nw * (z * _sigmoid(z))
        parts.append(o.astype(BF16))
    ob = jnp.concatenate(parts, axis=1)
    y = _dot(ob, w_ref[...])
    out_ref[...] = (a_ref[...] + _sigmoid(gp_ref[...] + bg_ref[...]) * y).astype(out_ref.dtype)


def _branch_b(o_f, o_b, u_b2d, norm_w, w_b, gpre, b_gate, a_part, tm):
    t = a_part.shape[0]
    return pl.pallas_call(
        _branch_b_kernel,
        grid=(t // tm,),
        in_specs=[pl.BlockSpec((tm, B_VZ), lambda i: (i, 0)),
                  pl.BlockSpec((tm, B_VZ), lambda i: (i, 0)),
                  pl.BlockSpec((tm, B_VZ), lambda i: (i, (2 * B_QK + B_VZ) // B_VZ)),
                  pl.BlockSpec((1, B_DV), lambda i: (0, 0)),
                  pl.BlockSpec((B_VZ, D_MODEL), lambda i: (0, 0)),
                  pl.BlockSpec((tm, D_MODEL), lambda i: (i, 1)),
                  pl.BlockSpec((1, D_MODEL), lambda i: (0, 1)),
                  pl.BlockSpec((tm, D_MODEL), lambda i: (i, 0))],
        out_specs=pl.BlockSpec((tm, D_MODEL), lambda i: (i, 0)),
        out_shape=jax.ShapeDtypeStruct((t, D_MODEL), BF16),
        compiler_params=_cparams(("parallel",)),
        name="branch_b",
    )(o_f, o_b, u_b2d, norm_w, w_b, gpre, b_gate, a_part)


def _layer_norm(y, g, b):
    mu = jnp.mean(y, axis=1, keepdims=True)
    yc = y - mu
    var = jnp.mean(yc * yc, axis=1, keepdims=True)
    return yc * lax.rsqrt(var + LN_EPS) * g + b


def _out_ln_kernel(m_ref, w_ref, x_ref, g_ref, b_ref, out_ref):
    mix = _dot(m_ref[...], w_ref[...])
    out_ref[...] = _layer_norm(DEEPNORM_ALPHA * x_ref[...] + mix, g_ref[...], b_ref[...])


def _out_ln(merged, w_out, x2d, ln_g, ln_b, tm):
    t = merged.shape[0]
    vec = pl.BlockSpec((1, D_MODEL), lambda i: (0, 0))
    return pl.pallas_call(
        _out_ln_kernel,
        grid=(t // tm,),
        in_specs=[pl.BlockSpec((tm, D_MODEL), lambda i: (i, 0)),
                  pl.BlockSpec((D_MODEL, D_MODEL), lambda i: (0, 0)),
                  pl.BlockSpec((tm, D_MODEL), lambda i: (i, 0)), vec, vec],
        out_specs=pl.BlockSpec((tm, D_MODEL), lambda i: (i, 0)),
        out_shape=jax.ShapeDtypeStruct((t, D_MODEL), F32),
        compiler_params=_cparams(("parallel",)),
        name="out_proj_ln1",
    )(merged, w_out, x2d, ln_g, ln_b)


def _router_kernel(x_ref, rwh_ref, rwl_ref, rb_ref, idx_ref, gate_ref, rank_ref, cnt_ref, carry, *, tm):
    @pl.when(pl.program_id(0) == 0)
    def _():
        carry[...] = jnp.zeros_like(carry)

    lane = lax.broadcasted_iota(jnp.int32, (tm, LANES), 1)
    lane_f = lane.astype(F32)
    x = x_ref[...]
    x_hi = x.astype(BF16)
    x_lo = (x - x_hi.astype(F32)).astype(BF16)
    logits = _dot(x_hi, rwh_ref[...]) + (_dot(x_lo, rwh_ref[...]) + _dot(x_hi, rwl_ref[...])) + rb_ref[...]
    cur = jnp.where(lane < N_EXPERTS, logits, -jnp.inf)
    vals, idxs = [], []
    for _k in range(TOP_K):
        m = jnp.max(cur, axis=1, keepdims=True)
        idx = jnp.min(jnp.where(cur == m, lane_f, float(LANES)), axis=1, keepdims=True).astype(jnp.int32)
        vals.append(m)
        idxs.append(idx)
        cur = jnp.where(lane == idx, -jnp.inf, cur)
    es = [jnp.exp(v - vals[0]) for v in vals]
    den = es[0] + es[1] + es[2] + es[3]
    onehot = jnp.zeros((tm, LANES), F32)
    for idx in idxs:
        onehot = onehot + (lane == idx).astype(F32)
    ri = lax.broadcasted_iota(jnp.int32, (tm, tm), 0)
    ci = lax.broadcasted_iota(jnp.int32, (tm, tm), 1)
    before = (ci < ri).astype(BF16)
    prefix = _dot(before, onehot.astype(BF16)) + carry[0:1, :]
    idx_out = jnp.zeros((tm, LANES), jnp.int32)
    gate_out = jnp.zeros((tm, LANES), F32)
    rank_out = jnp.zeros((tm, LANES), jnp.int32)
    for k in range(TOP_K):
        rk = jnp.sum(jnp.where(lane == idxs[k], prefix, 0.0), axis=1, keepdims=True)
        idx_out = jnp.where(lane == k, idxs[k], idx_out)
        gate_out = jnp.where(lane == k, es[k] / den, gate_out)
        rank_out = jnp.where(lane == k, rk.astype(jnp.int32), rank_out)
    idx_ref[...] = idx_out
    gate_ref[...] = gate_out
    rank_ref[...] = rank_out
    total = carry[0:1, :] + jnp.sum(onehot, axis=0, keepdims=True)
    carry[...] = jnp.broadcast_to(total, carry.shape)
    cnt_ref[...] = jnp.broadcast_to(total, cnt_ref.shape)


def _router(x1, rw, rb, tm):
    rw_hi = rw.astype(BF16)
    rw_lo = (rw - rw_hi.astype(F32)).astype(BF16)
    t = x1.shape[0]
    row = pl.BlockSpec((tm, LANES), lambda i: (i, 0))
    kern = functools.partial(_router_kernel, tm=tm)
    return pl.pallas_call(
        kern,
        grid=(t // tm,),
        in_specs=[pl.BlockSpec((tm, D_MODEL), lambda i: (i, 0)),
                  pl.BlockSpec((D_MODEL, LANES), lambda i: (0, 0)),
                  pl.BlockSpec((D_MODEL, LANES), lambda i: (0, 0)),
                  pl.BlockSpec((1, LANES), lambda i: (0, 0))],
        out_specs=[row, row, row, pl.BlockSpec((8, LANES), lambda i: (0, 0))],
        out_shape=[jax.ShapeDtypeStruct((t, LANES), jnp.int32),
                   jax.ShapeDtypeStruct((t, LANES), F32),
                   jax.ShapeDtypeStruct((t, LANES), jnp.int32),
                   jax.ShapeDtypeStruct((8, LANES), F32)],
        scratch_shapes=[pltpu.VMEM((8, LANES), F32)],
        compiler_params=_cparams(("arbitrary",)),
        name="router",
    )(x1, rw_hi, rw_lo, rb)


def _pack_rows(src_ref, dst_ref):
    half = D_MODEL // 2
    n = src_ref.shape[0]
    for s in range(ROW_SUBLANES):
        lo = pltpu.bitcast(src_ref[:, s * LANES:(s + 1) * LANES].astype(BF16).astype(F32), jnp.uint32)
        hi = pltpu.bitcast(src_ref[:, half + s * LANES:half + (s + 1) * LANES].astype(BF16).astype(F32), jnp.uint32)
        dst_ref[pl.ds(s, n, stride=ROW_SUBLANES), :] = (lo >> 16) | hi


def _unpack_words(w):
    return pltpu.bitcast(w << 16, F32), pltpu.bitcast(w & jnp.uint32(0xFFFF0000), F32)


def _tile(ref, i):
    start = i * ROW_SUBLANES
    if not isinstance(i, int):
        start = pl.multiple_of(start, ROW_SUBLANES)
    return ref.at[pl.ds(start, ROW_SUBLANES)]


def _tile_copy(src, i, dst, j, sem):
    return pltpu.make_async_copy(_tile(src, i), _tile(dst, j), sem)


def _wait_tiles(hbm, n, sem):
    pltpu.make_async_copy(hbm.at[pl.ds(0, n * ROW_SUBLANES)], hbm.at[pl.ds(0, n * ROW_SUBLANES)], sem).wait()


def _dispatch_kernel(pad_start_ref, pad_cnt_ref, dest_ref, x_ref, xs_hbm, stage, zero, sem, *, tm, nt):
    i = pl.program_id(0)

    @pl.when(i < nt)
    def _():
        _pack_rows(x_ref, stage)

        def issue(t_, carry):
            src = _tile(stage, t_)
            for k in range(TOP_K):
                pltpu.make_async_copy(src, _tile(xs_hbm, dest_ref[0, 0, t_ * TOP_K + k]), sem).start()
            return carry

        lax.fori_loop(0, tm, issue, 0, unroll=2)
        _wait_tiles(xs_hbm, tm * TOP_K, sem)

    @pl.when((i >= nt) & (i < nt + N_EXPERTS))
    def _():
        e = i - nt
        zero[...] = jnp.zeros_like(zero)
        start = pad_start_ref[e]
        cnt = pad_cnt_ref[e]

        def issue(r, carry):
            _tile_copy(zero, 0, xs_hbm, start + r, sem).start()
            return carry

        lax.fori_loop(0, cnt, issue, 0)

        def wait(r, carry):
            _tile_copy(zero, 0, xs_hbm, start, sem).wait()
            return carry

        lax.fori_loop(0, cnt, wait, 0)

    @pl.when(i == nt + N_EXPERTS)
    def _():
        stage[...] = jnp.zeros_like(stage)
        start = pad_start_ref[N_EXPERTS]

        def fill(j, carry):
            piece = xs_hbm.at[pl.ds(pl.multiple_of((start + j * tm) * ROW_SUBLANES, ROW_SUBLANES), tm * ROW_SUBLANES)]
            cp = pltpu.make_async_copy(stage, piece, sem)
            cp.start()
            cp.wait()
            return carry

        lax.fori_loop(0, pad_cnt_ref[N_EXPERTS] // tm, fill, 0)


def _dispatch(x1, dest, pad_from, pad_cnt, n_pad, tm):
    t = x1.shape[0]
    nt = t // tm
    assert MOE_BLOCK % tm == 0 and pad_from.shape[0] == N_EXPERTS + 1
    kern = functools.partial(_dispatch_kernel, tm=tm, nt=nt)
    grid_spec = pltpu.PrefetchScalarGridSpec(
        num_scalar_prefetch=2,
        grid=(nt + pad_from.shape[0],),
        in_specs=[pl.BlockSpec((1, 1, tm * TOP_K), lambda i, ps, pc: (jnp.minimum(i, nt - 1), 0, 0),
                               memory_space=pltpu.SMEM),
                  pl.BlockSpec((tm, D_MODEL), lambda i, ps, pc: (jnp.minimum(i, nt - 1), 0))],
        out_specs=pl.BlockSpec(memory_space=pl.ANY),
        scratch_shapes=[pltpu.VMEM((tm * ROW_SUBLANES, LANES), jnp.uint32),
                        pltpu.VMEM((ROW_SUBLANES, LANES), jnp.uint32),
                        pltpu.SemaphoreType.DMA(())],
    )
    return pl.pallas_call(
        kern,
        grid_spec=grid_spec,
        out_shape=jax.ShapeDtypeStruct((n_pad * ROW_SUBLANES, LANES), jnp.uint32),
        compiler_params=_cparams(("arbitrary",)),
        name="moe_dispatch",
    )(pad_from, pad_cnt, dest.reshape(nt, 1, tm * TOP_K), x1)


GU_GROUP = 2 * LANES


def _regroup_kernel(w_ref, o_ref):
    ri = lax.broadcasted_iota(jnp.int32, (GU_GROUP, GU_GROUP), 0)
    ci = lax.broadcasted_iota(jnp.int32, (GU_GROUP, GU_GROUP), 1)
    src = jnp.where(ci < LANES, 2 * ci, 2 * (ci - LANES) + 1)
    perm = jnp.where(ri == src, 1.0, 0.0).astype(BF16)
    for c in range(w_ref.shape[2] // GU_GROUP):
        cs = slice(c * GU_GROUP, (c + 1) * GU_GROUP)
        o_ref[0, :, cs] = _dot(w_ref[0, :, cs].astype(BF16), perm).astype(BF16)


def _regroup_gate_up(w_gate_up):
    e, d, n2 = w_gate_up.shape
    rt = 512
    return pl.pallas_call(
        _regroup_kernel,
        grid=(e, d // rt),
        in_specs=[pl.BlockSpec((1, rt, n2), lambda i, j: (i, j, 0))],
        out_specs=pl.BlockSpec((1, rt, n2), lambda i, j: (i, j, 0)),
        out_shape=jax.ShapeDtypeStruct((e, d, n2), BF16),
        compiler_params=_cparams(("parallel", "parallel")),
        name="regroup_gate_up",
    )(w_gate_up)

def _ffn_kernel(be_ref, nused_ref, rows_ref, x_ref, wgu_ref, bgu_ref, wd_ref, bd_ref, o_ref, acc, xb):
    i = pl.program_id(0)
    f = pl.program_id(1)
    nf = pl.num_programs(1)
    half = D_MODEL // 2

    @pl.when(f == 0)
    def _():
        acc[...] = jnp.broadcast_to(bd_ref[0], acc.shape)
        for s in range(ROW_SUBLANES):
            lo, hi = _unpack_words(x_ref[pl.ds(s, MOE_BLOCK, stride=ROW_SUBLANES), :])
            xb[:, s * LANES:(s + 1) * LANES] = lo.astype(BF16)
            xb[:, half + s * LANES:half + (s + 1) * LANES] = hi.astype(BF16)

    def compute(nrows):
        gu = _dot(xb[0:nrows, :], wgu_ref[0]) + bgu_ref[0]
        acts = []
        for m in range(MOE_TF // LANES):
            gate = jnp.minimum(gu[:, m * GU_GROUP:m * GU_GROUP + LANES], SWIGLU_LIMIT)
            up = jnp.clip(gu[:, m * GU_GROUP + LANES:(m + 1) * GU_GROUP], -SWIGLU_LIMIT, SWIGLU_LIMIT)
            acts.append(((up + 1.0) * gate * _sigmoid(gate * SWIGLU_ALPHA)).astype(BF16))
        acc[0:nrows, :] += _dot(jnp.concatenate(acts, axis=1), wd_ref[0].astype(BF16))

    rows = rows_ref[i]
    pl.when(rows > MOE_BLOCK // 2)(lambda: compute(MOE_BLOCK))
    pl.when((rows > MOE_BLOCK // 4) & (rows <= MOE_BLOCK // 2))(lambda: compute(MOE_BLOCK // 2))
    pl.when((rows > 0) & (rows <= MOE_BLOCK // 4))(lambda: compute(MOE_BLOCK // 4))

    @pl.when(f == nf - 1)
    def _():
        _pack_rows(acc, o_ref)


def _expert_ffn(xs, block_expert, n_used, block_rows, wgu, bgu, wd, bd):
    n_pad = xs.shape[0] // ROW_SUBLANES
    nb = n_pad // MOE_BLOCK
    d_ff = wgu.shape[2] // 2
    nf = d_ff // MOE_TF
    row_blk = (MOE_BLOCK * ROW_SUBLANES, LANES)
    grid_spec = pltpu.PrefetchScalarGridSpec(
        num_scalar_prefetch=3,
        grid=(nb, nf),
        in_specs=[pl.BlockSpec(row_blk, lambda i, f, be, nu, rw: (jnp.minimum(i, nu[0] - 1), 0)),
                  pl.BlockSpec((1, D_MODEL, 2 * MOE_TF), lambda i, f, be, nu, rw: (be[i], 0, f)),
                  pl.BlockSpec((1, 1, 2 * MOE_TF), lambda i, f, be, nu, rw: (be[i], 0, f)),
                  pl.BlockSpec((1, MOE_TF, D_MODEL), lambda i, f, be, nu, rw: (be[i], f, 0)),
                  pl.BlockSpec((1, 1, D_MODEL), lambda i, f, be, nu, rw: (be[i], 0, 0))],
        out_specs=pl.BlockSpec(row_blk, lambda i, f, be, nu, rw: (i, 0)),
        scratch_shapes=[pltpu.VMEM((MOE_BLOCK, D_MODEL), F32), pltpu.VMEM((MOE_BLOCK, D_MODEL), BF16)],
    )
    return pl.pallas_call(
        _ffn_kernel,
        grid_spec=grid_spec,
        out_shape=jax.ShapeDtypeStruct((n_pad * ROW_SUBLANES, LANES), jnp.uint32),
        compiler_params=_cparams(("arbitrary", "arbitrary")),
        name="expert_ffn",
    )(block_expert, n_used, block_rows, xs, wgu, bgu, wd, bd)


def _final_kernel(dest_ref, gate_ref, x_ref, p_ref, wpg_ref, bpg_ref, wple_ref, g_ref, b_ref, ys_hbm, out_ref,
                  rows, sem, *, tm):
    def issue(t_, carry):
        for k in range(TOP_K):
            _tile_copy(ys_hbm, dest_ref[0, 0, t_ * TOP_K + k], rows, k * tm + t_, sem).start()
        return carry

    lax.fori_loop(0, tm, issue, 0, unroll=2)
    x = x_ref[...]
    pg = _dot(x.astype(BF16), wpg_ref[...]) + bpg_ref[...]
    ple = _sigmoid(pg) * _dot(p_ref[...].astype(BF16), wple_ref[...])
    _wait_tiles(ys_hbm, tm * TOP_K, sem)
    gates = gate_ref[...]
    lo_parts, hi_parts = [], []
    for s in range(ROW_SUBLANES):
        acc_lo = acc_hi = None
        for k in range(TOP_K):
            lo, hi = _unpack_words(rows[pl.ds(k * tm * ROW_SUBLANES + s, tm, stride=ROW_SUBLANES), :])
            g = gates[:, k:k + 1]
            acc_lo = g * lo if acc_lo is None else acc_lo + g * lo
            acc_hi = g * hi if acc_hi is None else acc_hi + g * hi
        lo_parts.append(acc_lo)
        hi_parts.append(acc_hi)
    y = jnp.concatenate(lo_parts + hi_parts, axis=1)
    out_ref[...] = _layer_norm(DEEPNORM_ALPHA * x + y + ple, g_ref[...], b_ref[...])


def _final(dest, gates, x1, p2d, w_pg, b_pg, w_ple, ln_g, ln_b, ys, tm):
    t = x1.shape[0]
    vec = pl.BlockSpec((1, D_MODEL), lambda i: (0, 0))
    kern = functools.partial(_final_kernel, tm=tm)
    return pl.pallas_call(
        kern,
        grid=(t // tm,),
        in_specs=[pl.BlockSpec((1, 1, tm * TOP_K), lambda i: (i, 0, 0), memory_space=pltpu.SMEM),
                  pl.BlockSpec((tm, LANES), lambda i: (i, 0)),
                  pl.BlockSpec((tm, D_MODEL), lambda i: (i, 0)),
                  pl.BlockSpec((tm, PLE_DIM), lambda i: (i, 0)),
                  pl.BlockSpec((D_MODEL, D_MODEL), lambda i: (0, 0)), vec,
                  pl.BlockSpec((PLE_DIM, D_MODEL), lambda i: (0, 0)), vec, vec,
                  pl.BlockSpec(memory_space=pl.ANY)],
        out_specs=pl.BlockSpec((tm, D_MODEL), lambda i: (i, 0)),
        out_shape=jax.ShapeDtypeStruct((t, D_MODEL), F32),
        scratch_shapes=[pltpu.VMEM((tm * TOP_K * ROW_SUBLANES, LANES), jnp.uint32), pltpu.SemaphoreType.DMA(())],
        compiler_params=_cparams(("arbitrary",)),
        name="combine_ple_ln2",
    )(dest.reshape(t // tm, 1, tm * TOP_K), gates, x1, p2d, w_pg, b_pg, w_ple, ln_g, ln_b, ys)


def _row_tile(t):
    return min(512, t)


def _layer(x, p, w_in, b_gate, conv_w, a_log, dt_bias, dn_norm_w, w_branch_a, w_branch_b, w_out, ln1_g, ln1_b,
           router_w, router_b, w_gate_up, b_gate_up, w_down, b_down, w_ple, w_ple_gate, b_ple_gate, ln2_g, ln2_b):
    bsz, seq, _ = x.shape
    t = bsz * seq
    tm = _row_tile(t)
    x2d = x.reshape(t, D_MODEL)
    xb = x2d.astype(BF16)

    c_a = 3 * A_QKV
    c_b = 2 * B_QK + 2 * B_VZ
    w_bf = w_in.astype(BF16)
    tp = min(IN_PROJ_TM, seq)
    u_b = _matmul(xb, w_bf[:, c_a:c_a + c_b], BF16, tp, IN_PROJ_TN)
    w_bab = jnp.pad(w_bf[:, c_a + c_b:c_a + c_b + B_GATES], ((0, 0), (0, LANES - B_GATES)))
    bab = _matmul(xb, w_bab, F32, tp, LANES)
    gpre = _matmul(xb, w_bf[:, c_a + c_b + B_GATES:], F32, tp, IN_PROJ_TN)

    slopes = _alibi_slopes()
    gw = A_HEADS * HEAD_DIM
    outs, lses = [], []
    for gi, (_win, dil) in enumerate(DILATION_GROUPS):
        w_g = jnp.concatenate([w_bf[:, part * A_QKV + gi * gw:part * A_QKV + (gi + 1) * gw] for part in range(3)], axis=1)
        o_g, l_g = _attention_group(_matmul_classes(xb, w_g, bsz, dil, tp, IN_PROJ_TN), gi, dil, slopes[gi])
        outs.append(o_g)
        lses.append(l_g)

    cw = jnp.pad(conv_w.astype(F32), ((0, 8 - CONV_W), (0, 0)))
    lane_is_g = (np.arange(LANES) % (2 * B_V_HEADS) >= B_V_HEADS) & (np.arange(LANES) < B_GATES)
    neg_a = jnp.zeros((LANES,), F32).at[B_V_HEADS:2 * B_V_HEADS].set(-jnp.exp(a_log[0].astype(F32)))
    neg_a = neg_a.at[3 * B_V_HEADS:4 * B_V_HEADS].set(-jnp.exp(a_log[1].astype(F32)))
    dtb = jnp.zeros((LANES,), F32).at[B_V_HEADS:2 * B_V_HEADS].set(dt_bias[0].astype(F32))
    dtb = dtb.at[3 * B_V_HEADS:4 * B_V_HEADS].set(dt_bias[1].astype(F32))
    gate_params = jnp.zeros((8, LANES), F32).at[0].set(neg_a).at[1].set(dtb).at[2].set(jnp.asarray(lane_is_g, F32))
    qn, kn, vn, gts = _dn_prep(u_b.reshape(bsz, seq, c_b), cw, bab.reshape(bsz, seq, LANES), gate_params)
    gts_t = gts.reshape(bsz, seq // DN_CHUNK, DN_CHUNK, LANES).transpose(0, 1, 3, 2)
    o_f, o_b = _dn_state(*_dn_intra(qn, kn, vn, gts, gts_t), seq)

    bg = b_gate.astype(F32).reshape(1, 2 * D_MODEL)
    a_part = _branch_a(outs, lses, w_branch_a.astype(BF16), gpre, bg, tm)
    merged = _branch_b(o_f.reshape(t, B_VZ), o_b.reshape(t, B_VZ), u_b, dn_norm_w.astype(F32).reshape(1, B_DV), w_branch_b.astype(BF16), gpre, bg,
                       a_part, tm)
    x1 = _out_ln(merged, w_out.astype(BF16), x2d, ln1_g.reshape(1, -1), ln1_b.reshape(1, -1), tm)

    rw = jnp.pad(router_w.astype(F32), ((0, 0), (0, LANES - N_EXPERTS)))
    rb = jnp.pad(router_b.astype(F32), (0, LANES - N_EXPERTS)).reshape(1, LANES)
    idx, gates, rank, cnt = _router(x1, rw, rb, tm)
    counts = cnt[0, :N_EXPERTS].astype(jnp.int32)
    padded = (counts + MOE_BLOCK - 1) // MOE_BLOCK * MOE_BLOCK
    pad_end = jnp.cumsum(padded)
    pad_start = pad_end - padded
    dest = pad_start[idx[:, :TOP_K]] + rank[:, :TOP_K]
    n_pad = t * TOP_K + N_EXPERTS * MOE_BLOCK
    nb = n_pad // MOE_BLOCK
    block_start = jnp.arange(nb, dtype=jnp.int32) * MOE_BLOCK
    block_expert = jnp.minimum(jnp.sum(pad_end[None, :] <= block_start[:, None], axis=1), N_EXPERTS - 1).astype(jnp.int32)
    n_used = (pad_end[-1:] // MOE_BLOCK).astype(jnp.int32)

    zero_from = jnp.concatenate([pad_start + counts, pad_end[-1:]]).astype(jnp.int32)
    zero_cnt = jnp.concatenate([padded - counts, n_pad - pad_end[-1:]]).astype(jnp.int32)
    xs = _dispatch(x1, dest, zero_from, zero_cnt, n_pad, tm)
    d_ff = w_down.shape[1]
    wgu = _regroup_gate_up(w_gate_up.astype(F32))
    bgu = b_gate_up.astype(F32).reshape(N_EXPERTS, d_ff // LANES, LANES, 2).transpose(0, 1, 3, 2)
    bgu = bgu.reshape(N_EXPERTS, 1, 2 * d_ff)
    seg_end = (pad_start + counts)[block_expert]
    block_rows = jnp.where(block_start < pad_end[-1], jnp.clip(seg_end - block_start, 0, MOE_BLOCK), 0).astype(jnp.int32)
    ys = _expert_ffn(xs, block_expert, n_used, block_rows, wgu, bgu, w_down.astype(F32),
                     b_down.astype(F32).reshape(N_EXPERTS, 1, D_MODEL))

    out = _final(dest, gates, x1, p.reshape(t, PLE_DIM), w_ple_gate.astype(BF16),
                 b_ple_gate.astype(F32).reshape(1, -1), w_ple.astype(BF16), ln2_g.reshape(1, -1),
                 ln2_b.reshape(1, -1), ys, min(COMBINE_TM, t))
    return out.reshape(bsz, seq, D_MODEL)


def kernel(x, p, w_in, b_gate, conv_w, a_log, dt_bias, dn_norm_w, w_branch_a, w_branch_b, w_out, ln1_g, ln1_b,
           router_w, router_b, w_gate_up, b_gate_up, w_down, b_down, w_ple, w_ple_gate, b_ple_gate, ln2_g, ln2_b):
    assert w_in.shape[0] == DEPTH
    return _layer(x, p[0], w_in[0], b_gate[0], conv_w[0], a_log[0], dt_bias[0], dn_norm_w[0], w_branch_a[0],
                  w_branch_b[0], w_out[0], ln1_g[0], ln1_b[0], router_w[0], router_b[0], w_gate_up[0],
                  b_gate_up[0], w_down[0], b_down[0], w_ple[0], w_ple_gate[0], b_ple_gate[0], ln2_g[0], ln2_b[0])
```

```python
import functools

import numpy as np
import jax
import jax.numpy as jnp
from jax import lax
from jax.experimental import pallas as pl
from jax.experimental.pallas import tpu as pltpu

F32 = jnp.float32
BF16 = jnp.bfloat16

D_MODEL = 2048
HEAD_DIM = 128
A_HEADS = 8
DILATION_GROUPS = ((128, 1), (512, 4), (2048, 16))
N_GROUPS = 3
NEG_INF = -1e30
B_QK_HEADS = 8
B_V_HEADS = 16
B_DK = 128
B_DV = 128
CONV_W = 5
RMS_EPS = 1e-6
N_EXPERTS = 32
TOP_K = 4
SWIGLU_ALPHA = 1.702
SWIGLU_LIMIT = 7.0
PLE_DIM = 256
DEPTH = 1
DEEPNORM_ALPHA = (2 * DEPTH) ** 0.25
LN_EPS = 1e-5
A_QKV = N_GROUPS * A_HEADS * HEAD_DIM
B_QK = B_QK_HEADS * B_DK
B_VZ = B_V_HEADS * B_DV
B_GATES = 4 * B_V_HEADS

LANES = 128
N_SIDE = 64
Q_SUB = 128
DN_CHUNK = 64
DN_PAIR = B_V_HEADS // B_QK_HEADS
DN_INTRA_CHUNKS = 4
DN_STATE_CHUNKS = 4
TRI_BASE = 16
ROW_SUBLANES = 8
IN_PROJ_TM = 1024
IN_PROJ_TN = 1024
MOE_BLOCK = 1024
MOE_TF = 512
COMBINE_TM = 256
VMEM_LIMIT = 56 * 1024 * 1024
assert D_MODEL == 2 * ROW_SUBLANES * LANES


def _cparams(sem):
    return pltpu.CompilerParams(dimension_semantics=sem, vmem_limit_bytes=VMEM_LIMIT)


def _sigmoid(x):
    return 1.0 / (1.0 + jnp.exp(-x))


def _dot(a, b):
    return jnp.dot(a, b, preferred_element_type=F32)


def _dot_nt(a, b):
    return lax.dot_general(a, b, (((1,), (1,)), ((), ())), preferred_element_type=F32)


def _dot_tn(a, b):
    return lax.dot_general(a, b, (((0,), (0,)), ((), ())), preferred_element_type=F32)


def _mm_kernel(x_ref, w_ref, o_ref):
    o_ref[...] = _dot(x_ref[...], w_ref[...]).astype(o_ref.dtype)


def _matmul(x, w, out_dtype, tm, tn):
    m, k = x.shape
    n = w.shape[1]
    return pl.pallas_call(
        _mm_kernel,
        grid=(m // tm, n // tn),
        in_specs=[pl.BlockSpec((tm, k), lambda i, j: (i, 0)),
                  pl.BlockSpec((k, tn), lambda i, j: (0, j))],
        out_specs=pl.BlockSpec((tm, tn), lambda i, j: (i, j)),
        out_shape=jax.ShapeDtypeStruct((m, n), out_dtype),
        compiler_params=_cparams(("parallel", "parallel")),
        name="in_proj",
    )(x, w)


def _mm_classes_kernel(x_ref, w_ref, o_ref, acc_ref, *, dil, rows):
    y = _dot(x_ref[...], w_ref[...])
    for j in range(acc_ref.shape[0]):
        ls = slice(j * LANES, (j + 1) * LANES)
        acc_ref[j] = y[:, ls]
        for c in range(dil):
            o_ref[0, c, :, ls] = acc_ref[j, pl.ds(c, rows, stride=dil), :].astype(o_ref.dtype)


def _matmul_classes(x, w, bsz, dil, tm, tn):
    m, k = x.shape
    n = w.shape[1]
    seq = m // bsz
    tiles_per_seq = seq // tm
    rows = tm // dil
    kern = functools.partial(_mm_classes_kernel, dil=dil, rows=rows)
    return pl.pallas_call(
        kern,
        grid=(m // tm, n // tn),
        in_specs=[pl.BlockSpec((tm, k), lambda i, j: (i, 0)),
                  pl.BlockSpec((k, tn), lambda i, j: (0, j))],
        out_specs=pl.BlockSpec((1, dil, rows, tn), lambda i, j: (i // tiles_per_seq, 0, i % tiles_per_seq, j)),
        out_shape=jax.ShapeDtypeStruct((bsz, dil, seq // dil, n), BF16),
        scratch_shapes=[pltpu.VMEM((tn // LANES, tm, LANES), F32)],
        compiler_params=_cparams(("parallel", "parallel")),
        name=f"in_proj_dil{dil}",
    )(x, w)


def _attn_kernel(q_ref, kp_ref, kc_ref, kn_ref, vp_ref, vc_ref, vn_ref, o_ref, lse_ref, kbuf, vbuf,
                 *, dil, sub_len, tl, slopes):
    i0 = pl.program_id(2) * tl
    kbuf[0:N_SIDE, :] = kp_ref[...]
    kbuf[N_SIDE:N_SIDE + tl, :] = kc_ref[...]
    kbuf[N_SIDE + tl:, :] = kn_ref[...]
    vbuf[0:N_SIDE, :] = vp_ref[...]
    vbuf[N_SIDE:N_SIDE + tl, :] = vc_ref[...]
    vbuf[N_SIDE + tl:, :] = vn_ref[...]
    span = Q_SUB + 2 * N_SIDE
    qq = lax.broadcasted_iota(jnp.int32, (Q_SUB, span), 0)
    kk = lax.broadcasted_iota(jnp.int32, (Q_SUB, span), 1)
    delta = kk - N_SIDE - qq
    absd = jnp.abs(delta)
    band = absd <= N_SIDE
    dist = (dil * absd).astype(F32)
    lane = lax.broadcasted_iota(jnp.int32, (Q_SUB, LANES), 1)
    scale = HEAD_DIM ** -0.5
    for j in range(tl // Q_SUB):
        pos = i0 + (j * Q_SUB - N_SIDE) + kk
        valid = band & (pos >= 0) & (pos < sub_len)
        lse_tile = jnp.zeros((Q_SUB, LANES), F32)
        for h in range(A_HEADS):
            hs = slice(h * HEAD_DIM, (h + 1) * HEAD_DIM)
            q = q_ref[j * Q_SUB:(j + 1) * Q_SUB, hs]
            k = kbuf[j * Q_SUB:j * Q_SUB + span, hs]
            v = vbuf[j * Q_SUB:j * Q_SUB + span, hs]
            s = _dot_nt(q, k) * scale
            s = jnp.where(valid, s - float(slopes[h]) * dist, NEG_INF)
            m = jnp.max(s, axis=1, keepdims=True)
            p = jnp.exp(s - m)
            l = jnp.sum(p, axis=1, keepdims=True)
            o = _dot(p.astype(BF16), v) / l
            o_ref[j * Q_SUB:(j + 1) * Q_SUB, hs] = o.astype(o_ref.dtype)
            lse_tile = jnp.where(lane == h, m + jnp.log(l), lse_tile)
        lse_ref[j * Q_SUB:(j + 1) * Q_SUB, :] = lse_tile


def _attention_group(qkv, gi, dil, slopes):
    bsz, _, sub_len, _ = qkv.shape
    tl = min(512, sub_len)
    assert sub_len % tl == 0 and tl % Q_SUB == 0 and sub_len % N_SIDE == 0
    width = A_HEADS * HEAD_DIM
    halo_per_tile = tl // N_SIDE
    n_halo = sub_len // N_SIDE
    prev = lambda li: jnp.maximum(li * halo_per_tile - 1, 0)
    nxt = lambda li: jnp.minimum((li + 1) * halo_per_tile, n_halo - 1)
    halo = lambda col, rowf: pl.BlockSpec((None, None, N_SIDE, width), lambda b, c, li: (b, c, rowf(li), col))
    cur = lambda col: pl.BlockSpec((None, None, tl, width), lambda b, c, li: (b, c, li, col))
    kern = functools.partial(_attn_kernel, dil=dil, sub_len=sub_len, tl=tl, slopes=tuple(float(s) for s in slopes))
    return pl.pallas_call(
        kern,
        grid=(bsz, dil, sub_len // tl),
        in_specs=[cur(0), halo(1, prev), cur(1), halo(1, nxt), halo(2, prev), cur(2), halo(2, nxt)],
        out_specs=[pl.BlockSpec((None, None, tl, width), lambda b, c, li: (b, c, li, 0)),
                   pl.BlockSpec((None, None, tl, LANES), lambda b, c, li: (b, c, li, 0))],
        out_shape=[jax.ShapeDtypeStruct((bsz, dil, sub_len, width), BF16),
                   jax.ShapeDtypeStruct((bsz, dil, sub_len, LANES), F32)],
        scratch_shapes=[pltpu.VMEM((tl + 2 * N_SIDE, width), BF16),
                        pltpu.VMEM((tl + 2 * N_SIDE, width), BF16)],
        compiler_params=_cparams(("parallel", "parallel", "parallel")),
        name=f"dilated_attn_g{gi}",
    )(qkv, qkv, qkv, qkv, qkv, qkv, qkv)


def _alibi_slopes():
    n = N_GROUPS * A_HEADS
    s = 2.0 ** (-8.0 * np.arange(1, n + 1) / n)
    return s.astype(np.float32).reshape(N_GROUPS, A_HEADS)


def _dn_prep_kernel(prev_ref, cur_ref, next_ref, cw_ref, bab_ref, gp_ref, q_ref, k_ref, v_ref, g_ref, *, ts):
    ti = pl.program_id(1)
    nt = pl.num_programs(1)
    halo = CONV_W // 2
    keep_prev = (ti > 0).astype(F32)
    keep_next = (ti < nt - 1).astype(F32)
    for c in range((2 * B_QK + B_VZ) // LANES):
        cs = slice(c * LANES, (c + 1) * LANES)
        xp = prev_ref[0, :, cs].astype(F32)[8:16] * keep_prev
        xc = cur_ref[0, :, cs].astype(F32)
        xn = next_ref[0, :, cs].astype(F32)[0:8] * keep_next
        ext = jnp.concatenate([xp, xc, xn], axis=0)
        acc = jnp.zeros((ts, LANES), F32)
        for j in range(CONV_W):
            off = 8 - halo + j
            acc = acc + ext[off:off + ts, :] * cw_ref[j:j + 1, cs]
        y = acc * _sigmoid(acc)
        if c < 2 * B_QK // LANES:
            y = y * lax.rsqrt(jnp.sum(y * y, axis=1, keepdims=True) + 1e-6)
        if c < B_QK // LANES:
            q_ref[0, :, cs] = y * (B_DK ** -0.5)
        elif c < 2 * B_QK // LANES:
            k_ref[0, :, c * LANES - B_QK:(c + 1) * LANES - B_QK] = y
        else:
            v_ref[0, :, c * LANES - 2 * B_QK:(c + 1) * LANES - 2 * B_QK] = y
    x = bab_ref[0]
    neg_a = gp_ref[0:1, :]
    dtb = gp_ref[1:2, :]
    is_g = gp_ref[2:3, :] > 0.5
    z = x + dtb
    softplus = jnp.maximum(z, 0.0) + jnp.log(1.0 + jnp.exp(-jnp.abs(z)))
    g_ref[0] = jnp.where(is_g, neg_a * softplus, _sigmoid(x))


def _dn_prep(u_b, conv_w, bab, gate_params):
    bsz, seq, _ = u_b.shape
    ts = min(256, seq)
    cq = 2 * B_QK + B_VZ
    nhalo = seq // 16
    per = ts // 16
    kern = functools.partial(_dn_prep_kernel, ts=ts)
    return pl.pallas_call(
        kern,
        grid=(bsz, seq // ts),
        in_specs=[pl.BlockSpec((1, 16, cq), lambda b, t: (b, jnp.maximum(t * per - 1, 0), 0)),
                  pl.BlockSpec((1, ts, cq), lambda b, t: (b, t, 0)),
                  pl.BlockSpec((1, 16, cq), lambda b, t: (b, jnp.minimum((t + 1) * per, nhalo - 1), 0)),
                  pl.BlockSpec((8, cq), lambda b, t: (0, 0)),
                  pl.BlockSpec((1, ts, LANES), lambda b, t: (b, t, 0)),
                  pl.BlockSpec((8, LANES), lambda b, t: (0, 0))],
        out_specs=[pl.BlockSpec((1, ts, B_QK), lambda b, t: (b, t, 0)),
                   pl.BlockSpec((1, ts, B_QK), lambda b, t: (b, t, 0)),
                   pl.BlockSpec((1, ts, B_VZ), lambda b, t: (b, t, 0)),
                   pl.BlockSpec((1, ts, LANES), lambda b, t: (b, t, 0))],
        out_shape=[jax.ShapeDtypeStruct((bsz, seq, B_QK), F32),
                   jax.ShapeDtypeStruct((bsz, seq, B_QK), F32),
                   jax.ShapeDtypeStruct((bsz, seq, B_VZ), F32),
                   jax.ShapeDtypeStruct((bsz, seq, LANES), F32)],
        compiler_params=_cparams(("parallel", "parallel")),
        name="dn_prep",
    )(u_b, u_b, u_b, conv_w, bab, gate_params)


def _dn_intra_kernel(q_ref, k_ref, v_ref, g_ref, gt_ref, u_ref, wq_ref, kq_ref, et_ref, *, c, per):
    n_units = 2 * DN_PAIR
    w4 = n_units * c
    hp = lax.Precision.HIGHEST
    bf = lambda t_: t_.astype(BF16)
    ii = lax.broadcasted_iota(jnp.int32, (c, w4), 0)
    ll = lax.broadcasted_iota(jnp.int32, (c, w4), 1)
    jj = ll % c
    ub = ll // c
    ub_row = ub[0:1, :]
    lo = jnp.where(ub >= DN_PAIR, jj - ii, ii - jj)
    incl = lo >= 0
    strict = lo > 0
    eye = (ii == jj).astype(F32)
    blk = (ii // TRI_BASE) == (jj // TRI_BASE)

    def pack(parts, sel):
        out = parts[n_units - 1]
        for u_ in range(n_units - 2, -1, -1):
            out = jnp.where(sel == u_, parts[u_], out)
        return out

    unit_mask = [jnp.where(ub == u_, 1.0, 0.0).astype(BF16) for u_ in range(n_units)]

    def block_diag(y16):
        return jnp.concatenate([y16 * m_ for m_ in unit_mask], axis=0)

    def mm(xs, ys):
        return [_dot(bf(x_), block_diag(bf(y_))) for x_, y_ in zip(xs, ys)]

    ri = lax.broadcasted_iota(jnp.int32, (c, c), 0)
    ci = lax.broadcasted_iota(jnp.int32, (c, c), 1)
    tri4 = (lo <= 0).astype(F32)
    tri_f = (ci <= ri).astype(F32)
    tri_b = (ci >= ri).astype(F32)
    items = [(cc, p) for cc in range(per) for p in range(B_V_HEADS // DN_PAIR)]
    qs_, ks_, lms, qkms, betas, egcs, kscales, etots = [], [], [], [], [], [], [], []
    for cc, p in items:
        rows_cc = slice(cc * c, (cc + 1) * c)
        if p == 0:
            g_all = g_ref[0, rows_cc, :]
            gc_dir = [jnp.dot(tri_f, g_all, precision=hp, preferred_element_type=F32),
                      jnp.dot(tri_b, g_all, precision=hp, preferred_element_type=F32)]
            gcr_all = jnp.dot(gt_ref[0, cc], tri4, precision=hp, preferred_element_type=F32)
            tot_all = jnp.sum(g_all, axis=0, keepdims=True)
        cs = slice(p * B_DK, (p + 1) * B_DK)
        q = q_ref[0, rows_cc, cs]
        k = k_ref[0, rows_cc, cs]
        k16 = bf(k)
        k4 = jnp.concatenate([k16] * n_units, axis=0)
        gram = _dot_nt(k16, k4)
        qk = _dot_nt(bf(q), k4)
        beta_u, gc_u, gcr_u, tot_u = [], [], [], []
        for u_ in range(n_units):
            d_, e_ = divmod(u_, DN_PAIR)
            h = p * DN_PAIR + e_
            bl = d_ * 2 * B_V_HEADS + h
            gl = bl + B_V_HEADS
            beta_u.append(g_all[:, bl:bl + 1])
            gc_u.append(gc_dir[d_][:, gl:gl + 1])
            gcr_u.append(gcr_all[gl:gl + 1, :])
            tot_u.append(tot_all[:, gl:gl + 1])
        gc_p = pack(gc_u, ub)
        gcr_p = pack(gcr_u, ub_row)
        tot_p = pack(tot_u, ub_row)
        dec = jnp.where(incl, jnp.exp(jnp.where(incl, gc_p - gcr_p, 0.0)), 0.0)
        lms.append(jnp.where(strict, pack(beta_u, ub) * gram * dec, 0.0))
        qkms.append(jnp.where(incl, qk * dec, 0.0))
        qs_.append(q)
        ks_.append(k)
        betas.append(beta_u)
        egcs.append([jnp.exp(g_) for g_ in gc_u])
        kscales.append(jnp.exp(tot_p - gcr_p))
        etots.append([jnp.exp(t_) for t_ in tot_u])

    d1 = [jnp.where(blk, lm, 0.0) for lm in lms]
    d2 = mm(d1, d1)
    d4 = mm(d2, d2)
    d8 = mm(d4, d4)
    tm_ = [eye - d_ for d_ in d1]
    for dk in (d2, d4, d8):
        tm_ = [a + b for a, b in zip(tm_, mm(tm_, dk))]
    size = TRI_BASE
    while size < c:
        off = ((ii // (2 * size)) == (jj // (2 * size))) & ((ii // size) != (jj // size))
        cm = [jnp.where(off, lm, 0.0) for lm in lms]
        pc = mm(tm_, cm)
        tm_ = [a - b for a, b in zip(tm_, mm(pc, tm_))]
        size *= 2

    lane2 = lax.broadcasted_iota(jnp.int32, (1, DN_PAIR * B_DV), 1)
    for it, (cc, p) in enumerate(items):
        q, k = qs_[it], ks_[it]
        rows = []
        for u_ in range(n_units):
            d_, e_ = divmod(u_, DN_PAIR)
            h = p * DN_PAIR + e_
            v = v_ref[0, cc * c:(cc + 1) * c, h * B_DV:(h + 1) * B_DV]
            beta = betas[it][u_]
            rows.append(jnp.concatenate([v * beta, k * (beta * egcs[it][u_])], axis=1))
        uw = _dot(block_diag(bf(tm_[it])), bf(jnp.concatenate(rows, axis=0)))
        k_t = jnp.transpose(jnp.concatenate([k] * DN_PAIR, axis=0))
        for d_ in range(2):
            u0, u1 = d_ * DN_PAIR, d_ * DN_PAIR + 1
            u_ref[d_, 0, cc, p] = jnp.concatenate([uw[u0 * c:(u0 + 1) * c, :B_DV], uw[u1 * c:(u1 + 1) * c, :B_DV]], axis=1)
            wq = jnp.concatenate([uw[u0 * c:(u0 + 1) * c, B_DV:], q * egcs[it][u0],
                                  uw[u1 * c:(u1 + 1) * c, B_DV:], q * egcs[it][u1]], axis=0)
            wq_ref[d_, 0, cc, p] = bf(wq)
            ls = slice(d_ * DN_PAIR * c, (d_ + 1) * DN_PAIR * c)
            kq = jnp.concatenate([k_t * kscales[it][:, ls], qkms[it][:, ls]], axis=0)
            kq_ref[d_, 0, cc, p] = bf(kq)
            et_ref[d_, 0, cc, p:p + 1, :] = jnp.where(lane2 < B_DV, etots[it][u0], etots[it][u1])


def _dn_intra(qn, kn, vn, gts, gts_t):
    bsz, seq, _ = qn.shape
    c = DN_CHUNK
    assert DN_PAIR * c == B_DK and seq % c == 0
    n = seq // c
    npair = B_V_HEADS // DN_PAIR
    wide = DN_PAIR * B_DV
    per = DN_INTRA_CHUNKS if n % DN_INTRA_CHUNKS == 0 else 1
    kern = functools.partial(_dn_intra_kernel, c=c, per=per)
    out5 = lambda r, cdim: pl.BlockSpec((2, 1, per, npair, r, cdim), lambda b, i: (0, b, i, 0, 0, 0))
    return pl.pallas_call(
        kern,
        grid=(bsz, n // per),
        in_specs=[pl.BlockSpec((1, per * c, B_QK), lambda b, i: (b, i, 0)),
                  pl.BlockSpec((1, per * c, B_QK), lambda b, i: (b, i, 0)),
                  pl.BlockSpec((1, per * c, B_VZ), lambda b, i: (b, i, 0)),
                  pl.BlockSpec((1, per * c, LANES), lambda b, i: (b, i, 0)),
                  pl.BlockSpec((1, per, LANES, c), lambda b, i: (b, i, 0, 0))],
        out_specs=[out5(c, wide), out5(2 * DN_PAIR * c, B_DK), out5(B_DK + c, DN_PAIR * c),
                   pl.BlockSpec((2, 1, per, npair, wide), lambda b, i: (0, b, i, 0, 0))],
        out_shape=[jax.ShapeDtypeStruct((2, bsz, n, npair, c, wide), F32),
                   jax.ShapeDtypeStruct((2, bsz, n, npair, 2 * DN_PAIR * c, B_DK), BF16),
                   jax.ShapeDtypeStruct((2, bsz, n, npair, B_DK + c, DN_PAIR * c), BF16),
                   jax.ShapeDtypeStruct((2, bsz, n, npair, wide), F32)],
        compiler_params=_cparams(("parallel", "parallel")),
        name="dn_intra",
    )(qn, kn, vn, gts, gts_t)


def _dn_state_kernel(uf_ref, wqf_ref, kqf_ref, etf_ref, ub_ref, wqb_ref, kqb_ref, etb_ref, of_ref, ob_ref, s_ref,
                     *, c, per):
    @pl.when(pl.program_id(1) == 0)
    def _():
        s_ref[...] = jnp.zeros_like(s_ref)

    npair = B_V_HEADS // DN_PAIR
    bf = lambda t_: t_.astype(BF16)
    chains = [(d_, p) for d_ in range(2) for p in range(npair)]
    refs = ((uf_ref, wqf_ref, kqf_ref, etf_ref, of_ref), (ub_ref, wqb_ref, kqb_ref, etb_ref, ob_ref))
    zero = jnp.zeros((c, B_DV), F32)
    for step in range(per):
        local = (step, per - 1 - step)
        states = [s_ref[d_, p] for d_, p in chains]
        a_res = [_dot(refs[d_][1][0, 0, local[d_], p], bf(s_)) for (d_, p), s_ in zip(chains, states)]
        b_res = []
        for (d_, p), a_ in zip(chains, a_res):
            u = refs[d_][0][0, 0, local[d_], p]
            v0 = u[:, :B_DV] - a_[0:c, :B_DV]
            v1 = u[:, B_DV:] - a_[2 * c:3 * c, B_DV:]
            bd_v = jnp.concatenate([jnp.concatenate([v0, zero], axis=1), jnp.concatenate([zero, v1], axis=1)], axis=0)
            b_res.append(_dot(refs[d_][2][0, 0, local[d_], p], bf(bd_v)))
        for (d_, p), a_, b_, s_ in zip(chains, a_res, b_res, states):
            s_ref[d_, p] = s_ * refs[d_][3][0, 0, local[d_], p:p + 1, :] + b_[:B_DK]
            o_ref = refs[d_][4]
            rs = slice(local[d_] * c, (local[d_] + 1) * c)
            o_ref[0, rs, (2 * p) * B_DV:(2 * p + 1) * B_DV] = a_[c:2 * c, :B_DV] + b_[B_DK:, :B_DV]
            o_ref[0, rs, (2 * p + 1) * B_DV:(2 * p + 2) * B_DV] = a_[3 * c:4 * c, B_DV:] + b_[B_DK:, B_DV:]


def _dn_state(u_all, wq_all, kq_all, et_all, seq):
    _, bsz, n, npair, c, wide = u_all.shape
    per = DN_STATE_CHUNKS if n % DN_STATE_CHUNKS == 0 else 1
    nb = n // per
    kern = functools.partial(_dn_state_kernel, c=c, per=per)
    fwd = lambda r, cdim: pl.BlockSpec((1, 1, per, npair, r, cdim), lambda b, i: (0, b, i, 0, 0, 0))
    bwd = lambda r, cdim: pl.BlockSpec((1, 1, per, npair, r, cdim), lambda b, i: (1, b, nb - 1 - i, 0, 0, 0))
    et_f = pl.BlockSpec((1, 1, per, npair, wide), lambda b, i: (0, b, i, 0, 0))
    et_b = pl.BlockSpec((1, 1, per, npair, wide), lambda b, i: (1, b, nb - 1 - i, 0, 0))
    shapes = ((c, wide), (2 * DN_PAIR * c, B_DK), (B_DK + c, DN_PAIR * c))
    return pl.pallas_call(
        kern,
        grid=(bsz, nb),
        in_specs=[fwd(*shapes[0]), fwd(*shapes[1]), fwd(*shapes[2]), et_f,
                  bwd(*shapes[0]), bwd(*shapes[1]), bwd(*shapes[2]), et_b],
        out_specs=[pl.BlockSpec((1, per * c, B_VZ), lambda b, i: (b, i, 0)),
                   pl.BlockSpec((1, per * c, B_VZ), lambda b, i: (b, nb - 1 - i, 0))],
        out_shape=[jax.ShapeDtypeStruct((bsz, seq, B_VZ), F32), jax.ShapeDtypeStruct((bsz, seq, B_VZ), F32)],
        scratch_shapes=[pltpu.VMEM((2, npair, B_DK, wide), F32)],
        compiler_params=_cparams(("parallel", "arbitrary")),
        name="dn_state",
    )(u_all, wq_all, kq_all, et_all, u_all, wq_all, kq_all, et_all)


def _branch_a_kernel(o0_ref, o1_ref, o2_ref, l0_ref, l1_ref, l2_ref, w_ref, gp_ref, bg_ref, out_ref, l_sc,
                     *, dils, tm):
    tok = lax.broadcasted_iota(jnp.int32, (tm, tm), 0)
    src = lax.broadcasted_iota(jnp.int32, (tm, tm), 1)
    o_tok = []
    for g, (o_ref, l_ref) in enumerate(((o0_ref, l0_ref), (o1_ref, l1_ref), (o2_ref, l2_ref))):
        r = dils[g]
        rows = tm // r
        for c in range(r):
            l_sc[g, pl.ds(c, rows, stride=r), :] = l_ref[c]
        if r == 1:
            o_tok.append(o_ref[0].astype(F32))
        else:
            perm = jnp.where(tok == (src % rows) * r + src // rows, 1.0, 0.0).astype(BF16)
            o_tok.append(_dot(perm, jnp.concatenate([o_ref[c] for c in range(r)], axis=0)))
    ls = [l_sc[g] for g in range(N_GROUPS)]
    m = jnp.maximum(jnp.maximum(ls[0], ls[1]), ls[2])
    es = [jnp.exp(l - m) for l in ls]
    den = es[0] + es[1] + es[2]
    ws = [e / den for e in es]
    parts = []
    for h in range(A_HEADS):
        hs = slice(h * HEAD_DIM, (h + 1) * HEAD_DIM)
        acc = ws[0][:, h:h + 1] * o_tok[0][:, hs]
        acc = acc + ws[1][:, h:h + 1] * o_tok[1][:, hs]
        acc = acc + ws[2][:, h:h + 1] * o_tok[2][:, hs]
        parts.append(acc.astype(BF16))
    oa = jnp.concatenate(parts, axis=1)
    y = _dot(oa, w_ref[...])
    out_ref[...] = _sigmoid(gp_ref[...] + bg_ref[...]) * y


def _branch_a(outs, lses, w_a, gpre, b_gate, tm):
    bsz = outs[0].shape[0]
    dils = tuple(o.shape[1] for o in outs)
    seq = dils[0] * outs[0].shape[2]
    t = bsz * seq
    per_seq = seq // tm
    wd = A_HEADS * HEAD_DIM
    cls = lambda r, width: pl.BlockSpec((None, r, tm // r, width), lambda i: (i // per_seq, 0, i % per_seq, 0))
    kern = functools.partial(_branch_a_kernel, dils=dils, tm=tm)
    return pl.pallas_call(
        kern,
        grid=(t // tm,),
        in_specs=[cls(dils[0], wd), cls(dils[1], wd), cls(dils[2], wd),
                  cls(dils[0], LANES), cls(dils[1], LANES), cls(dils[2], LANES),
                  pl.BlockSpec((wd, D_MODEL), lambda i: (0, 0)),
                  pl.BlockSpec((tm, D_MODEL), lambda i: (i, 0)),
                  pl.BlockSpec((1, D_MODEL), lambda i: (0, 0))],
        out_specs=pl.BlockSpec((tm, D_MODEL), lambda i: (i, 0)),
        out_shape=jax.ShapeDtypeStruct((t, D_MODEL), F32),
        scratch_shapes=[pltpu.VMEM((N_GROUPS, tm, LANES), F32)],
        compiler_params=_cparams(("parallel",)),
        name="branch_a",
    )(*outs, *lses, w_a, gpre, b_gate)


def _branch_b_kernel(of_ref, ob_ref, z_ref, nw_ref, w_ref, gp_ref, bg_ref, a_ref, out_ref):
    nw = nw_ref[...]
    parts = []
    for h in range(B_V_HEADS):
        hs = slice(h * B_DV, (h + 1) * B_DV)
        o = of_ref[:, hs] + ob_ref[:, hs]
        z = z_ref[:, hs].astype(F32)
        o = o * lax.rsqrt(jnp.mean(o * o, axis=1, keepdims=True) + RMS_EPS) * nw * (z * _sigmoid(z))
        parts.append(o.astype(BF16))
    ob = jnp.concatenate(parts, axis=1)
    y = _dot(ob, w_ref[...])
    out_ref[...] = (a_ref[...] + _sigmoid(gp_ref[...] + bg_ref[...]) * y).astype(out_ref.dtype)


def _branch_b(o_f, o_b, u_b2d, norm_w, w_b, gpre, b_gate, a_part, tm):
    t = a_part.shape[0]
    return pl.pallas_call(
        _branch_b_kernel,
        grid=(t // tm,),
        in_specs=[pl.BlockSpec((tm, B_VZ), lambda i: (i, 0)),
                  pl.BlockSpec((tm, B_VZ), lambda i: (i, 0)),
                  pl.BlockSpec((tm, B_VZ), lambda i: (i, (2 * B_QK + B_VZ) // B_VZ)),
                  pl.BlockSpec((1, B_DV), lambda i: (0, 0)),
                  pl.BlockSpec((B_VZ, D_MODEL), lambda i: (0, 0)),
                  pl.BlockSpec((tm, D_MODEL), lambda i: (i, 1)),
                  pl.BlockSpec((1, D_MODEL), lambda i: (0, 1)),
                  pl.BlockSpec((tm, D_MODEL), lambda i: (i, 0))],
        out_specs=pl.BlockSpec((tm, D_MODEL), lambda i: (i, 0)),
        out_shape=jax.ShapeDtypeStruct((t, D_MODEL), BF16),
        compiler_params=_cparams(("parallel",)),
        name="branch_b",
    )(o_f, o_b, u_b2d, norm_w, w_b, gpre, b_gate, a_part)


def _layer_norm(y, g, b):
    mu = jnp.mean(y, axis=1, keepdims=True)
    yc = y - mu
    var = jnp.mean(yc * yc, axis=1, keepdims=True)
    return yc * lax.rsqrt(var + LN_EPS) * g + b


def _out_ln_kernel(m_ref, w_ref, x_ref, g_ref, b_ref, out_ref):
    mix = _dot(m_ref[...], w_ref[...])
    out_ref[...] = _layer_norm(DEEPNORM_ALPHA * x_ref[...] + mix, g_ref[...], b_ref[...])


def _out_ln(merged, w_out, x2d, ln_g, ln_b, tm):
    t = merged.shape[0]
    vec = pl.BlockSpec((1, D_MODEL), lambda i: (0, 0))
    return pl.pallas_call(
        _out_ln_kernel,
        grid=(t // tm,),
        in_specs=[pl.BlockSpec((tm, D_MODEL), lambda i: (i, 0)),
                  pl.BlockSpec((D_MODEL, D_MODEL), lambda i: (0, 0)),
                  pl.BlockSpec((tm, D_MODEL), lambda i: (i, 0)), vec, vec],
        out_specs=pl.BlockSpec((tm, D_MODEL), lambda i: (i, 0)),
        out_shape=jax.ShapeDtypeStruct((t, D_MODEL), F32),
        compiler_params=_cparams(("parallel",)),
        name="out_proj_ln1",
    )(merged, w_out, x2d, ln_g, ln_b)


def _router_kernel(x_ref, rwh_ref, rwl_ref, rb_ref, idx_ref, gate_ref, rank_ref, cnt_ref, carry, *, tm):
    @pl.when(pl.program_id(0) == 0)
    def _():
        carry[...] = jnp.zeros_like(carry)

    lane = lax.broadcasted_iota(jnp.int32, (tm, LANES), 1)
    lane_f = lane.astype(F32)
    x = x_ref[...]
    x_hi = x.astype(BF16)
    x_lo = (x - x_hi.astype(F32)).astype(BF16)
    logits = _dot(x_hi, rwh_ref[...]) + (_dot(x_lo, rwh_ref[...]) + _dot(x_hi, rwl_ref[...])) + rb_ref[...]
    cur = jnp.where(lane < N_EXPERTS, logits, -jnp.inf)
    vals, idxs = [], []
    for _k in range(TOP_K):
        m = jnp.max(cur, axis=1, keepdims=True)
        idx = jnp.min(jnp.where(cur == m, lane_f, float(LANES)), axis=1, keepdims=True).astype(jnp.int32)
        vals.append(m)
        idxs.append(idx)
        cur = jnp.where(lane == idx, -jnp.inf, cur)
    es = [jnp.exp(v - vals[0]) for v in vals]
    den = es[0] + es[1] + es[2] + es[3]
    onehot = jnp.zeros((tm, LANES), F32)
    for idx in idxs:
        onehot = onehot + (lane == idx).astype(F32)
    ri = lax.broadcasted_iota(jnp.int32, (tm, tm), 0)
    ci = lax.broadcasted_iota(jnp.int32, (tm, tm), 1)
    before = (ci < ri).astype(BF16)
    prefix = _dot(before, onehot.astype(BF16)) + carry[0:1, :]
    idx_out = jnp.zeros((tm, LANES), jnp.int32)
    gate_out = jnp.zeros((tm, LANES), F32)
    rank_out = jnp.zeros((tm, LANES), jnp.int32)
    for k in range(TOP_K):
        rk = jnp.sum(jnp.where(lane == idxs[k], prefix, 0.0), axis=1, keepdims=True)
        idx_out = jnp.where(lane == k, idxs[k], idx_out)
        gate_out = jnp.where(lane == k, es[k] / den, gate_out)
        rank_out = jnp.where(lane == k, rk.astype(jnp.int32), rank_out)
    idx_ref[...] = idx_out
    gate_ref[...] = gate_out
    rank_ref[...] = rank_out
    total = carry[0:1, :] + jnp.sum(onehot, axis=0, keepdims=True)
    carry[...] = jnp.broadcast_to(total, carry.shape)
    cnt_ref[...] = jnp.broadcast_to(total, cnt_ref.shape)


def _router(x1, rw, rb, tm):
    rw_hi = rw.astype(BF16)
    rw_lo = (rw - rw_hi.astype(F32)).astype(BF16)
    t = x1.shape[0]
    row = pl.BlockSpec((tm, LANES), lambda i: (i, 0))
    kern = functools.partial(_router_kernel, tm=tm)
    return pl.pallas_call(
        kern,
        grid=(t // tm,),
        in_specs=[pl.BlockSpec((tm, D_MODEL), lambda i: (i, 0)),
                  pl.BlockSpec((D_MODEL, LANES), lambda i: (0, 0)),
                  pl.BlockSpec((D_MODEL, LANES), lambda i: (0, 0)),
                  pl.BlockSpec((1, LANES), lambda i: (0, 0))],
        out_specs=[row, row, row, pl.BlockSpec((8, LANES), lambda i: (0, 0))],
        out_shape=[jax.ShapeDtypeStruct((t, LANES), jnp.int32),
                   jax.ShapeDtypeStruct((t, LANES), F32),
                   jax.ShapeDtypeStruct((t, LANES), jnp.int32),
                   jax.ShapeDtypeStruct((8, LANES), F32)],
        scratch_shapes=[pltpu.VMEM((8, LANES), F32)],
        compiler_params=_cparams(("arbitrary",)),
        name="router",
    )(x1, rw_hi, rw_lo, rb)


def _pack_rows(src_ref, dst_ref):
    half = D_MODEL // 2
    n = src_ref.shape[0]
    for s in range(ROW_SUBLANES):
        lo = pltpu.bitcast(src_ref[:, s * LANES:(s + 1) * LANES].astype(BF16).astype(F32), jnp.uint32)
        hi = pltpu.bitcast(src_ref[:, half + s * LANES:half + (s + 1) * LANES].astype(BF16).astype(F32), jnp.uint32)
        dst_ref[pl.ds(s, n, stride=ROW_SUBLANES), :] = (lo >> 16) | hi


def _unpack_words(w):
    return pltpu.bitcast(w << 16, F32), pltpu.bitcast(w & jnp.uint32(0xFFFF0000), F32)


def _tile(ref, i):
    start = i * ROW_SUBLANES
    if not isinstance(i, int):
        start = pl.multiple_of(start, ROW_SUBLANES)
    return ref.at[pl.ds(start, ROW_SUBLANES)]


def _tile_copy(src, i, dst, j, sem):
    return pltpu.make_async_copy(_tile(src, i), _tile(dst, j), sem)


def _wait_tiles(hbm, n, sem):
    pltpu.make_async_copy(hbm.at[pl.ds(0, n * ROW_SUBLANES)], hbm.at[pl.ds(0, n * ROW_SUBLANES)], sem).wait()


def _dispatch_kernel(pad_start_ref, pad_cnt_ref, dest_ref, x_ref, xs_hbm, stage, zero, sem, *, tm, nt):
    i = pl.program_id(0)

    @pl.when(i < nt)
    def _():
        _pack_rows(x_ref, stage)

        def issue(t_, carry):
            src = _tile(stage, t_)
            for k in range(TOP_K):
                pltpu.make_async_copy(src, _tile(xs_hbm, dest_ref[0, 0, t_ * TOP_K + k]), sem).start()
            return carry

        lax.fori_loop(0, tm, issue, 0, unroll=2)
        _wait_tiles(xs_hbm, tm * TOP_K, sem)

    @pl.when((i >= nt) & (i < nt + N_EXPERTS))
    def _():
        e = i - nt
        zero[...] = jnp.zeros_like(zero)
        start = pad_start_ref[e]
        cnt = pad_cnt_ref[e]

        def issue(r, carry):
            _tile_copy(zero, 0, xs_hbm, start + r, sem).start()
            return carry

        lax.fori_loop(0, cnt, issue, 0)

        def wait(r, carry):
            _tile_copy(zero, 0, xs_hbm, start, sem).wait()
            return carry

        lax.fori_loop(0, cnt, wait, 0)

    @pl.when(i == nt + N_EXPERTS)
    def _():
        stage[...] = jnp.zeros_like(stage)
        start = pad_start_ref[N_EXPERTS]

        def fill(j, carry):
            piece = xs_hbm.at[pl.ds(pl.multiple_of((start + j * tm) * ROW_SUBLANES, ROW_SUBLANES), tm * ROW_SUBLANES)]
            cp = pltpu.make_async_copy(stage, piece, sem)
            cp.start()
            cp.wait()
            return carry

        lax.fori_loop(0, pad_cnt_ref[N_EXPERTS] // tm, fill, 0)


def _dispatch(x1, dest, pad_from, pad_cnt, n_pad, tm):
    t = x1.shape[0]
    nt = t // tm
    assert MOE_BLOCK % tm == 0 and pad_from.shape[0] == N_EXPERTS + 1
    kern = functools.partial(_dispatch_kernel, tm=tm, nt=nt)
    grid_spec = pltpu.PrefetchScalarGridSpec(
        num_scalar_prefetch=2,
        grid=(nt + pad_from.shape[0],),
        in_specs=[pl.BlockSpec((1, 1, tm * TOP_K), lambda i, ps, pc: (jnp.minimum(i, nt - 1), 0, 0),
                               memory_space=pltpu.SMEM),
                  pl.BlockSpec((tm, D_MODEL), lambda i, ps, pc: (jnp.minimum(i, nt - 1), 0))],
        out_specs=pl.BlockSpec(memory_space=pl.ANY),
        scratch_shapes=[pltpu.VMEM((tm * ROW_SUBLANES, LANES), jnp.uint32),
                        pltpu.VMEM((ROW_SUBLANES, LANES), jnp.uint32),
                        pltpu.SemaphoreType.DMA(())],
    )
    return pl.pallas_call(
        kern,
        grid_spec=grid_spec,
        out_shape=jax.ShapeDtypeStruct((n_pad * ROW_SUBLANES, LANES), jnp.uint32),
        compiler_params=_cparams(("arbitrary",)),
        name="moe_dispatch",
    )(pad_from, pad_cnt, dest.reshape(nt, 1, tm * TOP_K), x1)


GU_GROUP = 2 * LANES


def _regroup_kernel(w_ref, o_ref):
    ri = lax.broadcasted_iota(jnp.int32, (GU_GROUP, GU_GROUP), 0)
    ci = lax.broadcasted_iota(jnp.int32, (GU_GROUP, GU_GROUP), 1)
    src = jnp.where(ci < LANES, 2 * ci, 2 * (ci - LANES) + 1)
    perm = jnp.where(ri == src, 1.0, 0.0).astype(BF16)
    for c in range(w_ref.shape[2] // GU_GROUP):
        cs = slice(c * GU_GROUP, (c + 1) * GU_GROUP)
        o_ref[0, :, cs] = _dot(w_ref[0, :, cs].astype(BF16), perm).astype(BF16)


def _regroup_gate_up(w_gate_up):
    e, d, n2 = w_gate_up.shape
    rt = 512
    return pl.pallas_call(
        _regroup_kernel,
        grid=(e, d // rt),
        in_specs=[pl.BlockSpec((1, rt, n2), lambda i, j: (i, j, 0))],
        out_specs=pl.BlockSpec((1, rt, n2), lambda i, j: (i, j, 0)),
        out_shape=jax.ShapeDtypeStruct((e, d, n2), BF16),
        compiler_params=_cparams(("parallel", "parallel")),
        name="regroup_gate_up",
    )(w_gate_up)

def _ffn_kernel(be_ref, nused_ref, rows_ref, x_ref, wgu_ref, bgu_ref, wd_ref, bd_ref, o_ref, acc, xb):
    i = pl.program_id(0)
    f = pl.program_id(1)
    nf = pl.num_programs(1)
    half = D_MODEL // 2

    @pl.when(f == 0)
    def _():
        acc[...] = jnp.broadcast_to(bd_ref[0], acc.shape)
        for s in range(ROW_SUBLANES):
            lo, hi = _unpack_words(x_ref[pl.ds(s, MOE_BLOCK, stride=ROW_SUBLANES), :])
            xb[:, s * LANES:(s + 1) * LANES] = lo.astype(BF16)
            xb[:, half + s * LANES:half + (s + 1) * LANES] = hi.astype(BF16)

    def compute(nrows):
        gu = _dot(xb[0:nrows, :], wgu_ref[0]) + bgu_ref[0]
        acts = []
        for m in range(MOE_TF // LANES):
            gate = jnp.minimum(gu[:, m * GU_GROUP:m * GU_GROUP + LANES], SWIGLU_LIMIT)
            up = jnp.clip(gu[:, m * GU_GROUP + LANES:(m + 1) * GU_GROUP], -SWIGLU_LIMIT, SWIGLU_LIMIT)
            acts.append(((up + 1.0) * gate * _sigmoid(gate * SWIGLU_ALPHA)).astype(BF16))
        acc[0:nrows, :] += _dot(jnp.concatenate(acts, axis=1), wd_ref[0].astype(BF16))

    rows = rows_ref[i]
    pl.when(rows > MOE_BLOCK // 2)(lambda: compute(MOE_BLOCK))
    pl.when((rows > MOE_BLOCK // 4) & (rows <= MOE_BLOCK // 2))(lambda: compute(MOE_BLOCK // 2))
    pl.when((rows > 0) & (rows <= MOE_BLOCK // 4))(lambda: compute(MOE_BLOCK // 4))

    @pl.when(f == nf - 1)
    def _():
        _pack_rows(acc, o_ref)


def _expert_ffn(xs, block_expert, n_used, block_rows, wgu, bgu, wd, bd):
    n_pad = xs.shape[0] // ROW_SUBLANES
    nb = n_pad // MOE_BLOCK
    d_ff = wgu.shape[2] // 2
    nf = d_ff // MOE_TF
    row_blk = (MOE_BLOCK * ROW_SUBLANES, LANES)
    grid_spec = pltpu.PrefetchScalarGridSpec(
        num_scalar_prefetch=3,
        grid=(nb, nf),
        in_specs=[pl.BlockSpec(row_blk, lambda i, f, be, nu, rw: (jnp.minimum(i, nu[0] - 1), 0)),
                  pl.BlockSpec((1, D_MODEL, 2 * MOE_TF), lambda i, f, be, nu, rw: (be[i], 0, f)),
                  pl.BlockSpec((1, 1, 2 * MOE_TF), lambda i, f, be, nu, rw: (be[i], 0, f)),
                  pl.BlockSpec((1, MOE_TF, D_MODEL), lambda i, f, be, nu, rw: (be[i], f, 0)),
                  pl.BlockSpec((1, 1, D_MODEL), lambda i, f, be, nu, rw: (be[i], 0, 0))],
        out_specs=pl.BlockSpec(row_blk, lambda i, f, be, nu, rw: (i, 0)),
        scratch_shapes=[pltpu.VMEM((MOE_BLOCK, D_MODEL), F32), pltpu.VMEM((MOE_BLOCK, D_MODEL), BF16)],
    )
    return pl.pallas_call(
        _ffn_kernel,
        grid_spec=grid_spec,
        out_shape=jax.ShapeDtypeStruct((n_pad * ROW_SUBLANES, LANES), jnp.uint32),
        compiler_params=_cparams(("arbitrary", "arbitrary")),
        name="expert_ffn",
    )(block_expert, n_used, block_rows, xs, wgu, bgu, wd, bd)


def _final_kernel(dest_ref, gate_ref, x_ref, p_ref, wpg_ref, bpg_ref, wple_ref, g_ref, b_ref, ys_hbm, out_ref,
                  rows, sem, *, tm):
    def issue(t_, carry):
        for k in range(TOP_K):
            _tile_copy(ys_hbm, dest_ref[0, 0, t_ * TOP_K + k], rows, k * tm + t_, sem).start()
        return carry

    lax.fori_loop(0, tm, issue, 0, unroll=2)
    x = x_ref[...]
    pg = _dot(x.astype(BF16), wpg_ref[...]) + bpg_ref[...]
    ple = _sigmoid(pg) * _dot(p_ref[...].astype(BF16), wple_ref[...])
    _wait_tiles(ys_hbm, tm * TOP_K, sem)
    gates = gate_ref[...]
    lo_parts, hi_parts = [], []
    for s in range(ROW_SUBLANES):
        acc_lo = acc_hi = None
        for k in range(TOP_K):
            lo, hi = _unpack_words(rows[pl.ds(k * tm * ROW_SUBLANES + s, tm, stride=ROW_SUBLANES), :])
            g = gates[:, k:k + 1]
            acc_lo = g * lo if acc_lo is None else acc_lo + g * lo
            acc_hi = g * hi if acc_hi is None else acc_hi + g * hi
        lo_parts.append(acc_lo)
        hi_parts.append(acc_hi)
    y = jnp.concatenate(lo_parts + hi_parts, axis=1)
    out_ref[...] = _layer_norm(DEEPNORM_ALPHA * x + y + ple, g_ref[...], b_ref[...])


def _final(dest, gates, x1, p2d, w_pg, b_pg, w_ple, ln_g, ln_b, ys, tm):
    t = x1.shape[0]
    vec = pl.BlockSpec((1, D_MODEL), lambda i: (0, 0))
    kern = functools.partial(_final_kernel, tm=tm)
    return pl.pallas_call(
        kern,
        grid=(t // tm,),
        in_specs=[pl.BlockSpec((1, 1, tm * TOP_K), lambda i: (i, 0, 0), memory_space=pltpu.SMEM),
                  pl.BlockSpec((tm, LANES), lambda i: (i, 0)),
                  pl.BlockSpec((tm, D_MODEL), lambda i: (i, 0)),
                  pl.BlockSpec((tm, PLE_DIM), lambda i: (i, 0)),
                  pl.BlockSpec((D_MODEL, D_MODEL), lambda i: (0, 0)), vec,
                  pl.BlockSpec((PLE_DIM, D_MODEL), lambda i: (0, 0)), vec, vec,
                  pl.BlockSpec(memory_space=pl.ANY)],
        out_specs=pl.BlockSpec((tm, D_MODEL), lambda i: (i, 0)),
        out_shape=jax.ShapeDtypeStruct((t, D_MODEL), F32),
        scratch_shapes=[pltpu.VMEM((tm * TOP_K * ROW_SUBLANES, LANES), jnp.uint32), pltpu.SemaphoreType.DMA(())],
        compiler_params=_cparams(("arbitrary",)),
        name="combine_ple_ln2",
    )(dest.reshape(t // tm, 1, tm * TOP_K), gates, x1, p2d, w_pg, b_pg, w_ple, ln_g, ln_b, ys)


def _row_tile(t):
    return min(512, t)


def _layer(x, p, w_in, b_gate, conv_w, a_log, dt_bias, dn_norm_w, w_branch_a, w_branch_b, w_out, ln1_g, ln1_b,
           router_w, router_b, w_gate_up, b_gate_up, w_down, b_down, w_ple, w_ple_gate, b_ple_gate, ln2_g, ln2_b):
    bsz, seq, _ = x.shape
    t = bsz * seq
    tm = _row_tile(t)
    x2d = x.reshape(t, D_MODEL)
    xb = x2d.astype(BF16)

    c_a = 3 * A_QKV
    c_b = 2 * B_QK + 2 * B_VZ
    w_bf = w_in.astype(BF16)
    tp = min(IN_PROJ_TM, seq)
    u_b = _matmul(xb, w_bf[:, c_a:c_a + c_b], BF16, tp, IN_PROJ_TN)
    w_bab = jnp.pad(w_bf[:, c_a + c_b:c_a + c_b + B_GATES], ((0, 0), (0, LANES - B_GATES)))
    bab = _matmul(xb, w_bab, F32, tp, LANES)
    gpre = _matmul(xb, w_bf[:, c_a + c_b + B_GATES:], F32, tp, IN_PROJ_TN)

    slopes = _alibi_slopes()
    gw = A_HEADS * HEAD_DIM
    outs, lses = [], []
    for gi, (_win, dil) in enumerate(DILATION_GROUPS):
        w_g = jnp.concatenate([w_bf[:, part * A_QKV + gi * gw:part * A_QKV + (gi + 1) * gw] for part in range(3)], axis=1)
        o_g, l_g = _attention_group(_matmul_classes(xb, w_g, bsz, dil, tp, IN_PROJ_TN), gi, dil, slopes[gi])
        outs.append(o_g)
        lses.append(l_g)

    cw = jnp.pad(conv_w.astype(F32), ((0, 8 - CONV_W), (0, 0)))
    lane_is_g = (np.arange(LANES) % (2 * B_V_HEADS) >= B_V_HEADS) & (np.arange(LANES) < B_GATES)
    neg_a = jnp.zeros((LANES,), F32).at[B_V_HEADS:2 * B_V_HEADS].set(-jnp.exp(a_log[0].astype(F32)))
    neg_a = neg_a.at[3 * B_V_HEADS:4 * B_V_HEADS].set(-jnp.exp(a_log[1].astype(F32)))
    dtb = jnp.zeros((LANES,), F32).at[B_V_HEADS:2 * B_V_HEADS].set(dt_bias[0].astype(F32))
    dtb = dtb.at[3 * B_V_HEADS:4 * B_V_HEADS].set(dt_bias[1].astype(F32))
    gate_params = jnp.zeros((8, LANES), F32).at[0].set(neg_a).at[1].set(dtb).at[2].set(jnp.asarray(lane_is_g, F32))
    qn, kn, vn, gts = _dn_prep(u_b.reshape(bsz, seq, c_b), cw, bab.reshape(bsz, seq, LANES), gate_params)
    gts_t = gts.reshape(bsz, seq // DN_CHUNK, DN_CHUNK, LANES).transpose(0, 1, 3, 2)
    o_f, o_b = _dn_state(*_dn_intra(qn, kn, vn, gts, gts_t), seq)

    bg = b_gate.astype(F32).reshape(1, 2 * D_MODEL)
    a_part = _branch_a(outs, lses, w_branch_a.astype(BF16), gpre, bg, tm)
    merged = _branch_b(o_f.reshape(t, B_VZ), o_b.reshape(t, B_VZ), u_b, dn_norm_w.astype(F32).reshape(1, B_DV), w_branch_b.astype(BF16), gpre, bg,
                       a_part, tm)
    x1 = _out_ln(merged, w_out.astype(BF16), x2d, ln1_g.reshape(1, -1), ln1_b.reshape(1, -1), tm)

    rw = jnp.pad(router_w.astype(F32), ((0, 0), (0, LANES - N_EXPERTS)))
    rb = jnp.pad(router_b.astype(F32), (0, LANES - N_EXPERTS)).reshape(1, LANES)
    idx, gates, rank, cnt = _router(x1, rw, rb, tm)
    counts = cnt[0, :N_EXPERTS].astype(jnp.int32)
    padded = (counts + MOE_BLOCK - 1) // MOE_BLOCK * MOE_BLOCK
    pad_end = jnp.cumsum(padded)
    pad_start = pad_end - padded
    dest = pad_start[idx[:, :TOP_K]] + rank[:, :TOP_K]
    n_pad = t * TOP_K + N_EXPERTS * MOE_BLOCK
    nb = n_pad // MOE_BLOCK
    block_start = jnp.arange(nb, dtype=jnp.int32) * MOE_BLOCK
    block_expert = jnp.minimum(jnp.sum(pad_end[None, :] <= block_start[:, None], axis=1), N_EXPERTS - 1).astype(jnp.int32)
    n_used = (pad_end[-1:] // MOE_BLOCK).astype(jnp.int32)

    zero_from = jnp.concatenate([pad_start + counts, pad_end[-1:]]).astype(jnp.int32)
    zero_cnt = jnp.concatenate([padded - counts, n_pad - pad_end[-1:]]).astype(jnp.int32)
    xs = _dispatch(x1, dest, zero_from, zero_cnt, n_pad, tm)
    d_ff = w_down.shape[1]
    wgu = _regroup_gate_up(w_gate_up.astype(F32))
    bgu = b_gate_up.astype(F32).reshape(N_EXPERTS, d_ff // LANES, LANES, 2).transpose(0, 1, 3, 2)
    bgu = bgu.reshape(N_EXPERTS, 1, 2 * d_ff)
    seg_end = (pad_start + counts)[block_expert]
    block_rows = jnp.where(block_start < pad_end[-1], jnp.clip(seg_end - block_start, 0, MOE_BLOCK), 0).astype(jnp.int32)
    ys = _expert_ffn(xs, block_expert, n_used, block_rows, wgu, bgu, w_down.astype(F32),
                     b_down.astype(F32).reshape(N_EXPERTS, 1, D_MODEL))

    out = _final(dest, gates, x1, p.reshape(t, PLE_DIM), w_ple_gate.astype(BF16),
                 b_ple_gate.astype(F32).reshape(1, -1), w_ple.astype(BF16), ln2_g.reshape(1, -1),
                 ln2_b.reshape(1, -1), ys, min(COMBINE_TM, t))
    return out.reshape(bsz, seq, D_MODEL)


def kernel(x, p, w_in, b_gate, conv_w, a_log, dt_bias, dn_norm_w, w_branch_a, w_branch_b, w_out, ln1_g, ln1_b,
           router_w, router_b, w_gate_up, b_gate_up, w_down, b_down, w_ple, w_ple_gate, b_ple_gate, ln2_g, ln2_b):
    assert w_in.shape[0] == DEPTH
    return _layer(x, p[0], w_in[0], b_gate[0], conv_w[0], a_log[0], dt_bias[0], dn_norm_w[0], w_branch_a[0],
                  w_branch_b[0], w_out[0], ln1_g[0], ln1_b[0], router_w[0], router_b[0], w_gate_up[0],
                  b_gate_up[0], w_down[0], b_down[0], w_ple[0], w_ple_gate[0], b_ple_gate[0], ln2_g[0], ln2_b[0])
```

```python
import functools

import numpy as np
import jax
import jax.numpy as jnp
from jax import lax
from jax.experimental import pallas as pl
from jax.experimental.pallas import tpu as pltpu

F32 = jnp.float32
BF16 = jnp.bfloat16

D_MODEL = 2048
HEAD_DIM = 128
A_HEADS = 8
DILATION_GROUPS = ((128, 1), (512, 4), (2048, 16))
N_GROUPS = 3
NEG_INF = -1e30
B_QK_HEADS = 8
B_V_HEADS = 16
B_DK = 128
B_DV = 128
CONV_W = 5
RMS_EPS = 1e-6
N_EXPERTS = 32
TOP_K = 4
SWIGLU_ALPHA = 1.702
SWIGLU_LIMIT = 7.0
PLE_DIM = 256
DEPTH = 1
DEEPNORM_ALPHA = (2 * DEPTH) ** 0.25
LN_EPS = 1e-5
A_QKV = N_GROUPS * A_HEADS * HEAD_DIM
B_QK = B_QK_HEADS * B_DK
B_VZ = B_V_HEADS * B_DV
B_GATES = 4 * B_V_HEADS

LANES = 128
N_SIDE = 64
Q_SUB = 128
DN_CHUNK = 64
DN_PAIR = B_V_HEADS // B_QK_HEADS
DN_INTRA_CHUNKS = 4
DN_STATE_CHUNKS = 4
TRI_BASE = 16
ROW_SUBLANES = 8
IN_PROJ_TM = 1024
IN_PROJ_TN = 1024
MOE_BLOCK = 1024
MOE_TF = 512
COMBINE_TM = 256
VMEM_LIMIT = 56 * 1024 * 1024
assert D_MODEL == 2 * ROW_SUBLANES * LANES


def _cparams(sem):
    return pltpu.CompilerParams(dimension_semantics=sem, vmem_limit_bytes=VMEM_LIMIT)


def _sigmoid(x):
    return 1.0 / (1.0 + jnp.exp(-x))


def _dot(a, b):
    return jnp.dot(a, b, preferred_element_type=F32)


def _dot_nt(a, b):
    return lax.dot_general(a, b, (((1,), (1,)), ((), ())), preferred_element_type=F32)


def _dot_tn(a, b):
    return lax.dot_general(a, b, (((0,), (0,)), ((), ())), preferred_element_type=F32)


def _mm_kernel(x_ref, w_ref, o_ref):
    o_ref[...] = _dot(x_ref[...], w_ref[...]).astype(o_ref.dtype)


def _matmul(x, w, out_dtype, tm, tn):
    m, k = x.shape
    n = w.shape[1]
    return pl.pallas_call(
        _mm_kernel,
        grid=(m // tm, n // tn),
        in_specs=[pl.BlockSpec((tm, k), lambda i, j: (i, 0)),
                  pl.BlockSpec((k, tn), lambda i, j: (0, j))],
        out_specs=pl.BlockSpec((tm, tn), lambda i, j: (i, j)),
        out_shape=jax.ShapeDtypeStruct((m, n), out_dtype),
        compiler_params=_cparams(("parallel", "parallel")),
        name="in_proj",
    )(x, w)


def _mm_classes_kernel(x_ref, w_ref, o_ref, acc_ref, *, dil, rows):
    y = _dot(x_ref[...], w_ref[...])
    for j in range(acc_ref.shape[0]):
        ls = slice(j * LANES, (j + 1) * LANES)
        acc_ref[j] = y[:, ls]
        for c in range(dil):
            o_ref[0, c, :, ls] = acc_ref[j, pl.ds(c, rows, stride=dil), :].astype(o_ref.dtype)


def _matmul_classes(x, w, bsz, dil, tm, tn):
    m, k = x.shape
    n = w.shape[1]
    seq = m // bsz
    tiles_per_seq = seq // tm
    rows = tm // dil
    kern = functools.partial(_mm_classes_kernel, dil=dil, rows=rows)
    return pl.pallas_call(
        kern,
        grid=(m // tm, n // tn),
        in_specs=[pl.BlockSpec((tm, k), lambda i, j: (i, 0)),
                  pl.BlockSpec((k, tn), lambda i, j: (0, j))],
        out_specs=pl.BlockSpec((1, dil, rows, tn), lambda i, j: (i // tiles_per_seq, 0, i % tiles_per_seq, j)),
        out_shape=jax.ShapeDtypeStruct((bsz, dil, seq // dil, n), BF16),
        scratch_shapes=[pltpu.VMEM((tn // LANES, tm, LANES), F32)],
        compiler_params=_cparams(("parallel", "parallel")),
        name=f"in_proj_dil{dil}",
    )(x, w)


def _attn_kernel(q_ref, kp_ref, kc_ref, kn_ref, vp_ref, vc_ref, vn_ref, o_ref, lse_ref, kbuf, vbuf,
                 *, dil, sub_len, tl, slopes):
    i0 = pl.program_id(2) * tl
    kbuf[0:N_SIDE, :] = kp_ref[...]
    kbuf[N_SIDE:N_SIDE + tl, :] = kc_ref[...]
    kbuf[N_SIDE + tl:, :] = kn_ref[...]
    vbuf[0:N_SIDE, :] = vp_ref[...]
    vbuf[N_SIDE:N_SIDE + tl, :] = vc_ref[...]
    vbuf[N_SIDE + tl:, :] = vn_ref[...]
    span = Q_SUB + 2 * N_SIDE
    qq = lax.broadcasted_iota(jnp.int32, (Q_SUB, span), 0)
    kk = lax.broadcasted_iota(jnp.int32, (Q_SUB, span), 1)
    delta = kk - N_SIDE - qq
    absd = jnp.abs(delta)
    band = absd <= N_SIDE
    dist = (dil * absd).astype(F32)
    lane = lax.broadcasted_iota(jnp.int32, (Q_SUB, LANES), 1)
    scale = HEAD_DIM ** -0.5
    for j in range(tl // Q_SUB):
        pos = i0 + (j * Q_SUB - N_SIDE) + kk
        valid = band & (pos >= 0) & (pos < sub_len)
        lse_tile = jnp.zeros((Q_SUB, LANES), F32)
        for h in range(A_HEADS):
            hs = slice(h * HEAD_DIM, (h + 1) * HEAD_DIM)
            q = q_ref[j * Q_SUB:(j + 1) * Q_SUB, hs]
            k = kbuf[j * Q_SUB:j * Q_SUB + span, hs]
            v = vbuf[j * Q_SUB:j * Q_SUB + span, hs]
            s = _dot_nt(q, k) * scale
            s = jnp.where(valid, s - float(slopes[h]) * dist, NEG_INF)
            m = jnp.max(s, axis=1, keepdims=True)
            p = jnp.exp(s - m)
            l = jnp.sum(p, axis=1, keepdims=True)
            o = _dot(p.astype(BF16), v) / l
            o_ref[j * Q_SUB:(j + 1) * Q_SUB, hs] = o.astype(o_ref.dtype)
            lse_tile = jnp.where(lane == h, m + jnp.log(l), lse_tile)
        lse_ref[j * Q_SUB:(j + 1) * Q_SUB, :] = lse_tile


def _attention_group(qkv, gi, dil, slopes):
    bsz, _, sub_len, _ = qkv.shape
    tl = min(512, sub_len)
    assert sub_len % tl == 0 and tl % Q_SUB == 0 and sub_len % N_SIDE == 0
    width = A_HEADS * HEAD_DIM
    halo_per_tile = tl // N_SIDE
    n_halo = sub_len // N_SIDE
    prev = lambda li: jnp.maximum(li * halo_per_tile - 1, 0)
    nxt = lambda li: jnp.minimum((li + 1) * halo_per_tile, n_halo - 1)
    halo = lambda col, rowf: pl.BlockSpec((None, None, N_SIDE, width), lambda b, c, li: (b, c, rowf(li), col))
    cur = lambda col: pl.BlockSpec((None, None, tl, width), lambda b, c, li: (b, c, li, col))
    kern = functools.partial(_attn_kernel, dil=dil, sub_len=sub_len, tl=tl, slopes=tuple(float(s) for s in slopes))
    return pl.pallas_call(
        kern,
        grid=(bsz, dil, sub_len // tl),
        in_specs=[cur(0), halo(1, prev), cur(1), halo(1, nxt), halo(2, prev), cur(2), halo(2, nxt)],
        out_specs=[pl.BlockSpec((None, None, tl, width), lambda b, c, li: (b, c, li, 0)),
                   pl.BlockSpec((None, None, tl, LANES), lambda b, c, li: (b, c, li, 0))],
        out_shape=[jax.ShapeDtypeStruct((bsz, dil, sub_len, width), BF16),
                   jax.ShapeDtypeStruct((bsz, dil, sub_len, LANES), F32)],
        scratch_shapes=[pltpu.VMEM((tl + 2 * N_SIDE, width), BF16),
                        pltpu.VMEM((tl + 2 * N_SIDE, width), BF16)],
        compiler_params=_cparams(("parallel", "parallel", "parallel")),
        name=f"dilated_attn_g{gi}",
    )(qkv, qkv, qkv, qkv, qkv, qkv, qkv)


def _alibi_slopes():
    n = N_GROUPS * A_HEADS
    s = 2.0 ** (-8.0 * np.arange(1, n + 1) / n)
    return s.astype(np.float32).reshape(N_GROUPS, A_HEADS)


def _dn_prep_kernel(prev_ref, cur_ref, next_ref, cw_ref, bab_ref, gp_ref, q_ref, k_ref, v_ref, g_ref, *, ts):
    ti = pl.program_id(1)
    nt = pl.num_programs(1)
    halo = CONV_W // 2
    keep_prev = (ti > 0).astype(F32)
    keep_next = (ti < nt - 1).astype(F32)
    for c in range((2 * B_QK + B_VZ) // LANES):
        cs = slice(c * LANES, (c + 1) * LANES)
        xp = prev_ref[0, :, cs].astype(F32)[8:16] * keep_prev
        xc = cur_ref[0, :, cs].astype(F32)
        xn = next_ref[0, :, cs].astype(F32)[0:8] * keep_next
        ext = jnp.concatenate([xp, xc, xn], axis=0)
        acc = jnp.zeros((ts, LANES), F32)
        for j in range(CONV_W):
            off = 8 - halo + j
            acc = acc + ext[off:off + ts, :] * cw_ref[j:j + 1, cs]
        y = acc * _sigmoid(acc)
        if c < 2 * B_QK // LANES:
            y = y * lax.rsqrt(jnp.sum(y * y, axis=1, keepdims=True) + 1e-6)
        if c < B_QK // LANES:
            q_ref[0, :, cs] = y * (B_DK ** -0.5)
        elif c < 2 * B_QK // LANES:
            k_ref[0, :, c * LANES - B_QK:(c + 1) * LANES - B_QK] = y
        else:
            v_ref[0, :, c * LANES - 2 * B_QK:(c + 1) * LANES - 2 * B_QK] = y
    x = bab_ref[0]
    neg_a = gp_ref[0:1, :]
    dtb = gp_ref[1:2, :]
    is_g = gp_ref[2:3, :] > 0.5
    z = x + dtb
    softplus = jnp.maximum(z, 0.0) + jnp.log(1.0 + jnp.exp(-jnp.abs(z)))
    g_ref[0] = jnp.where(is_g, neg_a * softplus, _sigmoid(x))


def _dn_prep(u_b, conv_w, bab, gate_params):
    bsz, seq, _ = u_b.shape
    ts = min(256, seq)
    cq = 2 * B_QK + B_VZ
    nhalo = seq // 16
    per = ts // 16
    kern = functools.partial(_dn_prep_kernel, ts=ts)
    return pl.pallas_call(
        kern,
        grid=(bsz, seq // ts),
        in_specs=[pl.BlockSpec((1, 16, cq), lambda b, t: (b, jnp.maximum(t * per - 1, 0), 0)),
                  pl.BlockSpec((1, ts, cq), lambda b, t: (b, t, 0)),
                  pl.BlockSpec((1, 16, cq), lambda b, t: (b, jnp.minimum((t + 1) * per, nhalo - 1), 0)),
                  pl.BlockSpec((8, cq), lambda b, t: (0, 0)),
                  pl.BlockSpec((1, ts, LANES), lambda b, t: (b, t, 0)),
                  pl.BlockSpec((8, LANES), lambda b, t: (0, 0))],
        out_specs=[pl.BlockSpec((1, ts, B_QK), lambda b, t: (b, t, 0)),
                   pl.BlockSpec((1, ts, B_QK), lambda b, t: (b, t, 0)),
                   pl.BlockSpec((1, ts, B_VZ), lambda b, t: (b, t, 0)),
                   pl.BlockSpec((1, ts, LANES), lambda b, t: (b, t, 0))],
        out_shape=[jax.ShapeDtypeStruct((bsz, seq, B_QK), F32),
                   jax.ShapeDtypeStruct((bsz, seq, B_QK), F32),
                   jax.ShapeDtypeStruct((bsz, seq, B_VZ), F32),
                   jax.ShapeDtypeStruct((bsz, seq, LANES), F32)],
        compiler_params=_cparams(("parallel", "parallel")),
        name="dn_prep",
    )(u_b, u_b, u_b, conv_w, bab, gate_params)


def _dn_intra_kernel(q_ref, k_ref, v_ref, g_ref, gt_ref, u_ref, wq_ref, kq_ref, et_ref, *, c, per):
    n_units = 2 * DN_PAIR
    w4 = n_units * c
    hp = lax.Precision.HIGHEST
    bf = lambda t_: t_.astype(BF16)
    ii = lax.broadcasted_iota(jnp.int32, (c, w4), 0)
    ll = lax.broadcasted_iota(jnp.int32, (c, w4), 1)
    jj = ll % c
    ub = ll // c
    ub_row = ub[0:1, :]
    lo = jnp.where(ub >= DN_PAIR, jj - ii, ii - jj)
    incl = lo >= 0
    strict = lo > 0
    eye = (ii == jj).astype(F32)
    blk = (ii // TRI_BASE) == (jj // TRI_BASE)

    def pack(parts, sel):
        out = parts[n_units - 1]
        for u_ in range(n_units - 2, -1, -1):
            out = jnp.where(sel == u_, parts[u_], out)
        return out

    unit_mask = [jnp.where(ub == u_, 1.0, 0.0).astype(BF16) for u_ in range(n_units)]

    def block_diag(y16):
        return jnp.concatenate([y16 * m_ for m_ in unit_mask], axis=0)

    def mm(xs, ys):
        return [_dot(bf(x_), block_diag(bf(y_))) for x_, y_ in zip(xs, ys)]

    ri = lax.broadcasted_iota(jnp.int32, (c, c), 0)
    ci = lax.broadcasted_iota(jnp.int32, (c, c), 1)
    tri4 = (lo <= 0).astype(F32)
    tri_f = (ci <= ri).astype(F32)
    tri_b = (ci >= ri).astype(F32)
    items = [(cc, p) for cc in range(per) for p in range(B_V_HEADS // DN_PAIR)]
    qs_, ks_, lms, qkms, betas, egcs, kscales, etots = [], [], [], [], [], [], [], []
    for cc, p in items:
        rows_cc = slice(cc * c, (cc + 1) * c)
        if p == 0:
            g_all = g_ref[0, rows_cc, :]
            gc_dir = [jnp.dot(tri_f, g_all, precision=hp, preferred_element_type=F32),
                      jnp.dot(tri_b, g_all, precision=hp, preferred_element_type=F32)]
            gcr_all = jnp.dot(gt_ref[0, cc], tri4, precision=hp, preferred_element_type=F32)
            tot_all = jnp.sum(g_all, axis=0, keepdims=True)
        cs = slice(p * B_DK, (p + 1) * B_DK)
        q = q_ref[0, rows_cc, cs]
        k = k_ref[0, rows_cc, cs]
        k16 = bf(k)
        k4 = jnp.concatenate([k16] * n_units, axis=0)
        gram = _dot_nt(k16, k4)
        qk = _dot_nt(bf(q), k4)
        beta_u, gc_u, gcr_u, tot_u = [], [], [], []
        for u_ in range(n_units):
            d_, e_ = divmod(u_, DN_PAIR)
            h = p * DN_PAIR + e_
            bl = d_ * 2 * B_V_HEADS + h
            gl = bl + B_V_HEADS
            beta_u.append(g_all[:, bl:bl + 1])
            gc_u.append(gc_dir[d_][:, gl:gl + 1])
            gcr_u.append(gcr_all[gl:gl + 1, :])
            tot_u.append(tot_all[:, gl:gl + 1])
        gc_p = pack(gc_u, ub)
        gcr_p = pack(gcr_u, ub_row)
        tot_p = pack(tot_u, ub_row)
        dec = jnp.where(incl, jnp.exp(jnp.where(incl, gc_p - gcr_p, 0.0)), 0.0)
        lms.append(jnp.where(strict, pack(beta_u, ub) * gram * dec, 0.0))
        qkms.append(jnp.where(incl, qk * dec, 0.0))
        qs_.append(q)
        ks_.append(k)
        betas.append(beta_u)
        egcs.append([jnp.exp(g_) for g_ in gc_u])
        kscales.append(jnp.exp(tot_p - gcr_p))
        etots.append([jnp.exp(t_) for t_ in tot_u])

    d1 = [jnp.where(blk, lm, 0.0) for lm in lms]
    d2 = mm(d1, d1)
    d4 = mm(d2, d2)
    d8 = mm(d4, d4)
    tm_ = [eye - d_ for d_ in d1]
    for dk in (d2, d4, d8):
        tm_ = [a + b for a, b in zip(tm_, mm(tm_, dk))]
    size = TRI_BASE
    while size < c:
        off = ((ii // (2 * size)) == (jj // (2 * size))) & ((ii // size) != (jj // size))
        cm = [jnp.where(off, lm, 0.0) for lm in lms]
        pc = mm(tm_, cm)
        tm_ = [a - b for a, b in zip(tm_, mm(pc, tm_))]
        size *= 2

    lane2 = lax.broadcasted_iota(jnp.int32, (1, DN_PAIR * B_DV), 1)
    for it, (cc, p) in enumerate(items):
        q, k = qs_[it], ks_[it]
        rows = []
        for u_ in range(n_units):
            d_, e_ = divmod(u_, DN_PAIR)
            h = p * DN_PAIR + e_
            v = v_ref[0, cc * c:(cc + 1) * c, h * B_DV:(h + 1) * B_DV]
            beta = betas[it][u_]
            rows.append(jnp.concatenate([v * beta, k * (beta * egcs[it][u_])], axis=1))
        uw = _dot(block_diag(bf(tm_[it])), bf(jnp.concatenate(rows, axis=0)))
        k_t = jnp.transpose(jnp.concatenate([k] * DN_PAIR, axis=0))
        for d_ in range(2):
            u0, u1 = d_ * DN_PAIR, d_ * DN_PAIR + 1
            u_ref[d_, 0, cc, p] = jnp.concatenate([uw[u0 * c:(u0 + 1) * c, :B_DV], uw[u1 * c:(u1 + 1) * c, :B_DV]], axis=1)
            wq = jnp.concatenate([uw[u0 * c:(u0 + 1) * c, B_DV:], q * egcs[it][u0],
                                  uw[u1 * c:(u1 + 1) * c, B_DV:], q * egcs[it][u1]], axis=0)
            wq_ref[d_, 0, cc, p] = bf(wq)
            ls = slice(d_ * DN_PAIR * c, (d_ + 1) * DN_PAIR * c)
            kq = jnp.concatenate([k_t * kscales[it][:, ls], qkms[it][:, ls]], axis=0)
            kq_ref[d_, 0, cc, p] = bf(kq)
            et_ref[d_, 0, cc, p:p + 1, :] = jnp.where(lane2 < B_DV, etots[it][u0], etots[it][u1])


def _dn_intra(qn, kn, vn, gts, gts_t):
    bsz, seq, _ = qn.shape
    c = DN_CHUNK
    assert DN_PAIR * c == B_DK and seq % c == 0
    n = seq // c
    npair = B_V_HEADS // DN_PAIR
    wide = DN_PAIR * B_DV
    per = DN_INTRA_CHUNKS if n % DN_INTRA_CHUNKS == 0 else 1
    kern = functools.partial(_dn_intra_kernel, c=c, per=per)
    out5 = lambda r, cdim: pl.BlockSpec((2, 1, per, npair, r, cdim), lambda b, i: (0, b, i, 0, 0, 0))
    return pl.pallas_call(
        kern,
        grid=(bsz, n // per),
        in_specs=[pl.BlockSpec((1, per * c, B_QK), lambda b, i: (b, i, 0)),
                  pl.BlockSpec((1, per * c, B_QK), lambda b, i: (b, i, 0)),
                  pl.BlockSpec((1, per * c, B_VZ), lambda b, i: (b, i, 0)),
                  pl.BlockSpec((1, per * c, LANES), lambda b, i: (b, i, 0)),
                  pl.BlockSpec((1, per, LANES, c), lambda b, i: (b, i, 0, 0))],
        out_specs=[out5(c, wide), out5(2 * DN_PAIR * c, B_DK), out5(B_DK + c, DN_PAIR * c),
                   pl.BlockSpec((2, 1, per, npair, wide), lambda b, i: (0, b, i, 0, 0))],
        out_shape=[jax.ShapeDtypeStruct((2, bsz, n, npair, c, wide), F32),
                   jax.ShapeDtypeStruct((2, bsz, n, npair, 2 * DN_PAIR * c, B_DK), BF16),
                   jax.ShapeDtypeStruct((2, bsz, n, npair, B_DK + c, DN_PAIR * c), BF16),
                   jax.ShapeDtypeStruct((2, bsz, n, npair, wide), F32)],
        compiler_params=_cparams(("parallel", "parallel")),
        name="dn_intra",
    )(qn, kn, vn, gts, gts_t)


def _dn_state_kernel(uf_ref, wqf_ref, kqf_ref, etf_ref, ub_ref, wqb_ref, kqb_ref, etb_ref, of_ref, ob_ref, s_ref,
                     *, c, per):
    @pl.when(pl.program_id(1) == 0)
    def _():
        s_ref[...] = jnp.zeros_like(s_ref)

    npair = B_V_HEADS // DN_PAIR
    bf = lambda t_: t_.astype(BF16)
    chains = [(d_, p) for d_ in range(2) for p in range(npair)]
    refs = ((uf_ref, wqf_ref, kqf_ref, etf_ref, of_ref), (ub_ref, wqb_ref, kqb_ref, etb_ref, ob_ref))
    zero = jnp.zeros((c, B_DV), F32)
    for step in range(per):
        local = (step, per - 1 - step)
        states = [s_ref[d_, p] for d_, p in chains]
        a_res = [_dot(refs[d_][1][0, 0, local[d_], p], bf(s_)) for (d_, p), s_ in zip(chains, states)]
        b_res = []
        for (d_, p), a_ in zip(chains, a_res):
            u = refs[d_][0][0, 0, local[d_], p]
            v0 = u[:, :B_DV] - a_[0:c, :B_DV]
            v1 = u[:, B_DV:] - a_[2 * c:3 * c, B_DV:]
            bd_v = jnp.concatenate([jnp.concatenate([v0, zero], axis=1), jnp.concatenate([zero, v1], axis=1)], axis=0)
            b_res.append(_dot(refs[d_][2][0, 0, local[d_], p], bf(bd_v)))
        for (d_, p), a_, b_, s_ in zip(chains, a_res, b_res, states):
            s_ref[d_, p] = s_ * refs[d_][3][0, 0, local[d_], p:p + 1, :] + b_[:B_DK]
            o_ref = refs[d_][4]
            rs = slice(local[d_] * c, (local[d_] + 1) * c)
            o_ref[0, rs, (2 * p) * B_DV:(2 * p + 1) * B_DV] = a_[c:2 * c, :B_DV] + b_[B_DK:, :B_DV]
            o_ref[0, rs, (2 * p + 1) * B_DV:(2 * p + 2) * B_DV] = a_[3 * c:4 * c, B_DV:] + b_[B_DK:, B_DV:]


def _dn_state(u_all, wq_all, kq_all, et_all, seq):
    _, bsz, n, npair, c, wide = u_all.shape
    per = DN_STATE_CHUNKS if n % DN_STATE_CHUNKS == 0 else 1
    nb = n // per
    kern = functools.partial(_dn_state_kernel, c=c, per=per)
    fwd = lambda r, cdim: pl.BlockSpec((1, 1, per, npair, r, cdim), lambda b, i: (0, b, i, 0, 0, 0))
    bwd = lambda r, cdim: pl.BlockSpec((1, 1, per, npair, r, cdim), lambda b, i: (1, b, nb - 1 - i, 0, 0, 0))
    et_f = pl.BlockSpec((1, 1, per, npair, wide), lambda b, i: (0, b, i, 0, 0))
    et_b = pl.BlockSpec((1, 1, per, npair, wide), lambda b, i: (1, b, nb - 1 - i, 0, 0))
    shapes = ((c, wide), (2 * DN_PAIR * c, B_DK), (B_DK + c, DN_PAIR * c))
    return pl.pallas_call(
        kern,
        grid=(bsz, nb),
        in_specs=[fwd(*shapes[0]), fwd(*shapes[1]), fwd(*shapes[2]), et_f,
                  bwd(*shapes[0]), bwd(*shapes[1]), bwd(*shapes[2]), et_b],
        out_specs=[pl.BlockSpec((1, per * c, B_VZ), lambda b, i: (b, i, 0)),
                   pl.BlockSpec((1, per * c, B_VZ), lambda b, i: (b, nb - 1 - i, 0))],
        out_shape=[jax.ShapeDtypeStruct((bsz, seq, B_VZ), F32), jax.ShapeDtypeStruct((bsz, seq, B_VZ), F32)],
        scratch_shapes=[pltpu.VMEM((2, npair, B_DK, wide), F32)],
        compiler_params=_cparams(("parallel", "arbitrary")),
        name="dn_state",
    )(u_all, wq_all, kq_all, et_all, u_all, wq_all, kq_all, et_all)


def _branch_a_kernel(o0_ref, o1_ref, o2_ref, l0_ref, l1_ref, l2_ref, w_ref, gp_ref, bg_ref, out_ref, l_sc,
                     *, dils, tm):
    tok = lax.broadcasted_iota(jnp.int32, (tm, tm), 0)
    src = lax.broadcasted_iota(jnp.int32, (tm, tm), 1)
    o_tok = []
    for g, (o_ref, l_ref) in enumerate(((o0_ref, l0_ref), (o1_ref, l1_ref), (o2_ref, l2_ref))):
        r = dils[g]
        rows = tm // r
        for c in range(r):
            l_sc[g, pl.ds(c, rows, stride=r), :] = l_ref[c]
        if r == 1:
            o_tok.append(o_ref[0].astype(F32))
        else:
            perm = jnp.where(tok == (src % rows) * r + src // rows, 1.0, 0.0).astype(BF16)
            o_tok.append(_dot(perm, jnp.concatenate([o_ref[c] for c in range(r)], axis=0)))
    ls = [l_sc[g] for g in range(N_GROUPS)]
    m = jnp.maximum(jnp.maximum(ls[0], ls[1]), ls[2])
    es = [jnp.exp(l - m) for l in ls]
    den = es[0] + es[1] + es[2]
    ws = [e / den for e in es]
    parts = []
    for h in range(A_HEADS):
        hs = slice(h * HEAD_DIM, (h + 1) * HEAD_DIM)
        acc = ws[0][:, h:h + 1] * o_tok[0][:, hs]
        acc = acc + ws[1][:, h:h + 1] * o_tok[1][:, hs]
        acc = acc + ws[2][:, h:h + 1] * o_tok[2][:, hs]
        parts.append(acc.astype(BF16))
    oa = jnp.concatenate(parts, axis=1)
    y = _dot(oa, w_ref[...])
    out_ref[...] = _sigmoid(gp_ref[...] + bg_ref[...]) * y


def _branch_a(outs, lses, w_a, gpre, b_gate, tm):
    bsz = outs[0].shape[0]
    dils = tuple(o.shape[1] for o in outs)
    seq = dils[0] * outs[0].shape[2]
    t = bsz * seq
    per_seq = seq // tm
    wd = A_HEADS * HEAD_DIM
    cls = lambda r, width: pl.BlockSpec((None, r, tm // r, width), lambda i: (i // per_seq, 0, i % per_seq, 0))
    kern = functools.partial(_branch_a_kernel, dils=dils, tm=tm)
    return pl.pallas_call(
        kern,
        grid=(t // tm,),
        in_specs=[cls(dils[0], wd), cls(dils[1], wd), cls(dils[2], wd),
                  cls(dils[0], LANES), cls(dils[1], LANES), cls(dils[2], LANES),
                  pl.BlockSpec((wd, D_MODEL), lambda i: (0, 0)),
                  pl.BlockSpec((tm, D_MODEL), lambda i: (i, 0)),
                  pl.BlockSpec((1, D_MODEL), lambda i: (0, 0))],
        out_specs=pl.BlockSpec((tm, D_MODEL), lambda i: (i, 0)),
        out_shape=jax.ShapeDtypeStruct((t, D_MODEL), F32),
        scratch_shapes=[pltpu.VMEM((N_GROUPS, tm, LANES), F32)],
        compiler_params=_cparams(("parallel",)),
        name="branch_a",
    )(*outs, *lses, w_a, gpre, b_gate)


def _branch_b_kernel(of_ref, ob_ref, z_ref, nw_ref, w_ref, gp_ref, bg_ref, a_ref, out_ref):
    nw = nw_ref[...]
    parts = []
    for h in range(B_V_HEADS):
        hs = slice(h * B_DV, (h + 1) * B_DV)
        o = of_ref[:, hs] + ob_ref[:, hs]
        z = z_ref[:, hs].astype(F32)
        o = o * lax.rsqrt(jnp.mean(o * o, axis=1, keepdims=True) + RMS_EPS) * nw * (z * _sigmoid(z))
        parts.append(o.astype(BF16))
    ob = jnp.concatenate(parts, axis=1)
    y = _dot(ob, w_ref[...])
    out_ref[...] = (a_ref[...] + _sigmoid(gp_ref[...] + bg_ref[...]) * y).astype(out_ref.dtype)


def _branch_b(o_f, o_b, u_b2d, norm_w, w_b, gpre, b_gate, a_part, tm):
    t = a_part.shape[0]
    return pl.pallas_call(
        _branch_b_kernel,
        grid=(t // tm,),
        in_specs=[pl.BlockSpec((tm, B_VZ), lambda i: (i, 0)),
                  pl.BlockSpec((tm, B_VZ), lambda i: (i, 0)),
                  pl.BlockSpec((tm, B_VZ), lambda i: (i, (2 * B_QK + B_VZ) // B_VZ)),
                  pl.BlockSpec((1, B_DV), lambda i: (0, 0)),
                  pl.BlockSpec((B_VZ, D_MODEL), lambda i: (0, 0)),
                  pl.BlockSpec((tm, D_MODEL), lambda i: (i, 1)),
                  pl.BlockSpec((1, D_MODEL), lambda i: (0, 1)),
                  pl.BlockSpec((tm, D_MODEL), lambda i: (i, 0))],
        out_specs=pl.BlockSpec((tm, D_MODEL), lambda i: (i, 0)),
        out_shape=jax.ShapeDtypeStruct((t, D_MODEL), BF16),
        compiler_params=_cparams(("parallel",)),
        name="branch_b",
    )(o_f, o_b, u_b2d, norm_w, w_b, gpre, b_gate, a_part)


def _layer_norm(y, g, b):
    mu = jnp.mean(y, axis=1, keepdims=True)
    yc = y - mu
    var = jnp.mean(yc * yc, axis=1, keepdims=True)
    return yc * lax.rsqrt(var + LN_EPS) * g + b


def _out_ln_kernel(m_ref, w_ref, x_ref, g_ref, b_ref, out_ref):
    mix = _dot(m_ref[...], w_ref[...])
    out_ref[...] = _layer_norm(DEEPNORM_ALPHA * x_ref[...] + mix, g_ref[...], b_ref[...])


def _out_ln(merged, w_out, x2d, ln_g, ln_b, tm):
    t = merged.shape[0]
    vec = pl.BlockSpec((1, D_MODEL), lambda i: (0, 0))
    return pl.pallas_call(
        _out_ln_kernel,
        grid=(t // tm,),
        in_specs=[pl.BlockSpec((tm, D_MODEL), lambda i: (i, 0)),
                  pl.BlockSpec((D_MODEL, D_MODEL), lambda i: (0, 0)),
                  pl.BlockSpec((tm, D_MODEL), lambda i: (i, 0)), vec, vec],
        out_specs=pl.BlockSpec((tm, D_MODEL), lambda i: (i, 0)),
        out_shape=jax.ShapeDtypeStruct((t, D_MODEL), F32),
        compiler_params=_cparams(("parallel",)),
        name="out_proj_ln1",
    )(merged, w_out, x2d, ln_g, ln_b)


def _router_kernel(x_ref, rwh_ref, rwl_ref, rb_ref, idx_ref, gate_ref, rank_ref, cnt_ref, carry, *, tm):
    @pl.when(pl.program_id(0) == 0)
    def _():
        carry[...] = jnp.zeros_like(carry)

    lane = lax.broadcasted_iota(jnp.int32, (tm, LANES), 1)
    lane_f = lane.astype(F32)
    x = x_ref[...]
    x_hi = x.astype(BF16)
    x_lo = (x - x_hi.astype(F32)).astype(BF16)
    logits = _dot(x_hi, rwh_ref[...]) + (_dot(x_lo, rwh_ref[...]) + _dot(x_hi, rwl_ref[...])) + rb_ref[...]
    cur = jnp.where(lane < N_EXPERTS, logits, -jnp.inf)
    vals, idxs = [], []
    for _k in range(TOP_K):
        m = jnp.max(cur, axis=1, keepdims=True)
        idx = jnp.min(jnp.where(cur == m, lane_f, float(LANES)), axis=1, keepdims=True).astype(jnp.int32)
        vals.append(m)
        idxs.append(idx)
        cur = jnp.where(lane == idx, -jnp.inf, cur)
    es = [jnp.exp(v - vals[0]) for v in vals]
    den = es[0] + es[1] + es[2] + es[3]
    onehot = jnp.zeros((tm, LANES), F32)
    for idx in idxs:
        onehot = onehot + (lane == idx).astype(F32)
    ri = lax.broadcasted_iota(jnp.int32, (tm, tm), 0)
    ci = lax.broadcasted_iota(jnp.int32, (tm, tm), 1)
    before = (ci < ri).astype(BF16)
    prefix = _dot(before, onehot.astype(BF16)) + carry[0:1, :]
    idx_out = jnp.zeros((tm, LANES), jnp.int32)
    gate_out = jnp.zeros((tm, LANES), F32)
    rank_out = jnp.zeros((tm, LANES), jnp.int32)
    for k in range(TOP_K):
        rk = jnp.sum(jnp.where(lane == idxs[k], prefix, 0.0), axis=1, keepdims=True)
        idx_out = jnp.where(lane == k, idxs[k], idx_out)
        gate_out = jnp.where(lane == k, es[k] / den, gate_out)
        rank_out = jnp.where(lane == k, rk.astype(jnp.int32), rank_out)
    idx_ref[...] = idx_out
    gate_ref[...] = gate_out
    rank_ref[...] = rank_out
    total = carry[0:1, :] + jnp.sum(onehot, axis=0, keepdims=True)
    carry[...] = jnp.broadcast_to(total, carry.shape)
    cnt_ref[...] = jnp.broadcast_to(total, cnt_ref.shape)


def _router(x1, rw, rb, tm):
    rw_hi = rw.astype(BF16)
    rw_lo = (rw - rw_hi.astype(F32)).astype(BF16)
    t = x1.shape[0]
    row = pl.BlockSpec((tm, LANES), lambda i: (i, 0))
    kern = functools.partial(_router_kernel, tm=tm)
    return pl.pallas_call(
        kern,
        grid=(t // tm,),
        in_specs=[pl.BlockSpec((tm, D_MODEL), lambda i: (i, 0)),
                  pl.BlockSpec((D_MODEL, LANES), lambda i: (0, 0)),
                  pl.BlockSpec((D_MODEL, LANES), lambda i: (0, 0)),
                  pl.BlockSpec((1, LANES), lambda i: (0, 0))],
        out_specs=[row, row, row, pl.BlockSpec((8, LANES), lambda i: (0, 0))],
        out_shape=[jax.ShapeDtypeStruct((t, LANES), jnp.int32),
                   jax.ShapeDtypeStruct((t, LANES), F32),
                   jax.ShapeDtypeStruct((t, LANES), jnp.int32),
                   jax.ShapeDtypeStruct((8, LANES), F32)],
        scratch_shapes=[pltpu.VMEM((8, LANES), F32)],
        compiler_params=_cparams(("arbitrary",)),
        name="router",
    )(x1, rw_hi, rw_lo, rb)


def _pack_rows(src_ref, dst_ref):
    half = D_MODEL // 2
    n = src_ref.shape[0]
    for s in range(ROW_SUBLANES):
        lo = pltpu.bitcast(src_ref[:, s * LANES:(s + 1) * LANES].astype(BF16).astype(F32), jnp.uint32)
        hi = pltpu.bitcast(src_ref[:, half + s * LANES:half + (s + 1) * LANES].astype(BF16).astype(F32), jnp.uint32)
        dst_ref[pl.ds(s, n, stride=ROW_SUBLANES), :] = (lo >> 16) | hi


def _unpack_words(w):
    return pltpu.bitcast(w << 16, F32), pltpu.bitcast(w & jnp.uint32(0xFFFF0000), F32)


def _tile(ref, i):
    start = i * ROW_SUBLANES
    if not isinstance(i, int):
        start = pl.multiple_of(start, ROW_SUBLANES)
    return ref.at[pl.ds(start, ROW_SUBLANES)]


def _tile_copy(src, i, dst, j, sem):
    return pltpu.make_async_copy(_tile(src, i), _tile(dst, j), sem)


def _wait_tiles(hbm, n, sem):
    pltpu.make_async_copy(hbm.at[pl.ds(0, n * ROW_SUBLANES)], hbm.at[pl.ds(0, n * ROW_SUBLANES)], sem).wait()


def _dispatch_kernel(pad_start_ref, pad_cnt_ref, dest_ref, x_ref, xs_hbm, stage, zero, sem, *, tm, nt):
    i = pl.program_id(0)

    @pl.when(i < nt)
    def _():
        _pack_rows(x_ref, stage)

        def issue(t_, carry):
            src = _tile(stage, t_)
            for k in range(TOP_K):
                pltpu.make_async_copy(src, _tile(xs_hbm, dest_ref[0, 0, t_ * TOP_K + k]), sem).start(priority=k % 2)
            return carry

        lax.fori_loop(0, tm, issue, 0, unroll=2)
        _wait_tiles(xs_hbm, tm * TOP_K, sem)

    @pl.when((i >= nt) & (i < nt + N_EXPERTS))
    def _():
        e = i - nt
        zero[...] = jnp.zeros_like(zero)
        start = pad_start_ref[e]
        cnt = pad_cnt_ref[e]

        def issue(r, carry):
            _tile_copy(zero, 0, xs_hbm, start + r, sem).start()
            return carry

        lax.fori_loop(0, cnt, issue, 0)

        def wait(r, carry):
            _tile_copy(zero, 0, xs_hbm, start, sem).wait()
            return carry

        lax.fori_loop(0, cnt, wait, 0)

    @pl.when(i == nt + N_EXPERTS)
    def _():
        stage[...] = jnp.zeros_like(stage)
        start = pad_start_ref[N_EXPERTS]

        def fill(j, carry):
            piece = xs_hbm.at[pl.ds(pl.multiple_of((start + j * tm) * ROW_SUBLANES, ROW_SUBLANES), tm * ROW_SUBLANES)]
            cp = pltpu.make_async_copy(stage, piece, sem)
            cp.start()
            cp.wait()
            return carry

        lax.fori_loop(0, pad_cnt_ref[N_EXPERTS] // tm, fill, 0)


def _dispatch(x1, dest, pad_from, pad_cnt, n_pad, tm):
    t = x1.shape[0]
    nt = t // tm
    assert MOE_BLOCK % tm == 0 and pad_from.shape[0] == N_EXPERTS + 1
    kern = functools.partial(_dispatch_kernel, tm=tm, nt=nt)
    grid_spec = pltpu.PrefetchScalarGridSpec(
        num_scalar_prefetch=2,
        grid=(nt + pad_from.shape[0],),
        in_specs=[pl.BlockSpec((1, 1, tm * TOP_K), lambda i, ps, pc: (jnp.minimum(i, nt - 1), 0, 0),
                               memory_space=pltpu.SMEM),
                  pl.BlockSpec((tm, D_MODEL), lambda i, ps, pc: (jnp.minimum(i, nt - 1), 0))],
        out_specs=pl.BlockSpec(memory_space=pl.ANY),
        scratch_shapes=[pltpu.VMEM((tm * ROW_SUBLANES, LANES), jnp.uint32),
                        pltpu.VMEM((ROW_SUBLANES, LANES), jnp.uint32),
                        pltpu.SemaphoreType.DMA(())],
    )
    return pl.pallas_call(
        kern,
        grid_spec=grid_spec,
        out_shape=jax.ShapeDtypeStruct((n_pad * ROW_SUBLANES, LANES), jnp.uint32),
        compiler_params=_cparams(("arbitrary",)),
        name="moe_dispatch",
    )(pad_from, pad_cnt, dest.reshape(nt, 1, tm * TOP_K), x1)


GU_GROUP = 2 * LANES


def _regroup_kernel(w_ref, o_ref):
    ri = lax.broadcasted_iota(jnp.int32, (GU_GROUP, GU_GROUP), 0)
    ci = lax.broadcasted_iota(jnp.int32, (GU_GROUP, GU_GROUP), 1)
    src = jnp.where(ci < LANES, 2 * ci, 2 * (ci - LANES) + 1)
    perm = jnp.where(ri == src, 1.0, 0.0).astype(BF16)
    for c in range(w_ref.shape[2] // GU_GROUP):
        cs = slice(c * GU_GROUP, (c + 1) * GU_GROUP)
        o_ref[0, :, cs] = _dot(w_ref[0, :, cs].astype(BF16), perm).astype(BF16)


def _regroup_gate_up(w_gate_up):
    e, d, n2 = w_gate_up.shape
    rt = 512
    return pl.pallas_call(
        _regroup_kernel,
        grid=(e, d // rt),
        in_specs=[pl.BlockSpec((1, rt, n2), lambda i, j: (i, j, 0))],
        out_specs=pl.BlockSpec((1, rt, n2), lambda i, j: (i, j, 0)),
        out_shape=jax.ShapeDtypeStruct((e, d, n2), BF16),
        compiler_params=_cparams(("parallel", "parallel")),
        name="regroup_gate_up",
    )(w_gate_up)

def _ffn_kernel(be_ref, nused_ref, rows_ref, x_ref, wgu_ref, bgu_ref, wd_ref, bd_ref, o_ref, acc, xb):
    i = pl.program_id(0)
    f = pl.program_id(1)
    nf = pl.num_programs(1)
    half = D_MODEL // 2

    @pl.when(f == 0)
    def _():
        acc[...] = jnp.broadcast_to(bd_ref[0], acc.shape)
        for s in range(ROW_SUBLANES):
            lo, hi = _unpack_words(x_ref[pl.ds(s, MOE_BLOCK, stride=ROW_SUBLANES), :])
            xb[:, s * LANES:(s + 1) * LANES] = lo.astype(BF16)
            xb[:, half + s * LANES:half + (s + 1) * LANES] = hi.astype(BF16)

    def compute(nrows):
        gu = _dot(xb[0:nrows, :], wgu_ref[0]) + bgu_ref[0]
        acts = []
        for m in range(MOE_TF // LANES):
            gate = jnp.minimum(gu[:, m * GU_GROUP:m * GU_GROUP + LANES], SWIGLU_LIMIT)
            up = jnp.clip(gu[:, m * GU_GROUP + LANES:(m + 1) * GU_GROUP], -SWIGLU_LIMIT, SWIGLU_LIMIT)
            acts.append(((up + 1.0) * gate * _sigmoid(gate * SWIGLU_ALPHA)).astype(BF16))
        acc[0:nrows, :] += _dot(jnp.concatenate(acts, axis=1), wd_ref[0].astype(BF16))

    rows = rows_ref[i]
    pl.when(rows > MOE_BLOCK // 2)(lambda: compute(MOE_BLOCK))
    pl.when((rows > MOE_BLOCK // 4) & (rows <= MOE_BLOCK // 2))(lambda: compute(MOE_BLOCK // 2))
    pl.when((rows > 0) & (rows <= MOE_BLOCK // 4))(lambda: compute(MOE_BLOCK // 4))

    @pl.when(f == nf - 1)
    def _():
        _pack_rows(acc, o_ref)


def _expert_ffn(xs, block_expert, n_used, block_rows, wgu, bgu, wd, bd):
    n_pad = xs.shape[0] // ROW_SUBLANES
    nb = n_pad // MOE_BLOCK
    d_ff = wgu.shape[2] // 2
    nf = d_ff // MOE_TF
    row_blk = (MOE_BLOCK * ROW_SUBLANES, LANES)
    grid_spec = pltpu.PrefetchScalarGridSpec(
        num_scalar_prefetch=3,
        grid=(nb, nf),
        in_specs=[pl.BlockSpec(row_blk, lambda i, f, be, nu, rw: (jnp.minimum(i, nu[0] - 1), 0)),
                  pl.BlockSpec((1, D_MODEL, 2 * MOE_TF), lambda i, f, be, nu, rw: (be[i], 0, f)),
                  pl.BlockSpec((1, 1, 2 * MOE_TF), lambda i, f, be, nu, rw: (be[i], 0, f)),
                  pl.BlockSpec((1, MOE_TF, D_MODEL), lambda i, f, be, nu, rw: (be[i], f, 0)),
                  pl.BlockSpec((1, 1, D_MODEL), lambda i, f, be, nu, rw: (be[i], 0, 0))],
        out_specs=pl.BlockSpec(row_blk, lambda i, f, be, nu, rw: (i, 0)),
        scratch_shapes=[pltpu.VMEM((MOE_BLOCK, D_MODEL), F32), pltpu.VMEM((MOE_BLOCK, D_MODEL), BF16)],
    )
    return pl.pallas_call(
        _ffn_kernel,
        grid_spec=grid_spec,
        out_shape=jax.ShapeDtypeStruct((n_pad * ROW_SUBLANES, LANES), jnp.uint32),
        compiler_params=_cparams(("arbitrary", "arbitrary")),
        name="expert_ffn",
    )(block_expert, n_used, block_rows, xs, wgu, bgu, wd, bd)


def _final_kernel(dest_ref, gate_ref, x_ref, p_ref, wpg_ref, bpg_ref, wple_ref, g_ref, b_ref, ys_hbm, out_ref,
                  rows, sem, *, tm):
    def issue(t_, carry):
        for k in range(TOP_K):
            _tile_copy(ys_hbm, dest_ref[0, 0, t_ * TOP_K + k], rows, k * tm + t_, sem).start(priority=k % 2)
        return carry

    lax.fori_loop(0, tm, issue, 0, unroll=2)
    x = x_ref[...]
    pg = _dot(x.astype(BF16), wpg_ref[...]) + bpg_ref[...]
    ple = _sigmoid(pg) * _dot(p_ref[...].astype(BF16), wple_ref[...])
    _wait_tiles(ys_hbm, tm * TOP_K, sem)
    gates = gate_ref[...]
    lo_parts, hi_parts = [], []
    for s in range(ROW_SUBLANES):
        acc_lo = acc_hi = None
        for k in range(TOP_K):
            lo, hi = _unpack_words(rows[pl.ds(k * tm * ROW_SUBLANES + s, tm, stride=ROW_SUBLANES), :])
            g = gates[:, k:k + 1]
            acc_lo = g * lo if acc_lo is None else acc_lo + g * lo
            acc_hi = g * hi if acc_hi is None else acc_hi + g * hi
        lo_parts.append(acc_lo)
        hi_parts.append(acc_hi)
    y = jnp.concatenate(lo_parts + hi_parts, axis=1)
    out_ref[...] = _layer_norm(DEEPNORM_ALPHA * x + y + ple, g_ref[...], b_ref[...])


def _final(dest, gates, x1, p2d, w_pg, b_pg, w_ple, ln_g, ln_b, ys, tm):
    t = x1.shape[0]
    vec = pl.BlockSpec((1, D_MODEL), lambda i: (0, 0))
    kern = functools.partial(_final_kernel, tm=tm)
    return pl.pallas_call(
        kern,
        grid=(t // tm,),
        in_specs=[pl.BlockSpec((1, 1, tm * TOP_K), lambda i: (i, 0, 0), memory_space=pltpu.SMEM),
                  pl.BlockSpec((tm, LANES), lambda i: (i, 0)),
                  pl.BlockSpec((tm, D_MODEL), lambda i: (i, 0)),
                  pl.BlockSpec((tm, PLE_DIM), lambda i: (i, 0)),
                  pl.BlockSpec((D_MODEL, D_MODEL), lambda i: (0, 0)), vec,
                  pl.BlockSpec((PLE_DIM, D_MODEL), lambda i: (0, 0)), vec, vec,
                  pl.BlockSpec(memory_space=pl.ANY)],
        out_specs=pl.BlockSpec((tm, D_MODEL), lambda i: (i, 0)),
        out_shape=jax.ShapeDtypeStruct((t, D_MODEL), F32),
        scratch_shapes=[pltpu.VMEM((tm * TOP_K * ROW_SUBLANES, LANES), jnp.uint32), pltpu.SemaphoreType.DMA(())],
        compiler_params=_cparams(("arbitrary",)),
        name="combine_ple_ln2",
    )(dest.reshape(t // tm, 1, tm * TOP_K), gates, x1, p2d, w_pg, b_pg, w_ple, ln_g, ln_b, ys)


def _row_tile(t):
    return min(512, t)


def _layer(x, p, w_in, b_gate, conv_w, a_log, dt_bias, dn_norm_w, w_branch_a, w_branch_b, w_out, ln1_g, ln1_b,
           router_w, router_b, w_gate_up, b_gate_up, w_down, b_down, w_ple, w_ple_gate, b_ple_gate, ln2_g, ln2_b):
    bsz, seq, _ = x.shape
    t = bsz * seq
    tm = _row_tile(t)
    x2d = x.reshape(t, D_MODEL)
    xb = x2d.astype(BF16)

    c_a = 3 * A_QKV
    c_b = 2 * B_QK + 2 * B_VZ
    w_bf = w_in.astype(BF16)
    tp = min(IN_PROJ_TM, seq)
    u_b = _matmul(xb, w_bf[:, c_a:c_a + c_b], BF16, tp, IN_PROJ_TN)
    w_bab = jnp.pad(w_bf[:, c_a + c_b:c_a + c_b + B_GATES], ((0, 0), (0, LANES - B_GATES)))
    bab = _matmul(xb, w_bab, F32, tp, LANES)
    gpre = _matmul(xb, w_bf[:, c_a + c_b + B_GATES:], F32, tp, IN_PROJ_TN)

    slopes = _alibi_slopes()
    gw = A_HEADS * HEAD_DIM
    outs, lses = [], []
    for gi, (_win, dil) in enumerate(DILATION_GROUPS):
        w_g = jnp.concatenate([w_bf[:, part * A_QKV + gi * gw:part * A_QKV + (gi + 1) * gw] for part in range(3)], axis=1)
        o_g, l_g = _attention_group(_matmul_classes(xb, w_g, bsz, dil, tp, IN_PROJ_TN), gi, dil, slopes[gi])
        outs.append(o_g)
        lses.append(l_g)

    cw = jnp.pad(conv_w.astype(F32), ((0, 8 - CONV_W), (0, 0)))
    lane_is_g = (np.arange(LANES) % (2 * B_V_HEADS) >= B_V_HEADS) & (np.arange(LANES) < B_GATES)
    neg_a = jnp.zeros((LANES,), F32).at[B_V_HEADS:2 * B_V_HEADS].set(-jnp.exp(a_log[0].astype(F32)))
    neg_a = neg_a.at[3 * B_V_HEADS:4 * B_V_HEADS].set(-jnp.exp(a_log[1].astype(F32)))
    dtb = jnp.zeros((LANES,), F32).at[B_V_HEADS:2 * B_V_HEADS].set(dt_bias[0].astype(F32))
    dtb = dtb.at[3 * B_V_HEADS:4 * B_V_HEADS].set(dt_bias[1].astype(F32))
    gate_params = jnp.zeros((8, LANES), F32).at[0].set(neg_a).at[1].set(dtb).at[2].set(jnp.asarray(lane_is_g, F32))
    qn, kn, vn, gts = _dn_prep(u_b.reshape(bsz, seq, c_b), cw, bab.reshape(bsz, seq, LANES), gate_params)
    gts_t = gts.reshape(bsz, seq // DN_CHUNK, DN_CHUNK, LANES).transpose(0, 1, 3, 2)
    o_f, o_b = _dn_state(*_dn_intra(qn, kn, vn, gts, gts_t), seq)

    bg = b_gate.astype(F32).reshape(1, 2 * D_MODEL)
    a_part = _branch_a(outs, lses, w_branch_a.astype(BF16), gpre, bg, tm)
    merged = _branch_b(o_f.reshape(t, B_VZ), o_b.reshape(t, B_VZ), u_b, dn_norm_w.astype(F32).reshape(1, B_DV), w_branch_b.astype(BF16), gpre, bg,
                       a_part, tm)
    x1 = _out_ln(merged, w_out.astype(BF16), x2d, ln1_g.reshape(1, -1), ln1_b.reshape(1, -1), tm)

    rw = jnp.pad(router_w.astype(F32), ((0, 0), (0, LANES - N_EXPERTS)))
    rb = jnp.pad(router_b.astype(F32), (0, LANES - N_EXPERTS)).reshape(1, LANES)
    idx, gates, rank, cnt = _router(x1, rw, rb, tm)
    counts = cnt[0, :N_EXPERTS].astype(jnp.int32)
    padded = (counts + MOE_BLOCK - 1) // MOE_BLOCK * MOE_BLOCK
    pad_end = jnp.cumsum(padded)
    pad_start = pad_end - padded
    dest = pad_start[idx[:, :TOP_K]] + rank[:, :TOP_K]
    n_pad = t * TOP_K + N_EXPERTS * MOE_BLOCK
    nb = n_pad // MOE_BLOCK
    block_start = jnp.arange(nb, dtype=jnp.int32) * MOE_BLOCK
    block_expert = jnp.minimum(jnp.sum(pad_end[None, :] <= block_start[:, None], axis=1), N_EXPERTS - 1).astype(jnp.int32)
    n_used = (pad_end[-1:] // MOE_BLOCK).astype(jnp.int32)

    zero_from = jnp.concatenate([pad_start + counts, pad_end[-1:]]).astype(jnp.int32)
    zero_cnt = jnp.concatenate([padded - counts, n_pad - pad_end[-1:]]).astype(jnp.int32)
    xs = _dispatch(x1, dest, zero_from, zero_cnt, n_pad, tm)
    d_ff = w_down.shape[1]
    wgu = _regroup_gate_up(w_gate_up.astype(F32))
    bgu = b_gate_up.astype(F32).reshape(N_EXPERTS, d_ff // LANES, LANES, 2).transpose(0, 1, 3, 2)
    bgu = bgu.reshape(N_EXPERTS, 1, 2 * d_ff)
    seg_end = (pad_start + counts)[block_expert]
    block_rows = jnp.where(block_start < pad_end[-1], jnp.clip(seg_end - block_start, 0, MOE_BLOCK), 0).astype(jnp.int32)
    ys = _expert_ffn(xs, block_expert, n_used, block_rows, wgu, bgu, w_down.astype(F32),
                     b_down.astype(F32).reshape(N_EXPERTS, 1, D_MODEL))

    out = _final(dest, gates, x1, p.reshape(t, PLE_DIM), w_ple_gate.astype(BF16),
                 b_ple_gate.astype(F32).reshape(1, -1), w_ple.astype(BF16), ln2_g.reshape(1, -1),
                 ln2_b.reshape(1, -1), ys, min(COMBINE_TM, t))
    return out.reshape(bsz, seq, D_MODEL)


def kernel(x, p, w_in, b_gate, conv_w, a_log, dt_bias, dn_norm_w, w_branch_a, w_branch_b, w_out, ln1_g, ln1_b,
           router_w, router_b, w_gate_up, b_gate_up, w_down, b_down, w_ple, w_ple_gate, b_ple_gate, ln2_g, ln2_b):
    assert w_in.shape[0] == DEPTH
    return _layer(x, p[0], w_in[0], b_gate[0], conv_w[0], a_log[0], dt_bias[0], dn_norm_w[0], w_branch_a[0],
                  w_branch_b[0], w_out[0], ln1_g[0], ln1_b[0], router_w[0], router_b[0], w_gate_up[0],
                  b_gate_up[0], w_down[0], b_down[0], w_ple[0], w_ple_gate[0], b_ple_gate[0], ln2_g[0], ln2_b[0])
```
